```python
import math
import jax, jax.numpy as jnp
from jax import lax
import numpy as np

D_MODEL = 1024
BATCH = 8
SEQ = 4096
DEPTH = 2

N_MIXERS = 2
N_ATTN_LAYERS = (DEPTH + 1) // 2
N_CONV_LAYERS = DEPTH // 2
RMS_EPS = 1e-6

HEAD_DIM = 64
N_Q_HEADS = D_MODEL // HEAD_DIM
N_KV_HEADS = 4
GROUP = N_Q_HEADS // N_KV_HEADS
WINDOW = 128
ROT_DIM = HEAD_DIM // 4
ROPE_THETA = 500000.0
QKV_DIM = (N_Q_HEADS + 2 * N_KV_HEADS) * HEAD_DIM
NEG_INF = -1e30

CONV_WIDTH = 3

PEER_HEADS = 8
N_KEYS = 128
N_EXPERTS = N_KEYS * N_KEYS
PEER_TOPK = 16
QUERY_DIM = 256
QUERY_HALF = QUERY_DIM // 2
TOKEN_BLOCK = 128

kernel_name = "hybrid_swa_sink_shortconv_peer"


def rmsnorm(x, gain):
    x32 = x.astype(jnp.float32)
    y = x32 * lax.rsqrt(jnp.mean(x32 * x32, axis=-1, keepdims=True) + RMS_EPS)
    return (y * gain.astype(jnp.float32)).astype(x.dtype)


def partial_rope(x, pos):
    half = ROT_DIM // 2
    freqs = ROPE_THETA ** (-jnp.arange(0, ROT_DIM, 2, dtype=jnp.float32) / ROT_DIM)
    ang = pos.astype(jnp.float32)[:, None] * freqs[None, :]
    cos = jnp.cos(ang)[None, :, None, :]
    sin = jnp.sin(ang)[None, :, None, :]
    x32 = x.astype(jnp.float32)
    x1, x2, rest = x32[..., :half], x32[..., half:ROT_DIM], x32[..., ROT_DIM:]
    out = jnp.concatenate([x1 * cos - x2 * sin, x2 * cos + x1 * sin, rest], axis=-1)
    return out.astype(x.dtype)


def sliding_window_attention(xn, w_qkv, q_gain, k_gain, sinks, w_o):
    B, S, _ = xn.shape
    nb = S // WINDOW
    qkv = xn @ w_qkv
    q_end = N_Q_HEADS * HEAD_DIM
    k_end = q_end + N_KV_HEADS * HEAD_DIM
    q = qkv[..., :q_end].reshape(B, S, N_Q_HEADS, HEAD_DIM)
    k = qkv[..., q_end:k_end].reshape(B, S, N_KV_HEADS, HEAD_DIM)
    v = qkv[..., k_end:].reshape(B, S, N_KV_HEADS, HEAD_DIM)
    q = rmsnorm(q, q_gain)
    k = rmsnorm(k, k_gain)
    pos = jnp.arange(S)
    q = partial_rope(q, pos)
    k = partial_rope(k, pos)

    qb = q.reshape(B, nb, WINDOW, N_KV_HEADS, GROUP, HEAD_DIM)
    pad = ((0, 0), (WINDOW, 0), (0, 0), (0, 0))
    kp = jnp.pad(k, pad).reshape(B, nb + 1, WINDOW, N_KV_HEADS, HEAD_DIM)
    vp = jnp.pad(v, pad).reshape(B, nb + 1, WINDOW, N_KV_HEADS, HEAD_DIM)
    kband = jnp.concatenate([kp[:, :-1], kp[:, 1:]], axis=2)
    vband = jnp.concatenate([vp[:, :-1], vp[:, 1:]], axis=2)

    scale = 1.0 / math.sqrt(HEAD_DIM)
    s = jnp.einsum('bnqhgd,bnkhd->bnhgqk', qb, kband).astype(jnp.float32) * scale
    qi = jnp.arange(WINDOW)[:, None]
    ki = jnp.arange(2 * WINDOW)[None, :]
    rel = WINDOW + qi - ki
    band = (rel >= 0) & (rel < WINDOW)
    blk = jnp.arange(nb)[:, None, None]
    valid = band[None] & (blk * WINDOW + ki[None] >= WINDOW)
    s = jnp.where(valid[None, :, None, None], s, NEG_INF)

    sink = sinks.astype(jnp.float32).reshape(N_KV_HEADS, GROUP)[None, None, :, :, None, None]
    sink = jnp.broadcast_to(sink, s.shape[:-1] + (1,))
    p = jax.nn.softmax(jnp.concatenate([s, sink], axis=-1), axis=-1)[..., :-1]
    o = jnp.einsum('bnhgqk,bnkhd->bnqhgd', p.astype(vband.dtype), vband)
    return o.reshape(B, S, N_Q_HEADS * HEAD_DIM) @ w_o


def short_gated_conv(xn, w_in, conv_w, w_out):
    B, S, D = xn.shape
    bcu = xn @ w_in
    gate_b = bcu[..., :D]
    gate_c = bcu[..., D:2 * D]
    u = bcu[..., 2 * D:]
    z = gate_c * u
    zp = jnp.pad(z, ((0, 0), (CONV_WIDTH - 1, 0), (0, 0)))
    conv = sum(conv_w[j] * zp[:, j:j + S] for j in range(CONV_WIDTH))
    return (gate_b * conv) @ w_out


def peer_ffn(xn, w_query, sub_keys, expert_u, expert_v):
    B, S, D = xn.shape
    T = B * S
    xt = xn.reshape(T, D)
    q = (xt @ w_query).reshape(T, PEER_HEADS, 2, QUERY_HALF)
    s = jnp.einsum('thpd,hpnd->thpn', q, sub_keys).astype(jnp.float32)
    s_top, i_top = lax.top_k(s, PEER_TOPK)
    cand = (s_top[:, :, 0, :, None] + s_top[:, :, 1, None, :]).reshape(T, PEER_HEADS, PEER_TOPK * PEER_TOPK)
    cand_idx = (i_top[:, :, 0, :, None] * N_KEYS + i_top[:, :, 1, None, :]).reshape(T, PEER_HEADS, PEER_TOPK * PEER_TOPK)
    g_s, sel = lax.top_k(cand, PEER_TOPK)
    idx = jnp.take_along_axis(cand_idx, sel, axis=-1)
    g = jax.nn.softmax(g_s, axis=-1)

    nblk = T // TOKEN_BLOCK

    def expert_block(args):
        xc, ic, gc = args
        u = jnp.take(expert_u, ic, axis=0)
        a = jnp.einsum('td,thkd->thk', xc, u).astype(jnp.float32)
        w = (gc * jax.nn.gelu(a, approximate=False)).astype(xc.dtype)
        v = jnp.take(expert_v, ic, axis=0)
        return jnp.einsum('thk,thkd->td', w, v)

    y = lax.map(expert_block, (xt.reshape(nblk, TOKEN_BLOCK, D),
                               idx.reshape(nblk, TOKEN_BLOCK, PEER_HEADS, PEER_TOPK),
                               g.reshape(nblk, TOKEN_BLOCK, PEER_HEADS, PEER_TOPK)))
    return y.reshape(B, S, D)


def setup_inputs(seed: int = 0) -> dict:
    key = jax.random.key(seed)
    ks = jax.random.split(key, 16)
    D = D_MODEL
    nrm = jax.random.normal
    x = nrm(ks[0], (BATCH, SEQ, D), jnp.float32)
    norm_mix = 1.0 + 0.05 * nrm(ks[1], (DEPTH, D), jnp.float32)
    norm_ffn = 1.0 + 0.05 * nrm(ks[2], (DEPTH, D), jnp.float32)
    attn_w_qkv = nrm(ks[3], (N_ATTN_LAYERS, D, QKV_DIM), jnp.float32) * D ** -0.5
    attn_q_norm = 1.0 + 0.05 * nrm(ks[4], (N_ATTN_LAYERS, HEAD_DIM), jnp.float32)
    attn_k_norm = 1.0 + 0.05 * nrm(ks[5], (N_ATTN_LAYERS, HEAD_DIM), jnp.float32)
    attn_sinks = 0.5 * nrm(ks[6], (N_ATTN_LAYERS, N_Q_HEADS), jnp.float32)
    attn_w_o = nrm(ks[7], (N_ATTN_LAYERS, N_Q_HEADS * HEAD_DIM, D), jnp.float32) * (N_Q_HEADS * HEAD_DIM) ** -0.5
    conv_w_in = nrm(ks[8], (N_CONV_LAYERS, D, 3 * D), jnp.float32) * D ** -0.5
    conv_w = nrm(ks[9], (N_CONV_LAYERS, CONV_WIDTH, D), jnp.float32) * CONV_WIDTH ** -0.5
    conv_w_out = nrm(ks[10], (N_CONV_LAYERS, D, D), jnp.float32) * D ** -0.5
    peer_w_query = nrm(ks[11], (DEPTH, D, PEER_HEADS * QUERY_DIM), jnp.float32) * D ** -0.5
    peer_sub_keys = nrm(ks[12], (DEPTH, PEER_HEADS, 2, N_KEYS, QUERY_HALF), jnp.float32) * QUERY_HALF ** -0.5
    peer_u = nrm(ks[13], (DEPTH, N_EXPERTS, D), jnp.float32) * D ** -0.5
    peer_v = nrm(ks[14], (DEPTH, N_EXPERTS, D), jnp.float32) * (PEER_HEADS * PEER_TOPK) ** -0.5
    return {"x": x, "norm_mix": norm_mix, "norm_ffn": norm_ffn,
            "attn_w_qkv": attn_w_qkv, "attn_q_norm": attn_q_norm, "attn_k_norm": attn_k_norm,
            "attn_sinks": attn_sinks, "attn_w_o": attn_w_o,
            "conv_w_in": conv_w_in, "conv_w": conv_w, "conv_w_out": conv_w_out,
            "peer_w_query": peer_w_query, "peer_sub_keys": peer_sub_keys,
            "peer_u": peer_u, "peer_v": peer_v}


def reference(x, norm_mix, norm_ffn, attn_w_qkv, attn_q_norm, attn_k_norm, attn_sinks, attn_w_o,
              conv_w_in, conv_w, conv_w_out, peer_w_query, peer_sub_keys, peer_u, peer_v):
    for i in range(DEPTH):
        h = rmsnorm(x, norm_mix[i])
        j = i // N_MIXERS
        if i % N_MIXERS == 0:
            x = x + sliding_window_attention(h, attn_w_qkv[j], attn_q_norm[j], attn_k_norm[j],
                                             attn_sinks[j], attn_w_o[j])
        else:
            x = x + short_gated_conv(h, conv_w_in[j], conv_w[j], conv_w_out[j])
        h = rmsnorm(x, norm_ffn[i])
        x = x + peer_ffn(h, peer_w_query[i], peer_sub_keys[i], peer_u[i], peer_v[i])
    return x
```

```python
import functools
import math

import jax
import jax.numpy as jnp
from jax import lax
from jax.experimental import pallas as pl
from jax.experimental.pallas import tpu as pltpu

D_MODEL = 1024
RMS_EPS = 1e-6

HEAD_DIM = 64
N_Q_HEADS = 16
N_KV_HEADS = 4
GROUP = N_Q_HEADS // N_KV_HEADS
WINDOW = 128
ROT_DIM = HEAD_DIM // 4
ROPE_THETA = 500000.0
Q_COLS = N_Q_HEADS * HEAD_DIM
KV_COLS = N_KV_HEADS * HEAD_DIM
NEG_INF = -1e30

CONV_WIDTH = 3

PEER_HEADS = 8
N_KEYS = 128
N_EXPERTS = N_KEYS * N_KEYS
PEER_TOPK = 16
QUERY_HALF = 128
SLOTS = PEER_HEADS * PEER_TOPK

LANES = 128
SUBLANES = 8
ROW_WORDS = D_MODEL // 2 // LANES
VMEM_LIMIT = 48 * 1024 * 1024

BF16 = jnp.bfloat16
F32 = jnp.float32


def _rms(x, gain):
    return x * lax.rsqrt(jnp.mean(x * x, axis=-1, keepdims=True) + RMS_EPS) * gain


def _norm_matmul_kernel(x_ref, g_ref, w_ref, o_ref):
    h = _rms(x_ref[...], g_ref[...])
    o_ref[...] = jnp.dot(h.astype(BF16), w_ref[...], preferred_element_type=F32)


def _norm_matmul(x, gain, w, tm=512):
    t, d = x.shape
    n = w.shape[1]
    return pl.pallas_call(
        _norm_matmul_kernel,
        grid=(t // tm,),
        in_specs=[pl.BlockSpec((tm, d), lambda i: (i, 0)),
                  pl.BlockSpec((1, d), lambda i: (0, 0)),
                  pl.BlockSpec((d, n), lambda i: (0, 0))],
        out_specs=pl.BlockSpec((tm, n), lambda i: (i, 0)),
        out_shape=jax.ShapeDtypeStruct((t, n), F32),
        compiler_params=pltpu.CompilerParams(dimension_semantics=("arbitrary",), vmem_limit_bytes=VMEM_LIMIT),
        name="norm_matmul",
    )(x, gain.reshape(1, d), w)


def _matmul_residual_kernel(a_ref, w_ref, r_ref, o_ref):
    o_ref[...] = r_ref[...] + jnp.dot(a_ref[...].astype(BF16), w_ref[...], preferred_element_type=F32)


def _matmul_residual(a, w, res, tm=512):
    t, k = a.shape
    n = w.shape[1]
    return pl.pallas_call(
        _matmul_residual_kernel,
        grid=(t // tm,),
        in_specs=[pl.BlockSpec((tm, k), lambda i: (i, 0)),
                  pl.BlockSpec((k, n), lambda i: (0, 0)),
                  pl.BlockSpec((tm, n), lambda i: (i, 0))],
        out_specs=pl.BlockSpec((tm, n), lambda i: (i, 0)),
        out_shape=jax.ShapeDtypeStruct((t, n), F32),
        compiler_params=pltpu.CompilerParams(dimension_semantics=("arbitrary",), vmem_limit_bytes=VMEM_LIMIT),
        name="matmul_residual",
    )(a, w, res)


def _rope_tables(seq):
    half = ROT_DIM // 2
    freqs = ROPE_THETA ** (-jnp.arange(0, ROT_DIM, 2, dtype=F32) / ROT_DIM)
    ang = jnp.arange(seq, dtype=F32)[:, None] * freqs[None, :]
    cos, sin = jnp.cos(ang), jnp.sin(ang)
    ones = jnp.ones((seq, HEAD_DIM - ROT_DIM), F32)
    zeros = jnp.zeros((seq, HEAD_DIM - ROT_DIM), F32)
    zh = jnp.zeros((seq, half), F32)
    c = jnp.concatenate([cos, cos, ones], axis=1)
    s_next = jnp.concatenate([-sin, zh, zeros], axis=1)
    s_prev = jnp.concatenate([zh, sin, zeros], axis=1)
    return jnp.stack([jnp.tile(c, (1, 2)), jnp.tile(s_next, (1, 2)), jnp.tile(s_prev, (1, 2))])


def _head_norm_rope(x, gain2, rope, lo):
    sq = x * x
    s_lo = jnp.sum(jnp.where(lo, sq, 0.0), axis=1, keepdims=True)
    s_hi = jnp.sum(jnp.where(lo, 0.0, sq), axis=1, keepdims=True)
    ms = jnp.where(lo, s_lo, s_hi) * (1.0 / HEAD_DIM)
    xn = x * lax.rsqrt(ms + RMS_EPS) * gain2
    half = ROT_DIM // 2
    return xn * rope[0] + pltpu.roll(xn, LANES - half, 1) * rope[1] + pltpu.roll(xn, half, 1) * rope[2]


def _attn_kernel(sinks_ref, q_ref, kc_ref, kp_ref, vc_ref, vp_ref, rc_ref, rp_ref, qg_ref, kg_ref, o_ref):
    n = pl.program_id(1)
    lo = lax.broadcasted_iota(jnp.int32, (WINDOW, LANES), 1) < HEAD_DIM
    rope_c = rc_ref[...]
    rope_p = rp_ref[...]
    qg = qg_ref[...]
    kg = kg_ref[...]

    qi = lax.broadcasted_iota(jnp.int32, (WINDOW, 2 * WINDOW), 0)
    ki = lax.broadcasted_iota(jnp.int32, (WINDOW, 2 * WINDOW), 1)
    rel = WINDOW + qi - ki
    valid = (rel >= 0) & (rel < WINDOW) & ((n > 0) | (ki >= WINDOW))
    scale = 1.0 / math.sqrt(HEAD_DIM)

    kfull, vfull = [], []
    for c in range(KV_COLS // LANES):
        cols = slice(c * LANES, (c + 1) * LANES)
        kprev = _head_norm_rope(kp_ref[:, cols], kg, rope_p, lo)
        kcur = _head_norm_rope(kc_ref[:, cols], kg, rope_c, lo)
        kfull.append(jnp.concatenate([kprev, kcur], axis=0).astype(BF16))
        vfull.append(jnp.concatenate([vp_ref[:, cols], vc_ref[:, cols]], axis=0).astype(BF16))

    for c in range(Q_COLS // LANES):
        q2 = _head_norm_rope(q_ref[:, c * LANES:(c + 1) * LANES], qg, rope_c, lo).astype(BF16)
        for hh in range(2):
            j = 2 * c + hh
            h = j // GROUP
            kcols = slice((h % 2) * HEAD_DIM, (h % 2 + 1) * HEAD_DIM)
            qh = q2[:, hh * HEAD_DIM:(hh + 1) * HEAD_DIM]
            kh = kfull[h // 2][:, kcols]
            vh = vfull[h // 2][:, kcols]
            s = lax.dot_general(qh, kh, (((1,), (1,)), ((), ())), preferred_element_type=F32) * scale
            s = jnp.where(valid, s, NEG_INF)
            sink = sinks_ref[j]
            m = jnp.maximum(jnp.max(s, axis=1, keepdims=True), sink)
            p = jnp.exp(s - m)
            denom = jnp.sum(p, axis=1, keepdims=True) + jnp.exp(sink - m)
            o = jnp.dot(p.astype(BF16), vh, preferred_element_type=F32) / denom
            o_ref[:, j * HEAD_DIM:(j + 1) * HEAD_DIM] = o


def _attention(qkv, q_gain, k_gain, sinks, batch, seq):
    t = batch * seq
    nb = seq // WINDOW
    rope = _rope_tables(seq)
    kcol = Q_COLS // KV_COLS
    cur = lambda b, n: (b * nb + n, 0)
    kcur = lambda b, n: (b * nb + n, kcol)
    kprev = lambda b, n: (b * nb + jnp.maximum(n - 1, 0), kcol)
    vcur = lambda b, n: (b * nb + n, kcol + 1)
    vprev = lambda b, n: (b * nb + jnp.maximum(n - 1, 0), kcol + 1)
    return pl.pallas_call(
        _attn_kernel,
        grid=(batch, nb),
        in_specs=[pl.BlockSpec(memory_space=pltpu.SMEM),
                  pl.BlockSpec((WINDOW, Q_COLS), cur),
                  pl.BlockSpec((WINDOW, KV_COLS), kcur),
                  pl.BlockSpec((WINDOW, KV_COLS), kprev),
                  pl.BlockSpec((WINDOW, KV_COLS), vcur),
                  pl.BlockSpec((WINDOW, KV_COLS), vprev),
                  pl.BlockSpec((3, WINDOW, LANES), lambda b, n: (0, n, 0)),
                  pl.BlockSpec((3, WINDOW, LANES), lambda b, n: (0, jnp.maximum(n - 1, 0), 0)),
                  pl.BlockSpec((1, LANES), lambda b, n: (0, 0)),
                  pl.BlockSpec((1, LANES), lambda b, n: (0, 0))],
        out_specs=pl.BlockSpec((WINDOW, Q_COLS), cur),
        out_shape=jax.ShapeDtypeStruct((t, Q_COLS), F32),
        compiler_params=pltpu.CompilerParams(dimension_semantics=("arbitrary", "arbitrary"),
                                             vmem_limit_bytes=VMEM_LIMIT),
        name="swa_attention",
    )(sinks, qkv, qkv, qkv, qkv, qkv, rope, rope,
      jnp.tile(q_gain, 2).reshape(1, LANES), jnp.tile(k_gain, 2).reshape(1, LANES))


def _conv_kernel(x_ref, g_ref, win_ref, cw_ref, wout_ref, o_ref, zprev_ref):
    n = pl.program_id(1)
    d = D_MODEL

    @pl.when(n == 0)
    def _():
        zprev_ref[...] = jnp.zeros_like(zprev_ref)

    x = x_ref[...]
    h = _rms(x, g_ref[...])
    bcu = jnp.dot(h.astype(BF16), win_ref[...], preferred_element_type=F32)
    gate_b = bcu[:, :d]
    z = bcu[:, d:2 * d] * bcu[:, 2 * d:]
    tm = z.shape[0]
    row = lax.broadcasted_iota(jnp.int32, z.shape, 0)
    prev = zprev_ref[...]
    p_last = prev[SUBLANES - 1:SUBLANES, :]
    p_last2 = prev[SUBLANES - 2:SUBLANES - 1, :]
    z1 = jnp.where(row == 0, p_last, pltpu.roll(z, 1, 0))
    z2 = jnp.where(row == 0, p_last2, jnp.where(row == 1, p_last, pltpu.roll(z, 2, 0)))
    cw = cw_ref[...]
    conv = cw[0:1, :] * z2 + cw[1:2, :] * z1 + cw[2:3, :] * z
    zprev_ref[...] = z[tm - SUBLANES:, :]
    o_ref[...] = x + jnp.dot((gate_b * conv).astype(BF16), wout_ref[...], preferred_element_type=F32)


def _conv_mixer(x, gain, w_in, conv_w, w_out, batch, seq, tm=256):
    t, d = x.shape
    nblk = seq // tm
    blk = lambda b, n: (b * nblk + n, 0)
    const = lambda b, n: (0, 0)
    return pl.pallas_call(
        _conv_kernel,
        grid=(batch, nblk),
        in_specs=[pl.BlockSpec((tm, d), blk),
                  pl.BlockSpec((1, d), const),
                  pl.BlockSpec((d, 3 * d), const),
                  pl.BlockSpec((CONV_WIDTH, d), const),
                  pl.BlockSpec((d, d), const)],
        out_specs=pl.BlockSpec((tm, d), blk),
        out_shape=jax.ShapeDtypeStruct((t, d), F32),
        scratch_shapes=[pltpu.VMEM((SUBLANES, d), F32)],
        compiler_params=pltpu.CompilerParams(dimension_semantics=("arbitrary", "arbitrary"),
                                             vmem_limit_bytes=VMEM_LIMIT),
        name="conv_mixer",
    )(x, gain.reshape(1, d), w_in.astype(BF16), conv_w, w_out.astype(BF16))


def _topk_axis0(s, k, payload=None):
    n, tm = s.shape
    row = lax.broadcasted_iota(jnp.int32, (n, tm), 0)
    krow = lax.broadcasted_iota(jnp.int32, (k, tm), 0)
    vals = jnp.zeros((k, tm), F32)
    idxs = jnp.zeros((k, tm), jnp.int32)
    for r in range(k):
        m = jnp.max(s, axis=0, keepdims=True)
        pos = jnp.min(jnp.where(s == m, row, n), axis=0, keepdims=True)
        sel = row == pos
        if payload is None:
            picked = pos
        else:
            picked = jnp.sum(jnp.where(sel, payload, 0), axis=0, keepdims=True)
        vals = jnp.where(krow == r, m, vals)
        idxs = jnp.where(krow == r, picked, idxs)
        s = jnp.where(sel, -jnp.inf, s)
    return vals, idxs


def _route_kernel(x_ref, g_ref, wq_ref, keys_ref, h_ref, idx_ref, gate_ref):
    h = _rms(x_ref[...], g_ref[...])
    h_ref[...] = h
    q = jnp.dot(h.astype(BF16), wq_ref[...], preferred_element_type=F32).astype(BF16)
    idx_rows, gate_rows = [], []
    for head in range(PEER_HEADS):
        tops = []
        for part in range(2):
            col = (head * 2 + part) * QUERY_HALF
            s = lax.dot_general(keys_ref[head, part], q[:, col:col + QUERY_HALF],
                                (((1,), (1,)), ((), ())), preferred_element_type=F32)
            tops.append(_topk_axis0(s, PEER_TOPK))
        (s1, i1), (s2, i2) = tops
        cand = jnp.concatenate([s1[i:i + 1, :] + s2 for i in range(PEER_TOPK)], axis=0)
        cand_idx = jnp.concatenate([i1[i:i + 1, :] * N_KEYS + i2 for i in range(PEER_TOPK)], axis=0)
        g_s, e_idx = _topk_axis0(cand, PEER_TOPK, payload=cand_idx)
        e = jnp.exp(g_s - jnp.max(g_s, axis=0, keepdims=True))
        gate_rows.append(e / jnp.sum(e, axis=0, keepdims=True))
        idx_rows.append(e_idx * ROW_WORDS)
    idx_ref[...] = jnp.concatenate(idx_rows, axis=0).T
    gate_ref[...] = jnp.concatenate(gate_rows, axis=0).T


def _route(x, gain, w_query, sub_keys, tm=128):
    t, d = x.shape
    nq = w_query.shape[1]
    return pl.pallas_call(
        _route_kernel,
        grid=(t // tm,),
        in_specs=[pl.BlockSpec((tm, d), lambda i: (i, 0)),
                  pl.BlockSpec((1, d), lambda i: (0, 0)),
                  pl.BlockSpec((d, nq), lambda i: (0, 0)),
                  pl.BlockSpec((PEER_HEADS, 2, N_KEYS, QUERY_HALF), lambda i: (0, 0, 0, 0))],
        out_specs=[pl.BlockSpec((tm, d), lambda i: (i, 0)),
                   pl.BlockSpec((tm, SLOTS), lambda i: (i, 0)),
                   pl.BlockSpec((tm, SLOTS), lambda i: (i, 0))],
        out_shape=[jax.ShapeDtypeStruct((t, d), F32),
                   jax.ShapeDtypeStruct((t, SLOTS), jnp.int32),
                   jax.ShapeDtypeStruct((t, SLOTS), F32)],
        compiler_params=pltpu.CompilerParams(dimension_semantics=("arbitrary",), vmem_limit_bytes=VMEM_LIMIT),
        name="peer_route",
    )(x, gain.reshape(1, d), w_query.astype(BF16), sub_keys.astype(BF16))


def _pack_table(tab):
    bits = lax.bitcast_convert_type(tab.astype(BF16), jnp.uint16).astype(jnp.uint32)
    half = D_MODEL // 2
    words = (bits[:, :half] << 16) | bits[:, half:]
    return lax.bitcast_convert_type(words, jnp.int32).reshape(tab.shape[0] * ROW_WORDS, LANES)


def _load_table_once(tab_hbm, tab, sem):
    @pl.when(pl.program_id(0) == 0)
    def _():
        cp = pltpu.make_async_copy(tab_hbm, tab, sem)
        cp.start()
        cp.wait()


def _gather_pair(tab, off_a, off_b):
    ra = tab[pl.ds(pl.multiple_of(off_a, ROW_WORDS), ROW_WORDS), :]
    rb = tab[pl.ds(pl.multiple_of(off_b, ROW_WORDS), ROW_WORDS), :]
    words = jnp.concatenate([ra, rb], axis=0)
    hi = pltpu.bitcast(words & jnp.int32(-65536), F32)
    lo = pltpu.bitcast(words << 16, F32)
    return hi, lo


_PAIR_SLOTS = ((3, 7), (5, 1), (4, 0), (6, 2))


def _reduce8(vs, sub):
    ts = [v + pltpu.roll(v, 2, 0) for v in vs]
    upper = (sub & 3) >= 2
    m01 = jnp.where(upper, ts[0], pltpu.roll(ts[1], 2, 0))
    m23 = jnp.where(upper, ts[2], pltpu.roll(ts[3], 2, 0))
    n01 = m01 + pltpu.roll(m01, 1, 0)
    n23 = m23 + pltpu.roll(m23, 1, 0)
    return jnp.where((sub & 1) == 1, n01, pltpu.roll(n23, 1, 0))


def _expert_in_kernel(idx_ref, h_ref, gate_ref, tab_hbm, w_ref, tab, sem, a_ref):
    _load_table_once(tab_hbm, tab, sem)
    sub = lax.broadcasted_iota(jnp.int32, (SUBLANES, LANES), 0)
    tb = h_ref.shape[0]

    def token(t, carry):
        x = h_ref[t]
        xa = jnp.concatenate([x[0:ROW_WORDS], x[0:ROW_WORDS]], axis=0)
        xb = jnp.concatenate([x[ROW_WORDS:], x[ROW_WORDS:]], axis=0)
        groups = []
        for grp in range(SLOTS // SUBLANES):
            vs = []
            for sa, sb in _PAIR_SLOTS:
                hi, lo = _gather_pair(tab, idx_ref[t, grp * SUBLANES + sa], idx_ref[t, grp * SUBLANES + sb])
                vs.append(hi * xa + lo * xb)
            groups.append(_reduce8(vs, sub))
        partial = jnp.concatenate(groups, axis=0)
        a_ref[pl.ds(t, 1), :] = jnp.sum(partial, axis=1)[None, :]
        return carry

    lax.fori_loop(0, tb, token, 0)
    a = a_ref[...]
    w_ref[...] = gate_ref[...] * (0.5 * a * (1.0 + lax.erf(a * (1.0 / math.sqrt(2.0)))))


def _expert_out_kernel(idx_ref, w_ref, x_ref, tab_hbm, o_ref, tab, sem):
    _load_table_once(tab_hbm, tab, sem)
    sub = lax.broadcasted_iota(jnp.int32, (SUBLANES, LANES), 0)
    lower = sub < ROW_WORDS
    tb = x_ref.shape[0]
    nacc = 4

    def token(t, carry):
        acc_h = [jnp.zeros((SUBLANES, LANES), F32) for _ in range(nacc)]
        acc_l = [jnp.zeros((SUBLANES, LANES), F32) for _ in range(nacc)]
        for p in range(SLOTS // 2):
            ka, kb = 2 * p, 2 * p + 1
            hi, lo = _gather_pair(tab, idx_ref[t, ka], idx_ref[t, kb])
            wt = jnp.where(lower, w_ref[t, ka], w_ref[t, kb])
            acc_h[p % nacc] = acc_h[p % nacc] + hi * wt
            acc_l[p % nacc] = acc_l[p % nacc] + lo * wt
        ah = (acc_h[0] + acc_h[1]) + (acc_h[2] + acc_h[3])
        al = (acc_l[0] + acc_l[1]) + (acc_l[2] + acc_l[3])
        ah = ah + pltpu.roll(ah, ROW_WORDS, 0)
        al = al + pltpu.roll(al, ROW_WORDS, 0)
        o_ref[t] = x_ref[t] + jnp.where(lower, ah, al)
        return carry

    lax.fori_loop(0, tb, token, 0)


def _expert_stage(x, h, idx, gate, u_packed, v_packed, tb=128):
    t, d = x.shape
    rows = d // LANES
    smem_blk = pl.BlockSpec((tb, SLOTS), lambda i: (i, 0), memory_space=pltpu.SMEM)
    vmem_blk = pl.BlockSpec((tb, SLOTS), lambda i: (i, 0))
    tok_blk = pl.BlockSpec((tb, rows, LANES), lambda i: (i, 0, 0))
    table_scratch = [pltpu.VMEM(u_packed.shape, jnp.int32), pltpu.SemaphoreType.DMA]
    params = pltpu.CompilerParams(dimension_semantics=("arbitrary",), vmem_limit_bytes=VMEM_LIMIT)
    w = pl.pallas_call(
        _expert_in_kernel,
        grid=(t // tb,),
        in_specs=[smem_blk, tok_blk, vmem_blk, pl.BlockSpec(memory_space=pl.ANY)],
        out_specs=vmem_blk,
        out_shape=jax.ShapeDtypeStruct((t, SLOTS), F32),
        scratch_shapes=table_scratch + [pltpu.VMEM((tb, SLOTS), F32)],
        compiler_params=params,
        name="peer_expert_in",
    )(idx, h.reshape(t, rows, LANES), gate, u_packed)
    out = pl.pallas_call(
        _expert_out_kernel,
        grid=(t // tb,),
        in_specs=[smem_blk, smem_blk, tok_blk, pl.BlockSpec(memory_space=pl.ANY)],
        out_specs=tok_blk,
        out_shape=jax.ShapeDtypeStruct((t, rows, LANES), F32),
        scratch_shapes=table_scratch,
        compiler_params=params,
        name="peer_expert_out",
    )(idx, w, x.reshape(t, rows, LANES), v_packed)
    return out.reshape(t, d)


def _peer(x, gain, w_query, sub_keys, expert_u, expert_v):
    h, idx, gate = _route(x, gain, w_query, sub_keys)
    return _expert_stage(x, h, idx, gate, _pack_table(expert_u), _pack_table(expert_v))


def kernel(x, norm_mix, norm_ffn, attn_w_qkv, attn_q_norm, attn_k_norm, attn_sinks, attn_w_o, conv_w_in, conv_w, conv_w_out, peer_w_query, peer_sub_keys, peer_u, peer_v):
    batch, seq, d = x.shape
    xt = x.reshape(batch * seq, d)
    for i in range(norm_mix.shape[0]):
        j = i // 2
        if i % 2 == 0:
            qkv = _norm_matmul(xt, norm_mix[i], attn_w_qkv[j].astype(BF16))
            o = _attention(qkv, attn_q_norm[j], attn_k_norm[j], attn_sinks[j], batch, seq)
            xt = _matmul_residual(o, attn_w_o[j].astype(BF16), xt)
        else:
            xt = _conv_mixer(xt, norm_mix[i], conv_w_in[j], conv_w[j], conv_w_out[j], batch, seq)
        xt = _peer(xt, norm_ffn[i], peer_w_query[i], peer_sub_keys[i], peer_u[i], peer_v[i])
    return xt.reshape(batch, seq, d)
```

```python
import functools
import math

import jax
import jax.numpy as jnp
from jax import lax
from jax.experimental import pallas as pl
from jax.experimental.pallas import tpu as pltpu

D_MODEL = 1024
RMS_EPS = 1e-6

HEAD_DIM = 64
N_Q_HEADS = 16
N_KV_HEADS = 4
GROUP = N_Q_HEADS // N_KV_HEADS
WINDOW = 128
ROT_DIM = HEAD_DIM // 4
ROPE_THETA = 500000.0
Q_COLS = N_Q_HEADS * HEAD_DIM
KV_COLS = N_KV_HEADS * HEAD_DIM
NEG_INF = -1e30

CONV_WIDTH = 3

PEER_HEADS = 8
N_KEYS = 128
N_EXPERTS = N_KEYS * N_KEYS
PEER_TOPK = 16
QUERY_HALF = 128
SLOTS = PEER_HEADS * PEER_TOPK

LANES = 128
SUBLANES = 8
ROW_WORDS = D_MODEL // 2 // LANES
VMEM_LIMIT = 48 * 1024 * 1024

BF16 = jnp.bfloat16
F32 = jnp.float32


def _rms(x, gain):
    return x * lax.rsqrt(jnp.mean(x * x, axis=-1, keepdims=True) + RMS_EPS) * gain


def _norm_matmul_kernel(x_ref, g_ref, w_ref, o_ref):
    h = _rms(x_ref[...], g_ref[...])
    o_ref[...] = jnp.dot(h.astype(BF16), w_ref[...], preferred_element_type=F32)


def _norm_matmul(x, gain, w, tm=512):
    t, d = x.shape
    n = w.shape[1]
    return pl.pallas_call(
        _norm_matmul_kernel,
        grid=(t // tm,),
        in_specs=[pl.BlockSpec((tm, d), lambda i: (i, 0)),
                  pl.BlockSpec((1, d), lambda i: (0, 0)),
                  pl.BlockSpec((d, n), lambda i: (0, 0))],
        out_specs=pl.BlockSpec((tm, n), lambda i: (i, 0)),
        out_shape=jax.ShapeDtypeStruct((t, n), F32),
        compiler_params=pltpu.CompilerParams(dimension_semantics=("arbitrary",), vmem_limit_bytes=VMEM_LIMIT),
        name="norm_matmul",
    )(x, gain.reshape(1, d), w)


def _matmul_residual_kernel(a_ref, w_ref, r_ref, o_ref):
    o_ref[...] = r_ref[...] + jnp.dot(a_ref[...].astype(BF16), w_ref[...], preferred_element_type=F32)


def _matmul_residual(a, w, res, tm=512):
    t, k = a.shape
    n = w.shape[1]
    return pl.pallas_call(
        _matmul_residual_kernel,
        grid=(t // tm,),
        in_specs=[pl.BlockSpec((tm, k), lambda i: (i, 0)),
                  pl.BlockSpec((k, n), lambda i: (0, 0)),
                  pl.BlockSpec((tm, n), lambda i: (i, 0))],
        out_specs=pl.BlockSpec((tm, n), lambda i: (i, 0)),
        out_shape=jax.ShapeDtypeStruct((t, n), F32),
        compiler_params=pltpu.CompilerParams(dimension_semantics=("arbitrary",), vmem_limit_bytes=VMEM_LIMIT),
        name="matmul_residual",
    )(a, w, res)


def _rope_tables(seq):
    half = ROT_DIM // 2
    freqs = ROPE_THETA ** (-jnp.arange(0, ROT_DIM, 2, dtype=F32) / ROT_DIM)
    ang = jnp.arange(seq, dtype=F32)[:, None] * freqs[None, :]
    cos, sin = jnp.cos(ang), jnp.sin(ang)
    ones = jnp.ones((seq, HEAD_DIM - ROT_DIM), F32)
    zeros = jnp.zeros((seq, HEAD_DIM - ROT_DIM), F32)
    zh = jnp.zeros((seq, half), F32)
    c = jnp.concatenate([cos, cos, ones], axis=1)
    s_next = jnp.concatenate([-sin, zh, zeros], axis=1)
    s_prev = jnp.concatenate([zh, sin, zeros], axis=1)
    return jnp.stack([jnp.tile(c, (1, 2)), jnp.tile(s_next, (1, 2)), jnp.tile(s_prev, (1, 2))])


def _head_norm_rope(x, gain2, rope, lo):
    sq = x * x
    s_lo = jnp.sum(jnp.where(lo, sq, 0.0), axis=1, keepdims=True)
    s_hi = jnp.sum(jnp.where(lo, 0.0, sq), axis=1, keepdims=True)
    ms = jnp.where(lo, s_lo, s_hi) * (1.0 / HEAD_DIM)
    xn = x * lax.rsqrt(ms + RMS_EPS) * gain2
    half = ROT_DIM // 2
    return xn * rope[0] + pltpu.roll(xn, LANES - half, 1) * rope[1] + pltpu.roll(xn, half, 1) * rope[2]


def _attn_kernel(sinks_ref, q_ref, kc_ref, kp_ref, vc_ref, vp_ref, rc_ref, rp_ref, qg_ref, kg_ref, o_ref):
    n = pl.program_id(1)
    lo = lax.broadcasted_iota(jnp.int32, (WINDOW, LANES), 1) < HEAD_DIM
    rope_c = rc_ref[...]
    rope_p = rp_ref[...]
    qg = qg_ref[...]
    kg = kg_ref[...]

    qi = lax.broadcasted_iota(jnp.int32, (WINDOW, 2 * WINDOW), 0)
    ki = lax.broadcasted_iota(jnp.int32, (WINDOW, 2 * WINDOW), 1)
    rel = WINDOW + qi - ki
    valid = (rel >= 0) & (rel < WINDOW) & ((n > 0) | (ki >= WINDOW))
    scale = 1.0 / math.sqrt(HEAD_DIM)

    kfull, vfull = [], []
    for c in range(KV_COLS // LANES):
        cols = slice(c * LANES, (c + 1) * LANES)
        kprev = _head_norm_rope(kp_ref[:, cols], kg, rope_p, lo)
        kcur = _head_norm_rope(kc_ref[:, cols], kg, rope_c, lo)
        kfull.append(jnp.concatenate([kprev, kcur], axis=0).astype(BF16))
        vfull.append(jnp.concatenate([vp_ref[:, cols], vc_ref[:, cols]], axis=0).astype(BF16))

    for c in range(Q_COLS // LANES):
        q2 = _head_norm_rope(q_ref[:, c * LANES:(c + 1) * LANES], qg, rope_c, lo).astype(BF16)
        for hh in range(2):
            j = 2 * c + hh
            h = j // GROUP
            kcols = slice((h % 2) * HEAD_DIM, (h % 2 + 1) * HEAD_DIM)
            qh = q2[:, hh * HEAD_DIM:(hh + 1) * HEAD_DIM]
            kh = kfull[h // 2][:, kcols]
            vh = vfull[h // 2][:, kcols]
            s = lax.dot_general(qh, kh, (((1,), (1,)), ((), ())), preferred_element_type=F32) * scale
            s = jnp.where(valid, s, NEG_INF)
            sink = sinks_ref[j]
            m = jnp.maximum(jnp.max(s, axis=1, keepdims=True), sink)
            p = jnp.exp(s - m)
            denom = jnp.sum(p, axis=1, keepdims=True) + jnp.exp(sink - m)
            o = jnp.dot(p.astype(BF16), vh, preferred_element_type=F32) / denom
            o_ref[:, j * HEAD_DIM:(j + 1) * HEAD_DIM] = o


def _attention(qkv, q_gain, k_gain, sinks, batch, seq):
    t = batch * seq
    nb = seq // WINDOW
    rope = _rope_tables(seq)
    kcol = Q_COLS // KV_COLS
    cur = lambda b, n: (b * nb + n, 0)
    kcur = lambda b, n: (b * nb + n, kcol)
    kprev = lambda b, n: (b * nb + jnp.maximum(n - 1, 0), kcol)
    vcur = lambda b, n: (b * nb + n, kcol + 1)
    vprev = lambda b, n: (b * nb + jnp.maximum(n - 1, 0), kcol + 1)
    return pl.pallas_call(
        _attn_kernel,
        grid=(batch, nb),
        in_specs=[pl.BlockSpec(memory_space=pltpu.SMEM),
                  pl.BlockSpec((WINDOW, Q_COLS), cur),
                  pl.BlockSpec((WINDOW, KV_COLS), kcur),
                  pl.BlockSpec((WINDOW, KV_COLS), kprev),
                  pl.BlockSpec((WINDOW, KV_COLS), vcur),
                  pl.BlockSpec((WINDOW, KV_COLS), vprev),
                  pl.BlockSpec((3, WINDOW, LANES), lambda b, n: (0, n, 0)),
                  pl.BlockSpec((3, WINDOW, LANES), lambda b, n: (0, jnp.maximum(n - 1, 0), 0)),
                  pl.BlockSpec((1, LANES), lambda b, n: (0, 0)),
                  pl.BlockSpec((1, LANES), lambda b, n: (0, 0))],
        out_specs=pl.BlockSpec((WINDOW, Q_COLS), cur),
        out_shape=jax.ShapeDtypeStruct((t, Q_COLS), F32),
        compiler_params=pltpu.CompilerParams(dimension_semantics=("arbitrary", "arbitrary"),
                                             vmem_limit_bytes=VMEM_LIMIT),
        name="swa_attention",
    )(sinks, qkv, qkv, qkv, qkv, qkv, rope, rope,
      jnp.tile(q_gain, 2).reshape(1, LANES), jnp.tile(k_gain, 2).reshape(1, LANES))


def _conv_kernel(x_ref, g_ref, win_ref, cw_ref, wout_ref, o_ref, zprev_ref):
    n = pl.program_id(1)
    d = D_MODEL

    @pl.when(n == 0)
    def _():
        zprev_ref[...] = jnp.zeros_like(zprev_ref)

    x = x_ref[...]
    h = _rms(x, g_ref[...])
    bcu = jnp.dot(h.astype(BF16), win_ref[...], preferred_element_type=F32)
    gate_b = bcu[:, :d]
    z = bcu[:, d:2 * d] * bcu[:, 2 * d:]
    tm = z.shape[0]
    row = lax.broadcasted_iota(jnp.int32, z.shape, 0)
    prev = zprev_ref[...]
    p_last = prev[SUBLANES - 1:SUBLANES, :]
    p_last2 = prev[SUBLANES - 2:SUBLANES - 1, :]
    z1 = jnp.where(row == 0, p_last, pltpu.roll(z, 1, 0))
    z2 = jnp.where(row == 0, p_last2, jnp.where(row == 1, p_last, pltpu.roll(z, 2, 0)))
    cw = cw_ref[...]
    conv = cw[0:1, :] * z2 + cw[1:2, :] * z1 + cw[2:3, :] * z
    zprev_ref[...] = z[tm - SUBLANES:, :]
    o_ref[...] = x + jnp.dot((gate_b * conv).astype(BF16), wout_ref[...], preferred_element_type=F32)


def _conv_mixer(x, gain, w_in, conv_w, w_out, batch, seq, tm=256):
    t, d = x.shape
    nblk = seq // tm
    blk = lambda b, n: (b * nblk + n, 0)
    const = lambda b, n: (0, 0)
    return pl.pallas_call(
        _conv_kernel,
        grid=(batch, nblk),
        in_specs=[pl.BlockSpec((tm, d), blk),
                  pl.BlockSpec((1, d), const),
                  pl.BlockSpec((d, 3 * d), const),
                  pl.BlockSpec((CONV_WIDTH, d), const),
                  pl.BlockSpec((d, d), const)],
        out_specs=pl.BlockSpec((tm, d), blk),
        out_shape=jax.ShapeDtypeStruct((t, d), F32),
        scratch_shapes=[pltpu.VMEM((SUBLANES, d), F32)],
        compiler_params=pltpu.CompilerParams(dimension_semantics=("arbitrary", "arbitrary"),
                                             vmem_limit_bytes=VMEM_LIMIT),
        name="conv_mixer",
    )(x, gain.reshape(1, d), w_in.astype(BF16), conv_w, w_out.astype(BF16))


def _topk_axis0(s, k, ids=None, payload=None):
    n, tm = s.shape
    if ids is None:
        ids = lax.broadcasted_iota(jnp.int32, (n, tm), 0)
    krow = lax.broadcasted_iota(jnp.int32, (k, tm), 0)
    vals = jnp.zeros((k, tm), F32)
    picks = jnp.zeros((k, tm), jnp.int32)
    for r in range(k):
        m = jnp.max(s, axis=0, keepdims=True)
        pos = jnp.min(jnp.where(s == m, ids, jnp.iinfo(jnp.int32).max), axis=0, keepdims=True)
        sel = ids == pos
        if payload is None:
            picked = pos
        else:
            picked = jnp.sum(jnp.where(sel, payload, 0), axis=0, keepdims=True)
        vals = jnp.where(krow == r, m, vals)
        picks = jnp.where(krow == r, picked, picks)
        s = jnp.where(sel, -jnp.inf, s)
    return vals, picks


def _pair_candidates(s1, i1, s2, i2):
    k, tm = s1.shape
    sub = lax.broadcasted_iota(jnp.int32, (SUBLANES, tm), 0)
    scores, flat, expert = [], [], []
    for i in range(k // 2):
        width = k if i == 0 else SUBLANES
        sc = s1[i:i + 1, :] + s2[0:width, :]
        ex = i1[i:i + 1, :] * N_KEYS + i2[0:width, :]
        fl = i * k + lax.broadcasted_iota(jnp.int32, (width, tm), 0)
        reach = k // (i + 1)
        if reach < width:
            sc = jnp.where(sub < reach, sc, -jnp.inf)
        scores.append(sc)
        flat.append(fl)
        expert.append(ex)
    scores.append(s1[k // 2:, :] + s2[0:1, :])
    expert.append(i1[k // 2:, :] * N_KEYS + i2[0:1, :])
    flat.append((k // 2 + sub) * k)
    return jnp.concatenate(scores, axis=0), jnp.concatenate(flat, axis=0), jnp.concatenate(expert, axis=0)


def _route_kernel(x_ref, g_ref, wq_ref, keys_ref, h_ref, idx_ref, gate_ref):
    h = _rms(x_ref[...], g_ref[...])
    h_ref[...] = h
    q = jnp.dot(h.astype(BF16), wq_ref[...], preferred_element_type=F32).astype(BF16)
    idx_rows, gate_rows = [], []
    for head in range(PEER_HEADS):
        tops = []
        for part in range(2):
            col = (head * 2 + part) * QUERY_HALF
            s = lax.dot_general(keys_ref[head, part], q[:, col:col + QUERY_HALF],
                                (((1,), (1,)), ((), ())), preferred_element_type=F32)
            tops.append(_topk_axis0(s, PEER_TOPK))
        (s1, i1), (s2, i2) = tops
        cand, flat_ids, cand_idx = _pair_candidates(s1, i1, s2, i2)
        g_s, e_idx = _topk_axis0(cand, PEER_TOPK, ids=flat_ids, payload=cand_idx)
        e = jnp.exp(g_s - jnp.max(g_s, axis=0, keepdims=True))
        gate_rows.append(e / jnp.sum(e, axis=0, keepdims=True))
        idx_rows.append(e_idx * ROW_WORDS)
    idx_ref[...] = jnp.concatenate(idx_rows, axis=0).T
    gate_ref[...] = jnp.concatenate(gate_rows, axis=0).T


def _route(x, gain, w_query, sub_keys, tm=128):
    t, d = x.shape
    nq = w_query.shape[1]
    return pl.pallas_call(
        _route_kernel,
        grid=(t // tm,),
        in_specs=[pl.BlockSpec((tm, d), lambda i: (i, 0)),
                  pl.BlockSpec((1, d), lambda i: (0, 0)),
                  pl.BlockSpec((d, nq), lambda i: (0, 0)),
                  pl.BlockSpec((PEER_HEADS, 2, N_KEYS, QUERY_HALF), lambda i: (0, 0, 0, 0))],
        out_specs=[pl.BlockSpec((tm, d), lambda i: (i, 0)),
                   pl.BlockSpec((tm, SLOTS), lambda i: (i, 0)),
                   pl.BlockSpec((tm, SLOTS), lambda i: (i, 0))],
        out_shape=[jax.ShapeDtypeStruct((t, d), F32),
                   jax.ShapeDtypeStruct((t, SLOTS), jnp.int32),
                   jax.ShapeDtypeStruct((t, SLOTS), F32)],
        compiler_params=pltpu.CompilerParams(dimension_semantics=("arbitrary",), vmem_limit_bytes=VMEM_LIMIT),
        name="peer_route",
    )(x, gain.reshape(1, d), w_query.astype(BF16), sub_keys.astype(BF16))


def _pack_table(tab):
    bits = lax.bitcast_convert_type(tab.astype(BF16), jnp.uint16).astype(jnp.uint32)
    half = D_MODEL // 2
    words = (bits[:, :half] << 16) | bits[:, half:]
    return lax.bitcast_convert_type(words, jnp.int32).reshape(tab.shape[0] * ROW_WORDS, LANES)


def _load_table_once(tab_hbm, tab, sem):
    @pl.when(pl.program_id(0) == 0)
    def _():
        cp = pltpu.make_async_copy(tab_hbm, tab, sem)
        cp.start()
        cp.wait()


def _gather_pair(tab, off_a, off_b):
    ra = tab[pl.ds(pl.multiple_of(off_a, ROW_WORDS), ROW_WORDS), :]
    rb = tab[pl.ds(pl.multiple_of(off_b, ROW_WORDS), ROW_WORDS), :]
    words = jnp.concatenate([ra, rb], axis=0)
    hi = pltpu.bitcast(words & jnp.int32(-65536), F32)
    lo = pltpu.bitcast(words << 16, F32)
    return hi, lo


TOKEN_UNROLL = 8


def _expert_in_kernel(*refs):
    idx_refs = refs[:TOKEN_UNROLL]
    h_ref, gate_ref, tab_hbm, w_ref, tab, sem, a_ref = refs[TOKEN_UNROLL:]
    _load_table_once(tab_hbm, tab, sem)
    sub = lax.broadcasted_iota(jnp.int32, (SUBLANES, LANES), 0)
    lane = lax.broadcasted_iota(jnp.int32, (SUBLANES, LANES), 1)
    own_half = (sub >= ROW_WORDS) == ((lane & 1) == 1)
    tb = h_ref.shape[0]

    def step(i, carry):
        xs = []
        for u in range(TOKEN_UNROLL):
            x = h_ref[i * TOKEN_UNROLL + u]
            xs.append((jnp.concatenate([x[0:ROW_WORDS], x[0:ROW_WORDS]], axis=0),
                       jnp.concatenate([x[ROW_WORDS:], x[ROW_WORDS:]], axis=0)))
        accs = [jnp.zeros((SUBLANES, LANES), F32) for _ in range(TOKEN_UNROLL)]
        for p in range(SLOTS // 2):
            for u in range(TOKEN_UNROLL):
                hi, lo = _gather_pair(tab, idx_refs[u][i, 2 * p], idx_refs[u][i, 2 * p + 1])
                part = jnp.sum(hi * xs[u][0] + lo * xs[u][1], axis=1, keepdims=True)
                accs[u] = jnp.where((lane >> 1) == p, part, accs[u])
        for u in range(TOKEN_UNROLL):
            a_ref[pl.ds(i * TOKEN_UNROLL + u, 1), :] = jnp.sum(jnp.where(own_half, accs[u], 0.0),
                                                                axis=0, keepdims=True)
        return carry

    lax.fori_loop(0, tb // TOKEN_UNROLL, step, 0)
    a = a_ref[...]
    w_ref[...] = gate_ref[...] * (0.5 * a * (1.0 + lax.erf(a * (1.0 / math.sqrt(2.0)))))


def _expert_out_kernel(*refs):
    idx_refs = refs[:TOKEN_UNROLL]
    w_ref, x_ref, tab_hbm, o_ref, tab, sem, wb_ref = refs[TOKEN_UNROLL:]
    _load_table_once(tab_hbm, tab, sem)
    sub = lax.broadcasted_iota(jnp.int32, (SUBLANES, LANES), 0)
    lower = sub < ROW_WORDS
    tb = x_ref.shape[0]

    def step(i, carry):
        for u in range(TOKEN_UNROLL):
            row = w_ref[pl.ds(i * TOKEN_UNROLL + u, 1), :]
            wb_ref[u] = jnp.broadcast_to(row, (SLOTS, LANES)).T
        acc_h = [jnp.zeros((SUBLANES, LANES), F32) for _ in range(TOKEN_UNROLL)]
        acc_l = [jnp.zeros((SUBLANES, LANES), F32) for _ in range(TOKEN_UNROLL)]
        for p in range(SLOTS // 2):
            ka, kb = 2 * p, 2 * p + 1
            for u in range(TOKEN_UNROLL):
                hi, lo = _gather_pair(tab, idx_refs[u][i, ka], idx_refs[u][i, kb])
                wa = jnp.broadcast_to(wb_ref[u, ka:ka + 1, :], (SUBLANES, LANES))
                wb = jnp.broadcast_to(wb_ref[u, kb:kb + 1, :], (SUBLANES, LANES))
                wt = jnp.where(lower, wa, wb)
                acc_h[u] = acc_h[u] + hi * wt
                acc_l[u] = acc_l[u] + lo * wt
        for u in range(TOKEN_UNROLL):
            t = i * TOKEN_UNROLL + u
            ah = acc_h[u] + pltpu.roll(acc_h[u], ROW_WORDS, 0)
            al = acc_l[u] + pltpu.roll(acc_l[u], ROW_WORDS, 0)
            o_ref[t] = x_ref[t] + jnp.where(lower, ah, al)
        return carry

    lax.fori_loop(0, tb // TOKEN_UNROLL, step, 0)


def _expert_stage(x, h, idx, gate, u_packed, v_packed, tb=128):
    t, d = x.shape
    rows = d // LANES
    un = TOKEN_UNROLL
    idx_split = idx.reshape(t // un, un, SLOTS).transpose(1, 0, 2)
    smem_blks = [pl.BlockSpec((None, tb // un, SLOTS), functools.partial(lambda u, i: (u, i, 0), u),
                              memory_space=pltpu.SMEM) for u in range(un)]
    vmem_blk = pl.BlockSpec((tb, SLOTS), lambda i: (i, 0))
    tok_blk = pl.BlockSpec((tb, rows, LANES), lambda i: (i, 0, 0))
    table_scratch = [pltpu.VMEM(u_packed.shape, jnp.int32), pltpu.SemaphoreType.DMA]
    params = pltpu.CompilerParams(dimension_semantics=("arbitrary",), vmem_limit_bytes=VMEM_LIMIT)
    w = pl.pallas_call(
        _expert_in_kernel,
        grid=(t // tb,),
        in_specs=smem_blks + [tok_blk, vmem_blk, pl.BlockSpec(memory_space=pl.ANY)],
        out_specs=vmem_blk,
        out_shape=jax.ShapeDtypeStruct((t, SLOTS), F32),
        scratch_shapes=table_scratch + [pltpu.VMEM((tb, SLOTS), F32)],
        compiler_params=params,
        name="peer_expert_in",
    )(*([idx_split] * un), h.reshape(t, rows, LANES), gate, u_packed)
    out = pl.pallas_call(
        _expert_out_kernel,
        grid=(t // tb,),
        in_specs=smem_blks + [vmem_blk, tok_blk, pl.BlockSpec(memory_space=pl.ANY)],
        out_specs=tok_blk,
        out_shape=jax.ShapeDtypeStruct((t, rows, LANES), F32),
        scratch_shapes=table_scratch + [pltpu.VMEM((un, SLOTS, LANES), F32)],
        compiler_params=params,
        name="peer_expert_out",
    )(*([idx_split] * un), w, x.reshape(t, rows, LANES), v_packed)
    return out.reshape(t, d)


def _peer(x, gain, w_query, sub_keys, expert_u, expert_v):
    h, idx, gate = _route(x, gain, w_query, sub_keys)
    return _expert_stage(x, h, idx, gate, _pack_table(expert_u), _pack_table(expert_v))


def kernel(x, norm_mix, norm_ffn, attn_w_qkv, attn_q_norm, attn_k_norm, attn_sinks, attn_w_o, conv_w_in, conv_w, conv_w_out, peer_w_query, peer_sub_keys, peer_u, peer_v):
    batch, seq, d = x.shape
    xt = x.reshape(batch * seq, d)
    for i in range(norm_mix.shape[0]):
        j = i // 2
        if i % 2 == 0:
            qkv = _norm_matmul(xt, norm_mix[i], attn_w_qkv[j].astype(BF16))
            o = _attention(qkv, attn_q_norm[j], attn_k_norm[j], attn_sinks[j], batch, seq)
            xt = _matmul_residual(o, attn_w_o[j].astype(BF16), xt)
        else:
            xt = _conv_mixer(xt, norm_mix[i], conv_w_in[j], conv_w[j], conv_w_out[j], batch, seq)
        xt = _peer(xt, norm_ffn[i], peer_w_query[i], peer_sub_keys[i], peer_u[i], peer_v[i])
    return xt.reshape(batch, seq, d)
```

```python
import functools
import math

import jax
import jax.numpy as jnp
from jax import lax
from jax.experimental import pallas as pl
from jax.experimental.pallas import tpu as pltpu

D_MODEL = 1024
RMS_EPS = 1e-6

HEAD_DIM = 64
N_Q_HEADS = 16
N_KV_HEADS = 4
GROUP = N_Q_HEADS // N_KV_HEADS
WINDOW = 128
ROT_DIM = HEAD_DIM // 4
ROPE_THETA = 500000.0
Q_COLS = N_Q_HEADS * HEAD_DIM
KV_COLS = N_KV_HEADS * HEAD_DIM
NEG_INF = -1e30

CONV_WIDTH = 3

PEER_HEADS = 8
N_KEYS = 128
N_EXPERTS = N_KEYS * N_KEYS
PEER_TOPK = 16
QUERY_HALF = 128
SLOTS = PEER_HEADS * PEER_TOPK

LANES = 128
SUBLANES = 8
ROW_WORDS = D_MODEL // 2 // LANES
VMEM_LIMIT = 48 * 1024 * 1024

BF16 = jnp.bfloat16
F32 = jnp.float32


def _rms(x, gain):
    return x * lax.rsqrt(jnp.mean(x * x, axis=-1, keepdims=True) + RMS_EPS) * gain


def _norm_matmul_kernel(x_ref, g_ref, w_ref, o_ref):
    h = _rms(x_ref[...], g_ref[...])
    o_ref[...] = jnp.dot(h.astype(BF16), w_ref[...], preferred_element_type=F32)


def _norm_matmul(x, gain, w, tm=512):
    t, d = x.shape
    n = w.shape[1]
    return pl.pallas_call(
        _norm_matmul_kernel,
        grid=(t // tm,),
        in_specs=[pl.BlockSpec((tm, d), lambda i: (i, 0)),
                  pl.BlockSpec((1, d), lambda i: (0, 0)),
                  pl.BlockSpec((d, n), lambda i: (0, 0))],
        out_specs=pl.BlockSpec((tm, n), lambda i: (i, 0)),
        out_shape=jax.ShapeDtypeStruct((t, n), F32),
        compiler_params=pltpu.CompilerParams(dimension_semantics=("arbitrary",), vmem_limit_bytes=VMEM_LIMIT),
        name="norm_matmul",
    )(x, gain.reshape(1, d), w)


def _matmul_residual_kernel(a_ref, w_ref, r_ref, o_ref):
    o_ref[...] = r_ref[...] + jnp.dot(a_ref[...].astype(BF16), w_ref[...], preferred_element_type=F32)


def _matmul_residual(a, w, res, tm=512):
    t, k = a.shape
    n = w.shape[1]
    return pl.pallas_call(
        _matmul_residual_kernel,
        grid=(t // tm,),
        in_specs=[pl.BlockSpec((tm, k), lambda i: (i, 0)),
                  pl.BlockSpec((k, n), lambda i: (0, 0)),
                  pl.BlockSpec((tm, n), lambda i: (i, 0))],
        out_specs=pl.BlockSpec((tm, n), lambda i: (i, 0)),
        out_shape=jax.ShapeDtypeStruct((t, n), F32),
        compiler_params=pltpu.CompilerParams(dimension_semantics=("arbitrary",), vmem_limit_bytes=VMEM_LIMIT),
        name="matmul_residual",
    )(a, w, res)


def _rope_tables(seq):
    half = ROT_DIM // 2
    freqs = ROPE_THETA ** (-jnp.arange(0, ROT_DIM, 2, dtype=F32) / ROT_DIM)
    ang = jnp.arange(seq, dtype=F32)[:, None] * freqs[None, :]
    cos, sin = jnp.cos(ang), jnp.sin(ang)
    ones = jnp.ones((seq, HEAD_DIM - ROT_DIM), F32)
    zeros = jnp.zeros((seq, HEAD_DIM - ROT_DIM), F32)
    zh = jnp.zeros((seq, half), F32)
    c = jnp.concatenate([cos, cos, ones], axis=1)
    s_next = jnp.concatenate([-sin, zh, zeros], axis=1)
    s_prev = jnp.concatenate([zh, sin, zeros], axis=1)
    return jnp.stack([jnp.tile(c, (1, 2)), jnp.tile(s_next, (1, 2)), jnp.tile(s_prev, (1, 2))])


def _head_norm_rope(x, gain2, rope, lo):
    sq = x * x
    s_lo = jnp.sum(jnp.where(lo, sq, 0.0), axis=1, keepdims=True)
    s_hi = jnp.sum(jnp.where(lo, 0.0, sq), axis=1, keepdims=True)
    ms = jnp.where(lo, s_lo, s_hi) * (1.0 / HEAD_DIM)
    xn = x * lax.rsqrt(ms + RMS_EPS) * gain2
    half = ROT_DIM // 2
    return xn * rope[0] + pltpu.roll(xn, LANES - half, 1) * rope[1] + pltpu.roll(xn, half, 1) * rope[2]


def _attn_kernel(sinks_ref, q_ref, kc_ref, kp_ref, vc_ref, vp_ref, rc_ref, rp_ref, qg_ref, kg_ref, o_ref):
    n = pl.program_id(1)
    lo = lax.broadcasted_iota(jnp.int32, (WINDOW, LANES), 1) < HEAD_DIM
    rope_c = rc_ref[...]
    rope_p = rp_ref[...]
    qg = qg_ref[...]
    kg = kg_ref[...]

    rows = GROUP * WINDOW
    qi = lax.broadcasted_iota(jnp.int32, (rows, 2 * WINDOW), 0) & (WINDOW - 1)
    ki = lax.broadcasted_iota(jnp.int32, (rows, 2 * WINDOW), 1)
    rel = WINDOW + qi - ki
    valid = (rel >= 0) & (rel < WINDOW) & ((n > 0) | (ki >= WINDOW))
    head_of_row = lax.broadcasted_iota(jnp.int32, (rows, 1), 0) // WINDOW
    scale = 1.0 / math.sqrt(HEAD_DIM)

    q2 = [_head_norm_rope(q_ref[:, c * LANES:(c + 1) * LANES], qg, rope_c, lo).astype(BF16)
          for c in range(Q_COLS // LANES)]
    for c in range(KV_COLS // LANES):
        cols = slice(c * LANES, (c + 1) * LANES)
        kprev = _head_norm_rope(kp_ref[:, cols], kg, rope_p, lo)
        kcur = _head_norm_rope(kc_ref[:, cols], kg, rope_c, lo)
        kfull = jnp.concatenate([kprev, kcur], axis=0).astype(BF16)
        vfull = jnp.concatenate([vp_ref[:, cols], vc_ref[:, cols]], axis=0).astype(BF16)
        for hh in range(LANES // HEAD_DIM):
            h = (LANES // HEAD_DIM) * c + hh
            kh = kfull[:, hh * HEAD_DIM:(hh + 1) * HEAD_DIM]
            vh = vfull[:, hh * HEAD_DIM:(hh + 1) * HEAD_DIM]
            heads = [GROUP * h + g for g in range(GROUP)]
            q4 = jnp.concatenate([q2[j // 2][:, (j % 2) * HEAD_DIM:(j % 2 + 1) * HEAD_DIM] for j in heads], axis=0)
            s = lax.dot_general(q4, kh, (((1,), (1,)), ((), ())), preferred_element_type=F32) * scale
            s = jnp.where(valid, s, NEG_INF)
            sink = jnp.zeros((rows, 1), F32)
            for g, j in enumerate(heads):
                sink = jnp.where(head_of_row == g, sinks_ref[j], sink)
            m = jnp.maximum(jnp.max(s, axis=1, keepdims=True), sink)
            p = jnp.exp(s - m)
            denom = jnp.sum(p, axis=1, keepdims=True) + jnp.exp(sink - m)
            o = jnp.dot(p.astype(BF16), vh, preferred_element_type=F32) / denom
            for g, j in enumerate(heads):
                o_ref[:, j * HEAD_DIM:(j + 1) * HEAD_DIM] = o[g * WINDOW:(g + 1) * WINDOW]


def _attention(qkv, q_gain, k_gain, sinks, batch, seq):
    t = batch * seq
    nb = seq // WINDOW
    rope = _rope_tables(seq)
    kcol = Q_COLS // KV_COLS
    cur = lambda b, n: (b * nb + n, 0)
    kcur = lambda b, n: (b * nb + n, kcol)
    kprev = lambda b, n: (b * nb + jnp.maximum(n - 1, 0), kcol)
    vcur = lambda b, n: (b * nb + n, kcol + 1)
    vprev = lambda b, n: (b * nb + jnp.maximum(n - 1, 0), kcol + 1)
    return pl.pallas_call(
        _attn_kernel,
        grid=(batch, nb),
        in_specs=[pl.BlockSpec(memory_space=pltpu.SMEM),
                  pl.BlockSpec((WINDOW, Q_COLS), cur),
                  pl.BlockSpec((WINDOW, KV_COLS), kcur),
                  pl.BlockSpec((WINDOW, KV_COLS), kprev),
                  pl.BlockSpec((WINDOW, KV_COLS), vcur),
                  pl.BlockSpec((WINDOW, KV_COLS), vprev),
                  pl.BlockSpec((3, WINDOW, LANES), lambda b, n: (0, n, 0)),
                  pl.BlockSpec((3, WINDOW, LANES), lambda b, n: (0, jnp.maximum(n - 1, 0), 0)),
                  pl.BlockSpec((1, LANES), lambda b, n: (0, 0)),
                  pl.BlockSpec((1, LANES), lambda b, n: (0, 0))],
        out_specs=pl.BlockSpec((WINDOW, Q_COLS), cur),
        out_shape=jax.ShapeDtypeStruct((t, Q_COLS), F32),
        compiler_params=pltpu.CompilerParams(dimension_semantics=("arbitrary", "arbitrary"),
                                             vmem_limit_bytes=VMEM_LIMIT),
        name="swa_attention",
    )(sinks, qkv, qkv, qkv, qkv, qkv, rope, rope,
      jnp.tile(q_gain, 2).reshape(1, LANES), jnp.tile(k_gain, 2).reshape(1, LANES))


def _conv_kernel(x_ref, g_ref, win_ref, cw_ref, wout_ref, o_ref, zprev_ref):
    n = pl.program_id(1)
    d = D_MODEL

    @pl.when(n == 0)
    def _():
        zprev_ref[...] = jnp.zeros_like(zprev_ref)

    x = x_ref[...]
    h = _rms(x, g_ref[...])
    bcu = jnp.dot(h.astype(BF16), win_ref[...], preferred_element_type=F32)
    gate_b = bcu[:, :d]
    z = bcu[:, d:2 * d] * bcu[:, 2 * d:]
    tm = z.shape[0]
    row = lax.broadcasted_iota(jnp.int32, z.shape, 0)
    prev = zprev_ref[...]
    p_last = prev[SUBLANES - 1:SUBLANES, :]
    p_last2 = prev[SUBLANES - 2:SUBLANES - 1, :]
    z1 = jnp.where(row == 0, p_last, pltpu.roll(z, 1, 0))
    z2 = jnp.where(row == 0, p_last2, jnp.where(row == 1, p_last, pltpu.roll(z, 2, 0)))
    cw = cw_ref[...]
    conv = cw[0:1, :] * z2 + cw[1:2, :] * z1 + cw[2:3, :] * z
    zprev_ref[...] = z[tm - SUBLANES:, :]
    o_ref[...] = x + jnp.dot((gate_b * conv).astype(BF16), wout_ref[...], preferred_element_type=F32)


def _conv_mixer(x, gain, w_in, conv_w, w_out, batch, seq, tm=256):
    t, d = x.shape
    nblk = seq // tm
    blk = lambda b, n: (b * nblk + n, 0)
    const = lambda b, n: (0, 0)
    return pl.pallas_call(
        _conv_kernel,
        grid=(batch, nblk),
        in_specs=[pl.BlockSpec((tm, d), blk),
                  pl.BlockSpec((1, d), const),
                  pl.BlockSpec((d, 3 * d), const),
                  pl.BlockSpec((CONV_WIDTH, d), const),
                  pl.BlockSpec((d, d), const)],
        out_specs=pl.BlockSpec((tm, d), blk),
        out_shape=jax.ShapeDtypeStruct((t, d), F32),
        scratch_shapes=[pltpu.VMEM((SUBLANES, d), F32)],
        compiler_params=pltpu.CompilerParams(dimension_semantics=("arbitrary", "arbitrary"),
                                             vmem_limit_bytes=VMEM_LIMIT),
        name="conv_mixer",
    )(x, gain.reshape(1, d), w_in.astype(BF16), conv_w, w_out.astype(BF16))


def _topk_axis0(s, k, ids=None, payload=None):
    n, tm = s.shape
    if ids is None:
        ids = lax.broadcasted_iota(jnp.int32, (n, tm), 0)
    krow = lax.broadcasted_iota(jnp.int32, (k, tm), 0)
    vals = jnp.zeros((k, tm), F32)
    picks = jnp.zeros((k, tm), jnp.int32)
    for r in range(k):
        m = jnp.max(s, axis=0, keepdims=True)
        pos = jnp.min(jnp.where(s == m, ids, jnp.iinfo(jnp.int32).max), axis=0, keepdims=True)
        sel = ids == pos
        if payload is None:
            picked = pos
        else:
            picked = jnp.sum(jnp.where(sel, payload, 0), axis=0, keepdims=True)
        vals = jnp.where(krow == r, m, vals)
        picks = jnp.where(krow == r, picked, picks)
        s = jnp.where(sel, -jnp.inf, s)
    return vals, picks


def _pair_candidates(s1, i1, s2, i2):
    k, tm = s1.shape
    sub = lax.broadcasted_iota(jnp.int32, (SUBLANES, tm), 0)
    scores, flat, expert = [], [], []
    for i in range(k // 2):
        width = k if i == 0 else SUBLANES
        sc = s1[i:i + 1, :] + s2[0:width, :]
        ex = i1[i:i + 1, :] * N_KEYS + i2[0:width, :]
        fl = i * k + lax.broadcasted_iota(jnp.int32, (width, tm), 0)
        reach = k // (i + 1)
        if reach < width:
            sc = jnp.where(sub < reach, sc, -jnp.inf)
        scores.append(sc)
        flat.append(fl)
        expert.append(ex)
    scores.append(s1[k // 2:, :] + s2[0:1, :])
    expert.append(i1[k // 2:, :] * N_KEYS + i2[0:1, :])
    flat.append((k // 2 + sub) * k)
    return jnp.concatenate(scores, axis=0), jnp.concatenate(flat, axis=0), jnp.concatenate(expert, axis=0)


def _route_kernel(x_ref, g_ref, wq_ref, keys_ref, h_ref, idx_ref, gate_ref):
    h = _rms(x_ref[...], g_ref[...])
    h_ref[...] = h
    q = jnp.dot(h.astype(BF16), wq_ref[...], preferred_element_type=F32).astype(BF16)
    idx_rows, gate_rows = [], []
    for head in range(PEER_HEADS):
        tops = []
        for part in range(2):
            col = (head * 2 + part) * QUERY_HALF
            s = lax.dot_general(keys_ref[head, part], q[:, col:col + QUERY_HALF],
                                (((1,), (1,)), ((), ())), preferred_element_type=F32)
            tops.append(_topk_axis0(s, PEER_TOPK))
        (s1, i1), (s2, i2) = tops
        cand, flat_ids, cand_idx = _pair_candidates(s1, i1, s2, i2)
        g_s, e_idx = _topk_axis0(cand, PEER_TOPK, ids=flat_ids, payload=cand_idx)
        e = jnp.exp(g_s - jnp.max(g_s, axis=0, keepdims=True))
        gate_rows.append(e / jnp.sum(e, axis=0, keepdims=True))
        idx_rows.append(e_idx * ROW_WORDS)
    idx_ref[...] = jnp.concatenate(idx_rows, axis=0).T
    gate_ref[...] = jnp.concatenate(gate_rows, axis=0).T


def _route(x, gain, w_query, sub_keys, tm=128):
    t, d = x.shape
    nq = w_query.shape[1]
    return pl.pallas_call(
        _route_kernel,
        grid=(t // tm,),
        in_specs=[pl.BlockSpec((tm, d), lambda i: (i, 0)),
                  pl.BlockSpec((1, d), lambda i: (0, 0)),
                  pl.BlockSpec((d, nq), lambda i: (0, 0)),
                  pl.BlockSpec((PEER_HEADS, 2, N_KEYS, QUERY_HALF), lambda i: (0, 0, 0, 0))],
        out_specs=[pl.BlockSpec((tm, d), lambda i: (i, 0)),
                   pl.BlockSpec((tm, SLOTS), lambda i: (i, 0)),
                   pl.BlockSpec((tm, SLOTS), lambda i: (i, 0))],
        out_shape=[jax.ShapeDtypeStruct((t, d), F32),
                   jax.ShapeDtypeStruct((t, SLOTS), jnp.int32),
                   jax.ShapeDtypeStruct((t, SLOTS), F32)],
        compiler_params=pltpu.CompilerParams(dimension_semantics=("arbitrary",), vmem_limit_bytes=VMEM_LIMIT),
        name="peer_route",
    )(x, gain.reshape(1, d), w_query.astype(BF16), sub_keys.astype(BF16))


def _pack_table(tab):
    bits = lax.bitcast_convert_type(tab.astype(BF16), jnp.uint16).astype(jnp.uint32)
    half = D_MODEL // 2
    words = (bits[:, :half] << 16) | bits[:, half:]
    return lax.bitcast_convert_type(words, jnp.int32).reshape(tab.shape[0] * ROW_WORDS, LANES)


def _load_table_once(tab_hbm, tab, sem):
    @pl.when(pl.program_id(0) == 0)
    def _():
        cp = pltpu.make_async_copy(tab_hbm, tab, sem)
        cp.start()
        cp.wait()


def _gather_pair(tab, off_a, off_b):
    ra = tab[pl.ds(pl.multiple_of(off_a, ROW_WORDS), ROW_WORDS), :]
    rb = tab[pl.ds(pl.multiple_of(off_b, ROW_WORDS), ROW_WORDS), :]
    words = jnp.concatenate([ra, rb], axis=0)
    hi = pltpu.bitcast(words & jnp.int32(-65536), F32)
    lo = pltpu.bitcast(words << 16, F32)
    return hi, lo


TOKEN_UNROLL = 8


def _expert_in_kernel(*refs):
    idx_refs = refs[:TOKEN_UNROLL]
    h_ref, gate_ref, tab_hbm, w_ref, tab, sem, a_ref = refs[TOKEN_UNROLL:]
    _load_table_once(tab_hbm, tab, sem)
    sub = lax.broadcasted_iota(jnp.int32, (SUBLANES, LANES), 0)
    lane = lax.broadcasted_iota(jnp.int32, (SUBLANES, LANES), 1)
    own_half = (sub >= ROW_WORDS) == ((lane & 1) == 1)
    tb = h_ref.shape[0]

    def step(i, carry):
        xs = []
        for u in range(TOKEN_UNROLL):
            x = h_ref[i * TOKEN_UNROLL + u]
            xs.append((jnp.concatenate([x[0:ROW_WORDS], x[0:ROW_WORDS]], axis=0),
                       jnp.concatenate([x[ROW_WORDS:], x[ROW_WORDS:]], axis=0)))
        accs = [jnp.zeros((SUBLANES, LANES), F32) for _ in range(TOKEN_UNROLL)]
        for p in range(SLOTS // 2):
            for u in range(TOKEN_UNROLL):
                hi, lo = _gather_pair(tab, idx_refs[u][i, 2 * p], idx_refs[u][i, 2 * p + 1])
                part = jnp.sum(hi * xs[u][0] + lo * xs[u][1], axis=1, keepdims=True)
                accs[u] = jnp.where((lane >> 1) == p, part, accs[u])
        for u in range(TOKEN_UNROLL):
            a_ref[pl.ds(i * TOKEN_UNROLL + u, 1), :] = jnp.sum(jnp.where(own_half, accs[u], 0.0),
                                                                axis=0, keepdims=True)
        return carry

    lax.fori_loop(0, tb // TOKEN_UNROLL, step, 0)
    a = a_ref[...]
    w_ref[...] = gate_ref[...] * (0.5 * a * (1.0 + lax.erf(a * (1.0 / math.sqrt(2.0)))))


def _expert_out_kernel(*refs):
    idx_refs = refs[:TOKEN_UNROLL]
    w_ref, x_ref, tab_hbm, o_ref, tab, sem, wb_ref = refs[TOKEN_UNROLL:]
    _load_table_once(tab_hbm, tab, sem)
    sub = lax.broadcasted_iota(jnp.int32, (SUBLANES, LANES), 0)
    lower = sub < ROW_WORDS
    tb = x_ref.shape[0]

    def step(i, carry):
        for u in range(TOKEN_UNROLL):
            row = w_ref[pl.ds(i * TOKEN_UNROLL + u, 1), :]
            wb_ref[u] = jnp.broadcast_to(row, (SLOTS, LANES)).T
        acc_h = [jnp.zeros((SUBLANES, LANES), F32) for _ in range(TOKEN_UNROLL)]
        acc_l = [jnp.zeros((SUBLANES, LANES), F32) for _ in range(TOKEN_UNROLL)]
        for p in range(SLOTS // 2):
            ka, kb = 2 * p, 2 * p + 1
            for u in range(TOKEN_UNROLL):
                hi, lo = _gather_pair(tab, idx_refs[u][i, ka], idx_refs[u][i, kb])
                wa = jnp.broadcast_to(wb_ref[u, ka:ka + 1, :], (SUBLANES, LANES))
                wb = jnp.broadcast_to(wb_ref[u, kb:kb + 1, :], (SUBLANES, LANES))
                wt = jnp.where(lower, wa, wb)
                acc_h[u] = acc_h[u] + hi * wt
                acc_l[u] = acc_l[u] + lo * wt
        for u in range(TOKEN_UNROLL):
            t = i * TOKEN_UNROLL + u
            ah = acc_h[u] + pltpu.roll(acc_h[u], ROW_WORDS, 0)
            al = acc_l[u] + pltpu.roll(acc_l[u], ROW_WORDS, 0)
            o_ref[t] = x_ref[t] + jnp.where(lower, ah, al)
        return carry

    lax.fori_loop(0, tb // TOKEN_UNROLL, step, 0)


def _expert_stage(x, h, idx, gate, u_packed, v_packed, tb=128):
    t, d = x.shape
    rows = d // LANES
    un = TOKEN_UNROLL
    idx_split = idx.reshape(t // un, un, SLOTS).transpose(1, 0, 2)
    smem_blks = [pl.BlockSpec((None, tb // un, SLOTS), functools.partial(lambda u, i: (u, i, 0), u),
                              memory_space=pltpu.SMEM) for u in range(un)]
    vmem_blk = pl.BlockSpec((tb, SLOTS), lambda i: (i, 0))
    tok_blk = pl.BlockSpec((tb, rows, LANES), lambda i: (i, 0, 0))
    table_scratch = [pltpu.VMEM(u_packed.shape, jnp.int32), pltpu.SemaphoreType.DMA]
    params = pltpu.CompilerParams(dimension_semantics=("arbitrary",), vmem_limit_bytes=VMEM_LIMIT)
    w = pl.pallas_call(
        _expert_in_kernel,
        grid=(t // tb,),
        in_specs=smem_blks + [tok_blk, vmem_blk, pl.BlockSpec(memory_space=pl.ANY)],
        out_specs=vmem_blk,
        out_shape=jax.ShapeDtypeStruct((t, SLOTS), F32),
        scratch_shapes=table_scratch + [pltpu.VMEM((tb, SLOTS), F32)],
        compiler_params=params,
        name="peer_expert_in",
    )(*([idx_split] * un), h.reshape(t, rows, LANES), gate, u_packed)
    out = pl.pallas_call(
        _expert_out_kernel,
        grid=(t // tb,),
        in_specs=smem_blks + [vmem_blk, tok_blk, pl.BlockSpec(memory_space=pl.ANY)],
        out_specs=tok_blk,
        out_shape=jax.ShapeDtypeStruct((t, rows, LANES), F32),
        scratch_shapes=table_scratch + [pltpu.VMEM((un, SLOTS, LANES), F32)],
        compiler_params=params,
        name="peer_expert_out",
    )(*([idx_split] * un), w, x.reshape(t, rows, LANES), v_packed)
    return out.reshape(t, d)


def _peer(x, gain, w_query, sub_keys, expert_u, expert_v):
    h, idx, gate = _route(x, gain, w_query, sub_keys)
    return _expert_stage(x, h, idx, gate, _pack_table(expert_u), _pack_table(expert_v))


def kernel(x, norm_mix, norm_ffn, attn_w_qkv, attn_q_norm, attn_k_norm, attn_sinks, attn_w_o, conv_w_in, conv_w, conv_w_out, peer_w_query, peer_sub_keys, peer_u, peer_v):
    batch, seq, d = x.shape
    xt = x.reshape(batch * seq, d)
    for i in range(norm_mix.shape[0]):
        j = i // 2
        if i % 2 == 0:
            qkv = _norm_matmul(xt, norm_mix[i], attn_w_qkv[j].astype(BF16))
            o = _attention(qkv, attn_q_norm[j], attn_k_norm[j], attn_sinks[j], batch, seq)
            xt = _matmul_residual(o, attn_w_o[j].astype(BF16), xt)
        else:
            xt = _conv_mixer(xt, norm_mix[i], conv_w_in[j], conv_w[j], conv_w_out[j], batch, seq)
        xt = _peer(xt, norm_ffn[i], peer_w_query[i], peer_sub_keys[i], peer_u[i], peer_v[i])
    return xt.reshape(batch, seq, d)
```

```python
import dataclasses
import functools
import math

import jax
import jax.numpy as jnp
from jax import lax
from jax.experimental import pallas as pl
from jax.experimental.pallas import tpu as pltpu
from jax.experimental.pallas import tpu_sc as plsc

D_MODEL = 1024
RMS_EPS = 1e-6

HEAD_DIM = 64
N_Q_HEADS = 16
N_KV_HEADS = 4
GROUP = N_Q_HEADS // N_KV_HEADS
WINDOW = 128
ROT_DIM = HEAD_DIM // 4
ROPE_THETA = 500000.0
Q_COLS = N_Q_HEADS * HEAD_DIM
KV_COLS = N_KV_HEADS * HEAD_DIM
NEG_INF = -1e30

CONV_WIDTH = 3

PEER_HEADS = 8
N_KEYS = 128
N_EXPERTS = N_KEYS * N_KEYS
PEER_TOPK = 16
QUERY_HALF = 128
SLOTS = PEER_HEADS * PEER_TOPK

LANES = 128
SUBLANES = 8
ROW_WORDS = D_MODEL // 2 // LANES
VMEM_LIMIT = 48 * 1024 * 1024

BF16 = jnp.bfloat16
F32 = jnp.float32


def _rms(x, gain):
    return x * lax.rsqrt(jnp.mean(x * x, axis=-1, keepdims=True) + RMS_EPS) * gain


def _norm_matmul_kernel(x_ref, g_ref, w_ref, o_ref):
    h = _rms(x_ref[...], g_ref[...])
    o_ref[...] = jnp.dot(h.astype(BF16), w_ref[...], preferred_element_type=F32)


def _norm_matmul(x, gain, w, tm=512):
    t, d = x.shape
    n = w.shape[1]
    return pl.pallas_call(
        _norm_matmul_kernel,
        grid=(t // tm,),
        in_specs=[pl.BlockSpec((tm, d), lambda i: (i, 0)),
                  pl.BlockSpec((1, d), lambda i: (0, 0)),
                  pl.BlockSpec((d, n), lambda i: (0, 0))],
        out_specs=pl.BlockSpec((tm, n), lambda i: (i, 0)),
        out_shape=jax.ShapeDtypeStruct((t, n), F32),
        compiler_params=pltpu.CompilerParams(dimension_semantics=("arbitrary",), vmem_limit_bytes=VMEM_LIMIT),
        name="norm_matmul",
    )(x, gain.reshape(1, d), w)


def _matmul_residual_kernel(a_ref, w_ref, r_ref, o_ref):
    o_ref[...] = r_ref[...] + jnp.dot(a_ref[...].astype(BF16), w_ref[...], preferred_element_type=F32)


def _matmul_residual(a, w, res, tm=512):
    t, k = a.shape
    n = w.shape[1]
    return pl.pallas_call(
        _matmul_residual_kernel,
        grid=(t // tm,),
        in_specs=[pl.BlockSpec((tm, k), lambda i: (i, 0)),
                  pl.BlockSpec((k, n), lambda i: (0, 0)),
                  pl.BlockSpec((tm, n), lambda i: (i, 0))],
        out_specs=pl.BlockSpec((tm, n), lambda i: (i, 0)),
        out_shape=jax.ShapeDtypeStruct((t, n), F32),
        compiler_params=pltpu.CompilerParams(dimension_semantics=("arbitrary",), vmem_limit_bytes=VMEM_LIMIT),
        name="matmul_residual",
    )(a, w, res)


def _rope_tables(seq):
    half = ROT_DIM // 2
    freqs = ROPE_THETA ** (-jnp.arange(0, ROT_DIM, 2, dtype=F32) / ROT_DIM)
    ang = jnp.arange(seq, dtype=F32)[:, None] * freqs[None, :]
    cos, sin = jnp.cos(ang), jnp.sin(ang)
    ones = jnp.ones((seq, HEAD_DIM - ROT_DIM), F32)
    zeros = jnp.zeros((seq, HEAD_DIM - ROT_DIM), F32)
    zh = jnp.zeros((seq, half), F32)
    c = jnp.concatenate([cos, cos, ones], axis=1)
    s_next = jnp.concatenate([-sin, zh, zeros], axis=1)
    s_prev = jnp.concatenate([zh, sin, zeros], axis=1)
    return jnp.stack([jnp.tile(c, (1, 2)), jnp.tile(s_next, (1, 2)), jnp.tile(s_prev, (1, 2))])


def _head_norm_rope(x, gain2, rope, lo):
    sq = x * x
    s_lo = jnp.sum(jnp.where(lo, sq, 0.0), axis=1, keepdims=True)
    s_hi = jnp.sum(jnp.where(lo, 0.0, sq), axis=1, keepdims=True)
    ms = jnp.where(lo, s_lo, s_hi) * (1.0 / HEAD_DIM)
    xn = x * lax.rsqrt(ms + RMS_EPS) * gain2
    half = ROT_DIM // 2
    return xn * rope[0] + pltpu.roll(xn, LANES - half, 1) * rope[1] + pltpu.roll(xn, half, 1) * rope[2]


def _attn_kernel(sinks_ref, q_ref, kc_ref, kp_ref, vc_ref, vp_ref, rc_ref, rp_ref, qg_ref, kg_ref, o_ref):
    n = pl.program_id(1)
    lo = lax.broadcasted_iota(jnp.int32, (WINDOW, LANES), 1) < HEAD_DIM
    rope_c = rc_ref[...]
    rope_p = rp_ref[...]
    qg = qg_ref[...]
    kg = kg_ref[...]

    rows = GROUP * WINDOW
    qi = lax.broadcasted_iota(jnp.int32, (rows, 2 * WINDOW), 0) & (WINDOW - 1)
    ki = lax.broadcasted_iota(jnp.int32, (rows, 2 * WINDOW), 1)
    rel = WINDOW + qi - ki
    valid = (rel >= 0) & (rel < WINDOW) & ((n > 0) | (ki >= WINDOW))
    head_of_row = lax.broadcasted_iota(jnp.int32, (rows, 1), 0) // WINDOW
    scale = 1.0 / math.sqrt(HEAD_DIM)

    q2 = [_head_norm_rope(q_ref[:, c * LANES:(c + 1) * LANES], qg, rope_c, lo).astype(BF16)
          for c in range(Q_COLS // LANES)]
    for c in range(KV_COLS // LANES):
        cols = slice(c * LANES, (c + 1) * LANES)
        kprev = _head_norm_rope(kp_ref[:, cols], kg, rope_p, lo)
        kcur = _head_norm_rope(kc_ref[:, cols], kg, rope_c, lo)
        kfull = jnp.concatenate([kprev, kcur], axis=0).astype(BF16)
        vfull = jnp.concatenate([vp_ref[:, cols], vc_ref[:, cols]], axis=0).astype(BF16)
        for hh in range(LANES // HEAD_DIM):
            h = (LANES // HEAD_DIM) * c + hh
            kh = kfull[:, hh * HEAD_DIM:(hh + 1) * HEAD_DIM]
            vh = vfull[:, hh * HEAD_DIM:(hh + 1) * HEAD_DIM]
            heads = [GROUP * h + g for g in range(GROUP)]
            q4 = jnp.concatenate([q2[j // 2][:, (j % 2) * HEAD_DIM:(j % 2 + 1) * HEAD_DIM] for j in heads], axis=0)
            s = lax.dot_general(q4, kh, (((1,), (1,)), ((), ())), preferred_element_type=F32) * scale
            s = jnp.where(valid, s, NEG_INF)
            sink = jnp.zeros((rows, 1), F32)
            for g, j in enumerate(heads):
                sink = jnp.where(head_of_row == g, sinks_ref[j], sink)
            m = jnp.maximum(jnp.max(s, axis=1, keepdims=True), sink)
            p = jnp.exp(s - m)
            denom = jnp.sum(p, axis=1, keepdims=True) + jnp.exp(sink - m)
            o = jnp.dot(p.astype(BF16), vh, preferred_element_type=F32) / denom
            for g, j in enumerate(heads):
                o_ref[:, j * HEAD_DIM:(j + 1) * HEAD_DIM] = o[g * WINDOW:(g + 1) * WINDOW]


def _attention(qkv, q_gain, k_gain, sinks, batch, seq):
    t = batch * seq
    nb = seq // WINDOW
    rope = _rope_tables(seq)
    kcol = Q_COLS // KV_COLS
    cur = lambda b, n: (b * nb + n, 0)
    kcur = lambda b, n: (b * nb + n, kcol)
    kprev = lambda b, n: (b * nb + jnp.maximum(n - 1, 0), kcol)
    vcur = lambda b, n: (b * nb + n, kcol + 1)
    vprev = lambda b, n: (b * nb + jnp.maximum(n - 1, 0), kcol + 1)
    return pl.pallas_call(
        _attn_kernel,
        grid=(batch, nb),
        in_specs=[pl.BlockSpec(memory_space=pltpu.SMEM),
                  pl.BlockSpec((WINDOW, Q_COLS), cur),
                  pl.BlockSpec((WINDOW, KV_COLS), kcur),
                  pl.BlockSpec((WINDOW, KV_COLS), kprev),
                  pl.BlockSpec((WINDOW, KV_COLS), vcur),
                  pl.BlockSpec((WINDOW, KV_COLS), vprev),
                  pl.BlockSpec((3, WINDOW, LANES), lambda b, n: (0, n, 0)),
                  pl.BlockSpec((3, WINDOW, LANES), lambda b, n: (0, jnp.maximum(n - 1, 0), 0)),
                  pl.BlockSpec((1, LANES), lambda b, n: (0, 0)),
                  pl.BlockSpec((1, LANES), lambda b, n: (0, 0))],
        out_specs=pl.BlockSpec((WINDOW, Q_COLS), cur),
        out_shape=jax.ShapeDtypeStruct((t, Q_COLS), F32),
        compiler_params=pltpu.CompilerParams(dimension_semantics=("arbitrary", "arbitrary"),
                                             vmem_limit_bytes=VMEM_LIMIT),
        name="swa_attention",
    )(sinks, qkv, qkv, qkv, qkv, qkv, rope, rope,
      jnp.tile(q_gain, 2).reshape(1, LANES), jnp.tile(k_gain, 2).reshape(1, LANES))


def _conv_kernel(x_ref, g_ref, win_ref, cw_ref, wout_ref, o_ref, zprev_ref):
    n = pl.program_id(1)
    d = D_MODEL

    @pl.when(n == 0)
    def _():
        zprev_ref[...] = jnp.zeros_like(zprev_ref)

    x = x_ref[...]
    h = _rms(x, g_ref[...])
    bcu = jnp.dot(h.astype(BF16), win_ref[...], preferred_element_type=F32)
    gate_b = bcu[:, :d]
    z = bcu[:, d:2 * d] * bcu[:, 2 * d:]
    tm = z.shape[0]
    row = lax.broadcasted_iota(jnp.int32, z.shape, 0)
    prev = zprev_ref[...]
    p_last = prev[SUBLANES - 1:SUBLANES, :]
    p_last2 = prev[SUBLANES - 2:SUBLANES - 1, :]
    z1 = jnp.where(row == 0, p_last, pltpu.roll(z, 1, 0))
    z2 = jnp.where(row == 0, p_last2, jnp.where(row == 1, p_last, pltpu.roll(z, 2, 0)))
    cw = cw_ref[...]
    conv = cw[0:1, :] * z2 + cw[1:2, :] * z1 + cw[2:3, :] * z
    zprev_ref[...] = z[tm - SUBLANES:, :]
    o_ref[...] = x + jnp.dot((gate_b * conv).astype(BF16), wout_ref[...], preferred_element_type=F32)


def _conv_mixer(x, gain, w_in, conv_w, w_out, batch, seq, tm=256):
    t, d = x.shape
    nblk = seq // tm
    blk = lambda b, n: (b * nblk + n, 0)
    const = lambda b, n: (0, 0)
    return pl.pallas_call(
        _conv_kernel,
        grid=(batch, nblk),
        in_specs=[pl.BlockSpec((tm, d), blk),
                  pl.BlockSpec((1, d), const),
                  pl.BlockSpec((d, 3 * d), const),
                  pl.BlockSpec((CONV_WIDTH, d), const),
                  pl.BlockSpec((d, d), const)],
        out_specs=pl.BlockSpec((tm, d), blk),
        out_shape=jax.ShapeDtypeStruct((t, d), F32),
        scratch_shapes=[pltpu.VMEM((SUBLANES, d), F32)],
        compiler_params=pltpu.CompilerParams(dimension_semantics=("arbitrary", "arbitrary"),
                                             vmem_limit_bytes=VMEM_LIMIT),
        name="conv_mixer",
    )(x, gain.reshape(1, d), w_in.astype(BF16), conv_w, w_out.astype(BF16))


def _topk_axis0(s, k, ids=None, payload=None):
    n, tm = s.shape
    if ids is None:
        ids = lax.broadcasted_iota(jnp.int32, (n, tm), 0)
    krow = lax.broadcasted_iota(jnp.int32, (k, tm), 0)
    vals = jnp.zeros((k, tm), F32)
    picks = jnp.zeros((k, tm), jnp.int32)
    for r in range(k):
        m = jnp.max(s, axis=0, keepdims=True)
        pos = jnp.min(jnp.where(s == m, ids, jnp.iinfo(jnp.int32).max), axis=0, keepdims=True)
        sel = ids == pos
        if payload is None:
            picked = pos
        else:
            picked = jnp.sum(jnp.where(sel, payload, 0), axis=0, keepdims=True)
        vals = jnp.where(krow == r, m, vals)
        picks = jnp.where(krow == r, picked, picks)
        s = jnp.where(sel, -jnp.inf, s)
    return vals, picks


def _pair_candidates(s1, i1, s2, i2):
    k, tm = s1.shape
    sub = lax.broadcasted_iota(jnp.int32, (SUBLANES, tm), 0)
    scores, flat, expert = [], [], []
    for i in range(k // 2):
        width = k if i == 0 else SUBLANES
        sc = s1[i:i + 1, :] + s2[0:width, :]
        ex = i1[i:i + 1, :] * N_KEYS + i2[0:width, :]
        fl = i * k + lax.broadcasted_iota(jnp.int32, (width, tm), 0)
        reach = k // (i + 1)
        if reach < width:
            sc = jnp.where(sub < reach, sc, -jnp.inf)
        scores.append(sc)
        flat.append(fl)
        expert.append(ex)
    scores.append(s1[k // 2:, :] + s2[0:1, :])
    expert.append(i1[k // 2:, :] * N_KEYS + i2[0:1, :])
    flat.append((k // 2 + sub) * k)
    return jnp.concatenate(scores, axis=0), jnp.concatenate(flat, axis=0), jnp.concatenate(expert, axis=0)


def _route_kernel(x_ref, g_ref, wq_ref, keys_ref, h_ref, idx_ref, gate_ref):
    h = _rms(x_ref[...], g_ref[...])
    h_ref[...] = h
    q = jnp.dot(h.astype(BF16), wq_ref[...], preferred_element_type=F32).astype(BF16)
    idx_rows, gate_rows = [], []
    for head in range(PEER_HEADS):
        tops = []
        for part in range(2):
            col = (head * 2 + part) * QUERY_HALF
            s = lax.dot_general(keys_ref[head, part], q[:, col:col + QUERY_HALF],
                                (((1,), (1,)), ((), ())), preferred_element_type=F32)
            tops.append(_topk_axis0(s, PEER_TOPK))
        (s1, i1), (s2, i2) = tops
        cand, flat_ids, cand_idx = _pair_candidates(s1, i1, s2, i2)
        g_s, e_idx = _topk_axis0(cand, PEER_TOPK, ids=flat_ids, payload=cand_idx)
        e = jnp.exp(g_s - jnp.max(g_s, axis=0, keepdims=True))
        gate_rows.append(e / jnp.sum(e, axis=0, keepdims=True))
        idx_rows.append(e_idx * ROW_WORDS)
    idx_ref[...] = jnp.concatenate(idx_rows, axis=0).T
    gate_ref[...] = jnp.concatenate(gate_rows, axis=0).T


def _route(x, gain, w_query, sub_keys, tm=128):
    t, d = x.shape
    nq = w_query.shape[1]
    return pl.pallas_call(
        _route_kernel,
        grid=(t // tm,),
        in_specs=[pl.BlockSpec((tm, d), lambda i: (i, 0)),
                  pl.BlockSpec((1, d), lambda i: (0, 0)),
                  pl.BlockSpec((d, nq), lambda i: (0, 0)),
                  pl.BlockSpec((PEER_HEADS, 2, N_KEYS, QUERY_HALF), lambda i: (0, 0, 0, 0))],
        out_specs=[pl.BlockSpec((tm, d), lambda i: (i, 0)),
                   pl.BlockSpec((tm, SLOTS), lambda i: (i, 0)),
                   pl.BlockSpec((tm, SLOTS), lambda i: (i, 0))],
        out_shape=[jax.ShapeDtypeStruct((t, d), F32),
                   jax.ShapeDtypeStruct((t, SLOTS), jnp.int32),
                   jax.ShapeDtypeStruct((t, SLOTS), F32)],
        compiler_params=pltpu.CompilerParams(dimension_semantics=("arbitrary",), vmem_limit_bytes=VMEM_LIMIT),
        name="peer_route",
    )(x, gain.reshape(1, d), w_query.astype(BF16), sub_keys.astype(BF16))


def _pack_table(tab):
    bits = lax.bitcast_convert_type(tab.astype(BF16), jnp.uint16).astype(jnp.uint32)
    half = D_MODEL // 2
    words = (bits[:, :half] << 16) | bits[:, half:]
    return lax.bitcast_convert_type(words, jnp.int32).reshape(tab.shape[0] * ROW_WORDS, LANES)


def _load_table_once(tab_hbm, tab, sem):
    @pl.when(pl.program_id(0) == 0)
    def _():
        cp = pltpu.make_async_copy(tab_hbm, tab, sem)
        cp.start()
        cp.wait()


def _gather_pair(tab, off_a, off_b):
    ra = tab[pl.ds(pl.multiple_of(off_a, ROW_WORDS), ROW_WORDS), :]
    rb = tab[pl.ds(pl.multiple_of(off_b, ROW_WORDS), ROW_WORDS), :]
    words = jnp.concatenate([ra, rb], axis=0)
    hi = pltpu.bitcast(words & jnp.int32(-65536), F32)
    lo = pltpu.bitcast(words << 16, F32)
    return hi, lo


TOKEN_UNROLL = 8


def _expert_in_kernel(*refs):
    idx_refs = refs[:TOKEN_UNROLL]
    h_ref, gate_ref, tab_hbm, w_ref, tab, sem, a_ref = refs[TOKEN_UNROLL:]
    _load_table_once(tab_hbm, tab, sem)
    sub = lax.broadcasted_iota(jnp.int32, (SUBLANES, LANES), 0)
    lane = lax.broadcasted_iota(jnp.int32, (SUBLANES, LANES), 1)
    own_half = (sub >= ROW_WORDS) == ((lane & 1) == 1)
    tb = h_ref.shape[0]

    def step(i, carry):
        xs = []
        for u in range(TOKEN_UNROLL):
            x = h_ref[i * TOKEN_UNROLL + u]
            xs.append((jnp.concatenate([x[0:ROW_WORDS], x[0:ROW_WORDS]], axis=0),
                       jnp.concatenate([x[ROW_WORDS:], x[ROW_WORDS:]], axis=0)))
        accs = [jnp.zeros((SUBLANES, LANES), F32) for _ in range(TOKEN_UNROLL)]
        for p in range(SLOTS // 2):
            for u in range(TOKEN_UNROLL):
                hi, lo = _gather_pair(tab, idx_refs[u][i, 2 * p], idx_refs[u][i, 2 * p + 1])
                part = jnp.sum(hi * xs[u][0] + lo * xs[u][1], axis=1, keepdims=True)
                accs[u] = jnp.where((lane >> 1) == p, part, accs[u])
        for u in range(TOKEN_UNROLL):
            a_ref[pl.ds(i * TOKEN_UNROLL + u, 1), :] = jnp.sum(jnp.where(own_half, accs[u], 0.0),
                                                                axis=0, keepdims=True)
        return carry

    lax.fori_loop(0, tb // TOKEN_UNROLL, step, 0)
    a = a_ref[...]
    w_ref[...] = gate_ref[...] * (0.5 * a * (1.0 + lax.erf(a * (1.0 / math.sqrt(2.0)))))


def _expert_out_kernel(*refs):
    idx_refs = refs[:TOKEN_UNROLL]
    w_ref, x_ref, tab_hbm, o_ref, tab, sem, wb_ref = refs[TOKEN_UNROLL:]
    _load_table_once(tab_hbm, tab, sem)
    sub = lax.broadcasted_iota(jnp.int32, (SUBLANES, LANES), 0)
    lower = sub < ROW_WORDS
    tb = x_ref.shape[0]

    def step(i, carry):
        for u in range(TOKEN_UNROLL):
            row = w_ref[pl.ds(i * TOKEN_UNROLL + u, 1), :]
            wb_ref[u] = jnp.broadcast_to(row, (SLOTS, LANES)).T
        acc_h = [jnp.zeros((SUBLANES, LANES), F32) for _ in range(TOKEN_UNROLL)]
        acc_l = [jnp.zeros((SUBLANES, LANES), F32) for _ in range(TOKEN_UNROLL)]
        for p in range(SLOTS // 2):
            ka, kb = 2 * p, 2 * p + 1
            for u in range(TOKEN_UNROLL):
                hi, lo = _gather_pair(tab, idx_refs[u][i, ka], idx_refs[u][i, kb])
                wa = jnp.broadcast_to(wb_ref[u, ka:ka + 1, :], (SUBLANES, LANES))
                wb = jnp.broadcast_to(wb_ref[u, kb:kb + 1, :], (SUBLANES, LANES))
                wt = jnp.where(lower, wa, wb)
                acc_h[u] = acc_h[u] + hi * wt
                acc_l[u] = acc_l[u] + lo * wt
        for u in range(TOKEN_UNROLL):
            t = i * TOKEN_UNROLL + u
            ah = acc_h[u] + pltpu.roll(acc_h[u], ROW_WORDS, 0)
            al = acc_l[u] + pltpu.roll(acc_l[u], ROW_WORDS, 0)
            o_ref[t] = x_ref[t] + jnp.where(lower, ah, al)
        return carry

    lax.fori_loop(0, tb // TOKEN_UNROLL, step, 0)


def _expert_stage_tc(x, h, idx, gate, u_packed, v_packed, t_tc, tb=128):
    t, d = x.shape
    rows = d // LANES
    un = TOKEN_UNROLL
    idx_split = idx.reshape(t // un, un, SLOTS).transpose(1, 0, 2)
    smem_blks = [pl.BlockSpec((None, tb // un, SLOTS), functools.partial(lambda u, i: (u, i, 0), u),
                              memory_space=pltpu.SMEM) for u in range(un)]
    vmem_blk = pl.BlockSpec((tb, SLOTS), lambda i: (i, 0))
    tok_blk = pl.BlockSpec((tb, rows, LANES), lambda i: (i, 0, 0))
    table_scratch = [pltpu.VMEM(u_packed.shape, jnp.int32), pltpu.SemaphoreType.DMA]
    params = pltpu.CompilerParams(dimension_semantics=("arbitrary",), vmem_limit_bytes=VMEM_LIMIT)
    w = pl.pallas_call(
        _expert_in_kernel,
        grid=(t_tc // tb,),
        in_specs=smem_blks + [tok_blk, vmem_blk, pl.BlockSpec(memory_space=pl.ANY)],
        out_specs=vmem_blk,
        out_shape=jax.ShapeDtypeStruct((t_tc, SLOTS), F32),
        scratch_shapes=table_scratch + [pltpu.VMEM((tb, SLOTS), F32)],
        compiler_params=params,
        name="peer_expert_in",
    )(*([idx_split] * un), h.reshape(t, rows, LANES), gate, u_packed)
    out = pl.pallas_call(
        _expert_out_kernel,
        grid=(t_tc // tb,),
        in_specs=smem_blks + [vmem_blk, tok_blk, pl.BlockSpec(memory_space=pl.ANY)],
        out_specs=tok_blk,
        out_shape=jax.ShapeDtypeStruct((t_tc, rows, LANES), F32),
        scratch_shapes=table_scratch + [pltpu.VMEM((un, SLOTS, LANES), F32)],
        compiler_params=params,
        name="peer_expert_out",
    )(*([idx_split] * un), w, x.reshape(t, rows, LANES), v_packed)
    return out.reshape(t_tc, d)


SC_LANES = 16
SC_WORKERS = 32
SC_CHUNK = 32
SC_TOKENS = 1024


def _sc_params():
    cp = pltpu.CompilerParams()
    if "needs_layout_passes" in pltpu.CompilerParams.__dataclass_fields__:
        cp = dataclasses.replace(cp, needs_layout_passes=False)
    return cp


def _sc_expert_in(table, idx, h):
    n_tok, d = h.shape
    per = n_tok // SC_WORKERS
    nvec = d // SC_LANES
    mesh = plsc.VectorSubcoreMesh(core_axis_name="c", subcore_axis_name="s")

    @functools.partial(
        pl.kernel, mesh=mesh,
        out_type=jax.ShapeDtypeStruct((n_tok, SLOTS), F32),
        scratch_types=[pltpu.VMEM((SLOTS,), jnp.int32), pltpu.VMEM((d,), F32),
                       pltpu.VMEM((SC_CHUNK, d), F32), pltpu.VMEM((SLOTS,), F32),
                       pltpu.SemaphoreType.DMA],
        compiler_params=_sc_params(),
        name="peer_expert_in_sc",
    )
    def body(tab_hbm, idx_hbm, h_hbm, a_hbm, idx_v, x_v, rows_v, a_v, sem):
        base = (lax.axis_index("s") * 2 + lax.axis_index("c")) * per
        lanes = lax.iota(jnp.int32, SC_LANES)

        @pl.loop(0, per)
        def _(i):
            t = base + i
            pltpu.sync_copy(idx_hbm.at[t], idx_v)
            pltpu.sync_copy(h_hbm.at[t], x_v)
            for c in range(SLOTS // SC_CHUNK):
                pltpu.async_copy(tab_hbm.at[idx_v.at[pl.ds(c * SC_CHUNK, SC_CHUNK)]], rows_v, sem).wait()
                for g in range(SC_CHUNK // SC_LANES):
                    def row_body(r, avec):
                        def vec_body(j, acc):
                            return acc + (rows_v[g * SC_LANES + r, pl.ds(j * SC_LANES, SC_LANES)]
                                          * x_v[pl.ds(j * SC_LANES, SC_LANES)])
                        acc = lax.fori_loop(0, nvec, vec_body, jnp.zeros((SC_LANES,), F32))
                        return jnp.where(lanes == r, jnp.sum(acc), avec)
                    avec = lax.fori_loop(0, SC_LANES, row_body, jnp.zeros((SC_LANES,), F32))
                    a_v[pl.ds(c * SC_CHUNK + g * SC_LANES, SC_LANES)] = avec
            pltpu.sync_copy(a_v, a_hbm.at[t])

    return body(table, idx, h)


def _sc_expert_out(table, idx, w, x):
    n_tok, d = x.shape
    per = n_tok // SC_WORKERS
    nvec = d // SC_LANES
    mesh = plsc.VectorSubcoreMesh(core_axis_name="c", subcore_axis_name="s")

    @functools.partial(
        pl.kernel, mesh=mesh,
        out_type=jax.ShapeDtypeStruct((n_tok, d), F32),
        scratch_types=[pltpu.VMEM((SLOTS,), jnp.int32), pltpu.VMEM((SLOTS,), F32),
                       pltpu.VMEM((SC_CHUNK, d), F32), pltpu.VMEM((d,), F32),
                       pltpu.SemaphoreType.DMA],
        compiler_params=_sc_params(),
        name="peer_expert_out_sc",
    )
    def body(tab_hbm, idx_hbm, w_hbm, x_hbm, o_hbm, idx_v, w_v, rows_v, y_v, sem):
        base = (lax.axis_index("s") * 2 + lax.axis_index("c")) * per
        zero = jnp.zeros((SC_LANES,), jnp.int32)

        @pl.loop(0, per)
        def _(i):
            t = base + i
            pltpu.sync_copy(idx_hbm.at[t], idx_v)
            pltpu.sync_copy(w_hbm.at[t], w_v)
            pltpu.sync_copy(x_hbm.at[t], y_v)
            for c in range(SLOTS // SC_CHUNK):
                pltpu.async_copy(tab_hbm.at[idx_v.at[pl.ds(c * SC_CHUNK, SC_CHUNK)]], rows_v, sem).wait()

                def row_body(r, carry):
                    ws = plsc.load_gather(w_v, [zero + (c * SC_CHUNK + r)])

                    def vec_body(j, cc):
                        plsc.addupdate(y_v.at[pl.ds(j * SC_LANES, SC_LANES)],
                                       rows_v[r, pl.ds(j * SC_LANES, SC_LANES)] * ws)
                        return cc
                    lax.fori_loop(0, nvec, vec_body, 0)
                    return carry
                lax.fori_loop(0, SC_CHUNK, row_body, 0)
            pltpu.sync_copy(y_v, o_hbm.at[t])

    return body(table, idx, w, x)


def _gelu_gate_kernel(a_ref, g_ref, w_ref):
    a = a_ref[...]
    w_ref[...] = g_ref[...] * (0.5 * a * (1.0 + lax.erf(a * (1.0 / math.sqrt(2.0)))))


def _gelu_gate(a, gate):
    return pl.pallas_call(_gelu_gate_kernel, out_shape=jax.ShapeDtypeStruct(a.shape, F32), name="peer_gelu_gate")(a, gate)


def _expert_stage_sc(x, h, idx, gate, expert_u, expert_v):
    a = _sc_expert_in(expert_u, idx, h)
    return _sc_expert_out(expert_v, idx, _gelu_gate(a, gate), x)


def _peer(x, gain, w_query, sub_keys, expert_u, expert_v):
    h, idx, gate = _route(x, gain, w_query, sub_keys)
    t_tc = x.shape[0] - SC_TOKENS
    out_tc = _expert_stage_tc(x, h, idx, gate, _pack_table(expert_u), _pack_table(expert_v), t_tc)
    out_sc = _expert_stage_sc(x[t_tc:], h[t_tc:], idx[t_tc:] // ROW_WORDS, gate[t_tc:], expert_u, expert_v)
    return jnp.concatenate([out_tc, out_sc], axis=0)


def kernel(x, norm_mix, norm_ffn, attn_w_qkv, attn_q_norm, attn_k_norm, attn_sinks, attn_w_o, conv_w_in, conv_w, conv_w_out, peer_w_query, peer_sub_keys, peer_u, peer_v):
    batch, seq, d = x.shape
    xt = x.reshape(batch * seq, d)
    for i in range(norm_mix.shape[0]):
        j = i // 2
        if i % 2 == 0:
            qkv = _norm_matmul(xt, norm_mix[i], attn_w_qkv[j].astype(BF16))
            o = _attention(qkv, attn_q_norm[j], attn_k_norm[j], attn_sinks[j], batch, seq)
            xt = _matmul_residual(o, attn_w_o[j].astype(BF16), xt)
        else:
            xt = _conv_mixer(xt, norm_mix[i], conv_w_in[j], conv_w[j], conv_w_out[j], batch, seq)
        xt = _peer(xt, norm_ffn[i], peer_w_query[i], peer_sub_keys[i], peer_u[i], peer_v[i])
    return xt.reshape(batch, seq, d)
```

```python
import dataclasses
import functools
import math

import jax
import jax.numpy as jnp
from jax import lax
from jax.experimental import pallas as pl
from jax.experimental.pallas import tpu as pltpu
from jax.experimental.pallas import tpu_sc as plsc

D_MODEL = 1024
RMS_EPS = 1e-6

HEAD_DIM = 64
N_Q_HEADS = 16
N_KV_HEADS = 4
GROUP = N_Q_HEADS // N_KV_HEADS
WINDOW = 128
ROT_DIM = HEAD_DIM // 4
ROPE_THETA = 500000.0
Q_COLS = N_Q_HEADS * HEAD_DIM
KV_COLS = N_KV_HEADS * HEAD_DIM
NEG_INF = -1e30

CONV_WIDTH = 3

PEER_HEADS = 8
N_KEYS = 128
N_EXPERTS = N_KEYS * N_KEYS
PEER_TOPK = 16
QUERY_HALF = 128
SLOTS = PEER_HEADS * PEER_TOPK

LANES = 128
SUBLANES = 8
ROW_WORDS = D_MODEL // 2 // LANES
VMEM_LIMIT = 48 * 1024 * 1024

BF16 = jnp.bfloat16
F32 = jnp.float32


def _rms(x, gain):
    return x * lax.rsqrt(jnp.mean(x * x, axis=-1, keepdims=True) + RMS_EPS) * gain


def _norm_matmul_kernel(x_ref, g_ref, w_ref, o_ref):
    h = _rms(x_ref[...], g_ref[...])
    o_ref[...] = jnp.dot(h.astype(BF16), w_ref[...], preferred_element_type=F32)


def _norm_matmul(x, gain, w, tm=512):
    t, d = x.shape
    n = w.shape[1]
    return pl.pallas_call(
        _norm_matmul_kernel,
        grid=(t // tm,),
        in_specs=[pl.BlockSpec((tm, d), lambda i: (i, 0)),
                  pl.BlockSpec((1, d), lambda i: (0, 0)),
                  pl.BlockSpec((d, n), lambda i: (0, 0))],
        out_specs=pl.BlockSpec((tm, n), lambda i: (i, 0)),
        out_shape=jax.ShapeDtypeStruct((t, n), F32),
        compiler_params=pltpu.CompilerParams(dimension_semantics=("arbitrary",), vmem_limit_bytes=VMEM_LIMIT),
        name="norm_matmul",
    )(x, gain.reshape(1, d), w)


def _matmul_residual_kernel(a_ref, w_ref, r_ref, o_ref):
    o_ref[...] = r_ref[...] + jnp.dot(a_ref[...].astype(BF16), w_ref[...], preferred_element_type=F32)


def _matmul_residual(a, w, res, tm=512):
    t, k = a.shape
    n = w.shape[1]
    return pl.pallas_call(
        _matmul_residual_kernel,
        grid=(t // tm,),
        in_specs=[pl.BlockSpec((tm, k), lambda i: (i, 0)),
                  pl.BlockSpec((k, n), lambda i: (0, 0)),
                  pl.BlockSpec((tm, n), lambda i: (i, 0))],
        out_specs=pl.BlockSpec((tm, n), lambda i: (i, 0)),
        out_shape=jax.ShapeDtypeStruct((t, n), F32),
        compiler_params=pltpu.CompilerParams(dimension_semantics=("arbitrary",), vmem_limit_bytes=VMEM_LIMIT),
        name="matmul_residual",
    )(a, w, res)


def _rope_tables(seq):
    half = ROT_DIM // 2
    freqs = ROPE_THETA ** (-jnp.arange(0, ROT_DIM, 2, dtype=F32) / ROT_DIM)
    ang = jnp.arange(seq, dtype=F32)[:, None] * freqs[None, :]
    cos, sin = jnp.cos(ang), jnp.sin(ang)
    ones = jnp.ones((seq, HEAD_DIM - ROT_DIM), F32)
    zeros = jnp.zeros((seq, HEAD_DIM - ROT_DIM), F32)
    zh = jnp.zeros((seq, half), F32)
    c = jnp.concatenate([cos, cos, ones], axis=1)
    s_next = jnp.concatenate([-sin, zh, zeros], axis=1)
    s_prev = jnp.concatenate([zh, sin, zeros], axis=1)
    return jnp.stack([jnp.tile(c, (1, 2)), jnp.tile(s_next, (1, 2)), jnp.tile(s_prev, (1, 2))])


def _head_norm_rope(x, gain2, rope, lo):
    sq = x * x
    s_lo = jnp.sum(jnp.where(lo, sq, 0.0), axis=1, keepdims=True)
    s_hi = jnp.sum(jnp.where(lo, 0.0, sq), axis=1, keepdims=True)
    ms = jnp.where(lo, s_lo, s_hi) * (1.0 / HEAD_DIM)
    xn = x * lax.rsqrt(ms + RMS_EPS) * gain2
    half = ROT_DIM // 2
    return xn * rope[0] + pltpu.roll(xn, LANES - half, 1) * rope[1] + pltpu.roll(xn, half, 1) * rope[2]


def _attn_kernel(sinks_ref, q_ref, kc_ref, kp_ref, vc_ref, vp_ref, rc_ref, rp_ref, qg_ref, kg_ref, o_ref):
    n = pl.program_id(1)
    lo = lax.broadcasted_iota(jnp.int32, (WINDOW, LANES), 1) < HEAD_DIM
    rope_c = rc_ref[...]
    rope_p = rp_ref[...]
    qg = qg_ref[...]
    kg = kg_ref[...]

    rows = GROUP * WINDOW
    qi = lax.broadcasted_iota(jnp.int32, (rows, 2 * WINDOW), 0) & (WINDOW - 1)
    ki = lax.broadcasted_iota(jnp.int32, (rows, 2 * WINDOW), 1)
    rel = WINDOW + qi - ki
    valid = (rel >= 0) & (rel < WINDOW) & ((n > 0) | (ki >= WINDOW))
    head_of_row = lax.broadcasted_iota(jnp.int32, (rows, 1), 0) // WINDOW
    scale = 1.0 / math.sqrt(HEAD_DIM)

    q2 = [_head_norm_rope(q_ref[:, c * LANES:(c + 1) * LANES], qg, rope_c, lo).astype(BF16)
          for c in range(Q_COLS // LANES)]
    for c in range(KV_COLS // LANES):
        cols = slice(c * LANES, (c + 1) * LANES)
        kprev = _head_norm_rope(kp_ref[:, cols], kg, rope_p, lo)
        kcur = _head_norm_rope(kc_ref[:, cols], kg, rope_c, lo)
        kfull = jnp.concatenate([kprev, kcur], axis=0).astype(BF16)
        vfull = jnp.concatenate([vp_ref[:, cols], vc_ref[:, cols]], axis=0).astype(BF16)
        for hh in range(LANES // HEAD_DIM):
            h = (LANES // HEAD_DIM) * c + hh
            kh = kfull[:, hh * HEAD_DIM:(hh + 1) * HEAD_DIM]
            vh = vfull[:, hh * HEAD_DIM:(hh + 1) * HEAD_DIM]
            heads = [GROUP * h + g for g in range(GROUP)]
            q4 = jnp.concatenate([q2[j // 2][:, (j % 2) * HEAD_DIM:(j % 2 + 1) * HEAD_DIM] for j in heads], axis=0)
            s = lax.dot_general(q4, kh, (((1,), (1,)), ((), ())), preferred_element_type=F32) * scale
            s = jnp.where(valid, s, NEG_INF)
            sink = jnp.zeros((rows, 1), F32)
            for g, j in enumerate(heads):
                sink = jnp.where(head_of_row == g, sinks_ref[j], sink)
            m = jnp.maximum(jnp.max(s, axis=1, keepdims=True), sink)
            p = jnp.exp(s - m)
            denom = jnp.sum(p, axis=1, keepdims=True) + jnp.exp(sink - m)
            o = jnp.dot(p.astype(BF16), vh, preferred_element_type=F32) / denom
            for g, j in enumerate(heads):
                o_ref[:, j * HEAD_DIM:(j + 1) * HEAD_DIM] = o[g * WINDOW:(g + 1) * WINDOW]


def _attention(qkv, q_gain, k_gain, sinks, batch, seq):
    t = batch * seq
    nb = seq // WINDOW
    rope = _rope_tables(seq)
    kcol = Q_COLS // KV_COLS
    cur = lambda b, n: (b * nb + n, 0)
    kcur = lambda b, n: (b * nb + n, kcol)
    kprev = lambda b, n: (b * nb + jnp.maximum(n - 1, 0), kcol)
    vcur = lambda b, n: (b * nb + n, kcol + 1)
    vprev = lambda b, n: (b * nb + jnp.maximum(n - 1, 0), kcol + 1)
    return pl.pallas_call(
        _attn_kernel,
        grid=(batch, nb),
        in_specs=[pl.BlockSpec(memory_space=pltpu.SMEM),
                  pl.BlockSpec((WINDOW, Q_COLS), cur),
                  pl.BlockSpec((WINDOW, KV_COLS), kcur),
                  pl.BlockSpec((WINDOW, KV_COLS), kprev),
                  pl.BlockSpec((WINDOW, KV_COLS), vcur),
                  pl.BlockSpec((WINDOW, KV_COLS), vprev),
                  pl.BlockSpec((3, WINDOW, LANES), lambda b, n: (0, n, 0)),
                  pl.BlockSpec((3, WINDOW, LANES), lambda b, n: (0, jnp.maximum(n - 1, 0), 0)),
                  pl.BlockSpec((1, LANES), lambda b, n: (0, 0)),
                  pl.BlockSpec((1, LANES), lambda b, n: (0, 0))],
        out_specs=pl.BlockSpec((WINDOW, Q_COLS), cur),
        out_shape=jax.ShapeDtypeStruct((t, Q_COLS), F32),
        compiler_params=pltpu.CompilerParams(dimension_semantics=("arbitrary", "arbitrary"),
                                             vmem_limit_bytes=VMEM_LIMIT),
        name="swa_attention",
    )(sinks, qkv, qkv, qkv, qkv, qkv, rope, rope,
      jnp.tile(q_gain, 2).reshape(1, LANES), jnp.tile(k_gain, 2).reshape(1, LANES))


def _conv_kernel(x_ref, g_ref, win_ref, cw_ref, wout_ref, o_ref, zprev_ref):
    n = pl.program_id(1)
    d = D_MODEL

    @pl.when(n == 0)
    def _():
        zprev_ref[...] = jnp.zeros_like(zprev_ref)

    x = x_ref[...]
    h = _rms(x, g_ref[...])
    bcu = jnp.dot(h.astype(BF16), win_ref[...], preferred_element_type=F32)
    gate_b = bcu[:, :d]
    z = bcu[:, d:2 * d] * bcu[:, 2 * d:]
    tm = z.shape[0]
    row = lax.broadcasted_iota(jnp.int32, z.shape, 0)
    prev = zprev_ref[...]
    p_last = prev[SUBLANES - 1:SUBLANES, :]
    p_last2 = prev[SUBLANES - 2:SUBLANES - 1, :]
    z1 = jnp.where(row == 0, p_last, pltpu.roll(z, 1, 0))
    z2 = jnp.where(row == 0, p_last2, jnp.where(row == 1, p_last, pltpu.roll(z, 2, 0)))
    cw = cw_ref[...]
    conv = cw[0:1, :] * z2 + cw[1:2, :] * z1 + cw[2:3, :] * z
    zprev_ref[...] = z[tm - SUBLANES:, :]
    o_ref[...] = x + jnp.dot((gate_b * conv).astype(BF16), wout_ref[...], preferred_element_type=F32)


def _conv_mixer(x, gain, w_in, conv_w, w_out, batch, seq, tm=256):
    t, d = x.shape
    nblk = seq // tm
    blk = lambda b, n: (b * nblk + n, 0)
    const = lambda b, n: (0, 0)
    return pl.pallas_call(
        _conv_kernel,
        grid=(batch, nblk),
        in_specs=[pl.BlockSpec((tm, d), blk),
                  pl.BlockSpec((1, d), const),
                  pl.BlockSpec((d, 3 * d), const),
                  pl.BlockSpec((CONV_WIDTH, d), const),
                  pl.BlockSpec((d, d), const)],
        out_specs=pl.BlockSpec((tm, d), blk),
        out_shape=jax.ShapeDtypeStruct((t, d), F32),
        scratch_shapes=[pltpu.VMEM((SUBLANES, d), F32)],
        compiler_params=pltpu.CompilerParams(dimension_semantics=("arbitrary", "arbitrary"),
                                             vmem_limit_bytes=VMEM_LIMIT),
        name="conv_mixer",
    )(x, gain.reshape(1, d), w_in.astype(BF16), conv_w, w_out.astype(BF16))


def _topk_axis0(s, k, ids=None, payload=None):
    n, tm = s.shape
    if ids is None:
        ids = lax.broadcasted_iota(jnp.int32, (n, tm), 0)
    krow = lax.broadcasted_iota(jnp.int32, (k, tm), 0)
    vals = jnp.zeros((k, tm), F32)
    picks = jnp.zeros((k, tm), jnp.int32)
    for r in range(k):
        m = jnp.max(s, axis=0, keepdims=True)
        pos = jnp.min(jnp.where(s == m, ids, jnp.iinfo(jnp.int32).max), axis=0, keepdims=True)
        sel = ids == pos
        if payload is None:
            picked = pos
        else:
            picked = jnp.sum(jnp.where(sel, payload, 0), axis=0, keepdims=True)
        vals = jnp.where(krow == r, m, vals)
        picks = jnp.where(krow == r, picked, picks)
        s = jnp.where(sel, -jnp.inf, s)
    return vals, picks


def _pair_candidates(s1, i1, s2, i2):
    k, tm = s1.shape
    sub = lax.broadcasted_iota(jnp.int32, (SUBLANES, tm), 0)
    scores, flat, expert = [], [], []
    for i in range(k // 2):
        width = k if i == 0 else SUBLANES
        sc = s1[i:i + 1, :] + s2[0:width, :]
        ex = i1[i:i + 1, :] * N_KEYS + i2[0:width, :]
        fl = i * k + lax.broadcasted_iota(jnp.int32, (width, tm), 0)
        reach = k // (i + 1)
        if reach < width:
            sc = jnp.where(sub < reach, sc, -jnp.inf)
        scores.append(sc)
        flat.append(fl)
        expert.append(ex)
    scores.append(s1[k // 2:, :] + s2[0:1, :])
    expert.append(i1[k // 2:, :] * N_KEYS + i2[0:1, :])
    flat.append((k // 2 + sub) * k)
    return jnp.concatenate(scores, axis=0), jnp.concatenate(flat, axis=0), jnp.concatenate(expert, axis=0)


def _route_kernel(x_ref, g_ref, wq_ref, keys_ref, h_ref, idx_ref, gate_ref):
    h = _rms(x_ref[...], g_ref[...])
    h_ref[...] = h
    q = jnp.dot(h.astype(BF16), wq_ref[...], preferred_element_type=F32).astype(BF16)
    idx_rows, gate_rows = [], []
    for head in range(PEER_HEADS):
        tops = []
        for part in range(2):
            col = (head * 2 + part) * QUERY_HALF
            s = lax.dot_general(keys_ref[head, part], q[:, col:col + QUERY_HALF],
                                (((1,), (1,)), ((), ())), preferred_element_type=F32)
            tops.append(_topk_axis0(s, PEER_TOPK))
        (s1, i1), (s2, i2) = tops
        cand, flat_ids, cand_idx = _pair_candidates(s1, i1, s2, i2)
        g_s, e_idx = _topk_axis0(cand, PEER_TOPK, ids=flat_ids, payload=cand_idx)
        e = jnp.exp(g_s - jnp.max(g_s, axis=0, keepdims=True))
        gate_rows.append(e / jnp.sum(e, axis=0, keepdims=True))
        idx_rows.append(e_idx * ROW_WORDS)
    idx_ref[...] = jnp.concatenate(idx_rows, axis=0).T
    gate_ref[...] = jnp.concatenate(gate_rows, axis=0).T


def _route(x, gain, w_query, sub_keys, tm=128):
    t, d = x.shape
    nq = w_query.shape[1]
    return pl.pallas_call(
        _route_kernel,
        grid=(t // tm,),
        in_specs=[pl.BlockSpec((tm, d), lambda i: (i, 0)),
                  pl.BlockSpec((1, d), lambda i: (0, 0)),
                  pl.BlockSpec((d, nq), lambda i: (0, 0)),
                  pl.BlockSpec((PEER_HEADS, 2, N_KEYS, QUERY_HALF), lambda i: (0, 0, 0, 0))],
        out_specs=[pl.BlockSpec((tm, d), lambda i: (i, 0)),
                   pl.BlockSpec((tm, SLOTS), lambda i: (i, 0)),
                   pl.BlockSpec((tm, SLOTS), lambda i: (i, 0))],
        out_shape=[jax.ShapeDtypeStruct((t, d), F32),
                   jax.ShapeDtypeStruct((t, SLOTS), jnp.int32),
                   jax.ShapeDtypeStruct((t, SLOTS), F32)],
        compiler_params=pltpu.CompilerParams(dimension_semantics=("arbitrary",), vmem_limit_bytes=VMEM_LIMIT),
        name="peer_route",
    )(x, gain.reshape(1, d), w_query.astype(BF16), sub_keys.astype(BF16))


def _pack_table(tab):
    bits = lax.bitcast_convert_type(tab.astype(BF16), jnp.uint16).astype(jnp.uint32)
    half = D_MODEL // 2
    words = (bits[:, :half] << 16) | bits[:, half:]
    return lax.bitcast_convert_type(words, jnp.int32).reshape(tab.shape[0] * ROW_WORDS, LANES)


def _load_table_once(tab_hbm, tab, sem):
    @pl.when(pl.program_id(0) == 0)
    def _():
        cp = pltpu.make_async_copy(tab_hbm, tab, sem)
        cp.start()
        cp.wait()


def _gather_pair(tab, off_a, off_b):
    ra = tab[pl.ds(pl.multiple_of(off_a, ROW_WORDS), ROW_WORDS), :]
    rb = tab[pl.ds(pl.multiple_of(off_b, ROW_WORDS), ROW_WORDS), :]
    words = jnp.concatenate([ra, rb], axis=0)
    hi = pltpu.bitcast(words & jnp.int32(-65536), F32)
    lo = pltpu.bitcast(words << 16, F32)
    return hi, lo


TOKEN_UNROLL = 8


def _expert_in_kernel(*refs):
    idx_refs = refs[:TOKEN_UNROLL]
    h_ref, gate_ref, tab_hbm, w_ref, tab, sem, a_ref = refs[TOKEN_UNROLL:]
    _load_table_once(tab_hbm, tab, sem)
    sub = lax.broadcasted_iota(jnp.int32, (SUBLANES, LANES), 0)
    lane = lax.broadcasted_iota(jnp.int32, (SUBLANES, LANES), 1)
    own_half = (sub >= ROW_WORDS) == ((lane & 1) == 1)
    tb = h_ref.shape[0]

    def step(i, carry):
        xs = []
        for u in range(TOKEN_UNROLL):
            x = h_ref[i * TOKEN_UNROLL + u]
            xs.append((jnp.concatenate([x[0:ROW_WORDS], x[0:ROW_WORDS]], axis=0),
                       jnp.concatenate([x[ROW_WORDS:], x[ROW_WORDS:]], axis=0)))
        accs = [jnp.zeros((SUBLANES, LANES), F32) for _ in range(TOKEN_UNROLL)]
        for p in range(SLOTS // 2):
            for u in range(TOKEN_UNROLL):
                hi, lo = _gather_pair(tab, idx_refs[u][i, 2 * p], idx_refs[u][i, 2 * p + 1])
                part = jnp.sum(hi * xs[u][0] + lo * xs[u][1], axis=1, keepdims=True)
                accs[u] = jnp.where((lane >> 1) == p, part, accs[u])
        for u in range(TOKEN_UNROLL):
            a_ref[pl.ds(i * TOKEN_UNROLL + u, 1), :] = jnp.sum(jnp.where(own_half, accs[u], 0.0),
                                                                axis=0, keepdims=True)
        return carry

    lax.fori_loop(0, tb // TOKEN_UNROLL, step, 0)
    a = a_ref[...]
    w_ref[...] = gate_ref[...] * (0.5 * a * (1.0 + lax.erf(a * (1.0 / math.sqrt(2.0)))))


def _expert_out_kernel(*refs):
    idx_refs = refs[:TOKEN_UNROLL]
    w_ref, x_ref, tab_hbm, o_ref, tab, sem, wb_ref = refs[TOKEN_UNROLL:]
    _load_table_once(tab_hbm, tab, sem)
    sub = lax.broadcasted_iota(jnp.int32, (SUBLANES, LANES), 0)
    lower = sub < ROW_WORDS
    tb = x_ref.shape[0]

    def step(i, carry):
        for u in range(TOKEN_UNROLL):
            row = w_ref[pl.ds(i * TOKEN_UNROLL + u, 1), :]
            wb_ref[u] = jnp.broadcast_to(row, (SLOTS, LANES)).T
        acc_h = [jnp.zeros((SUBLANES, LANES), F32) for _ in range(TOKEN_UNROLL)]
        acc_l = [jnp.zeros((SUBLANES, LANES), F32) for _ in range(TOKEN_UNROLL)]
        for p in range(SLOTS // 2):
            ka, kb = 2 * p, 2 * p + 1
            for u in range(TOKEN_UNROLL):
                hi, lo = _gather_pair(tab, idx_refs[u][i, ka], idx_refs[u][i, kb])
                wa = jnp.broadcast_to(wb_ref[u, ka:ka + 1, :], (SUBLANES, LANES))
                wb = jnp.broadcast_to(wb_ref[u, kb:kb + 1, :], (SUBLANES, LANES))
                wt = jnp.where(lower, wa, wb)
                acc_h[u] = acc_h[u] + hi * wt
                acc_l[u] = acc_l[u] + lo * wt
        for u in range(TOKEN_UNROLL):
            t = i * TOKEN_UNROLL + u
            ah = acc_h[u] + pltpu.roll(acc_h[u], ROW_WORDS, 0)
            al = acc_l[u] + pltpu.roll(acc_l[u], ROW_WORDS, 0)
            o_ref[t] = x_ref[t] + jnp.where(lower, ah, al)
        return carry

    lax.fori_loop(0, tb // TOKEN_UNROLL, step, 0)


def _expert_stage_tc(x, h, idx, gate, u_packed, v_packed, t_tc, tb=128):
    t, d = x.shape
    rows = d // LANES
    un = TOKEN_UNROLL
    idx_split = idx.reshape(t // un, un, SLOTS).transpose(1, 0, 2)
    smem_blks = [pl.BlockSpec((None, tb // un, SLOTS), functools.partial(lambda u, i: (u, i, 0), u),
                              memory_space=pltpu.SMEM) for u in range(un)]
    vmem_blk = pl.BlockSpec((tb, SLOTS), lambda i: (i, 0))
    tok_blk = pl.BlockSpec((tb, rows, LANES), lambda i: (i, 0, 0))
    table_scratch = [pltpu.VMEM(u_packed.shape, jnp.int32), pltpu.SemaphoreType.DMA]
    params = pltpu.CompilerParams(dimension_semantics=("arbitrary",), vmem_limit_bytes=VMEM_LIMIT)
    w = pl.pallas_call(
        _expert_in_kernel,
        grid=(t // tb,),
        in_specs=smem_blks + [tok_blk, vmem_blk, pl.BlockSpec(memory_space=pl.ANY)],
        out_specs=vmem_blk,
        out_shape=jax.ShapeDtypeStruct((t, SLOTS), F32),
        scratch_shapes=table_scratch + [pltpu.VMEM((tb, SLOTS), F32)],
        compiler_params=params,
        name="peer_expert_in",
    )(*([idx_split] * un), h.reshape(t, rows, LANES), gate, u_packed)
    out = pl.pallas_call(
        _expert_out_kernel,
        grid=(t_tc // tb,),
        in_specs=smem_blks + [vmem_blk, tok_blk, pl.BlockSpec(memory_space=pl.ANY)],
        out_specs=tok_blk,
        out_shape=jax.ShapeDtypeStruct((t_tc, rows, LANES), F32),
        scratch_shapes=table_scratch + [pltpu.VMEM((un, SLOTS, LANES), F32)],
        compiler_params=params,
        name="peer_expert_out",
    )(*([idx_split] * un), w, x.reshape(t, rows, LANES), v_packed)
    return w, out.reshape(t_tc, d)


SC_LANES = 16
SC_WORKERS = 32
SC_CHUNK = 32
SC_TOKENS = 4096


def _sc_params():
    cp = pltpu.CompilerParams()
    if "needs_layout_passes" in pltpu.CompilerParams.__dataclass_fields__:
        cp = dataclasses.replace(cp, needs_layout_passes=False)
    return cp


def _sc_expert_out(table_words, idx, w, x):
    n_tok, d = x.shape
    per = n_tok // SC_WORKERS
    words = d // 2
    nq = words // SC_LANES // 2
    nchunk = SLOTS // SC_CHUNK
    mesh = plsc.VectorSubcoreMesh(core_axis_name="c", subcore_axis_name="s")

    @functools.partial(
        pl.kernel, mesh=mesh,
        out_type=jax.ShapeDtypeStruct((n_tok, d), F32),
        scratch_types=[pltpu.VMEM((SLOTS,), jnp.int32), pltpu.VMEM((SLOTS,), F32),
                       pltpu.VMEM((2, SC_CHUNK, words), jnp.int32), pltpu.VMEM((d,), F32),
                       pltpu.SemaphoreType.DMA, pltpu.SemaphoreType.DMA],
        compiler_params=_sc_params(),
        name="peer_expert_out_sc",
    )
    def body(tab_hbm, idx_hbm, w_hbm, x_hbm, o_hbm, idx_v, w_v, rows_v, y_v, sem0, sem1):
        base = (lax.axis_index("s") * 2 + lax.axis_index("c")) * per
        zero = jnp.zeros((SC_LANES,), jnp.int32)
        himask = jnp.full((SC_LANES,), -65536, jnp.int32)
        sems = (sem0, sem1)

        def gather(c):
            return pltpu.make_async_copy(tab_hbm.at[idx_v.at[pl.ds(c * SC_CHUNK, SC_CHUNK)]],
                                         rows_v.at[c % 2], sems[c % 2])

        @pl.loop(0, per)
        def _(i):
            t = base + i
            pltpu.sync_copy(idx_hbm.at[t], idx_v)
            pltpu.sync_copy(w_hbm.at[t], w_v)
            pltpu.sync_copy(x_hbm.at[t], y_v)
            gather(0).start()
            for c in range(nchunk):
                if c + 1 < nchunk:
                    gather(c + 1).start()
                gather(c).wait()
                for q in range(2):
                    first = q * nq
                    acc0 = (tuple(y_v[pl.ds((first + j) * SC_LANES, SC_LANES)] for j in range(nq))
                            + tuple(y_v[pl.ds(words + (first + j) * SC_LANES, SC_LANES)] for j in range(nq)))

                    def row_body(r, accs):
                        ws = plsc.load_gather(w_v, [zero + (c * SC_CHUNK + r)])
                        hi_acc, lo_acc = [], []
                        for j in range(nq):
                            wv = rows_v[c % 2, r, pl.ds((first + j) * SC_LANES, SC_LANES)]
                            hi_acc.append(accs[j] + plsc.bitcast(wv & himask, F32) * ws)
                            lo_acc.append(accs[nq + j] + plsc.bitcast(wv << 16, F32) * ws)
                        return tuple(hi_acc) + tuple(lo_acc)

                    accs = lax.fori_loop(0, SC_CHUNK, row_body, acc0)
                    for j in range(nq):
                        y_v[pl.ds((first + j) * SC_LANES, SC_LANES)] = accs[j]
                        y_v[pl.ds(words + (first + j) * SC_LANES, SC_LANES)] = accs[nq + j]
            pltpu.sync_copy(y_v, o_hbm.at[t])

    return body(table_words, idx, w, x)


def _peer(x, gain, w_query, sub_keys, expert_u, expert_v):
    h, idx, gate = _route(x, gain, w_query, sub_keys)
    t_tc = x.shape[0] - SC_TOKENS
    u_packed, v_packed = _pack_table(expert_u), _pack_table(expert_v)
    w, out_tc = _expert_stage_tc(x, h, idx, gate, u_packed, v_packed, t_tc)
    v_rows = v_packed.reshape(expert_v.shape[0], ROW_WORDS * LANES)
    out_sc = _sc_expert_out(v_rows, idx[t_tc:] // ROW_WORDS, w[t_tc:], x[t_tc:])
    return jnp.concatenate([out_tc, out_sc], axis=0)


def kernel(x, norm_mix, norm_ffn, attn_w_qkv, attn_q_norm, attn_k_norm, attn_sinks, attn_w_o, conv_w_in, conv_w, conv_w_out, peer_w_query, peer_sub_keys, peer_u, peer_v):
    batch, seq, d = x.shape
    xt = x.reshape(batch * seq, d)
    for i in range(norm_mix.shape[0]):
        j = i // 2
        if i % 2 == 0:
            qkv = _norm_matmul(xt, norm_mix[i], attn_w_qkv[j].astype(BF16))
            o = _attention(qkv, attn_q_norm[j], attn_k_norm[j], attn_sinks[j], batch, seq)
            xt = _matmul_residual(o, attn_w_o[j].astype(BF16), xt)
        else:
            xt = _conv_mixer(xt, norm_mix[i], conv_w_in[j], conv_w[j], conv_w_out[j], batch, seq)
        xt = _peer(xt, norm_ffn[i], peer_w_query[i], peer_sub_keys[i], peer_u[i], peer_v[i])
    return xt.reshape(batch, seq, d)
```

```python
import dataclasses
import functools
import math

import jax
import jax.numpy as jnp
from jax import lax
from jax.experimental import pallas as pl
from jax.experimental.pallas import tpu as pltpu
from jax.experimental.pallas import tpu_sc as plsc

D_MODEL = 1024
RMS_EPS = 1e-6

HEAD_DIM = 64
N_Q_HEADS = 16
N_KV_HEADS = 4
GROUP = N_Q_HEADS // N_KV_HEADS
WINDOW = 128
ROT_DIM = HEAD_DIM // 4
ROPE_THETA = 500000.0
Q_COLS = N_Q_HEADS * HEAD_DIM
KV_COLS = N_KV_HEADS * HEAD_DIM
NEG_INF = -1e30

CONV_WIDTH = 3

PEER_HEADS = 8
N_KEYS = 128
N_EXPERTS = N_KEYS * N_KEYS
PEER_TOPK = 16
QUERY_HALF = 128
SLOTS = PEER_HEADS * PEER_TOPK

LANES = 128
SUBLANES = 8
ROW_WORDS = D_MODEL // 2 // LANES
VMEM_LIMIT = 48 * 1024 * 1024

BF16 = jnp.bfloat16
F32 = jnp.float32


def _rms(x, gain):
    return x * lax.rsqrt(jnp.mean(x * x, axis=-1, keepdims=True) + RMS_EPS) * gain


def _norm_matmul_kernel(x_ref, g_ref, w_ref, o_ref):
    h = _rms(x_ref[...], g_ref[...])
    o_ref[...] = jnp.dot(h.astype(BF16), w_ref[...], preferred_element_type=F32)


def _norm_matmul(x, gain, w, tm=512):
    t, d = x.shape
    n = w.shape[1]
    return pl.pallas_call(
        _norm_matmul_kernel,
        grid=(t // tm,),
        in_specs=[pl.BlockSpec((tm, d), lambda i: (i, 0)),
                  pl.BlockSpec((1, d), lambda i: (0, 0)),
                  pl.BlockSpec((d, n), lambda i: (0, 0))],
        out_specs=pl.BlockSpec((tm, n), lambda i: (i, 0)),
        out_shape=jax.ShapeDtypeStruct((t, n), F32),
        compiler_params=pltpu.CompilerParams(dimension_semantics=("arbitrary",), vmem_limit_bytes=VMEM_LIMIT),
        name="norm_matmul",
    )(x, gain.reshape(1, d), w)


def _matmul_residual_kernel(a_ref, w_ref, r_ref, o_ref):
    o_ref[...] = r_ref[...] + jnp.dot(a_ref[...].astype(BF16), w_ref[...], preferred_element_type=F32)


def _matmul_residual(a, w, res, tm=512):
    t, k = a.shape
    n = w.shape[1]
    return pl.pallas_call(
        _matmul_residual_kernel,
        grid=(t // tm,),
        in_specs=[pl.BlockSpec((tm, k), lambda i: (i, 0)),
                  pl.BlockSpec((k, n), lambda i: (0, 0)),
                  pl.BlockSpec((tm, n), lambda i: (i, 0))],
        out_specs=pl.BlockSpec((tm, n), lambda i: (i, 0)),
        out_shape=jax.ShapeDtypeStruct((t, n), F32),
        compiler_params=pltpu.CompilerParams(dimension_semantics=("arbitrary",), vmem_limit_bytes=VMEM_LIMIT),
        name="matmul_residual",
    )(a, w, res)


def _rope_tables(seq):
    half = ROT_DIM // 2
    freqs = ROPE_THETA ** (-jnp.arange(0, ROT_DIM, 2, dtype=F32) / ROT_DIM)
    ang = jnp.arange(seq, dtype=F32)[:, None] * freqs[None, :]
    cos, sin = jnp.cos(ang), jnp.sin(ang)
    ones = jnp.ones((seq, HEAD_DIM - ROT_DIM), F32)
    zeros = jnp.zeros((seq, HEAD_DIM - ROT_DIM), F32)
    zh = jnp.zeros((seq, half), F32)
    c = jnp.concatenate([cos, cos, ones], axis=1)
    s_next = jnp.concatenate([-sin, zh, zeros], axis=1)
    s_prev = jnp.concatenate([zh, sin, zeros], axis=1)
    return jnp.stack([jnp.tile(c, (1, 2)), jnp.tile(s_next, (1, 2)), jnp.tile(s_prev, (1, 2))])


def _head_norm_rope(x, gain2, rope, lo):
    sq = x * x
    s_lo = jnp.sum(jnp.where(lo, sq, 0.0), axis=1, keepdims=True)
    s_hi = jnp.sum(jnp.where(lo, 0.0, sq), axis=1, keepdims=True)
    ms = jnp.where(lo, s_lo, s_hi) * (1.0 / HEAD_DIM)
    xn = x * lax.rsqrt(ms + RMS_EPS) * gain2
    half = ROT_DIM // 2
    return xn * rope[0] + pltpu.roll(xn, LANES - half, 1) * rope[1] + pltpu.roll(xn, half, 1) * rope[2]


def _attn_kernel(sinks_ref, q_ref, kc_ref, kp_ref, vc_ref, vp_ref, rc_ref, rp_ref, qg_ref, kg_ref, o_ref):
    n = pl.program_id(1)
    lo = lax.broadcasted_iota(jnp.int32, (WINDOW, LANES), 1) < HEAD_DIM
    rope_c = rc_ref[...]
    rope_p = rp_ref[...]
    qg = qg_ref[...]
    kg = kg_ref[...]

    rows = GROUP * WINDOW
    qi = lax.broadcasted_iota(jnp.int32, (rows, 2 * WINDOW), 0) & (WINDOW - 1)
    ki = lax.broadcasted_iota(jnp.int32, (rows, 2 * WINDOW), 1)
    rel = WINDOW + qi - ki
    valid = (rel >= 0) & (rel < WINDOW) & ((n > 0) | (ki >= WINDOW))
    head_of_row = lax.broadcasted_iota(jnp.int32, (rows, 1), 0) // WINDOW
    scale = 1.0 / math.sqrt(HEAD_DIM)

    q2 = [_head_norm_rope(q_ref[:, c * LANES:(c + 1) * LANES], qg, rope_c, lo).astype(BF16)
          for c in range(Q_COLS // LANES)]
    for c in range(KV_COLS // LANES):
        cols = slice(c * LANES, (c + 1) * LANES)
        kprev = _head_norm_rope(kp_ref[:, cols], kg, rope_p, lo)
        kcur = _head_norm_rope(kc_ref[:, cols], kg, rope_c, lo)
        kfull = jnp.concatenate([kprev, kcur], axis=0).astype(BF16)
        vfull = jnp.concatenate([vp_ref[:, cols], vc_ref[:, cols]], axis=0).astype(BF16)
        for hh in range(LANES // HEAD_DIM):
            h = (LANES // HEAD_DIM) * c + hh
            kh = kfull[:, hh * HEAD_DIM:(hh + 1) * HEAD_DIM]
            vh = vfull[:, hh * HEAD_DIM:(hh + 1) * HEAD_DIM]
            heads = [GROUP * h + g for g in range(GROUP)]
            q4 = jnp.concatenate([q2[j // 2][:, (j % 2) * HEAD_DIM:(j % 2 + 1) * HEAD_DIM] for j in heads], axis=0)
            s = lax.dot_general(q4, kh, (((1,), (1,)), ((), ())), preferred_element_type=F32) * scale
            s = jnp.where(valid, s, NEG_INF)
            sink = jnp.zeros((rows, 1), F32)
            for g, j in enumerate(heads):
                sink = jnp.where(head_of_row == g, sinks_ref[j], sink)
            m = jnp.maximum(jnp.max(s, axis=1, keepdims=True), sink)
            p = jnp.exp(s - m)
            denom = jnp.sum(p, axis=1, keepdims=True) + jnp.exp(sink - m)
            o = jnp.dot(p.astype(BF16), vh, preferred_element_type=F32) / denom
            for g, j in enumerate(heads):
                o_ref[:, j * HEAD_DIM:(j + 1) * HEAD_DIM] = o[g * WINDOW:(g + 1) * WINDOW]


def _attention(qkv, q_gain, k_gain, sinks, batch, seq):
    t = batch * seq
    nb = seq // WINDOW
    rope = _rope_tables(seq)
    kcol = Q_COLS // KV_COLS
    cur = lambda b, n: (b * nb + n, 0)
    kcur = lambda b, n: (b * nb + n, kcol)
    kprev = lambda b, n: (b * nb + jnp.maximum(n - 1, 0), kcol)
    vcur = lambda b, n: (b * nb + n, kcol + 1)
    vprev = lambda b, n: (b * nb + jnp.maximum(n - 1, 0), kcol + 1)
    return pl.pallas_call(
        _attn_kernel,
        grid=(batch, nb),
        in_specs=[pl.BlockSpec(memory_space=pltpu.SMEM),
                  pl.BlockSpec((WINDOW, Q_COLS), cur),
                  pl.BlockSpec((WINDOW, KV_COLS), kcur),
                  pl.BlockSpec((WINDOW, KV_COLS), kprev),
                  pl.BlockSpec((WINDOW, KV_COLS), vcur),
                  pl.BlockSpec((WINDOW, KV_COLS), vprev),
                  pl.BlockSpec((3, WINDOW, LANES), lambda b, n: (0, n, 0)),
                  pl.BlockSpec((3, WINDOW, LANES), lambda b, n: (0, jnp.maximum(n - 1, 0), 0)),
                  pl.BlockSpec((1, LANES), lambda b, n: (0, 0)),
                  pl.BlockSpec((1, LANES), lambda b, n: (0, 0))],
        out_specs=pl.BlockSpec((WINDOW, Q_COLS), cur),
        out_shape=jax.ShapeDtypeStruct((t, Q_COLS), F32),
        compiler_params=pltpu.CompilerParams(dimension_semantics=("arbitrary", "arbitrary"),
                                             vmem_limit_bytes=VMEM_LIMIT),
        name="swa_attention",
    )(sinks, qkv, qkv, qkv, qkv, qkv, rope, rope,
      jnp.tile(q_gain, 2).reshape(1, LANES), jnp.tile(k_gain, 2).reshape(1, LANES))


def _conv_kernel(x_ref, g_ref, win_ref, cw_ref, wout_ref, o_ref, zprev_ref):
    n = pl.program_id(1)
    d = D_MODEL

    @pl.when(n == 0)
    def _():
        zprev_ref[...] = jnp.zeros_like(zprev_ref)

    x = x_ref[...]
    h = _rms(x, g_ref[...])
    bcu = jnp.dot(h.astype(BF16), win_ref[...], preferred_element_type=F32)
    gate_b = bcu[:, :d]
    z = bcu[:, d:2 * d] * bcu[:, 2 * d:]
    tm = z.shape[0]
    row = lax.broadcasted_iota(jnp.int32, z.shape, 0)
    prev = zprev_ref[...]
    p_last = prev[SUBLANES - 1:SUBLANES, :]
    p_last2 = prev[SUBLANES - 2:SUBLANES - 1, :]
    z1 = jnp.where(row == 0, p_last, pltpu.roll(z, 1, 0))
    z2 = jnp.where(row == 0, p_last2, jnp.where(row == 1, p_last, pltpu.roll(z, 2, 0)))
    cw = cw_ref[...]
    conv = cw[0:1, :] * z2 + cw[1:2, :] * z1 + cw[2:3, :] * z
    zprev_ref[...] = z[tm - SUBLANES:, :]
    o_ref[...] = x + jnp.dot((gate_b * conv).astype(BF16), wout_ref[...], preferred_element_type=F32)


def _conv_mixer(x, gain, w_in, conv_w, w_out, batch, seq, tm=256):
    t, d = x.shape
    nblk = seq // tm
    blk = lambda b, n: (b * nblk + n, 0)
    const = lambda b, n: (0, 0)
    return pl.pallas_call(
        _conv_kernel,
        grid=(batch, nblk),
        in_specs=[pl.BlockSpec((tm, d), blk),
                  pl.BlockSpec((1, d), const),
                  pl.BlockSpec((d, 3 * d), const),
                  pl.BlockSpec((CONV_WIDTH, d), const),
                  pl.BlockSpec((d, d), const)],
        out_specs=pl.BlockSpec((tm, d), blk),
        out_shape=jax.ShapeDtypeStruct((t, d), F32),
        scratch_shapes=[pltpu.VMEM((SUBLANES, d), F32)],
        compiler_params=pltpu.CompilerParams(dimension_semantics=("arbitrary", "arbitrary"),
                                             vmem_limit_bytes=VMEM_LIMIT),
        name="conv_mixer",
    )(x, gain.reshape(1, d), w_in.astype(BF16), conv_w, w_out.astype(BF16))


def _topk_axis0(s, k, ids=None, payload=None):
    n, tm = s.shape
    if ids is None:
        ids = lax.broadcasted_iota(jnp.int32, (n, tm), 0)
    krow = lax.broadcasted_iota(jnp.int32, (k, tm), 0)
    vals = jnp.zeros((k, tm), F32)
    picks = jnp.zeros((k, tm), jnp.int32)
    for r in range(k):
        m = jnp.max(s, axis=0, keepdims=True)
        pos = jnp.min(jnp.where(s == m, ids, jnp.iinfo(jnp.int32).max), axis=0, keepdims=True)
        sel = ids == pos
        if payload is None:
            picked = pos
        else:
            picked = jnp.sum(jnp.where(sel, payload, 0), axis=0, keepdims=True)
        vals = jnp.where(krow == r, m, vals)
        picks = jnp.where(krow == r, picked, picks)
        s = jnp.where(sel, -jnp.inf, s)
    return vals, picks


def _pair_candidates(s1, i1, s2, i2):
    k, tm = s1.shape
    sub = lax.broadcasted_iota(jnp.int32, (SUBLANES, tm), 0)
    scores, flat, expert = [], [], []
    for i in range(k // 2):
        width = k if i == 0 else SUBLANES
        sc = s1[i:i + 1, :] + s2[0:width, :]
        ex = i1[i:i + 1, :] * N_KEYS + i2[0:width, :]
        fl = i * k + lax.broadcasted_iota(jnp.int32, (width, tm), 0)
        reach = k // (i + 1)
        if reach < width:
            sc = jnp.where(sub < reach, sc, -jnp.inf)
        scores.append(sc)
        flat.append(fl)
        expert.append(ex)
    scores.append(s1[k // 2:, :] + s2[0:1, :])
    expert.append(i1[k // 2:, :] * N_KEYS + i2[0:1, :])
    flat.append((k // 2 + sub) * k)
    return jnp.concatenate(scores, axis=0), jnp.concatenate(flat, axis=0), jnp.concatenate(expert, axis=0)


def _route_kernel(x_ref, g_ref, wq_ref, keys_ref, h_ref, idx_ref, gate_ref):
    h = _rms(x_ref[...], g_ref[...])
    h_ref[...] = h
    q = jnp.dot(h.astype(BF16), wq_ref[...], preferred_element_type=F32).astype(BF16)
    idx_rows, gate_rows = [], []
    for head in range(PEER_HEADS):
        tops = []
        for part in range(2):
            col = (head * 2 + part) * QUERY_HALF
            s = lax.dot_general(keys_ref[head, part], q[:, col:col + QUERY_HALF],
                                (((1,), (1,)), ((), ())), preferred_element_type=F32)
            tops.append(_topk_axis0(s, PEER_TOPK))
        (s1, i1), (s2, i2) = tops
        cand, flat_ids, cand_idx = _pair_candidates(s1, i1, s2, i2)
        g_s, e_idx = _topk_axis0(cand, PEER_TOPK, ids=flat_ids, payload=cand_idx)
        e = jnp.exp(g_s - jnp.max(g_s, axis=0, keepdims=True))
        gate_rows.append(e / jnp.sum(e, axis=0, keepdims=True))
        idx_rows.append(e_idx * ROW_WORDS)
    idx_ref[...] = jnp.concatenate(idx_rows, axis=0).T
    gate_ref[...] = jnp.concatenate(gate_rows, axis=0).T


def _route(x, gain, w_query, sub_keys, tm=128):
    t, d = x.shape
    nq = w_query.shape[1]
    return pl.pallas_call(
        _route_kernel,
        grid=(t // tm,),
        in_specs=[pl.BlockSpec((tm, d), lambda i: (i, 0)),
                  pl.BlockSpec((1, d), lambda i: (0, 0)),
                  pl.BlockSpec((d, nq), lambda i: (0, 0)),
                  pl.BlockSpec((PEER_HEADS, 2, N_KEYS, QUERY_HALF), lambda i: (0, 0, 0, 0))],
        out_specs=[pl.BlockSpec((tm, d), lambda i: (i, 0)),
                   pl.BlockSpec((tm, SLOTS), lambda i: (i, 0)),
                   pl.BlockSpec((tm, SLOTS), lambda i: (i, 0))],
        out_shape=[jax.ShapeDtypeStruct((t, d), F32),
                   jax.ShapeDtypeStruct((t, SLOTS), jnp.int32),
                   jax.ShapeDtypeStruct((t, SLOTS), F32)],
        compiler_params=pltpu.CompilerParams(dimension_semantics=("arbitrary",), vmem_limit_bytes=VMEM_LIMIT),
        name="peer_route",
    )(x, gain.reshape(1, d), w_query.astype(BF16), sub_keys.astype(BF16))


def _pack_table(tab):
    bits = lax.bitcast_convert_type(tab.astype(BF16), jnp.uint16).astype(jnp.uint32)
    half = D_MODEL // 2
    words = (bits[:, :half] << 16) | bits[:, half:]
    return lax.bitcast_convert_type(words, jnp.int32).reshape(tab.shape[0] * ROW_WORDS, LANES)


def _load_table_once(tab_hbm, tab, sem):
    @pl.when(pl.program_id(0) == 0)
    def _():
        cp = pltpu.make_async_copy(tab_hbm, tab, sem)
        cp.start()
        cp.wait()


def _gather_pair(tab, off_a, off_b):
    ra = tab[pl.ds(pl.multiple_of(off_a, ROW_WORDS), ROW_WORDS), :]
    rb = tab[pl.ds(pl.multiple_of(off_b, ROW_WORDS), ROW_WORDS), :]
    words = jnp.concatenate([ra, rb], axis=0)
    hi = pltpu.bitcast(words & jnp.int32(-65536), F32)
    lo = pltpu.bitcast(words << 16, F32)
    return hi, lo


TOKEN_UNROLL = 8


def _expert_in_kernel(*refs):
    idx_refs = refs[:TOKEN_UNROLL]
    h_ref, gate_ref, tab_hbm, w_ref, tab, sem, a_ref = refs[TOKEN_UNROLL:]
    _load_table_once(tab_hbm, tab, sem)
    sub = lax.broadcasted_iota(jnp.int32, (SUBLANES, LANES), 0)
    lane = lax.broadcasted_iota(jnp.int32, (SUBLANES, LANES), 1)
    own_half = (sub >= ROW_WORDS) == ((lane & 1) == 1)
    tb = h_ref.shape[0]

    def step(i, carry):
        xs = []
        for u in range(TOKEN_UNROLL):
            x = h_ref[i * TOKEN_UNROLL + u]
            xs.append((jnp.concatenate([x[0:ROW_WORDS], x[0:ROW_WORDS]], axis=0),
                       jnp.concatenate([x[ROW_WORDS:], x[ROW_WORDS:]], axis=0)))
        accs = [jnp.zeros((SUBLANES, LANES), F32) for _ in range(TOKEN_UNROLL)]
        for p in range(SLOTS // 2):
            for u in range(TOKEN_UNROLL):
                hi, lo = _gather_pair(tab, idx_refs[u][i, 2 * p], idx_refs[u][i, 2 * p + 1])
                part = jnp.sum(hi * xs[u][0] + lo * xs[u][1], axis=1, keepdims=True)
                accs[u] = jnp.where((lane >> 1) == p, part, accs[u])
        for u in range(TOKEN_UNROLL):
            a_ref[pl.ds(i * TOKEN_UNROLL + u, 1), :] = jnp.sum(jnp.where(own_half, accs[u], 0.0),
                                                                axis=0, keepdims=True)
        return carry

    lax.fori_loop(0, tb // TOKEN_UNROLL, step, 0)
    a = a_ref[...]
    w_ref[...] = gate_ref[...] * (0.5 * a * (1.0 + lax.erf(a * (1.0 / math.sqrt(2.0)))))


def _expert_out_kernel(*refs):
    idx_refs = refs[:TOKEN_UNROLL]
    w_ref, x_ref, tab_hbm, o_ref, tab, sem, wb_ref = refs[TOKEN_UNROLL:]
    _load_table_once(tab_hbm, tab, sem)
    sub = lax.broadcasted_iota(jnp.int32, (SUBLANES, LANES), 0)
    lower = sub < ROW_WORDS
    tb = x_ref.shape[0]

    def step(i, carry):
        for u in range(TOKEN_UNROLL):
            row = w_ref[pl.ds(i * TOKEN_UNROLL + u, 1), :]
            wb_ref[u] = jnp.broadcast_to(row, (SLOTS, LANES)).T
        acc_h = [jnp.zeros((SUBLANES, LANES), F32) for _ in range(TOKEN_UNROLL)]
        acc_l = [jnp.zeros((SUBLANES, LANES), F32) for _ in range(TOKEN_UNROLL)]
        for p in range(SLOTS // 2):
            ka, kb = 2 * p, 2 * p + 1
            for u in range(TOKEN_UNROLL):
                hi, lo = _gather_pair(tab, idx_refs[u][i, ka], idx_refs[u][i, kb])
                wa = jnp.broadcast_to(wb_ref[u, ka:ka + 1, :], (SUBLANES, LANES))
                wb = jnp.broadcast_to(wb_ref[u, kb:kb + 1, :], (SUBLANES, LANES))
                wt = jnp.where(lower, wa, wb)
                acc_h[u] = acc_h[u] + hi * wt
                acc_l[u] = acc_l[u] + lo * wt
        for u in range(TOKEN_UNROLL):
            t = i * TOKEN_UNROLL + u
            ah = acc_h[u] + pltpu.roll(acc_h[u], ROW_WORDS, 0)
            al = acc_l[u] + pltpu.roll(acc_l[u], ROW_WORDS, 0)
            o_ref[t] = x_ref[t] + jnp.where(lower, ah, al)
        return carry

    lax.fori_loop(0, tb // TOKEN_UNROLL, step, 0)


def _expert_stage_tc(x, h, idx, gate, u_packed, v_packed, t_tc, tb=128):
    t, d = x.shape
    rows = d // LANES
    un = TOKEN_UNROLL
    idx_split = idx.reshape(t // un, un, SLOTS).transpose(1, 0, 2)
    smem_blks = [pl.BlockSpec((None, tb // un, SLOTS), functools.partial(lambda u, i: (u, i, 0), u),
                              memory_space=pltpu.SMEM) for u in range(un)]
    vmem_blk = pl.BlockSpec((tb, SLOTS), lambda i: (i, 0))
    tok_blk = pl.BlockSpec((tb, rows, LANES), lambda i: (i, 0, 0))
    table_scratch = [pltpu.VMEM(u_packed.shape, jnp.int32), pltpu.SemaphoreType.DMA]
    params = pltpu.CompilerParams(dimension_semantics=("arbitrary",), vmem_limit_bytes=VMEM_LIMIT)
    w = pl.pallas_call(
        _expert_in_kernel,
        grid=(t // tb,),
        in_specs=smem_blks + [tok_blk, vmem_blk, pl.BlockSpec(memory_space=pl.ANY)],
        out_specs=vmem_blk,
        out_shape=jax.ShapeDtypeStruct((t, SLOTS), F32),
        scratch_shapes=table_scratch + [pltpu.VMEM((tb, SLOTS), F32)],
        compiler_params=params,
        name="peer_expert_in",
    )(*([idx_split] * un), h.reshape(t, rows, LANES), gate, u_packed)
    out = pl.pallas_call(
        _expert_out_kernel,
        grid=(t_tc // tb,),
        in_specs=smem_blks + [vmem_blk, tok_blk, pl.BlockSpec(memory_space=pl.ANY)],
        out_specs=tok_blk,
        out_shape=jax.ShapeDtypeStruct((t_tc, rows, LANES), F32),
        scratch_shapes=table_scratch + [pltpu.VMEM((un, SLOTS, LANES), F32)],
        compiler_params=params,
        name="peer_expert_out",
    )(*([idx_split] * un), w, x.reshape(t, rows, LANES), v_packed)
    return w, out.reshape(t_tc, d)


SC_LANES = 16
SC_WORKERS = 32
SC_CHUNK = 32
SC_GROUP = 8
SC_TOKENS = 8960


def _sc_params():
    cp = pltpu.CompilerParams()
    if "needs_layout_passes" in pltpu.CompilerParams.__dataclass_fields__:
        cp = dataclasses.replace(cp, needs_layout_passes=False)
    return cp


def _sc_expert_out(table_words, idx, w, x):
    n_tok, d = x.shape
    per = n_tok // SC_WORKERS
    words = d // 2
    nq = words // SC_LANES // 2
    nchunk = SLOTS // SC_CHUNK
    group_chunks = SC_GROUP * nchunk
    mesh = plsc.VectorSubcoreMesh(core_axis_name="c", subcore_axis_name="s")

    @functools.partial(
        pl.kernel, mesh=mesh,
        out_type=jax.ShapeDtypeStruct((n_tok, d), F32),
        scratch_types=[pltpu.VMEM((SC_GROUP * SLOTS,), jnp.int32), pltpu.VMEM((SC_GROUP * SLOTS,), F32),
                       pltpu.VMEM((2, SC_CHUNK, words), jnp.int32), pltpu.VMEM((SC_GROUP, d), F32),
                       pltpu.SemaphoreType.DMA, pltpu.SemaphoreType.DMA],
        compiler_params=_sc_params(),
        name="peer_expert_out_sc",
    )
    def body(tab_hbm, idx_hbm, w_hbm, x_hbm, o_hbm, idx_v, w_v, rows_v, y_v, sem0, sem1):
        base = (lax.axis_index("s") * 2 + lax.axis_index("c")) * per
        zero = jnp.zeros((SC_LANES,), jnp.int32)
        himask = jnp.full((SC_LANES,), -65536, jnp.int32)
        sems = (sem0, sem1)

        def gather(k, b):
            off = pl.multiple_of(k * SC_CHUNK, SC_CHUNK)
            return pltpu.make_async_copy(tab_hbm.at[idx_v.at[pl.ds(off, SC_CHUNK)]], rows_v.at[b], sems[b])

        def accumulate(k, b):
            tok = k // nchunk
            for q in range(2):
                first = q * nq
                acc0 = (tuple(y_v[tok, pl.ds((first + j) * SC_LANES, SC_LANES)] for j in range(nq))
                        + tuple(y_v[tok, pl.ds(words + (first + j) * SC_LANES, SC_LANES)] for j in range(nq)))

                def row_body(r, accs):
                    ws = plsc.load_gather(w_v, [zero + (k * SC_CHUNK + r)])
                    hi_acc, lo_acc = [], []
                    for j in range(nq):
                        wv = rows_v[b, r, pl.ds((first + j) * SC_LANES, SC_LANES)]
                        hi_acc.append(accs[j] + plsc.bitcast(wv & himask, F32) * ws)
                        lo_acc.append(accs[nq + j] + plsc.bitcast(wv << 16, F32) * ws)
                    return tuple(hi_acc) + tuple(lo_acc)

                accs = lax.fori_loop(0, SC_CHUNK, row_body, acc0)
                for j in range(nq):
                    y_v[tok, pl.ds((first + j) * SC_LANES, SC_LANES)] = accs[j]
                    y_v[tok, pl.ds(words + (first + j) * SC_LANES, SC_LANES)] = accs[nq + j]

        @pl.loop(0, per // SC_GROUP)
        def _(g):
            t0 = pl.multiple_of(base + g * SC_GROUP, SC_GROUP)
            pltpu.sync_copy(idx_hbm.at[pl.ds(t0 * SLOTS, SC_GROUP * SLOTS)], idx_v)
            pltpu.sync_copy(w_hbm.at[pl.ds(t0 * SLOTS, SC_GROUP * SLOTS)], w_v)
            pltpu.sync_copy(x_hbm.at[pl.ds(t0, SC_GROUP)], y_v)
            gather(0, 0).start()

            @pl.loop(0, group_chunks // 2)
            def _(kk):
                k0 = 2 * kk
                gather(k0 + 1, 1).start()
                gather(k0, 0).wait()
                accumulate(k0, 0)

                @pl.when(k0 + 2 < group_chunks)
                def _():
                    gather(k0 + 2, 0).start()
                gather(k0 + 1, 1).wait()
                accumulate(k0 + 1, 1)

            pltpu.sync_copy(y_v, o_hbm.at[pl.ds(t0, SC_GROUP)])

    return body(table_words, idx, w, x)


def _peer(x, gain, w_query, sub_keys, expert_u, expert_v):
    h, idx, gate = _route(x, gain, w_query, sub_keys)
    t_tc = x.shape[0] - SC_TOKENS
    u_packed, v_packed = _pack_table(expert_u), _pack_table(expert_v)
    w, out_tc = _expert_stage_tc(x, h, idx, gate, u_packed, v_packed, t_tc)
    v_rows = v_packed.reshape(expert_v.shape[0], ROW_WORDS * LANES)
    out_sc = _sc_expert_out(v_rows, (idx[t_tc:] // ROW_WORDS).reshape(-1), w[t_tc:].reshape(-1), x[t_tc:])
    return jnp.concatenate([out_tc, out_sc], axis=0)


def kernel(x, norm_mix, norm_ffn, attn_w_qkv, attn_q_norm, attn_k_norm, attn_sinks, attn_w_o, conv_w_in, conv_w, conv_w_out, peer_w_query, peer_sub_keys, peer_u, peer_v):
    batch, seq, d = x.shape
    xt = x.reshape(batch * seq, d)
    for i in range(norm_mix.shape[0]):
        j = i // 2
        if i % 2 == 0:
            qkv = _norm_matmul(xt, norm_mix[i], attn_w_qkv[j].astype(BF16))
            o = _attention(qkv, attn_q_norm[j], attn_k_norm[j], attn_sinks[j], batch, seq)
            xt = _matmul_residual(o, attn_w_o[j].astype(BF16), xt)
        else:
            xt = _conv_mixer(xt, norm_mix[i], conv_w_in[j], conv_w[j], conv_w_out[j], batch, seq)
        xt = _peer(xt, norm_ffn[i], peer_w_query[i], peer_sub_keys[i], peer_u[i], peer_v[i])
    return xt.reshape(batch, seq, d)
```

```python
import dataclasses
import functools
import math

import jax
import jax.numpy as jnp
from jax import lax
from jax.experimental import pallas as pl
from jax.experimental.pallas import tpu as pltpu
from jax.experimental.pallas import tpu_sc as plsc

D_MODEL = 1024
RMS_EPS = 1e-6

HEAD_DIM = 64
N_Q_HEADS = 16
N_KV_HEADS = 4
GROUP = N_Q_HEADS // N_KV_HEADS
WINDOW = 128
ROT_DIM = HEAD_DIM // 4
ROPE_THETA = 500000.0
Q_COLS = N_Q_HEADS * HEAD_DIM
KV_COLS = N_KV_HEADS * HEAD_DIM
NEG_INF = -1e30

CONV_WIDTH = 3

PEER_HEADS = 8
N_KEYS = 128
N_EXPERTS = N_KEYS * N_KEYS
PEER_TOPK = 16
QUERY_HALF = 128
SLOTS = PEER_HEADS * PEER_TOPK

LANES = 128
SUBLANES = 8
ROW_WORDS = D_MODEL // 2 // LANES
VMEM_LIMIT = 48 * 1024 * 1024

BF16 = jnp.bfloat16
F32 = jnp.float32


def _rms(x, gain):
    return x * lax.rsqrt(jnp.mean(x * x, axis=-1, keepdims=True) + RMS_EPS) * gain


def _norm_matmul_kernel(x_ref, g_ref, w_ref, o_ref):
    h = _rms(x_ref[...], g_ref[...])
    o_ref[...] = jnp.dot(h.astype(BF16), w_ref[...], preferred_element_type=F32)


def _norm_matmul(x, gain, w, tm=512):
    t, d = x.shape
    n = w.shape[1]
    return pl.pallas_call(
        _norm_matmul_kernel,
        grid=(t // tm,),
        in_specs=[pl.BlockSpec((tm, d), lambda i: (i, 0)),
                  pl.BlockSpec((1, d), lambda i: (0, 0)),
                  pl.BlockSpec((d, n), lambda i: (0, 0))],
        out_specs=pl.BlockSpec((tm, n), lambda i: (i, 0)),
        out_shape=jax.ShapeDtypeStruct((t, n), F32),
        compiler_params=pltpu.CompilerParams(dimension_semantics=("arbitrary",), vmem_limit_bytes=VMEM_LIMIT),
        name="norm_matmul",
    )(x, gain.reshape(1, d), w)


def _matmul_residual_kernel(a_ref, w_ref, r_ref, o_ref):
    o_ref[...] = r_ref[...] + jnp.dot(a_ref[...].astype(BF16), w_ref[...], preferred_element_type=F32)


def _matmul_residual(a, w, res, tm=512):
    t, k = a.shape
    n = w.shape[1]
    return pl.pallas_call(
        _matmul_residual_kernel,
        grid=(t // tm,),
        in_specs=[pl.BlockSpec((tm, k), lambda i: (i, 0)),
                  pl.BlockSpec((k, n), lambda i: (0, 0)),
                  pl.BlockSpec((tm, n), lambda i: (i, 0))],
        out_specs=pl.BlockSpec((tm, n), lambda i: (i, 0)),
        out_shape=jax.ShapeDtypeStruct((t, n), F32),
        compiler_params=pltpu.CompilerParams(dimension_semantics=("arbitrary",), vmem_limit_bytes=VMEM_LIMIT),
        name="matmul_residual",
    )(a, w, res)


def _rope_tables(seq):
    half = ROT_DIM // 2
    freqs = ROPE_THETA ** (-jnp.arange(0, ROT_DIM, 2, dtype=F32) / ROT_DIM)
    ang = jnp.arange(seq, dtype=F32)[:, None] * freqs[None, :]
    cos, sin = jnp.cos(ang), jnp.sin(ang)
    ones = jnp.ones((seq, HEAD_DIM - ROT_DIM), F32)
    zeros = jnp.zeros((seq, HEAD_DIM - ROT_DIM), F32)
    zh = jnp.zeros((seq, half), F32)
    c = jnp.concatenate([cos, cos, ones], axis=1)
    s_next = jnp.concatenate([-sin, zh, zeros], axis=1)
    s_prev = jnp.concatenate([zh, sin, zeros], axis=1)
    return jnp.stack([jnp.tile(c, (1, 2)), jnp.tile(s_next, (1, 2)), jnp.tile(s_prev, (1, 2))])


def _head_norm_rope(x, gain2, rope, lo):
    sq = x * x
    s_lo = jnp.sum(jnp.where(lo, sq, 0.0), axis=1, keepdims=True)
    s_hi = jnp.sum(jnp.where(lo, 0.0, sq), axis=1, keepdims=True)
    ms = jnp.where(lo, s_lo, s_hi) * (1.0 / HEAD_DIM)
    xn = x * lax.rsqrt(ms + RMS_EPS) * gain2
    half = ROT_DIM // 2
    return xn * rope[0] + pltpu.roll(xn, LANES - half, 1) * rope[1] + pltpu.roll(xn, half, 1) * rope[2]


def _attn_kernel(sinks_ref, q_ref, kc_ref, kp_ref, vc_ref, vp_ref, rc_ref, rp_ref, qg_ref, kg_ref, o_ref):
    n = pl.program_id(1)
    lo = lax.broadcasted_iota(jnp.int32, (WINDOW, LANES), 1) < HEAD_DIM
    rope_c = rc_ref[...]
    rope_p = rp_ref[...]
    qg = qg_ref[...]
    kg = kg_ref[...]

    rows = GROUP * WINDOW
    qi = lax.broadcasted_iota(jnp.int32, (rows, 2 * WINDOW), 0) & (WINDOW - 1)
    ki = lax.broadcasted_iota(jnp.int32, (rows, 2 * WINDOW), 1)
    rel = WINDOW + qi - ki
    valid = (rel >= 0) & (rel < WINDOW) & ((n > 0) | (ki >= WINDOW))
    head_of_row = lax.broadcasted_iota(jnp.int32, (rows, 1), 0) // WINDOW
    scale = 1.0 / math.sqrt(HEAD_DIM)

    q2 = [_head_norm_rope(q_ref[:, c * LANES:(c + 1) * LANES], qg, rope_c, lo).astype(BF16)
          for c in range(Q_COLS // LANES)]
    for c in range(KV_COLS // LANES):
        cols = slice(c * LANES, (c + 1) * LANES)
        kprev = _head_norm_rope(kp_ref[:, cols], kg, rope_p, lo)
        kcur = _head_norm_rope(kc_ref[:, cols], kg, rope_c, lo)
        kfull = jnp.concatenate([kprev, kcur], axis=0).astype(BF16)
        vfull = jnp.concatenate([vp_ref[:, cols], vc_ref[:, cols]], axis=0).astype(BF16)
        for hh in range(LANES // HEAD_DIM):
            h = (LANES // HEAD_DIM) * c + hh
            kh = kfull[:, hh * HEAD_DIM:(hh + 1) * HEAD_DIM]
            vh = vfull[:, hh * HEAD_DIM:(hh + 1) * HEAD_DIM]
            heads = [GROUP * h + g for g in range(GROUP)]
            q4 = jnp.concatenate([q2[j // 2][:, (j % 2) * HEAD_DIM:(j % 2 + 1) * HEAD_DIM] for j in heads], axis=0)
            s = lax.dot_general(q4, kh, (((1,), (1,)), ((), ())), preferred_element_type=F32) * scale
            s = jnp.where(valid, s, NEG_INF)
            sink = jnp.zeros((rows, 1), F32)
            for g, j in enumerate(heads):
                sink = jnp.where(head_of_row == g, sinks_ref[j], sink)
            m = jnp.maximum(jnp.max(s, axis=1, keepdims=True), sink)
            p = jnp.exp(s - m)
            denom = jnp.sum(p, axis=1, keepdims=True) + jnp.exp(sink - m)
            o = jnp.dot(p.astype(BF16), vh, preferred_element_type=F32) / denom
            for g, j in enumerate(heads):
                o_ref[:, j * HEAD_DIM:(j + 1) * HEAD_DIM] = o[g * WINDOW:(g + 1) * WINDOW]


def _attention(qkv, q_gain, k_gain, sinks, batch, seq):
    t = batch * seq
    nb = seq // WINDOW
    rope = _rope_tables(seq)
    kcol = Q_COLS // KV_COLS
    cur = lambda b, n: (b * nb + n, 0)
    kcur = lambda b, n: (b * nb + n, kcol)
    kprev = lambda b, n: (b * nb + jnp.maximum(n - 1, 0), kcol)
    vcur = lambda b, n: (b * nb + n, kcol + 1)
    vprev = lambda b, n: (b * nb + jnp.maximum(n - 1, 0), kcol + 1)
    return pl.pallas_call(
        _attn_kernel,
        grid=(batch, nb),
        in_specs=[pl.BlockSpec(memory_space=pltpu.SMEM),
                  pl.BlockSpec((WINDOW, Q_COLS), cur),
                  pl.BlockSpec((WINDOW, KV_COLS), kcur),
                  pl.BlockSpec((WINDOW, KV_COLS), kprev),
                  pl.BlockSpec((WINDOW, KV_COLS), vcur),
                  pl.BlockSpec((WINDOW, KV_COLS), vprev),
                  pl.BlockSpec((3, WINDOW, LANES), lambda b, n: (0, n, 0)),
                  pl.BlockSpec((3, WINDOW, LANES), lambda b, n: (0, jnp.maximum(n - 1, 0), 0)),
                  pl.BlockSpec((1, LANES), lambda b, n: (0, 0)),
                  pl.BlockSpec((1, LANES), lambda b, n: (0, 0))],
        out_specs=pl.BlockSpec((WINDOW, Q_COLS), cur),
        out_shape=jax.ShapeDtypeStruct((t, Q_COLS), F32),
        compiler_params=pltpu.CompilerParams(dimension_semantics=("arbitrary", "arbitrary"),
                                             vmem_limit_bytes=VMEM_LIMIT),
        name="swa_attention",
    )(sinks, qkv, qkv, qkv, qkv, qkv, rope, rope,
      jnp.tile(q_gain, 2).reshape(1, LANES), jnp.tile(k_gain, 2).reshape(1, LANES))


def _conv_kernel(x_ref, g_ref, win_ref, cw_ref, wout_ref, o_ref, zprev_ref):
    n = pl.program_id(1)
    d = D_MODEL

    @pl.when(n == 0)
    def _():
        zprev_ref[...] = jnp.zeros_like(zprev_ref)

    x = x_ref[...]
    h = _rms(x, g_ref[...])
    bcu = jnp.dot(h.astype(BF16), win_ref[...], preferred_element_type=F32)
    gate_b = bcu[:, :d]
    z = bcu[:, d:2 * d] * bcu[:, 2 * d:]
    tm = z.shape[0]
    row = lax.broadcasted_iota(jnp.int32, z.shape, 0)
    prev = zprev_ref[...]
    p_last = prev[SUBLANES - 1:SUBLANES, :]
    p_last2 = prev[SUBLANES - 2:SUBLANES - 1, :]
    z1 = jnp.where(row == 0, p_last, pltpu.roll(z, 1, 0))
    z2 = jnp.where(row == 0, p_last2, jnp.where(row == 1, p_last, pltpu.roll(z, 2, 0)))
    cw = cw_ref[...]
    conv = cw[0:1, :] * z2 + cw[1:2, :] * z1 + cw[2:3, :] * z
    zprev_ref[...] = z[tm - SUBLANES:, :]
    o_ref[...] = x + jnp.dot((gate_b * conv).astype(BF16), wout_ref[...], preferred_element_type=F32)


def _conv_mixer(x, gain, w_in, conv_w, w_out, batch, seq, tm=256):
    t, d = x.shape
    nblk = seq // tm
    blk = lambda b, n: (b * nblk + n, 0)
    const = lambda b, n: (0, 0)
    return pl.pallas_call(
        _conv_kernel,
        grid=(batch, nblk),
        in_specs=[pl.BlockSpec((tm, d), blk),
                  pl.BlockSpec((1, d), const),
                  pl.BlockSpec((d, 3 * d), const),
                  pl.BlockSpec((CONV_WIDTH, d), const),
                  pl.BlockSpec((d, d), const)],
        out_specs=pl.BlockSpec((tm, d), blk),
        out_shape=jax.ShapeDtypeStruct((t, d), F32),
        scratch_shapes=[pltpu.VMEM((SUBLANES, d), F32)],
        compiler_params=pltpu.CompilerParams(dimension_semantics=("arbitrary", "arbitrary"),
                                             vmem_limit_bytes=VMEM_LIMIT),
        name="conv_mixer",
    )(x, gain.reshape(1, d), w_in.astype(BF16), conv_w, w_out.astype(BF16))


def _topk_axis0(s, k, ids=None, payload=None):
    n, tm = s.shape
    if ids is None:
        ids = lax.broadcasted_iota(jnp.int32, (n, tm), 0)
    krow = lax.broadcasted_iota(jnp.int32, (k, tm), 0)
    vals = jnp.zeros((k, tm), F32)
    picks = jnp.zeros((k, tm), jnp.int32)
    for r in range(k):
        m = jnp.max(s, axis=0, keepdims=True)
        pos = jnp.min(jnp.where(s == m, ids, jnp.iinfo(jnp.int32).max), axis=0, keepdims=True)
        sel = ids == pos
        if payload is None:
            picked = pos
        else:
            picked = jnp.sum(jnp.where(sel, payload, 0), axis=0, keepdims=True)
        vals = jnp.where(krow == r, m, vals)
        picks = jnp.where(krow == r, picked, picks)
        s = jnp.where(sel, -jnp.inf, s)
    return vals, picks


def _pair_candidates(s1, i1, s2, i2):
    k, tm = s1.shape
    sub = lax.broadcasted_iota(jnp.int32, (SUBLANES, tm), 0)
    scores, flat, expert = [], [], []
    for i in range(k // 2):
        width = k if i == 0 else SUBLANES
        sc = s1[i:i + 1, :] + s2[0:width, :]
        ex = i1[i:i + 1, :] * N_KEYS + i2[0:width, :]
        fl = i * k + lax.broadcasted_iota(jnp.int32, (width, tm), 0)
        reach = k // (i + 1)
        if reach < width:
            sc = jnp.where(sub < reach, sc, -jnp.inf)
        scores.append(sc)
        flat.append(fl)
        expert.append(ex)
    scores.append(s1[k // 2:, :] + s2[0:1, :])
    expert.append(i1[k // 2:, :] * N_KEYS + i2[0:1, :])
    flat.append((k // 2 + sub) * k)
    return jnp.concatenate(scores, axis=0), jnp.concatenate(flat, axis=0), jnp.concatenate(expert, axis=0)


def _route_kernel(x_ref, g_ref, wq_ref, keys_ref, h_ref, idx_ref, gate_ref):
    h = _rms(x_ref[...], g_ref[...])
    h_ref[...] = h
    q = jnp.dot(h.astype(BF16), wq_ref[...], preferred_element_type=F32).astype(BF16)
    idx_rows, gate_rows = [], []
    for head in range(PEER_HEADS):
        tops = []
        for part in range(2):
            col = (head * 2 + part) * QUERY_HALF
            s = lax.dot_general(keys_ref[head, part], q[:, col:col + QUERY_HALF],
                                (((1,), (1,)), ((), ())), preferred_element_type=F32)
            tops.append(_topk_axis0(s, PEER_TOPK))
        (s1, i1), (s2, i2) = tops
        cand, flat_ids, cand_idx = _pair_candidates(s1, i1, s2, i2)
        g_s, e_idx = _topk_axis0(cand, PEER_TOPK, ids=flat_ids, payload=cand_idx)
        e = jnp.exp(g_s - jnp.max(g_s, axis=0, keepdims=True))
        gate_rows.append(e / jnp.sum(e, axis=0, keepdims=True))
        idx_rows.append(e_idx * ROW_WORDS)
    idx_ref[...] = jnp.concatenate(idx_rows, axis=0).T
    gate_ref[...] = jnp.concatenate(gate_rows, axis=0).T


def _route(x, gain, w_query, sub_keys, tm=128):
    t, d = x.shape
    nq = w_query.shape[1]
    return pl.pallas_call(
        _route_kernel,
        grid=(t // tm,),
        in_specs=[pl.BlockSpec((tm, d), lambda i: (i, 0)),
                  pl.BlockSpec((1, d), lambda i: (0, 0)),
                  pl.BlockSpec((d, nq), lambda i: (0, 0)),
                  pl.BlockSpec((PEER_HEADS, 2, N_KEYS, QUERY_HALF), lambda i: (0, 0, 0, 0))],
        out_specs=[pl.BlockSpec((tm, d), lambda i: (i, 0)),
                   pl.BlockSpec((tm, SLOTS), lambda i: (i, 0)),
                   pl.BlockSpec((tm, SLOTS), lambda i: (i, 0))],
        out_shape=[jax.ShapeDtypeStruct((t, d), F32),
                   jax.ShapeDtypeStruct((t, SLOTS), jnp.int32),
                   jax.ShapeDtypeStruct((t, SLOTS), F32)],
        compiler_params=pltpu.CompilerParams(dimension_semantics=("arbitrary",), vmem_limit_bytes=VMEM_LIMIT),
        name="peer_route",
    )(x, gain.reshape(1, d), w_query.astype(BF16), sub_keys.astype(BF16))


def _pack_table(tab):
    bits = lax.bitcast_convert_type(tab.astype(BF16), jnp.uint16).astype(jnp.uint32)
    half = D_MODEL // 2
    words = (bits[:, :half] << 16) | bits[:, half:]
    return lax.bitcast_convert_type(words, jnp.int32).reshape(tab.shape[0] * ROW_WORDS, LANES)


def _load_table_once(tab_hbm, tab, sem):
    @pl.when(pl.program_id(0) == 0)
    def _():
        cp = pltpu.make_async_copy(tab_hbm, tab, sem)
        cp.start()
        cp.wait()


def _gather_pair(tab, off_a, off_b):
    ra = tab[pl.ds(pl.multiple_of(off_a, ROW_WORDS), ROW_WORDS), :]
    rb = tab[pl.ds(pl.multiple_of(off_b, ROW_WORDS), ROW_WORDS), :]
    words = jnp.concatenate([ra, rb], axis=0)
    hi = pltpu.bitcast(words & jnp.int32(-65536), F32)
    lo = pltpu.bitcast(words << 16, F32)
    return hi, lo


TOKEN_UNROLL = 8


def _expert_in_kernel(*refs):
    idx_refs = refs[:TOKEN_UNROLL]
    h_ref, gate_ref, tab_hbm, w_ref, tab, sem, a_ref = refs[TOKEN_UNROLL:]
    _load_table_once(tab_hbm, tab, sem)
    sub = lax.broadcasted_iota(jnp.int32, (SUBLANES, LANES), 0)
    lane = lax.broadcasted_iota(jnp.int32, (SUBLANES, LANES), 1)
    own_half = (sub >= ROW_WORDS) == ((lane & 1) == 1)
    tb = h_ref.shape[0]

    def step(i, carry):
        xs = []
        for u in range(TOKEN_UNROLL):
            x = h_ref[i * TOKEN_UNROLL + u]
            xs.append((jnp.concatenate([x[0:ROW_WORDS], x[0:ROW_WORDS]], axis=0),
                       jnp.concatenate([x[ROW_WORDS:], x[ROW_WORDS:]], axis=0)))
        accs = [jnp.zeros((SUBLANES, LANES), F32) for _ in range(TOKEN_UNROLL)]
        for p in range(SLOTS // 2):
            for u in range(TOKEN_UNROLL):
                hi, lo = _gather_pair(tab, idx_refs[u][i, 2 * p], idx_refs[u][i, 2 * p + 1])
                part = jnp.sum(hi * xs[u][0] + lo * xs[u][1], axis=1, keepdims=True)
                accs[u] = jnp.where((lane >> 1) == p, part, accs[u])
        for u in range(TOKEN_UNROLL):
            a_ref[pl.ds(i * TOKEN_UNROLL + u, 1), :] = jnp.sum(jnp.where(own_half, accs[u], 0.0),
                                                                axis=0, keepdims=True)
        return carry

    lax.fori_loop(0, tb // TOKEN_UNROLL, step, 0)
    a = a_ref[...]
    w_ref[...] = gate_ref[...] * (0.5 * a * (1.0 + lax.erf(a * (1.0 / math.sqrt(2.0)))))


def _expert_out_kernel(*refs):
    idx_refs = refs[:TOKEN_UNROLL]
    w_ref, x_ref, tab_hbm, o_ref, tab, sem, wb_ref = refs[TOKEN_UNROLL:]
    _load_table_once(tab_hbm, tab, sem)
    sub = lax.broadcasted_iota(jnp.int32, (SUBLANES, LANES), 0)
    lower = sub < ROW_WORDS
    tb = x_ref.shape[0]

    def step(i, carry):
        for u in range(TOKEN_UNROLL):
            row = w_ref[pl.ds(i * TOKEN_UNROLL + u, 1), :]
            wb_ref[u] = jnp.broadcast_to(row, (SLOTS, LANES)).T
        acc_h = [jnp.zeros((SUBLANES, LANES), F32) for _ in range(TOKEN_UNROLL)]
        acc_l = [jnp.zeros((SUBLANES, LANES), F32) for _ in range(TOKEN_UNROLL)]
        for p in range(SLOTS // 2):
            ka, kb = 2 * p, 2 * p + 1
            for u in range(TOKEN_UNROLL):
                hi, lo = _gather_pair(tab, idx_refs[u][i, ka], idx_refs[u][i, kb])
                wa = jnp.broadcast_to(wb_ref[u, ka:ka + 1, :], (SUBLANES, LANES))
                wb = jnp.broadcast_to(wb_ref[u, kb:kb + 1, :], (SUBLANES, LANES))
                wt = jnp.where(lower, wa, wb)
                acc_h[u] = acc_h[u] + hi * wt
                acc_l[u] = acc_l[u] + lo * wt
        for u in range(TOKEN_UNROLL):
            t = i * TOKEN_UNROLL + u
            ah = acc_h[u] + pltpu.roll(acc_h[u], ROW_WORDS, 0)
            al = acc_l[u] + pltpu.roll(acc_l[u], ROW_WORDS, 0)
            o_ref[t] = x_ref[t] + jnp.where(lower, ah, al)
        return carry

    lax.fori_loop(0, tb // TOKEN_UNROLL, step, 0)


def _expert_stage_tc(x, h, idx, gate, u_packed, v_packed, t_tc, tb=128):
    t, d = x.shape
    rows = d // LANES
    un = TOKEN_UNROLL
    idx_split = idx.reshape(t // un, un, SLOTS).transpose(1, 0, 2)
    smem_blks = [pl.BlockSpec((None, tb // un, SLOTS), functools.partial(lambda u, i: (u, i, 0), u),
                              memory_space=pltpu.SMEM) for u in range(un)]
    vmem_blk = pl.BlockSpec((tb, SLOTS), lambda i: (i, 0))
    tok_blk = pl.BlockSpec((tb, rows, LANES), lambda i: (i, 0, 0))
    table_scratch = [pltpu.VMEM(u_packed.shape, jnp.int32), pltpu.SemaphoreType.DMA]
    params = pltpu.CompilerParams(dimension_semantics=("arbitrary",), vmem_limit_bytes=VMEM_LIMIT)
    h3 = h.reshape(t, rows, LANES)

    def expert_in(first_blk, n_blk):
        shift = lambda spec_map: (lambda i: spec_map(i + first_blk))
        return pl.pallas_call(
            _expert_in_kernel,
            grid=(n_blk,),
            in_specs=[pl.BlockSpec(b.block_shape, shift(b.index_map), memory_space=pltpu.SMEM) for b in smem_blks]
            + [pl.BlockSpec(tok_blk.block_shape, shift(tok_blk.index_map)),
               pl.BlockSpec(vmem_blk.block_shape, shift(vmem_blk.index_map)),
               pl.BlockSpec(memory_space=pl.ANY)],
            out_specs=vmem_blk,
            out_shape=jax.ShapeDtypeStruct((n_blk * tb, SLOTS), F32),
            scratch_shapes=table_scratch + [pltpu.VMEM((tb, SLOTS), F32)],
            compiler_params=params,
            name="peer_expert_in",
        )(*([idx_split] * un), h3, gate, u_packed)

    w_tail = expert_in(t_tc // tb, (t - t_tc) // tb)
    w = expert_in(0, t_tc // tb)
    out = pl.pallas_call(
        _expert_out_kernel,
        grid=(t_tc // tb,),
        in_specs=smem_blks + [vmem_blk, tok_blk, pl.BlockSpec(memory_space=pl.ANY)],
        out_specs=tok_blk,
        out_shape=jax.ShapeDtypeStruct((t_tc, rows, LANES), F32),
        scratch_shapes=table_scratch + [pltpu.VMEM((un, SLOTS, LANES), F32)],
        compiler_params=params,
        name="peer_expert_out",
    )(*([idx_split] * un), w, x.reshape(t, rows, LANES), v_packed)
    return w_tail, out.reshape(t_tc, d)


SC_LANES = 16
SC_WORKERS = 32
SC_CHUNK = 32
SC_GROUP = 8
SC_TOKENS = 16384


def _sc_params():
    cp = pltpu.CompilerParams()
    if "needs_layout_passes" in pltpu.CompilerParams.__dataclass_fields__:
        cp = dataclasses.replace(cp, needs_layout_passes=False)
    return cp


def _sc_expert_out(table_words, idx, w, x):
    n_tok, d = x.shape
    per = n_tok // SC_WORKERS
    words = d // 2
    nq = words // SC_LANES // 2
    nchunk = SLOTS // SC_CHUNK
    group_chunks = SC_GROUP * nchunk
    mesh = plsc.VectorSubcoreMesh(core_axis_name="c", subcore_axis_name="s")

    @functools.partial(
        pl.kernel, mesh=mesh,
        out_type=jax.ShapeDtypeStruct((n_tok, d), F32),
        scratch_types=[pltpu.VMEM((SC_GROUP * SLOTS,), jnp.int32), pltpu.VMEM((SC_GROUP * SLOTS,), F32),
                       pltpu.VMEM((2, SC_CHUNK, words), jnp.int32), pltpu.VMEM((SC_GROUP, d), F32),
                       pltpu.SemaphoreType.DMA, pltpu.SemaphoreType.DMA],
        compiler_params=_sc_params(),
        name="peer_expert_out_sc",
    )
    def body(tab_hbm, idx_hbm, w_hbm, x_hbm, o_hbm, idx_v, w_v, rows_v, y_v, sem0, sem1):
        base = (lax.axis_index("s") * 2 + lax.axis_index("c")) * per
        zero = jnp.zeros((SC_LANES,), jnp.int32)
        himask = jnp.full((SC_LANES,), -65536, jnp.int32)
        sems = (sem0, sem1)

        def gather(k, b):
            off = pl.multiple_of(k * SC_CHUNK, SC_CHUNK)
            return pltpu.make_async_copy(tab_hbm.at[idx_v.at[pl.ds(off, SC_CHUNK)]], rows_v.at[b], sems[b])

        def accumulate(k, b):
            tok = k // nchunk
            for q in range(2):
                first = q * nq
                acc0 = (tuple(y_v[tok, pl.ds((first + j) * SC_LANES, SC_LANES)] for j in range(nq))
                        + tuple(y_v[tok, pl.ds(words + (first + j) * SC_LANES, SC_LANES)] for j in range(nq)))

                def row_body(r, accs):
                    ws = plsc.load_gather(w_v, [zero + (k * SC_CHUNK + r)])
                    hi_acc, lo_acc = [], []
                    for j in range(nq):
                        wv = rows_v[b, r, pl.ds((first + j) * SC_LANES, SC_LANES)]
                        hi_acc.append(accs[j] + plsc.bitcast(wv & himask, F32) * ws)
                        lo_acc.append(accs[nq + j] + plsc.bitcast(wv << 16, F32) * ws)
                    return tuple(hi_acc) + tuple(lo_acc)

                accs = lax.fori_loop(0, SC_CHUNK, row_body, acc0)
                for j in range(nq):
                    y_v[tok, pl.ds((first + j) * SC_LANES, SC_LANES)] = accs[j]
                    y_v[tok, pl.ds(words + (first + j) * SC_LANES, SC_LANES)] = accs[nq + j]

        @pl.loop(0, per // SC_GROUP)
        def _(g):
            t0 = pl.multiple_of(base + g * SC_GROUP, SC_GROUP)
            pltpu.sync_copy(idx_hbm.at[pl.ds(t0 * SLOTS, SC_GROUP * SLOTS)], idx_v)
            pltpu.sync_copy(w_hbm.at[pl.ds(t0 * SLOTS, SC_GROUP * SLOTS)], w_v)
            pltpu.sync_copy(x_hbm.at[pl.ds(t0, SC_GROUP)], y_v)
            gather(0, 0).start()

            @pl.loop(0, group_chunks // 2)
            def _(kk):
                k0 = 2 * kk
                gather(k0 + 1, 1).start()
                gather(k0, 0).wait()
                accumulate(k0, 0)

                @pl.when(k0 + 2 < group_chunks)
                def _():
                    gather(k0 + 2, 0).start()
                gather(k0 + 1, 1).wait()
                accumulate(k0 + 1, 1)

            pltpu.sync_copy(y_v, o_hbm.at[pl.ds(t0, SC_GROUP)])

    return body(table_words, idx, w, x)


def _peer(x, gain, w_query, sub_keys, expert_u, expert_v):
    h, idx, gate = _route(x, gain, w_query, sub_keys)
    t_tc = x.shape[0] - SC_TOKENS
    u_packed, v_packed = _pack_table(expert_u), _pack_table(expert_v)
    w_tail, out_tc = _expert_stage_tc(x, h, idx, gate, u_packed, v_packed, t_tc)
    v_rows = v_packed.reshape(expert_v.shape[0], ROW_WORDS * LANES)
    out_sc = _sc_expert_out(v_rows, (idx[t_tc:] // ROW_WORDS).reshape(-1), w_tail.reshape(-1), x[t_tc:])
    return jnp.concatenate([out_tc, out_sc], axis=0)


def kernel(x, norm_mix, norm_ffn, attn_w_qkv, attn_q_norm, attn_k_norm, attn_sinks, attn_w_o, conv_w_in, conv_w, conv_w_out, peer_w_query, peer_sub_keys, peer_u, peer_v):
    batch, seq, d = x.shape
    xt = x.reshape(batch * seq, d)
    for i in range(norm_mix.shape[0]):
        j = i // 2
        if i % 2 == 0:
            qkv = _norm_matmul(xt, norm_mix[i], attn_w_qkv[j].astype(BF16))
            o = _attention(qkv, attn_q_norm[j], attn_k_norm[j], attn_sinks[j], batch, seq)
            xt = _matmul_residual(o, attn_w_o[j].astype(BF16), xt)
        else:
            xt = _conv_mixer(xt, norm_mix[i], conv_w_in[j], conv_w[j], conv_w_out[j], batch, seq)
        xt = _peer(xt, norm_ffn[i], peer_w_query[i], peer_sub_keys[i], peer_u[i], peer_v[i])
    return xt.reshape(batch, seq, d)
```

```python
import dataclasses
import functools
import math

import jax
import jax.numpy as jnp
from jax import lax
from jax.experimental import pallas as pl
from jax.experimental.pallas import tpu as pltpu
from jax.experimental.pallas import tpu_sc as plsc

D_MODEL = 1024
RMS_EPS = 1e-6

HEAD_DIM = 64
N_Q_HEADS = 16
N_KV_HEADS = 4
GROUP = N_Q_HEADS // N_KV_HEADS
WINDOW = 128
ROT_DIM = HEAD_DIM // 4
ROPE_THETA = 500000.0
Q_COLS = N_Q_HEADS * HEAD_DIM
KV_COLS = N_KV_HEADS * HEAD_DIM
NEG_INF = -1e30

CONV_WIDTH = 3

PEER_HEADS = 8
N_KEYS = 128
N_EXPERTS = N_KEYS * N_KEYS
PEER_TOPK = 16
QUERY_HALF = 128
SLOTS = PEER_HEADS * PEER_TOPK

LANES = 128
SUBLANES = 8
ROW_WORDS = D_MODEL // 2 // LANES
VMEM_LIMIT = 48 * 1024 * 1024

BF16 = jnp.bfloat16
F32 = jnp.float32


def _rms(x, gain):
    return x * lax.rsqrt(jnp.mean(x * x, axis=-1, keepdims=True) + RMS_EPS) * gain


def _norm_matmul_kernel(x_ref, g_ref, w_ref, o_ref):
    h = _rms(x_ref[...], g_ref[...])
    o_ref[...] = jnp.dot(h.astype(BF16), w_ref[...], preferred_element_type=F32)


def _norm_matmul(x, gain, w, tm=512):
    t, d = x.shape
    n = w.shape[1]
    return pl.pallas_call(
        _norm_matmul_kernel,
        grid=(t // tm,),
        in_specs=[pl.BlockSpec((tm, d), lambda i: (i, 0)),
                  pl.BlockSpec((1, d), lambda i: (0, 0)),
                  pl.BlockSpec((d, n), lambda i: (0, 0))],
        out_specs=pl.BlockSpec((tm, n), lambda i: (i, 0)),
        out_shape=jax.ShapeDtypeStruct((t, n), F32),
        compiler_params=pltpu.CompilerParams(dimension_semantics=("arbitrary",), vmem_limit_bytes=VMEM_LIMIT),
        name="norm_matmul",
    )(x, gain.reshape(1, d), w)


def _matmul_residual_kernel(a_ref, w_ref, r_ref, o_ref):
    o_ref[...] = r_ref[...] + jnp.dot(a_ref[...].astype(BF16), w_ref[...], preferred_element_type=F32)


def _matmul_residual(a, w, res, tm=512):
    t, k = a.shape
    n = w.shape[1]
    return pl.pallas_call(
        _matmul_residual_kernel,
        grid=(t // tm,),
        in_specs=[pl.BlockSpec((tm, k), lambda i: (i, 0)),
                  pl.BlockSpec((k, n), lambda i: (0, 0)),
                  pl.BlockSpec((tm, n), lambda i: (i, 0))],
        out_specs=pl.BlockSpec((tm, n), lambda i: (i, 0)),
        out_shape=jax.ShapeDtypeStruct((t, n), F32),
        compiler_params=pltpu.CompilerParams(dimension_semantics=("arbitrary",), vmem_limit_bytes=VMEM_LIMIT),
        name="matmul_residual",
    )(a, w, res)


def _rope_tables(seq):
    half = ROT_DIM // 2
    freqs = ROPE_THETA ** (-jnp.arange(0, ROT_DIM, 2, dtype=F32) / ROT_DIM)
    ang = jnp.arange(seq, dtype=F32)[:, None] * freqs[None, :]
    cos, sin = jnp.cos(ang), jnp.sin(ang)
    ones = jnp.ones((seq, HEAD_DIM - ROT_DIM), F32)
    zeros = jnp.zeros((seq, HEAD_DIM - ROT_DIM), F32)
    zh = jnp.zeros((seq, half), F32)
    c = jnp.concatenate([cos, cos, ones], axis=1)
    s_next = jnp.concatenate([-sin, zh, zeros], axis=1)
    s_prev = jnp.concatenate([zh, sin, zeros], axis=1)
    return jnp.stack([jnp.tile(c, (1, 2)), jnp.tile(s_next, (1, 2)), jnp.tile(s_prev, (1, 2))])


def _head_norm_rope(x, gain2, rope, lo):
    sq = x * x
    s_lo = jnp.sum(jnp.where(lo, sq, 0.0), axis=1, keepdims=True)
    s_hi = jnp.sum(jnp.where(lo, 0.0, sq), axis=1, keepdims=True)
    ms = jnp.where(lo, s_lo, s_hi) * (1.0 / HEAD_DIM)
    xn = x * lax.rsqrt(ms + RMS_EPS) * gain2
    half = ROT_DIM // 2
    return xn * rope[0] + pltpu.roll(xn, LANES - half, 1) * rope[1] + pltpu.roll(xn, half, 1) * rope[2]


def _attn_kernel(sinks_ref, q_ref, kc_ref, kp_ref, vc_ref, vp_ref, rc_ref, rp_ref, qg_ref, kg_ref, o_ref):
    n = pl.program_id(1)
    lo = lax.broadcasted_iota(jnp.int32, (WINDOW, LANES), 1) < HEAD_DIM
    rope_c = rc_ref[...]
    rope_p = rp_ref[...]
    qg = qg_ref[...]
    kg = kg_ref[...]

    rows = GROUP * WINDOW
    qi = lax.broadcasted_iota(jnp.int32, (rows, 2 * WINDOW), 0) & (WINDOW - 1)
    ki = lax.broadcasted_iota(jnp.int32, (rows, 2 * WINDOW), 1)
    rel = WINDOW + qi - ki
    valid = (rel >= 0) & (rel < WINDOW) & ((n > 0) | (ki >= WINDOW))
    head_of_row = lax.broadcasted_iota(jnp.int32, (rows, 1), 0) // WINDOW
    scale = 1.0 / math.sqrt(HEAD_DIM)

    q2 = [_head_norm_rope(q_ref[:, c * LANES:(c + 1) * LANES], qg, rope_c, lo).astype(BF16)
          for c in range(Q_COLS // LANES)]
    for c in range(KV_COLS // LANES):
        cols = slice(c * LANES, (c + 1) * LANES)
        kprev = _head_norm_rope(kp_ref[:, cols], kg, rope_p, lo)
        kcur = _head_norm_rope(kc_ref[:, cols], kg, rope_c, lo)
        kfull = jnp.concatenate([kprev, kcur], axis=0).astype(BF16)
        vfull = jnp.concatenate([vp_ref[:, cols], vc_ref[:, cols]], axis=0).astype(BF16)
        for hh in range(LANES // HEAD_DIM):
            h = (LANES // HEAD_DIM) * c + hh
            kh = kfull[:, hh * HEAD_DIM:(hh + 1) * HEAD_DIM]
            vh = vfull[:, hh * HEAD_DIM:(hh + 1) * HEAD_DIM]
            heads = [GROUP * h + g for g in range(GROUP)]
            q4 = jnp.concatenate([q2[j // 2][:, (j % 2) * HEAD_DIM:(j % 2 + 1) * HEAD_DIM] for j in heads], axis=0)
            s = lax.dot_general(q4, kh, (((1,), (1,)), ((), ())), preferred_element_type=F32) * scale
            s = jnp.where(valid, s, NEG_INF)
            sink = jnp.zeros((rows, 1), F32)
            for g, j in enumerate(heads):
                sink = jnp.where(head_of_row == g, sinks_ref[j], sink)
            m = jnp.maximum(jnp.max(s, axis=1, keepdims=True), sink)
            p = jnp.exp(s - m)
            denom = jnp.sum(p, axis=1, keepdims=True) + jnp.exp(sink - m)
            o = jnp.dot(p.astype(BF16), vh, preferred_element_type=F32) / denom
            for g, j in enumerate(heads):
                o_ref[:, j * HEAD_DIM:(j + 1) * HEAD_DIM] = o[g * WINDOW:(g + 1) * WINDOW]


def _attention(qkv, q_gain, k_gain, sinks, batch, seq):
    t = batch * seq
    nb = seq // WINDOW
    rope = _rope_tables(seq)
    kcol = Q_COLS // KV_COLS
    cur = lambda b, n: (b * nb + n, 0)
    kcur = lambda b, n: (b * nb + n, kcol)
    kprev = lambda b, n: (b * nb + jnp.maximum(n - 1, 0), kcol)
    vcur = lambda b, n: (b * nb + n, kcol + 1)
    vprev = lambda b, n: (b * nb + jnp.maximum(n - 1, 0), kcol + 1)
    return pl.pallas_call(
        _attn_kernel,
        grid=(batch, nb),
        in_specs=[pl.BlockSpec(memory_space=pltpu.SMEM),
                  pl.BlockSpec((WINDOW, Q_COLS), cur),
                  pl.BlockSpec((WINDOW, KV_COLS), kcur),
                  pl.BlockSpec((WINDOW, KV_COLS), kprev),
                  pl.BlockSpec((WINDOW, KV_COLS), vcur),
                  pl.BlockSpec((WINDOW, KV_COLS), vprev),
                  pl.BlockSpec((3, WINDOW, LANES), lambda b, n: (0, n, 0)),
                  pl.BlockSpec((3, WINDOW, LANES), lambda b, n: (0, jnp.maximum(n - 1, 0), 0)),
                  pl.BlockSpec((1, LANES), lambda b, n: (0, 0)),
                  pl.BlockSpec((1, LANES), lambda b, n: (0, 0))],
        out_specs=pl.BlockSpec((WINDOW, Q_COLS), cur),
        out_shape=jax.ShapeDtypeStruct((t, Q_COLS), F32),
        compiler_params=pltpu.CompilerParams(dimension_semantics=("arbitrary", "arbitrary"),
                                             vmem_limit_bytes=VMEM_LIMIT),
        name="swa_attention",
    )(sinks, qkv, qkv, qkv, qkv, qkv, rope, rope,
      jnp.tile(q_gain, 2).reshape(1, LANES), jnp.tile(k_gain, 2).reshape(1, LANES))


def _conv_kernel(x_ref, g_ref, win_ref, cw_ref, wout_ref, o_ref, zprev_ref):
    n = pl.program_id(1)
    d = D_MODEL

    @pl.when(n == 0)
    def _():
        zprev_ref[...] = jnp.zeros_like(zprev_ref)

    x = x_ref[...]
    h = _rms(x, g_ref[...])
    bcu = jnp.dot(h.astype(BF16), win_ref[...], preferred_element_type=F32)
    gate_b = bcu[:, :d]
    z = bcu[:, d:2 * d] * bcu[:, 2 * d:]
    tm = z.shape[0]
    row = lax.broadcasted_iota(jnp.int32, z.shape, 0)
    prev = zprev_ref[...]
    p_last = prev[SUBLANES - 1:SUBLANES, :]
    p_last2 = prev[SUBLANES - 2:SUBLANES - 1, :]
    z1 = jnp.where(row == 0, p_last, pltpu.roll(z, 1, 0))
    z2 = jnp.where(row == 0, p_last2, jnp.where(row == 1, p_last, pltpu.roll(z, 2, 0)))
    cw = cw_ref[...]
    conv = cw[0:1, :] * z2 + cw[1:2, :] * z1 + cw[2:3, :] * z
    zprev_ref[...] = z[tm - SUBLANES:, :]
    o_ref[...] = x + jnp.dot((gate_b * conv).astype(BF16), wout_ref[...], preferred_element_type=F32)


def _conv_mixer(x, gain, w_in, conv_w, w_out, batch, seq, tm=256):
    t, d = x.shape
    nblk = seq // tm
    blk = lambda b, n: (b * nblk + n, 0)
    const = lambda b, n: (0, 0)
    return pl.pallas_call(
        _conv_kernel,
        grid=(batch, nblk),
        in_specs=[pl.BlockSpec((tm, d), blk),
                  pl.BlockSpec((1, d), const),
                  pl.BlockSpec((d, 3 * d), const),
                  pl.BlockSpec((CONV_WIDTH, d), const),
                  pl.BlockSpec((d, d), const)],
        out_specs=pl.BlockSpec((tm, d), blk),
        out_shape=jax.ShapeDtypeStruct((t, d), F32),
        scratch_shapes=[pltpu.VMEM((SUBLANES, d), F32)],
        compiler_params=pltpu.CompilerParams(dimension_semantics=("arbitrary", "arbitrary"),
                                             vmem_limit_bytes=VMEM_LIMIT),
        name="conv_mixer",
    )(x, gain.reshape(1, d), w_in.astype(BF16), conv_w, w_out.astype(BF16))


def _topk_axis0(s, k, ids=None, payload=None):
    n, tm = s.shape
    if ids is None:
        ids = lax.broadcasted_iota(jnp.int32, (n, tm), 0)
    krow = lax.broadcasted_iota(jnp.int32, (k, tm), 0)
    vals = jnp.zeros((k, tm), F32)
    picks = jnp.zeros((k, tm), jnp.int32)
    for r in range(k):
        m = jnp.max(s, axis=0, keepdims=True)
        pos = jnp.min(jnp.where(s == m, ids, jnp.iinfo(jnp.int32).max), axis=0, keepdims=True)
        sel = ids == pos
        if payload is None:
            picked = pos
        else:
            picked = jnp.sum(jnp.where(sel, payload, 0), axis=0, keepdims=True)
        vals = jnp.where(krow == r, m, vals)
        picks = jnp.where(krow == r, picked, picks)
        s = jnp.where(sel, -jnp.inf, s)
    return vals, picks


def _pair_candidates(s1, i1, s2, i2):
    k, tm = s1.shape
    sub = lax.broadcasted_iota(jnp.int32, (SUBLANES, tm), 0)
    scores, flat, expert = [], [], []
    for i in range(k // 2):
        width = k if i == 0 else SUBLANES
        sc = s1[i:i + 1, :] + s2[0:width, :]
        ex = i1[i:i + 1, :] * N_KEYS + i2[0:width, :]
        fl = i * k + lax.broadcasted_iota(jnp.int32, (width, tm), 0)
        reach = k // (i + 1)
        if reach < width:
            sc = jnp.where(sub < reach, sc, -jnp.inf)
        scores.append(sc)
        flat.append(fl)
        expert.append(ex)
    scores.append(s1[k // 2:, :] + s2[0:1, :])
    expert.append(i1[k // 2:, :] * N_KEYS + i2[0:1, :])
    flat.append((k // 2 + sub) * k)
    return jnp.concatenate(scores, axis=0), jnp.concatenate(flat, axis=0), jnp.concatenate(expert, axis=0)


def _route_kernel(x_ref, g_ref, wq_ref, keys_ref, h_ref, idx_ref, gate_ref):
    h = _rms(x_ref[...], g_ref[...])
    h_ref[...] = h
    q = jnp.dot(h.astype(BF16), wq_ref[...], preferred_element_type=F32).astype(BF16)
    idx_rows, gate_rows = [], []
    for head in range(PEER_HEADS):
        tops = []
        for part in range(2):
            col = (head * 2 + part) * QUERY_HALF
            s = lax.dot_general(keys_ref[head, part], q[:, col:col + QUERY_HALF],
                                (((1,), (1,)), ((), ())), preferred_element_type=F32)
            tops.append(_topk_axis0(s, PEER_TOPK))
        (s1, i1), (s2, i2) = tops
        cand, flat_ids, cand_idx = _pair_candidates(s1, i1, s2, i2)
        g_s, e_idx = _topk_axis0(cand, PEER_TOPK, ids=flat_ids, payload=cand_idx)
        e = jnp.exp(g_s - jnp.max(g_s, axis=0, keepdims=True))
        gate_rows.append(e / jnp.sum(e, axis=0, keepdims=True))
        idx_rows.append(e_idx * ROW_WORDS)
    idx_ref[...] = jnp.concatenate(idx_rows, axis=0).T
    gate_ref[...] = jnp.concatenate(gate_rows, axis=0).T


def _route(x, gain, w_query, sub_keys, first_tok, n_tok, tm=128):
    t, d = x.shape
    nq = w_query.shape[1]
    first_blk = first_tok // tm
    return pl.pallas_call(
        _route_kernel,
        grid=(n_tok // tm,),
        in_specs=[pl.BlockSpec((tm, d), lambda i: (i + first_blk, 0)),
                  pl.BlockSpec((1, d), lambda i: (0, 0)),
                  pl.BlockSpec((d, nq), lambda i: (0, 0)),
                  pl.BlockSpec((PEER_HEADS, 2, N_KEYS, QUERY_HALF), lambda i: (0, 0, 0, 0))],
        out_specs=[pl.BlockSpec((tm, d), lambda i: (i, 0)),
                   pl.BlockSpec((tm, SLOTS), lambda i: (i, 0)),
                   pl.BlockSpec((tm, SLOTS), lambda i: (i, 0))],
        out_shape=[jax.ShapeDtypeStruct((n_tok, d), F32),
                   jax.ShapeDtypeStruct((n_tok, SLOTS), jnp.int32),
                   jax.ShapeDtypeStruct((n_tok, SLOTS), F32)],
        compiler_params=pltpu.CompilerParams(dimension_semantics=("arbitrary",), vmem_limit_bytes=VMEM_LIMIT),
        name="peer_route",
    )(x, gain.reshape(1, d), w_query, sub_keys)


def _pack_table(tab):
    bits = lax.bitcast_convert_type(tab.astype(BF16), jnp.uint16).astype(jnp.uint32)
    half = D_MODEL // 2
    words = (bits[:, :half] << 16) | bits[:, half:]
    return lax.bitcast_convert_type(words, jnp.int32).reshape(tab.shape[0] * ROW_WORDS, LANES)


def _load_table_once(tab_hbm, tab, sem):
    @pl.when(pl.program_id(0) == 0)
    def _():
        cp = pltpu.make_async_copy(tab_hbm, tab, sem)
        cp.start()
        cp.wait()


def _gather_pair(tab, off_a, off_b):
    ra = tab[pl.ds(pl.multiple_of(off_a, ROW_WORDS), ROW_WORDS), :]
    rb = tab[pl.ds(pl.multiple_of(off_b, ROW_WORDS), ROW_WORDS), :]
    words = jnp.concatenate([ra, rb], axis=0)
    hi = pltpu.bitcast(words & jnp.int32(-65536), F32)
    lo = pltpu.bitcast(words << 16, F32)
    return hi, lo


TOKEN_UNROLL = 8


def _expert_in_kernel(*refs):
    idx_refs = refs[:TOKEN_UNROLL]
    h_ref, gate_ref, tab_hbm, w_ref, tab, sem, a_ref = refs[TOKEN_UNROLL:]
    _load_table_once(tab_hbm, tab, sem)
    sub = lax.broadcasted_iota(jnp.int32, (SUBLANES, LANES), 0)
    lane = lax.broadcasted_iota(jnp.int32, (SUBLANES, LANES), 1)
    own_half = (sub >= ROW_WORDS) == ((lane & 1) == 1)
    tb = h_ref.shape[0]

    def step(i, carry):
        xs = []
        for u in range(TOKEN_UNROLL):
            x = h_ref[i * TOKEN_UNROLL + u]
            xs.append((jnp.concatenate([x[0:ROW_WORDS], x[0:ROW_WORDS]], axis=0),
                       jnp.concatenate([x[ROW_WORDS:], x[ROW_WORDS:]], axis=0)))
        accs = [jnp.zeros((SUBLANES, LANES), F32) for _ in range(TOKEN_UNROLL)]
        for p in range(SLOTS // 2):
            for u in range(TOKEN_UNROLL):
                hi, lo = _gather_pair(tab, idx_refs[u][i, 2 * p], idx_refs[u][i, 2 * p + 1])
                part = jnp.sum(hi * xs[u][0] + lo * xs[u][1], axis=1, keepdims=True)
                accs[u] = jnp.where((lane >> 1) == p, part, accs[u])
        for u in range(TOKEN_UNROLL):
            a_ref[pl.ds(i * TOKEN_UNROLL + u, 1), :] = jnp.sum(jnp.where(own_half, accs[u], 0.0),
                                                                axis=0, keepdims=True)
        return carry

    lax.fori_loop(0, tb // TOKEN_UNROLL, step, 0)
    a = a_ref[...]
    w_ref[...] = gate_ref[...] * (0.5 * a * (1.0 + lax.erf(a * (1.0 / math.sqrt(2.0)))))


def _expert_out_kernel(*refs):
    idx_refs = refs[:TOKEN_UNROLL]
    w_ref, x_ref, tab_hbm, o_ref, tab, sem, wb_ref = refs[TOKEN_UNROLL:]
    _load_table_once(tab_hbm, tab, sem)
    sub = lax.broadcasted_iota(jnp.int32, (SUBLANES, LANES), 0)
    lower = sub < ROW_WORDS
    tb = x_ref.shape[0]

    def step(i, carry):
        for u in range(TOKEN_UNROLL):
            row = w_ref[pl.ds(i * TOKEN_UNROLL + u, 1), :]
            wb_ref[u] = jnp.broadcast_to(row, (SLOTS, LANES)).T
        acc_h = [jnp.zeros((SUBLANES, LANES), F32) for _ in range(TOKEN_UNROLL)]
        acc_l = [jnp.zeros((SUBLANES, LANES), F32) for _ in range(TOKEN_UNROLL)]
        for p in range(SLOTS // 2):
            ka, kb = 2 * p, 2 * p + 1
            for u in range(TOKEN_UNROLL):
                hi, lo = _gather_pair(tab, idx_refs[u][i, ka], idx_refs[u][i, kb])
                wa = jnp.broadcast_to(wb_ref[u, ka:ka + 1, :], (SUBLANES, LANES))
                wb = jnp.broadcast_to(wb_ref[u, kb:kb + 1, :], (SUBLANES, LANES))
                wt = jnp.where(lower, wa, wb)
                acc_h[u] = acc_h[u] + hi * wt
                acc_l[u] = acc_l[u] + lo * wt
        for u in range(TOKEN_UNROLL):
            t = i * TOKEN_UNROLL + u
            ah = acc_h[u] + pltpu.roll(acc_h[u], ROW_WORDS, 0)
            al = acc_l[u] + pltpu.roll(acc_l[u], ROW_WORDS, 0)
            o_ref[t] = x_ref[t] + jnp.where(lower, ah, al)
        return carry

    lax.fori_loop(0, tb // TOKEN_UNROLL, step, 0)


def _expert_specs(tb):
    un = TOKEN_UNROLL
    smem_blks = [pl.BlockSpec((None, tb // un, SLOTS), functools.partial(lambda u, i: (u, i, 0), u),
                              memory_space=pltpu.SMEM) for u in range(un)]
    vmem_blk = pl.BlockSpec((tb, SLOTS), lambda i: (i, 0))
    tok_blk = pl.BlockSpec((tb, D_MODEL // LANES, LANES), lambda i: (i, 0, 0))
    params = pltpu.CompilerParams(dimension_semantics=("arbitrary",), vmem_limit_bytes=VMEM_LIMIT)
    return smem_blks, vmem_blk, tok_blk, params


def _split_offsets(idx):
    n = idx.shape[0]
    return idx.reshape(n // TOKEN_UNROLL, TOKEN_UNROLL, SLOTS).transpose(1, 0, 2)


def _expert_in(h, idx_split, gate, u_packed, tb=128):
    n, d = h.shape
    smem_blks, vmem_blk, tok_blk, params = _expert_specs(tb)
    return pl.pallas_call(
        _expert_in_kernel,
        grid=(n // tb,),
        in_specs=smem_blks + [tok_blk, vmem_blk, pl.BlockSpec(memory_space=pl.ANY)],
        out_specs=vmem_blk,
        out_shape=jax.ShapeDtypeStruct((n, SLOTS), F32),
        scratch_shapes=[pltpu.VMEM(u_packed.shape, jnp.int32), pltpu.SemaphoreType.DMA, pltpu.VMEM((tb, SLOTS), F32)],
        compiler_params=params,
        name="peer_expert_in",
    )(*([idx_split] * TOKEN_UNROLL), h.reshape(n, d // LANES, LANES), gate, u_packed)


def _expert_out(x, idx_split, w, v_packed, tb=128):
    t, d = x.shape
    n = w.shape[0]
    smem_blks, vmem_blk, tok_blk, params = _expert_specs(tb)
    out = pl.pallas_call(
        _expert_out_kernel,
        grid=(n // tb,),
        in_specs=smem_blks + [vmem_blk, tok_blk, pl.BlockSpec(memory_space=pl.ANY)],
        out_specs=tok_blk,
        out_shape=jax.ShapeDtypeStruct((n, d // LANES, LANES), F32),
        scratch_shapes=[pltpu.VMEM(v_packed.shape, jnp.int32), pltpu.SemaphoreType.DMA,
                        pltpu.VMEM((TOKEN_UNROLL, SLOTS, LANES), F32)],
        compiler_params=params,
        name="peer_expert_out",
    )(*([idx_split] * TOKEN_UNROLL), w, x.reshape(t, d // LANES, LANES), v_packed)
    return out.reshape(n, d)


SC_LANES = 16
SC_WORKERS = 32
SC_CHUNK = 32
SC_GROUP = 8
SC_TOKENS = 19200


def _sc_params():
    cp = pltpu.CompilerParams()
    if "needs_layout_passes" in pltpu.CompilerParams.__dataclass_fields__:
        cp = dataclasses.replace(cp, needs_layout_passes=False)
    return cp


def _sc_expert_out(table_words, idx, w, x, first_tok):
    d = x.shape[1]
    n_tok = w.shape[0] // SLOTS
    per = n_tok // SC_WORKERS
    words = d // 2
    nq = words // SC_LANES // 2
    nchunk = SLOTS // SC_CHUNK
    group_chunks = SC_GROUP * nchunk
    mesh = plsc.VectorSubcoreMesh(core_axis_name="c", subcore_axis_name="s")

    @functools.partial(
        pl.kernel, mesh=mesh,
        out_type=jax.ShapeDtypeStruct((n_tok, d), F32),
        scratch_types=[pltpu.VMEM((SC_GROUP * SLOTS,), jnp.int32), pltpu.VMEM((SC_GROUP * SLOTS,), F32),
                       pltpu.VMEM((2, SC_CHUNK, words), jnp.int32), pltpu.VMEM((SC_GROUP, d), F32),
                       pltpu.SemaphoreType.DMA, pltpu.SemaphoreType.DMA],
        compiler_params=_sc_params(),
        name="peer_expert_out_sc",
    )
    def body(tab_hbm, idx_hbm, w_hbm, x_hbm, o_hbm, idx_v, w_v, rows_v, y_v, sem0, sem1):
        base = (lax.axis_index("s") * 2 + lax.axis_index("c")) * per
        zero = jnp.zeros((SC_LANES,), jnp.int32)
        himask = jnp.full((SC_LANES,), -65536, jnp.int32)
        sems = (sem0, sem1)

        def gather(k, b):
            off = pl.multiple_of(k * SC_CHUNK, SC_CHUNK)
            return pltpu.make_async_copy(tab_hbm.at[idx_v.at[pl.ds(off, SC_CHUNK)]], rows_v.at[b], sems[b])

        def accumulate(k, b):
            tok = k // nchunk
            for q in range(2):
                first = q * nq
                acc0 = (tuple(y_v[tok, pl.ds((first + j) * SC_LANES, SC_LANES)] for j in range(nq))
                        + tuple(y_v[tok, pl.ds(words + (first + j) * SC_LANES, SC_LANES)] for j in range(nq)))

                def row_body(r, accs):
                    ws = plsc.load_gather(w_v, [zero + (k * SC_CHUNK + r)])
                    hi_acc, lo_acc = [], []
                    for j in range(nq):
                        wv = rows_v[b, r, pl.ds((first + j) * SC_LANES, SC_LANES)]
                        hi_acc.append(accs[j] + plsc.bitcast(wv & himask, F32) * ws)
                        lo_acc.append(accs[nq + j] + plsc.bitcast(wv << 16, F32) * ws)
                    return tuple(hi_acc) + tuple(lo_acc)

                accs = lax.fori_loop(0, SC_CHUNK, row_body, acc0)
                for j in range(nq):
                    y_v[tok, pl.ds((first + j) * SC_LANES, SC_LANES)] = accs[j]
                    y_v[tok, pl.ds(words + (first + j) * SC_LANES, SC_LANES)] = accs[nq + j]

        @pl.loop(0, per // SC_GROUP)
        def _(g):
            t0 = pl.multiple_of(base + g * SC_GROUP, SC_GROUP)
            pltpu.sync_copy(idx_hbm.at[pl.ds(t0 * SLOTS, SC_GROUP * SLOTS)], idx_v)
            pltpu.sync_copy(w_hbm.at[pl.ds(t0 * SLOTS, SC_GROUP * SLOTS)], w_v)
            pltpu.sync_copy(x_hbm.at[pl.ds(first_tok + t0, SC_GROUP)], y_v)
            gather(0, 0).start()

            @pl.loop(0, group_chunks // 2)
            def _(kk):
                k0 = 2 * kk
                gather(k0 + 1, 1).start()
                gather(k0, 0).wait()
                accumulate(k0, 0)

                @pl.when(k0 + 2 < group_chunks)
                def _():
                    gather(k0 + 2, 0).start()
                gather(k0 + 1, 1).wait()
                accumulate(k0 + 1, 1)

            pltpu.sync_copy(y_v, o_hbm.at[pl.ds(t0, SC_GROUP)])

    return body(table_words, idx, w, x)


def _peer(x, gain, w_query, sub_keys, expert_u, expert_v):
    t = x.shape[0]
    t_tc = t - SC_TOKENS
    wq, keys = w_query.astype(BF16), sub_keys.astype(BF16)
    u_packed, v_packed = _pack_table(expert_u), _pack_table(expert_v)
    v_rows = v_packed.reshape(expert_v.shape[0], ROW_WORDS * LANES)

    h, idx, gate = _route(x, gain, wq, keys, t_tc, SC_TOKENS)
    w = _expert_in(h, _split_offsets(idx), gate, u_packed)
    out_sc = _sc_expert_out(v_rows, (idx // ROW_WORDS).reshape(-1), w.reshape(-1), x, t_tc)

    h, idx, gate = _route(x, gain, wq, keys, 0, t_tc)
    idx_split = _split_offsets(idx)
    w = _expert_in(h, idx_split, gate, u_packed)
    out_tc = _expert_out(x, idx_split, w, v_packed)
    return jnp.concatenate([out_tc, out_sc], axis=0)


def kernel(x, norm_mix, norm_ffn, attn_w_qkv, attn_q_norm, attn_k_norm, attn_sinks, attn_w_o, conv_w_in, conv_w, conv_w_out, peer_w_query, peer_sub_keys, peer_u, peer_v):
    batch, seq, d = x.shape
    xt = x.reshape(batch * seq, d)
    for i in range(norm_mix.shape[0]):
        j = i // 2
        if i % 2 == 0:
            qkv = _norm_matmul(xt, norm_mix[i], attn_w_qkv[j].astype(BF16))
            o = _attention(qkv, attn_q_norm[j], attn_k_norm[j], attn_sinks[j], batch, seq)
            xt = _matmul_residual(o, attn_w_o[j].astype(BF16), xt)
        else:
            xt = _conv_mixer(xt, norm_mix[i], conv_w_in[j], conv_w[j], conv_w_out[j], batch, seq)
        xt = _peer(xt, norm_ffn[i], peer_w_query[i], peer_sub_keys[i], peer_u[i], peer_v[i])
    return xt.reshape(batch, seq, d)
```

```python
import dataclasses
import functools
import math

import jax
import jax.numpy as jnp
from jax import lax
from jax.experimental import pallas as pl
from jax.experimental.pallas import tpu as pltpu
from jax.experimental.pallas import tpu_sc as plsc

D_MODEL = 1024
RMS_EPS = 1e-6

HEAD_DIM = 64
N_Q_HEADS = 16
N_KV_HEADS = 4
GROUP = N_Q_HEADS // N_KV_HEADS
WINDOW = 128
ROT_DIM = HEAD_DIM // 4
ROPE_THETA = 500000.0
Q_COLS = N_Q_HEADS * HEAD_DIM
KV_COLS = N_KV_HEADS * HEAD_DIM
NEG_INF = -1e30

CONV_WIDTH = 3

PEER_HEADS = 8
N_KEYS = 128
N_EXPERTS = N_KEYS * N_KEYS
PEER_TOPK = 16
QUERY_HALF = 128
SLOTS = PEER_HEADS * PEER_TOPK

LANES = 128
SUBLANES = 8
ROW_WORDS = D_MODEL // 2 // LANES
VMEM_LIMIT = 48 * 1024 * 1024

BF16 = jnp.bfloat16
F32 = jnp.float32


def _rms(x, gain):
    return x * lax.rsqrt(jnp.mean(x * x, axis=-1, keepdims=True) + RMS_EPS) * gain


def _norm_matmul_kernel(x_ref, g_ref, w_ref, o_ref):
    h = _rms(x_ref[...], g_ref[...])
    o_ref[...] = jnp.dot(h.astype(BF16), w_ref[...], preferred_element_type=F32)


def _norm_matmul(x, gain, w, tm=512):
    t, d = x.shape
    n = w.shape[1]
    return pl.pallas_call(
        _norm_matmul_kernel,
        grid=(t // tm,),
        in_specs=[pl.BlockSpec((tm, d), lambda i: (i, 0)),
                  pl.BlockSpec((1, d), lambda i: (0, 0)),
                  pl.BlockSpec((d, n), lambda i: (0, 0))],
        out_specs=pl.BlockSpec((tm, n), lambda i: (i, 0)),
        out_shape=jax.ShapeDtypeStruct((t, n), F32),
        compiler_params=pltpu.CompilerParams(dimension_semantics=("arbitrary",), vmem_limit_bytes=VMEM_LIMIT),
        name="norm_matmul",
    )(x, gain.reshape(1, d), w)


def _matmul_residual_kernel(a_ref, w_ref, r_ref, o_ref):
    o_ref[...] = r_ref[...] + jnp.dot(a_ref[...].astype(BF16), w_ref[...], preferred_element_type=F32)


def _matmul_residual(a, w, res, tm=512):
    t, k = a.shape
    n = w.shape[1]
    return pl.pallas_call(
        _matmul_residual_kernel,
        grid=(t // tm,),
        in_specs=[pl.BlockSpec((tm, k), lambda i: (i, 0)),
                  pl.BlockSpec((k, n), lambda i: (0, 0)),
                  pl.BlockSpec((tm, n), lambda i: (i, 0))],
        out_specs=pl.BlockSpec((tm, n), lambda i: (i, 0)),
        out_shape=jax.ShapeDtypeStruct((t, n), F32),
        compiler_params=pltpu.CompilerParams(dimension_semantics=("arbitrary",), vmem_limit_bytes=VMEM_LIMIT),
        name="matmul_residual",
    )(a, w, res)


def _rope_tables(seq):
    half = ROT_DIM // 2
    freqs = ROPE_THETA ** (-jnp.arange(0, ROT_DIM, 2, dtype=F32) / ROT_DIM)
    ang = jnp.arange(seq, dtype=F32)[:, None] * freqs[None, :]
    cos, sin = jnp.cos(ang), jnp.sin(ang)
    ones = jnp.ones((seq, HEAD_DIM - ROT_DIM), F32)
    zeros = jnp.zeros((seq, HEAD_DIM - ROT_DIM), F32)
    zh = jnp.zeros((seq, half), F32)
    c = jnp.concatenate([cos, cos, ones], axis=1)
    s_next = jnp.concatenate([-sin, zh, zeros], axis=1)
    s_prev = jnp.concatenate([zh, sin, zeros], axis=1)
    return jnp.stack([jnp.tile(c, (1, 2)), jnp.tile(s_next, (1, 2)), jnp.tile(s_prev, (1, 2))])


def _head_norm_rope(x, gain2, rope, lo):
    sq = x * x
    s_lo = jnp.sum(jnp.where(lo, sq, 0.0), axis=1, keepdims=True)
    s_hi = jnp.sum(jnp.where(lo, 0.0, sq), axis=1, keepdims=True)
    ms = jnp.where(lo, s_lo, s_hi) * (1.0 / HEAD_DIM)
    xn = x * lax.rsqrt(ms + RMS_EPS) * gain2
    half = ROT_DIM // 2
    return xn * rope[0] + pltpu.roll(xn, LANES - half, 1) * rope[1] + pltpu.roll(xn, half, 1) * rope[2]


def _attn_kernel(sinks_ref, q_ref, kc_ref, kp_ref, vc_ref, vp_ref, rc_ref, rp_ref, qg_ref, kg_ref, o_ref):
    n = pl.program_id(1)
    lo = lax.broadcasted_iota(jnp.int32, (WINDOW, LANES), 1) < HEAD_DIM
    rope_c = rc_ref[...]
    rope_p = rp_ref[...]
    qg = qg_ref[...]
    kg = kg_ref[...]

    rows = GROUP * WINDOW
    qi = lax.broadcasted_iota(jnp.int32, (rows, 2 * WINDOW), 0) & (WINDOW - 1)
    ki = lax.broadcasted_iota(jnp.int32, (rows, 2 * WINDOW), 1)
    rel = WINDOW + qi - ki
    valid = (rel >= 0) & (rel < WINDOW) & ((n > 0) | (ki >= WINDOW))
    head_of_row = lax.broadcasted_iota(jnp.int32, (rows, 1), 0) // WINDOW
    scale = 1.0 / math.sqrt(HEAD_DIM)

    q2 = [_head_norm_rope(q_ref[:, c * LANES:(c + 1) * LANES], qg, rope_c, lo).astype(BF16)
          for c in range(Q_COLS // LANES)]
    for c in range(KV_COLS // LANES):
        cols = slice(c * LANES, (c + 1) * LANES)
        kprev = _head_norm_rope(kp_ref[:, cols], kg, rope_p, lo)
        kcur = _head_norm_rope(kc_ref[:, cols], kg, rope_c, lo)
        kfull = jnp.concatenate([kprev, kcur], axis=0).astype(BF16)
        vfull = jnp.concatenate([vp_ref[:, cols], vc_ref[:, cols]], axis=0).astype(BF16)
        for hh in range(LANES // HEAD_DIM):
            h = (LANES // HEAD_DIM) * c + hh
            kh = kfull[:, hh * HEAD_DIM:(hh + 1) * HEAD_DIM]
            vh = vfull[:, hh * HEAD_DIM:(hh + 1) * HEAD_DIM]
            heads = [GROUP * h + g for g in range(GROUP)]
            q4 = jnp.concatenate([q2[j // 2][:, (j % 2) * HEAD_DIM:(j % 2 + 1) * HEAD_DIM] for j in heads], axis=0)
            s = lax.dot_general(q4, kh, (((1,), (1,)), ((), ())), preferred_element_type=F32) * scale
            s = jnp.where(valid, s, NEG_INF)
            sink = jnp.zeros((rows, 1), F32)
            for g, j in enumerate(heads):
                sink = jnp.where(head_of_row == g, sinks_ref[j], sink)
            m = jnp.maximum(jnp.max(s, axis=1, keepdims=True), sink)
            p = jnp.exp(s - m)
            denom = jnp.sum(p, axis=1, keepdims=True) + jnp.exp(sink - m)
            o = jnp.dot(p.astype(BF16), vh, preferred_element_type=F32) / denom
            for g, j in enumerate(heads):
                o_ref[:, j * HEAD_DIM:(j + 1) * HEAD_DIM] = o[g * WINDOW:(g + 1) * WINDOW]


def _attention(qkv, q_gain, k_gain, sinks, batch, seq):
    t = batch * seq
    nb = seq // WINDOW
    rope = _rope_tables(seq)
    kcol = Q_COLS // KV_COLS
    cur = lambda b, n: (b * nb + n, 0)
    kcur = lambda b, n: (b * nb + n, kcol)
    kprev = lambda b, n: (b * nb + jnp.maximum(n - 1, 0), kcol)
    vcur = lambda b, n: (b * nb + n, kcol + 1)
    vprev = lambda b, n: (b * nb + jnp.maximum(n - 1, 0), kcol + 1)
    return pl.pallas_call(
        _attn_kernel,
        grid=(batch, nb),
        in_specs=[pl.BlockSpec(memory_space=pltpu.SMEM),
                  pl.BlockSpec((WINDOW, Q_COLS), cur),
                  pl.BlockSpec((WINDOW, KV_COLS), kcur),
                  pl.BlockSpec((WINDOW, KV_COLS), kprev),
                  pl.BlockSpec((WINDOW, KV_COLS), vcur),
                  pl.BlockSpec((WINDOW, KV_COLS), vprev),
                  pl.BlockSpec((3, WINDOW, LANES), lambda b, n: (0, n, 0)),
                  pl.BlockSpec((3, WINDOW, LANES), lambda b, n: (0, jnp.maximum(n - 1, 0), 0)),
                  pl.BlockSpec((1, LANES), lambda b, n: (0, 0)),
                  pl.BlockSpec((1, LANES), lambda b, n: (0, 0))],
        out_specs=pl.BlockSpec((WINDOW, Q_COLS), cur),
        out_shape=jax.ShapeDtypeStruct((t, Q_COLS), F32),
        compiler_params=pltpu.CompilerParams(dimension_semantics=("arbitrary", "arbitrary"),
                                             vmem_limit_bytes=VMEM_LIMIT),
        name="swa_attention",
    )(sinks, qkv, qkv, qkv, qkv, qkv, rope, rope,
      jnp.tile(q_gain, 2).reshape(1, LANES), jnp.tile(k_gain, 2).reshape(1, LANES))


def _conv_kernel(x_ref, g_ref, win_ref, cw_ref, wout_ref, o_ref, zprev_ref):
    n = pl.program_id(1)
    d = D_MODEL

    @pl.when(n == 0)
    def _():
        zprev_ref[...] = jnp.zeros_like(zprev_ref)

    x = x_ref[...]
    h = _rms(x, g_ref[...])
    bcu = jnp.dot(h.astype(BF16), win_ref[...], preferred_element_type=F32)
    gate_b = bcu[:, :d]
    z = bcu[:, d:2 * d] * bcu[:, 2 * d:]
    tm = z.shape[0]
    row = lax.broadcasted_iota(jnp.int32, z.shape, 0)
    prev = zprev_ref[...]
    p_last = prev[SUBLANES - 1:SUBLANES, :]
    p_last2 = prev[SUBLANES - 2:SUBLANES - 1, :]
    z1 = jnp.where(row == 0, p_last, pltpu.roll(z, 1, 0))
    z2 = jnp.where(row == 0, p_last2, jnp.where(row == 1, p_last, pltpu.roll(z, 2, 0)))
    cw = cw_ref[...]
    conv = cw[0:1, :] * z2 + cw[1:2, :] * z1 + cw[2:3, :] * z
    zprev_ref[...] = z[tm - SUBLANES:, :]
    o_ref[...] = x + jnp.dot((gate_b * conv).astype(BF16), wout_ref[...], preferred_element_type=F32)


def _conv_mixer(x, gain, w_in, conv_w, w_out, batch, seq, tm=256):
    t, d = x.shape
    nblk = seq // tm
    blk = lambda b, n: (b * nblk + n, 0)
    const = lambda b, n: (0, 0)
    return pl.pallas_call(
        _conv_kernel,
        grid=(batch, nblk),
        in_specs=[pl.BlockSpec((tm, d), blk),
                  pl.BlockSpec((1, d), const),
                  pl.BlockSpec((d, 3 * d), const),
                  pl.BlockSpec((CONV_WIDTH, d), const),
                  pl.BlockSpec((d, d), const)],
        out_specs=pl.BlockSpec((tm, d), blk),
        out_shape=jax.ShapeDtypeStruct((t, d), F32),
        scratch_shapes=[pltpu.VMEM((SUBLANES, d), F32)],
        compiler_params=pltpu.CompilerParams(dimension_semantics=("arbitrary", "arbitrary"),
                                             vmem_limit_bytes=VMEM_LIMIT),
        name="conv_mixer",
    )(x, gain.reshape(1, d), w_in.astype(BF16), conv_w, w_out.astype(BF16))


def _topk_axis0(s, k, ids=None, payload=None):
    n, tm = s.shape
    if ids is None:
        ids = lax.broadcasted_iota(jnp.int32, (n, tm), 0)
    krow = lax.broadcasted_iota(jnp.int32, (k, tm), 0)
    vals = jnp.zeros((k, tm), F32)
    picks = jnp.zeros((k, tm), jnp.int32)
    for r in range(k):
        m = jnp.max(s, axis=0, keepdims=True)
        pos = jnp.min(jnp.where(s == m, ids, jnp.iinfo(jnp.int32).max), axis=0, keepdims=True)
        sel = ids == pos
        if payload is None:
            picked = pos
        else:
            picked = jnp.sum(jnp.where(sel, payload, 0), axis=0, keepdims=True)
        vals = jnp.where(krow == r, m, vals)
        picks = jnp.where(krow == r, picked, picks)
        s = jnp.where(sel, -jnp.inf, s)
    return vals, picks


def _pair_candidates(s1, i1, s2, i2):
    k, tm = s1.shape
    sub = lax.broadcasted_iota(jnp.int32, (SUBLANES, tm), 0)
    scores, flat, expert = [], [], []
    for i in range(k // 2):
        width = k if i == 0 else SUBLANES
        sc = s1[i:i + 1, :] + s2[0:width, :]
        ex = i1[i:i + 1, :] * N_KEYS + i2[0:width, :]
        fl = i * k + lax.broadcasted_iota(jnp.int32, (width, tm), 0)
        reach = k // (i + 1)
        if reach < width:
            sc = jnp.where(sub < reach, sc, -jnp.inf)
        scores.append(sc)
        flat.append(fl)
        expert.append(ex)
    scores.append(s1[k // 2:, :] + s2[0:1, :])
    expert.append(i1[k // 2:, :] * N_KEYS + i2[0:1, :])
    flat.append((k // 2 + sub) * k)
    return jnp.concatenate(scores, axis=0), jnp.concatenate(flat, axis=0), jnp.concatenate(expert, axis=0)


def _route_kernel(x_ref, g_ref, wq_ref, keys_ref, h_ref, idx_ref, gate_ref):
    h = _rms(x_ref[...], g_ref[...])
    h_ref[...] = h
    q = jnp.dot(h.astype(BF16), wq_ref[...], preferred_element_type=F32).astype(BF16)
    idx_rows, gate_rows = [], []
    for head in range(PEER_HEADS):
        tops = []
        for part in range(2):
            col = (head * 2 + part) * QUERY_HALF
            s = lax.dot_general(keys_ref[head, part], q[:, col:col + QUERY_HALF],
                                (((1,), (1,)), ((), ())), preferred_element_type=F32)
            tops.append(_topk_axis0(s, PEER_TOPK))
        (s1, i1), (s2, i2) = tops
        cand, flat_ids, cand_idx = _pair_candidates(s1, i1, s2, i2)
        g_s, e_idx = _topk_axis0(cand, PEER_TOPK, ids=flat_ids, payload=cand_idx)
        e = jnp.exp(g_s - jnp.max(g_s, axis=0, keepdims=True))
        gate_rows.append(e / jnp.sum(e, axis=0, keepdims=True))
        idx_rows.append(e_idx * ROW_WORDS)
    idx_ref[...] = jnp.concatenate(idx_rows, axis=0).T
    gate_ref[...] = jnp.concatenate(gate_rows, axis=0).T


def _route(x, gain, w_query, sub_keys, first_tok, n_tok, tm=128):
    t, d = x.shape
    nq = w_query.shape[1]
    first_blk = first_tok // tm
    return pl.pallas_call(
        _route_kernel,
        grid=(n_tok // tm,),
        in_specs=[pl.BlockSpec((tm, d), lambda i: (i + first_blk, 0)),
                  pl.BlockSpec((1, d), lambda i: (0, 0)),
                  pl.BlockSpec((d, nq), lambda i: (0, 0)),
                  pl.BlockSpec((PEER_HEADS, 2, N_KEYS, QUERY_HALF), lambda i: (0, 0, 0, 0))],
        out_specs=[pl.BlockSpec((tm, d), lambda i: (i, 0)),
                   pl.BlockSpec((tm, SLOTS), lambda i: (i, 0)),
                   pl.BlockSpec((tm, SLOTS), lambda i: (i, 0))],
        out_shape=[jax.ShapeDtypeStruct((n_tok, d), F32),
                   jax.ShapeDtypeStruct((n_tok, SLOTS), jnp.int32),
                   jax.ShapeDtypeStruct((n_tok, SLOTS), F32)],
        compiler_params=pltpu.CompilerParams(dimension_semantics=("arbitrary",), vmem_limit_bytes=VMEM_LIMIT),
        name="peer_route",
    )(x, gain.reshape(1, d), w_query, sub_keys)


def _pack_table(tab):
    bits = lax.bitcast_convert_type(tab.astype(BF16), jnp.uint16).astype(jnp.uint32)
    half = D_MODEL // 2
    words = (bits[:, :half] << 16) | bits[:, half:]
    return lax.bitcast_convert_type(words, jnp.int32).reshape(tab.shape[0] * ROW_WORDS, LANES)


def _load_table_once(tab_hbm, tab, sem):
    @pl.when(pl.program_id(0) == 0)
    def _():
        cp = pltpu.make_async_copy(tab_hbm, tab, sem)
        cp.start()
        cp.wait()


def _gather_pair(tab, off_a, off_b):
    ra = tab[pl.ds(pl.multiple_of(off_a, ROW_WORDS), ROW_WORDS), :]
    rb = tab[pl.ds(pl.multiple_of(off_b, ROW_WORDS), ROW_WORDS), :]
    words = jnp.concatenate([ra, rb], axis=0)
    hi = pltpu.bitcast(words & jnp.int32(-65536), F32)
    lo = pltpu.bitcast(words << 16, F32)
    return hi, lo


TOKEN_UNROLL = 8


def _expert_in_kernel(*refs):
    idx_refs = refs[:TOKEN_UNROLL]
    h_ref, gate_ref, tab_hbm, w_ref, tab, sem, a_ref = refs[TOKEN_UNROLL:]
    _load_table_once(tab_hbm, tab, sem)
    sub = lax.broadcasted_iota(jnp.int32, (SUBLANES, LANES), 0)
    lane = lax.broadcasted_iota(jnp.int32, (SUBLANES, LANES), 1)
    own_half = (sub >= ROW_WORDS) == ((lane & 1) == 1)
    tb = h_ref.shape[0]

    def step(i, carry):
        xs = []
        for u in range(TOKEN_UNROLL):
            x = h_ref[i * TOKEN_UNROLL + u]
            xs.append((jnp.concatenate([x[0:ROW_WORDS], x[0:ROW_WORDS]], axis=0),
                       jnp.concatenate([x[ROW_WORDS:], x[ROW_WORDS:]], axis=0)))
        accs = [jnp.zeros((SUBLANES, LANES), F32) for _ in range(TOKEN_UNROLL)]
        for p in range(SLOTS // 2):
            for u in range(TOKEN_UNROLL):
                hi, lo = _gather_pair(tab, idx_refs[u][i, 2 * p], idx_refs[u][i, 2 * p + 1])
                part = jnp.sum(hi * xs[u][0] + lo * xs[u][1], axis=1, keepdims=True)
                accs[u] = jnp.where((lane >> 1) == p, part, accs[u])
        for u in range(TOKEN_UNROLL):
            a_ref[pl.ds(i * TOKEN_UNROLL + u, 1), :] = jnp.sum(jnp.where(own_half, accs[u], 0.0),
                                                                axis=0, keepdims=True)
        return carry

    lax.fori_loop(0, tb // TOKEN_UNROLL, step, 0)
    a = a_ref[...]
    w_ref[...] = gate_ref[...] * (0.5 * a * (1.0 + lax.erf(a * (1.0 / math.sqrt(2.0)))))


def _expert_out_kernel(*refs):
    idx_refs = refs[:TOKEN_UNROLL]
    w_ref, x_ref, tab_hbm, o_ref, tab, sem, wb_ref = refs[TOKEN_UNROLL:]
    _load_table_once(tab_hbm, tab, sem)
    sub = lax.broadcasted_iota(jnp.int32, (SUBLANES, LANES), 0)
    lower = sub < ROW_WORDS
    tb = x_ref.shape[0]

    def step(i, carry):
        for u in range(TOKEN_UNROLL):
            row = w_ref[pl.ds(i * TOKEN_UNROLL + u, 1), :]
            wb_ref[u] = jnp.broadcast_to(row, (SLOTS, LANES)).T
        acc_h = [jnp.zeros((SUBLANES, LANES), F32) for _ in range(TOKEN_UNROLL)]
        acc_l = [jnp.zeros((SUBLANES, LANES), F32) for _ in range(TOKEN_UNROLL)]
        for p in range(SLOTS // 2):
            ka, kb = 2 * p, 2 * p + 1
            for u in range(TOKEN_UNROLL):
                hi, lo = _gather_pair(tab, idx_refs[u][i, ka], idx_refs[u][i, kb])
                wa = jnp.broadcast_to(wb_ref[u, ka:ka + 1, :], (SUBLANES, LANES))
                wb = jnp.broadcast_to(wb_ref[u, kb:kb + 1, :], (SUBLANES, LANES))
                wt = jnp.where(lower, wa, wb)
                acc_h[u] = acc_h[u] + hi * wt
                acc_l[u] = acc_l[u] + lo * wt
        for u in range(TOKEN_UNROLL):
            t = i * TOKEN_UNROLL + u
            ah = acc_h[u] + pltpu.roll(acc_h[u], ROW_WORDS, 0)
            al = acc_l[u] + pltpu.roll(acc_l[u], ROW_WORDS, 0)
            o_ref[t] = x_ref[t] + jnp.where(lower, ah, al)
        return carry

    lax.fori_loop(0, tb // TOKEN_UNROLL, step, 0)


def _expert_specs(tb):
    un = TOKEN_UNROLL
    smem_blks = [pl.BlockSpec((None, tb // un, SLOTS), functools.partial(lambda u, i: (u, i, 0), u),
                              memory_space=pltpu.SMEM) for u in range(un)]
    vmem_blk = pl.BlockSpec((tb, SLOTS), lambda i: (i, 0))
    tok_blk = pl.BlockSpec((tb, D_MODEL // LANES, LANES), lambda i: (i, 0, 0))
    params = pltpu.CompilerParams(dimension_semantics=("arbitrary",), vmem_limit_bytes=VMEM_LIMIT)
    return smem_blks, vmem_blk, tok_blk, params


def _split_offsets(idx):
    n = idx.shape[0]
    return idx.reshape(n // TOKEN_UNROLL, TOKEN_UNROLL, SLOTS).transpose(1, 0, 2)


def _expert_in(h, idx_split, gate, u_packed, tb=128):
    n, d = h.shape
    smem_blks, vmem_blk, tok_blk, params = _expert_specs(tb)
    return pl.pallas_call(
        _expert_in_kernel,
        grid=(n // tb,),
        in_specs=smem_blks + [tok_blk, vmem_blk, pl.BlockSpec(memory_space=pl.ANY)],
        out_specs=vmem_blk,
        out_shape=jax.ShapeDtypeStruct((n, SLOTS), F32),
        scratch_shapes=[pltpu.VMEM(u_packed.shape, jnp.int32), pltpu.SemaphoreType.DMA, pltpu.VMEM((tb, SLOTS), F32)],
        compiler_params=params,
        name="peer_expert_in",
    )(*([idx_split] * TOKEN_UNROLL), h.reshape(n, d // LANES, LANES), gate, u_packed)


def _expert_out(x, idx_split, w, v_packed, tb=128):
    t, d = x.shape
    n = w.shape[0]
    smem_blks, vmem_blk, tok_blk, params = _expert_specs(tb)
    out = pl.pallas_call(
        _expert_out_kernel,
        grid=(n // tb,),
        in_specs=smem_blks + [vmem_blk, tok_blk, pl.BlockSpec(memory_space=pl.ANY)],
        out_specs=tok_blk,
        out_shape=jax.ShapeDtypeStruct((n, d // LANES, LANES), F32),
        scratch_shapes=[pltpu.VMEM(v_packed.shape, jnp.int32), pltpu.SemaphoreType.DMA,
                        pltpu.VMEM((TOKEN_UNROLL, SLOTS, LANES), F32)],
        compiler_params=params,
        name="peer_expert_out",
    )(*([idx_split] * TOKEN_UNROLL), w, x.reshape(t, d // LANES, LANES), v_packed)
    return out.reshape(n, d)


SC_LANES = 16
SC_WORKERS = 32
SC_CHUNK = 32
SC_GROUP = 8
SC_TOKENS = 13568


def _sc_params():
    cp = pltpu.CompilerParams()
    if "needs_layout_passes" in pltpu.CompilerParams.__dataclass_fields__:
        cp = dataclasses.replace(cp, needs_layout_passes=False)
    return cp


def _tree_sum(vals):
    while len(vals) > 1:
        vals = [vals[i] + vals[i + 1] for i in range(0, len(vals) - 1, 2)] + ([vals[-1]] if len(vals) % 2 else [])
    return vals[0]

def _sc_expert_in(table_words, idx, h):
    n_tok, d = h.shape
    per = n_tok // SC_WORKERS
    words = d // 2
    nq = words // SC_LANES // 2
    nchunk = SLOTS // SC_CHUNK
    group_chunks = SC_GROUP * nchunk
    mesh = plsc.VectorSubcoreMesh(core_axis_name="c", subcore_axis_name="s")

    @functools.partial(
        pl.kernel, mesh=mesh,
        out_type=jax.ShapeDtypeStruct((n_tok * SLOTS,), F32),
        scratch_types=[pltpu.VMEM((SC_GROUP * SLOTS,), jnp.int32), pltpu.VMEM((SC_GROUP, d), F32),
                       pltpu.VMEM((2, SC_CHUNK, words), jnp.int32), pltpu.VMEM((SC_GROUP * SLOTS,), F32),
                       pltpu.VMEM((SC_CHUNK, SC_LANES), F32),
                       pltpu.SemaphoreType.DMA, pltpu.SemaphoreType.DMA],
        compiler_params=_sc_params(),
        name="peer_expert_in_sc",
    )
    def body(tab_hbm, idx_hbm, h_hbm, a_hbm, idx_v, x_v, rows_v, a_v, part_v, sem0, sem1):
        base = (lax.axis_index("s") * 2 + lax.axis_index("c")) * per
        lanes = lax.iota(jnp.int32, SC_LANES)
        himask = jnp.full((SC_LANES,), -65536, jnp.int32)
        sems = (sem0, sem1)

        def gather(k, b):
            off = pl.multiple_of(k * SC_CHUNK, SC_CHUNK)
            return pltpu.make_async_copy(tab_hbm.at[idx_v.at[pl.ds(off, SC_CHUNK)]], rows_v.at[b], sems[b])

        def dots(k, b):
            tok = k // nchunk
            for q in range(2):
                first = q * nq
                xh = [x_v[tok, pl.ds((first + j) * SC_LANES, SC_LANES)] for j in range(nq)]
                xl = [x_v[tok, pl.ds(words + (first + j) * SC_LANES, SC_LANES)] for j in range(nq)]

                def row_body(r, carry):
                    prods = []
                    for j in range(nq):
                        wv = rows_v[b, r, pl.ds((first + j) * SC_LANES, SC_LANES)]
                        prods.append(plsc.bitcast(wv & himask, F32) * xh[j] + plsc.bitcast(wv << 16, F32) * xl[j])
                    acc = _tree_sum(prods)
                    if q == 0:
                        part_v[r, :] = acc
                    else:
                        part_v[r, :] = part_v[r, :] + acc
                    return carry
                lax.fori_loop(0, SC_CHUNK, row_body, 0)
            for g in range(SC_CHUNK // SC_LANES):
                def red_body(r, avec):
                    return jnp.where(lanes == r, jnp.sum(part_v[g * SC_LANES + r, :]), avec)
                avec = lax.fori_loop(0, SC_LANES, red_body, jnp.zeros((SC_LANES,), F32))
                a_v[pl.ds(pl.multiple_of(k * SC_CHUNK + g * SC_LANES, SC_LANES), SC_LANES)] = avec

        @pl.loop(0, per // SC_GROUP)
        def _(g):
            t0 = pl.multiple_of(base + g * SC_GROUP, SC_GROUP)
            pltpu.sync_copy(idx_hbm.at[pl.ds(t0 * SLOTS, SC_GROUP * SLOTS)], idx_v)
            pltpu.sync_copy(h_hbm.at[pl.ds(t0, SC_GROUP)], x_v)
            gather(0, 0).start()

            @pl.loop(0, group_chunks // 2)
            def _(kk):
                k0 = 2 * kk
                gather(k0 + 1, 1).start()
                gather(k0, 0).wait()
                dots(k0, 0)

                @pl.when(k0 + 2 < group_chunks)
                def _():
                    gather(k0 + 2, 0).start()
                gather(k0 + 1, 1).wait()
                dots(k0 + 1, 1)

            pltpu.sync_copy(a_v, a_hbm.at[pl.ds(t0 * SLOTS, SC_GROUP * SLOTS)])

    return body(table_words, idx, h)

def _sc_expert_out(table_words, idx, w, x, first_tok):
    d = x.shape[1]
    n_tok = w.shape[0] // SLOTS
    per = n_tok // SC_WORKERS
    words = d // 2
    nq = words // SC_LANES // 2
    nchunk = SLOTS // SC_CHUNK
    group_chunks = SC_GROUP * nchunk
    mesh = plsc.VectorSubcoreMesh(core_axis_name="c", subcore_axis_name="s")

    @functools.partial(
        pl.kernel, mesh=mesh,
        out_type=jax.ShapeDtypeStruct((n_tok, d), F32),
        scratch_types=[pltpu.VMEM((SC_GROUP * SLOTS,), jnp.int32), pltpu.VMEM((SC_GROUP * SLOTS,), F32),
                       pltpu.VMEM((2, SC_CHUNK, words), jnp.int32), pltpu.VMEM((SC_GROUP, d), F32),
                       pltpu.SemaphoreType.DMA, pltpu.SemaphoreType.DMA],
        compiler_params=_sc_params(),
        name="peer_expert_out_sc",
    )
    def body(tab_hbm, idx_hbm, w_hbm, x_hbm, o_hbm, idx_v, w_v, rows_v, y_v, sem0, sem1):
        base = (lax.axis_index("s") * 2 + lax.axis_index("c")) * per
        zero = jnp.zeros((SC_LANES,), jnp.int32)
        himask = jnp.full((SC_LANES,), -65536, jnp.int32)
        sems = (sem0, sem1)

        def gather(k, b):
            off = pl.multiple_of(k * SC_CHUNK, SC_CHUNK)
            return pltpu.make_async_copy(tab_hbm.at[idx_v.at[pl.ds(off, SC_CHUNK)]], rows_v.at[b], sems[b])

        def accumulate(k, b):
            tok = k // nchunk
            for q in range(2):
                first = q * nq
                acc0 = (tuple(y_v[tok, pl.ds((first + j) * SC_LANES, SC_LANES)] for j in range(nq))
                        + tuple(y_v[tok, pl.ds(words + (first + j) * SC_LANES, SC_LANES)] for j in range(nq)))

                def row_body(r, accs):
                    ws = plsc.load_gather(w_v, [zero + (k * SC_CHUNK + r)])
                    hi_acc, lo_acc = [], []
                    for j in range(nq):
                        wv = rows_v[b, r, pl.ds((first + j) * SC_LANES, SC_LANES)]
                        hi_acc.append(accs[j] + plsc.bitcast(wv & himask, F32) * ws)
                        lo_acc.append(accs[nq + j] + plsc.bitcast(wv << 16, F32) * ws)
                    return tuple(hi_acc) + tuple(lo_acc)

                accs = lax.fori_loop(0, SC_CHUNK, row_body, acc0)
                for j in range(nq):
                    y_v[tok, pl.ds((first + j) * SC_LANES, SC_LANES)] = accs[j]
                    y_v[tok, pl.ds(words + (first + j) * SC_LANES, SC_LANES)] = accs[nq + j]

        @pl.loop(0, per // SC_GROUP)
        def _(g):
            t0 = pl.multiple_of(base + g * SC_GROUP, SC_GROUP)
            pltpu.sync_copy(idx_hbm.at[pl.ds(t0 * SLOTS, SC_GROUP * SLOTS)], idx_v)
            pltpu.sync_copy(w_hbm.at[pl.ds(t0 * SLOTS, SC_GROUP * SLOTS)], w_v)
            pltpu.sync_copy(x_hbm.at[pl.ds(first_tok + t0, SC_GROUP)], y_v)
            gather(0, 0).start()

            @pl.loop(0, group_chunks // 2)
            def _(kk):
                k0 = 2 * kk
                gather(k0 + 1, 1).start()
                gather(k0, 0).wait()
                accumulate(k0, 0)

                @pl.when(k0 + 2 < group_chunks)
                def _():
                    gather(k0 + 2, 0).start()
                gather(k0 + 1, 1).wait()
                accumulate(k0 + 1, 1)

            pltpu.sync_copy(y_v, o_hbm.at[pl.ds(t0, SC_GROUP)])

    return body(table_words, idx, w, x)


def _gelu_gate_kernel(a_ref, g_ref, after_ref, w_ref):
    del after_ref
    a = a_ref[...]
    w_ref[...] = g_ref[...] * (0.5 * a * (1.0 + lax.erf(a * (1.0 / math.sqrt(2.0)))))


def _gelu_gate(a, gate, after):
    return pl.pallas_call(_gelu_gate_kernel, out_shape=jax.ShapeDtypeStruct(a.shape, F32),
                          compiler_params=pltpu.CompilerParams(vmem_limit_bytes=VMEM_LIMIT),
                          name="peer_gelu_gate")(a, gate, after)


def _peer(x, gain, w_query, sub_keys, expert_u, expert_v):
    t = x.shape[0]
    t_tc = t - SC_TOKENS
    wq, keys = w_query.astype(BF16), sub_keys.astype(BF16)
    u_packed, v_packed = _pack_table(expert_u), _pack_table(expert_v)
    u_rows = u_packed.reshape(expert_u.shape[0], ROW_WORDS * LANES)
    v_rows = v_packed.reshape(expert_v.shape[0], ROW_WORDS * LANES)

    h_sc, idx_sc, gate_sc = _route(x, gain, wq, keys, t_tc, SC_TOKENS)
    experts_sc = (idx_sc // ROW_WORDS).reshape(-1)
    a_sc = _sc_expert_in(u_rows, experts_sc, h_sc)

    h, idx, gate = _route(x, gain, wq, keys, 0, t_tc)
    idx_split = _split_offsets(idx)
    w = _expert_in(h, idx_split, gate, u_packed)

    w_sc = _gelu_gate(a_sc.reshape(SC_TOKENS, SLOTS), gate_sc, w[:SUBLANES])
    out_sc = _sc_expert_out(v_rows, experts_sc, w_sc.reshape(-1), x, t_tc)
    out_tc = _expert_out(x, idx_split, w, v_packed)
    return jnp.concatenate([out_tc, out_sc], axis=0)


def kernel(x, norm_mix, norm_ffn, attn_w_qkv, attn_q_norm, attn_k_norm, attn_sinks, attn_w_o, conv_w_in, conv_w, conv_w_out, peer_w_query, peer_sub_keys, peer_u, peer_v):
    batch, seq, d = x.shape
    xt = x.reshape(batch * seq, d)
    for i in range(norm_mix.shape[0]):
        j = i // 2
        if i % 2 == 0:
            qkv = _norm_matmul(xt, norm_mix[i], attn_w_qkv[j].astype(BF16))
            o = _attention(qkv, attn_q_norm[j], attn_k_norm[j], attn_sinks[j], batch, seq)
            xt = _matmul_residual(o, attn_w_o[j].astype(BF16), xt)
        else:
            xt = _conv_mixer(xt, norm_mix[i], conv_w_in[j], conv_w[j], conv_w_out[j], batch, seq)
        xt = _peer(xt, norm_ffn[i], peer_w_query[i], peer_sub_keys[i], peer_u[i], peer_v[i])
    return xt.reshape(batch, seq, d)
```

```python
import dataclasses
import functools
import math

import jax
import jax.numpy as jnp
from jax import lax
from jax.experimental import pallas as pl
from jax.experimental.pallas import tpu as pltpu
from jax.experimental.pallas import tpu_sc as plsc

D_MODEL = 1024
RMS_EPS = 1e-6

HEAD_DIM = 64
N_Q_HEADS = 16
N_KV_HEADS = 4
GROUP = N_Q_HEADS // N_KV_HEADS
WINDOW = 128
ROT_DIM = HEAD_DIM // 4
ROPE_THETA = 500000.0
Q_COLS = N_Q_HEADS * HEAD_DIM
KV_COLS = N_KV_HEADS * HEAD_DIM
NEG_INF = -1e30

CONV_WIDTH = 3

PEER_HEADS = 8
N_KEYS = 128
N_EXPERTS = N_KEYS * N_KEYS
PEER_TOPK = 16
QUERY_HALF = 128
SLOTS = PEER_HEADS * PEER_TOPK

LANES = 128
SUBLANES = 8
ROW_WORDS = D_MODEL // 2 // LANES
VMEM_LIMIT = 48 * 1024 * 1024

BF16 = jnp.bfloat16
F32 = jnp.float32


def _rms(x, gain):
    return x * lax.rsqrt(jnp.mean(x * x, axis=-1, keepdims=True) + RMS_EPS) * gain


def _norm_matmul_kernel(x_ref, g_ref, w_ref, o_ref):
    h = _rms(x_ref[...], g_ref[...])
    o_ref[...] = jnp.dot(h.astype(BF16), w_ref[...], preferred_element_type=F32)


def _norm_matmul(x, gain, w, tm=512):
    t, d = x.shape
    n = w.shape[1]
    return pl.pallas_call(
        _norm_matmul_kernel,
        grid=(t // tm,),
        in_specs=[pl.BlockSpec((tm, d), lambda i: (i, 0)),
                  pl.BlockSpec((1, d), lambda i: (0, 0)),
                  pl.BlockSpec((d, n), lambda i: (0, 0))],
        out_specs=pl.BlockSpec((tm, n), lambda i: (i, 0)),
        out_shape=jax.ShapeDtypeStruct((t, n), F32),
        compiler_params=pltpu.CompilerParams(dimension_semantics=("arbitrary",), vmem_limit_bytes=VMEM_LIMIT),
        name="norm_matmul",
    )(x, gain.reshape(1, d), w)


def _matmul_residual_kernel(a_ref, w_ref, r_ref, o_ref):
    o_ref[...] = r_ref[...] + jnp.dot(a_ref[...].astype(BF16), w_ref[...], preferred_element_type=F32)


def _matmul_residual(a, w, res, tm=512):
    t, k = a.shape
    n = w.shape[1]
    return pl.pallas_call(
        _matmul_residual_kernel,
        grid=(t // tm,),
        in_specs=[pl.BlockSpec((tm, k), lambda i: (i, 0)),
                  pl.BlockSpec((k, n), lambda i: (0, 0)),
                  pl.BlockSpec((tm, n), lambda i: (i, 0))],
        out_specs=pl.BlockSpec((tm, n), lambda i: (i, 0)),
        out_shape=jax.ShapeDtypeStruct((t, n), F32),
        compiler_params=pltpu.CompilerParams(dimension_semantics=("arbitrary",), vmem_limit_bytes=VMEM_LIMIT),
        name="matmul_residual",
    )(a, w, res)


def _rope_tables(seq):
    half = ROT_DIM // 2
    freqs = ROPE_THETA ** (-jnp.arange(0, ROT_DIM, 2, dtype=F32) / ROT_DIM)
    ang = jnp.arange(seq, dtype=F32)[:, None] * freqs[None, :]
    cos, sin = jnp.cos(ang), jnp.sin(ang)
    ones = jnp.ones((seq, HEAD_DIM - ROT_DIM), F32)
    zeros = jnp.zeros((seq, HEAD_DIM - ROT_DIM), F32)
    zh = jnp.zeros((seq, half), F32)
    c = jnp.concatenate([cos, cos, ones], axis=1)
    s_next = jnp.concatenate([-sin, zh, zeros], axis=1)
    s_prev = jnp.concatenate([zh, sin, zeros], axis=1)
    return jnp.stack([jnp.tile(c, (1, 2)), jnp.tile(s_next, (1, 2)), jnp.tile(s_prev, (1, 2))])


def _head_norm_rope(x, gain2, rope, lo):
    sq = x * x
    s_lo = jnp.sum(jnp.where(lo, sq, 0.0), axis=1, keepdims=True)
    s_hi = jnp.sum(jnp.where(lo, 0.0, sq), axis=1, keepdims=True)
    ms = jnp.where(lo, s_lo, s_hi) * (1.0 / HEAD_DIM)
    xn = x * lax.rsqrt(ms + RMS_EPS) * gain2
    half = ROT_DIM // 2
    return xn * rope[0] + pltpu.roll(xn, LANES - half, 1) * rope[1] + pltpu.roll(xn, half, 1) * rope[2]


def _attn_kernel(sinks_ref, q_ref, kc_ref, kp_ref, vc_ref, vp_ref, rc_ref, rp_ref, qg_ref, kg_ref, o_ref):
    n = pl.program_id(1)
    lo = lax.broadcasted_iota(jnp.int32, (WINDOW, LANES), 1) < HEAD_DIM
    rope_c = rc_ref[...]
    rope_p = rp_ref[...]
    qg = qg_ref[...]
    kg = kg_ref[...]

    rows = GROUP * WINDOW
    qi = lax.broadcasted_iota(jnp.int32, (rows, 2 * WINDOW), 0) & (WINDOW - 1)
    ki = lax.broadcasted_iota(jnp.int32, (rows, 2 * WINDOW), 1)
    rel = WINDOW + qi - ki
    valid = (rel >= 0) & (rel < WINDOW) & ((n > 0) | (ki >= WINDOW))
    head_of_row = lax.broadcasted_iota(jnp.int32, (rows, 1), 0) // WINDOW
    scale = 1.0 / math.sqrt(HEAD_DIM)

    q2 = [_head_norm_rope(q_ref[:, c * LANES:(c + 1) * LANES], qg, rope_c, lo).astype(BF16)
          for c in range(Q_COLS // LANES)]
    for c in range(KV_COLS // LANES):
        cols = slice(c * LANES, (c + 1) * LANES)
        kprev = _head_norm_rope(kp_ref[:, cols], kg, rope_p, lo)
        kcur = _head_norm_rope(kc_ref[:, cols], kg, rope_c, lo)
        kfull = jnp.concatenate([kprev, kcur], axis=0).astype(BF16)
        vfull = jnp.concatenate([vp_ref[:, cols], vc_ref[:, cols]], axis=0).astype(BF16)
        for hh in range(LANES // HEAD_DIM):
            h = (LANES // HEAD_DIM) * c + hh
            kh = kfull[:, hh * HEAD_DIM:(hh + 1) * HEAD_DIM]
            vh = vfull[:, hh * HEAD_DIM:(hh + 1) * HEAD_DIM]
            heads = [GROUP * h + g for g in range(GROUP)]
            q4 = jnp.concatenate([q2[j // 2][:, (j % 2) * HEAD_DIM:(j % 2 + 1) * HEAD_DIM] for j in heads], axis=0)
            s = lax.dot_general(q4, kh, (((1,), (1,)), ((), ())), preferred_element_type=F32) * scale
            s = jnp.where(valid, s, NEG_INF)
            sink = jnp.zeros((rows, 1), F32)
            for g, j in enumerate(heads):
                sink = jnp.where(head_of_row == g, sinks_ref[j], sink)
            m = jnp.maximum(jnp.max(s, axis=1, keepdims=True), sink)
            p = jnp.exp(s - m)
            denom = jnp.sum(p, axis=1, keepdims=True) + jnp.exp(sink - m)
            o = jnp.dot(p.astype(BF16), vh, preferred_element_type=F32) / denom
            for g, j in enumerate(heads):
                o_ref[:, j * HEAD_DIM:(j + 1) * HEAD_DIM] = o[g * WINDOW:(g + 1) * WINDOW]


def _attention(qkv, q_gain, k_gain, sinks, batch, seq):
    t = batch * seq
    nb = seq // WINDOW
    rope = _rope_tables(seq)
    kcol = Q_COLS // KV_COLS
    cur = lambda b, n: (b * nb + n, 0)
    kcur = lambda b, n: (b * nb + n, kcol)
    kprev = lambda b, n: (b * nb + jnp.maximum(n - 1, 0), kcol)
    vcur = lambda b, n: (b * nb + n, kcol + 1)
    vprev = lambda b, n: (b * nb + jnp.maximum(n - 1, 0), kcol + 1)
    return pl.pallas_call(
        _attn_kernel,
        grid=(batch, nb),
        in_specs=[pl.BlockSpec(memory_space=pltpu.SMEM),
                  pl.BlockSpec((WINDOW, Q_COLS), cur),
                  pl.BlockSpec((WINDOW, KV_COLS), kcur),
                  pl.BlockSpec((WINDOW, KV_COLS), kprev),
                  pl.BlockSpec((WINDOW, KV_COLS), vcur),
                  pl.BlockSpec((WINDOW, KV_COLS), vprev),
                  pl.BlockSpec((3, WINDOW, LANES), lambda b, n: (0, n, 0)),
                  pl.BlockSpec((3, WINDOW, LANES), lambda b, n: (0, jnp.maximum(n - 1, 0), 0)),
                  pl.BlockSpec((1, LANES), lambda b, n: (0, 0)),
                  pl.BlockSpec((1, LANES), lambda b, n: (0, 0))],
        out_specs=pl.BlockSpec((WINDOW, Q_COLS), cur),
        out_shape=jax.ShapeDtypeStruct((t, Q_COLS), F32),
        compiler_params=pltpu.CompilerParams(dimension_semantics=("arbitrary", "arbitrary"),
                                             vmem_limit_bytes=VMEM_LIMIT),
        name="swa_attention",
    )(sinks, qkv, qkv, qkv, qkv, qkv, rope, rope,
      jnp.tile(q_gain, 2).reshape(1, LANES), jnp.tile(k_gain, 2).reshape(1, LANES))


def _conv_kernel(x_ref, g_ref, win_ref, cw_ref, wout_ref, o_ref, zprev_ref):
    n = pl.program_id(1)
    d = D_MODEL

    @pl.when(n == 0)
    def _():
        zprev_ref[...] = jnp.zeros_like(zprev_ref)

    x = x_ref[...]
    h = _rms(x, g_ref[...])
    bcu = jnp.dot(h.astype(BF16), win_ref[...], preferred_element_type=F32)
    gate_b = bcu[:, :d]
    z = bcu[:, d:2 * d] * bcu[:, 2 * d:]
    tm = z.shape[0]
    row = lax.broadcasted_iota(jnp.int32, z.shape, 0)
    prev = zprev_ref[...]
    p_last = prev[SUBLANES - 1:SUBLANES, :]
    p_last2 = prev[SUBLANES - 2:SUBLANES - 1, :]
    z1 = jnp.where(row == 0, p_last, pltpu.roll(z, 1, 0))
    z2 = jnp.where(row == 0, p_last2, jnp.where(row == 1, p_last, pltpu.roll(z, 2, 0)))
    cw = cw_ref[...]
    conv = cw[0:1, :] * z2 + cw[1:2, :] * z1 + cw[2:3, :] * z
    zprev_ref[...] = z[tm - SUBLANES:, :]
    o_ref[...] = x + jnp.dot((gate_b * conv).astype(BF16), wout_ref[...], preferred_element_type=F32)


def _conv_mixer(x, gain, w_in, conv_w, w_out, batch, seq, tm=256):
    t, d = x.shape
    nblk = seq // tm
    blk = lambda b, n: (b * nblk + n, 0)
    const = lambda b, n: (0, 0)
    return pl.pallas_call(
        _conv_kernel,
        grid=(batch, nblk),
        in_specs=[pl.BlockSpec((tm, d), blk),
                  pl.BlockSpec((1, d), const),
                  pl.BlockSpec((d, 3 * d), const),
                  pl.BlockSpec((CONV_WIDTH, d), const),
                  pl.BlockSpec((d, d), const)],
        out_specs=pl.BlockSpec((tm, d), blk),
        out_shape=jax.ShapeDtypeStruct((t, d), F32),
        scratch_shapes=[pltpu.VMEM((SUBLANES, d), F32)],
        compiler_params=pltpu.CompilerParams(dimension_semantics=("arbitrary", "arbitrary"),
                                             vmem_limit_bytes=VMEM_LIMIT),
        name="conv_mixer",
    )(x, gain.reshape(1, d), w_in.astype(BF16), conv_w, w_out.astype(BF16))


def _topk_axis0(s, k, ids=None, payload=None):
    n, tm = s.shape
    if ids is None:
        ids = lax.broadcasted_iota(jnp.int32, (n, tm), 0)
    krow = lax.broadcasted_iota(jnp.int32, (k, tm), 0)
    vals = jnp.zeros((k, tm), F32)
    picks = jnp.zeros((k, tm), jnp.int32)
    for r in range(k):
        m = jnp.max(s, axis=0, keepdims=True)
        pos = jnp.min(jnp.where(s == m, ids, jnp.iinfo(jnp.int32).max), axis=0, keepdims=True)
        sel = ids == pos
        if payload is None:
            picked = pos
        else:
            picked = jnp.sum(jnp.where(sel, payload, 0), axis=0, keepdims=True)
        vals = jnp.where(krow == r, m, vals)
        picks = jnp.where(krow == r, picked, picks)
        s = jnp.where(sel, -jnp.inf, s)
    return vals, picks


def _pair_candidates(s1, i1, s2, i2):
    k, tm = s1.shape
    sub = lax.broadcasted_iota(jnp.int32, (SUBLANES, tm), 0)
    scores, flat, expert = [], [], []
    for i in range(k // 2):
        width = k if i == 0 else SUBLANES
        sc = s1[i:i + 1, :] + s2[0:width, :]
        ex = i1[i:i + 1, :] * N_KEYS + i2[0:width, :]
        fl = i * k + lax.broadcasted_iota(jnp.int32, (width, tm), 0)
        reach = k // (i + 1)
        if reach < width:
            sc = jnp.where(sub < reach, sc, -jnp.inf)
        scores.append(sc)
        flat.append(fl)
        expert.append(ex)
    scores.append(s1[k // 2:, :] + s2[0:1, :])
    expert.append(i1[k // 2:, :] * N_KEYS + i2[0:1, :])
    flat.append((k // 2 + sub) * k)
    return jnp.concatenate(scores, axis=0), jnp.concatenate(flat, axis=0), jnp.concatenate(expert, axis=0)


def _route_kernel(x_ref, g_ref, wq_ref, keys_ref, h_ref, idx_ref, gate_ref):
    h = _rms(x_ref[...], g_ref[...])
    h_ref[...] = h
    q = jnp.dot(h.astype(BF16), wq_ref[...], preferred_element_type=F32).astype(BF16)
    idx_rows, gate_rows = [], []
    for head in range(PEER_HEADS):
        tops = []
        for part in range(2):
            col = (head * 2 + part) * QUERY_HALF
            s = lax.dot_general(keys_ref[head, part], q[:, col:col + QUERY_HALF],
                                (((1,), (1,)), ((), ())), preferred_element_type=F32)
            tops.append(_topk_axis0(s, PEER_TOPK))
        (s1, i1), (s2, i2) = tops
        cand, flat_ids, cand_idx = _pair_candidates(s1, i1, s2, i2)
        g_s, e_idx = _topk_axis0(cand, PEER_TOPK, ids=flat_ids, payload=cand_idx)
        e = jnp.exp(g_s - jnp.max(g_s, axis=0, keepdims=True))
        gate_rows.append(e / jnp.sum(e, axis=0, keepdims=True))
        idx_rows.append(e_idx * ROW_WORDS)
    idx_ref[...] = jnp.concatenate(idx_rows, axis=0).T
    gate_ref[...] = jnp.concatenate(gate_rows, axis=0).T


def _route(x, gain, w_query, sub_keys, first_tok, n_tok, tm=128):
    t, d = x.shape
    nq = w_query.shape[1]
    first_blk = first_tok // tm
    return pl.pallas_call(
        _route_kernel,
        grid=(n_tok // tm,),
        in_specs=[pl.BlockSpec((tm, d), lambda i: (i + first_blk, 0)),
                  pl.BlockSpec((1, d), lambda i: (0, 0)),
                  pl.BlockSpec((d, nq), lambda i: (0, 0)),
                  pl.BlockSpec((PEER_HEADS, 2, N_KEYS, QUERY_HALF), lambda i: (0, 0, 0, 0))],
        out_specs=[pl.BlockSpec((tm, d), lambda i: (i, 0)),
                   pl.BlockSpec((tm, SLOTS), lambda i: (i, 0)),
                   pl.BlockSpec((tm, SLOTS), lambda i: (i, 0))],
        out_shape=[jax.ShapeDtypeStruct((n_tok, d), F32),
                   jax.ShapeDtypeStruct((n_tok, SLOTS), jnp.int32),
                   jax.ShapeDtypeStruct((n_tok, SLOTS), F32)],
        compiler_params=pltpu.CompilerParams(dimension_semantics=("arbitrary",), vmem_limit_bytes=VMEM_LIMIT),
        name="peer_route",
    )(x, gain.reshape(1, d), w_query, sub_keys)


def _pack_table(tab):
    bits = lax.bitcast_convert_type(tab.astype(BF16), jnp.uint16).astype(jnp.uint32)
    half = D_MODEL // 2
    words = (bits[:, :half] << 16) | bits[:, half:]
    return lax.bitcast_convert_type(words, jnp.int32).reshape(tab.shape[0] * ROW_WORDS, LANES)


def _load_table_once(tab_hbm, tab, sem):
    @pl.when(pl.program_id(0) == 0)
    def _():
        cp = pltpu.make_async_copy(tab_hbm, tab, sem)
        cp.start()
        cp.wait()


def _gather_pair(tab, off_a, off_b):
    ra = tab[pl.ds(pl.multiple_of(off_a, ROW_WORDS), ROW_WORDS), :]
    rb = tab[pl.ds(pl.multiple_of(off_b, ROW_WORDS), ROW_WORDS), :]
    words = jnp.concatenate([ra, rb], axis=0)
    hi = pltpu.bitcast(words & jnp.int32(-65536), F32)
    lo = pltpu.bitcast(words << 16, F32)
    return hi, lo


TOKEN_UNROLL = 8


def _expert_in_kernel(*refs):
    idx_refs = refs[:TOKEN_UNROLL]
    h_ref, gate_ref, tab_hbm, w_ref, tab, sem, a_ref = refs[TOKEN_UNROLL:]
    _load_table_once(tab_hbm, tab, sem)
    sub = lax.broadcasted_iota(jnp.int32, (SUBLANES, LANES), 0)
    lane = lax.broadcasted_iota(jnp.int32, (SUBLANES, LANES), 1)
    own_half = (sub >= ROW_WORDS) == ((lane & 1) == 1)
    tb = h_ref.shape[0]

    def step(i, carry):
        xs = []
        for u in range(TOKEN_UNROLL):
            x = h_ref[i * TOKEN_UNROLL + u]
            xs.append((jnp.concatenate([x[0:ROW_WORDS], x[0:ROW_WORDS]], axis=0),
                       jnp.concatenate([x[ROW_WORDS:], x[ROW_WORDS:]], axis=0)))
        accs = [jnp.zeros((SUBLANES, LANES), F32) for _ in range(TOKEN_UNROLL)]
        for p in range(SLOTS // 2):
            for u in range(TOKEN_UNROLL):
                hi, lo = _gather_pair(tab, idx_refs[u][i, 2 * p], idx_refs[u][i, 2 * p + 1])
                part = jnp.sum(hi * xs[u][0] + lo * xs[u][1], axis=1, keepdims=True)
                accs[u] = jnp.where((lane >> 1) == p, part, accs[u])
        for u in range(TOKEN_UNROLL):
            a_ref[pl.ds(i * TOKEN_UNROLL + u, 1), :] = jnp.sum(jnp.where(own_half, accs[u], 0.0),
                                                                axis=0, keepdims=True)
        return carry

    lax.fori_loop(0, tb // TOKEN_UNROLL, step, 0)
    a = a_ref[...]
    w_ref[...] = gate_ref[...] * (0.5 * a * (1.0 + lax.erf(a * (1.0 / math.sqrt(2.0)))))


def _expert_out_kernel(*refs):
    idx_refs = refs[:TOKEN_UNROLL]
    w_ref, x_ref, tab_hbm, o_ref, tab, sem, wb_ref = refs[TOKEN_UNROLL:]
    _load_table_once(tab_hbm, tab, sem)
    sub = lax.broadcasted_iota(jnp.int32, (SUBLANES, LANES), 0)
    lower = sub < ROW_WORDS
    tb = x_ref.shape[0]

    def step(i, carry):
        for u in range(TOKEN_UNROLL):
            row = w_ref[pl.ds(i * TOKEN_UNROLL + u, 1), :]
            wb_ref[u] = jnp.broadcast_to(row, (SLOTS, LANES)).T
        acc_h = [jnp.zeros((SUBLANES, LANES), F32) for _ in range(TOKEN_UNROLL)]
        acc_l = [jnp.zeros((SUBLANES, LANES), F32) for _ in range(TOKEN_UNROLL)]
        for p in range(SLOTS // 2):
            ka, kb = 2 * p, 2 * p + 1
            for u in range(TOKEN_UNROLL):
                hi, lo = _gather_pair(tab, idx_refs[u][i, ka], idx_refs[u][i, kb])
                wa = jnp.broadcast_to(wb_ref[u, ka:ka + 1, :], (SUBLANES, LANES))
                wb = jnp.broadcast_to(wb_ref[u, kb:kb + 1, :], (SUBLANES, LANES))
                wt = jnp.where(lower, wa, wb)
                acc_h[u] = acc_h[u] + hi * wt
                acc_l[u] = acc_l[u] + lo * wt
        for u in range(TOKEN_UNROLL):
            t = i * TOKEN_UNROLL + u
            ah = acc_h[u] + pltpu.roll(acc_h[u], ROW_WORDS, 0)
            al = acc_l[u] + pltpu.roll(acc_l[u], ROW_WORDS, 0)
            o_ref[t] = x_ref[t] + jnp.where(lower, ah, al)
        return carry

    lax.fori_loop(0, tb // TOKEN_UNROLL, step, 0)


def _expert_specs(tb):
    un = TOKEN_UNROLL
    smem_blks = [pl.BlockSpec((None, tb // un, SLOTS), functools.partial(lambda u, i: (u, i, 0), u),
                              memory_space=pltpu.SMEM) for u in range(un)]
    vmem_blk = pl.BlockSpec((tb, SLOTS), lambda i: (i, 0))
    tok_blk = pl.BlockSpec((tb, D_MODEL // LANES, LANES), lambda i: (i, 0, 0))
    params = pltpu.CompilerParams(dimension_semantics=("arbitrary",), vmem_limit_bytes=VMEM_LIMIT)
    return smem_blks, vmem_blk, tok_blk, params


def _split_offsets(idx):
    n = idx.shape[0]
    return idx.reshape(n // TOKEN_UNROLL, TOKEN_UNROLL, SLOTS).transpose(1, 0, 2)


def _expert_in(h, idx_split, gate, u_packed, tb=128):
    n, d = h.shape
    smem_blks, vmem_blk, tok_blk, params = _expert_specs(tb)
    return pl.pallas_call(
        _expert_in_kernel,
        grid=(n // tb,),
        in_specs=smem_blks + [tok_blk, vmem_blk, pl.BlockSpec(memory_space=pl.ANY)],
        out_specs=vmem_blk,
        out_shape=jax.ShapeDtypeStruct((n, SLOTS), F32),
        scratch_shapes=[pltpu.VMEM(u_packed.shape, jnp.int32), pltpu.SemaphoreType.DMA, pltpu.VMEM((tb, SLOTS), F32)],
        compiler_params=params,
        name="peer_expert_in",
    )(*([idx_split] * TOKEN_UNROLL), h.reshape(n, d // LANES, LANES), gate, u_packed)


def _expert_out(x, idx_split, w, v_packed, tb=128):
    t, d = x.shape
    n = w.shape[0]
    smem_blks, vmem_blk, tok_blk, params = _expert_specs(tb)
    out = pl.pallas_call(
        _expert_out_kernel,
        grid=(n // tb,),
        in_specs=smem_blks + [vmem_blk, tok_blk, pl.BlockSpec(memory_space=pl.ANY)],
        out_specs=tok_blk,
        out_shape=jax.ShapeDtypeStruct((n, d // LANES, LANES), F32),
        scratch_shapes=[pltpu.VMEM(v_packed.shape, jnp.int32), pltpu.SemaphoreType.DMA,
                        pltpu.VMEM((TOKEN_UNROLL, SLOTS, LANES), F32)],
        compiler_params=params,
        name="peer_expert_out",
    )(*([idx_split] * TOKEN_UNROLL), w, x.reshape(t, d // LANES, LANES), v_packed)
    return out.reshape(n, d)


SC_LANES = 16
SC_WORKERS = 32
SC_CHUNK = 32
SC_GROUP = 8
SC_TOKENS = 10240


def _sc_params():
    cp = pltpu.CompilerParams()
    if "needs_layout_passes" in pltpu.CompilerParams.__dataclass_fields__:
        cp = dataclasses.replace(cp, needs_layout_passes=False)
    return cp


def _tree_sum(vals):
    while len(vals) > 1:
        vals = [vals[i] + vals[i + 1] for i in range(0, len(vals) - 1, 2)] + ([vals[-1]] if len(vals) % 2 else [])
    return vals[0]

def _sc_expert_in(table_words, idx, h):
    n_tok, d = h.shape
    per = n_tok // SC_WORKERS
    words = d // 2
    nq = words // SC_LANES // 2
    nchunk = SLOTS // SC_CHUNK
    group_chunks = SC_GROUP * nchunk
    mesh = plsc.VectorSubcoreMesh(core_axis_name="c", subcore_axis_name="s")

    @functools.partial(
        pl.kernel, mesh=mesh,
        out_type=jax.ShapeDtypeStruct((n_tok * SLOTS,), F32),
        scratch_types=[pltpu.VMEM((SC_GROUP * SLOTS,), jnp.int32), pltpu.VMEM((SC_GROUP, d), F32),
                       pltpu.VMEM((2, SC_CHUNK, words), jnp.int32), pltpu.VMEM((SC_GROUP * SLOTS,), F32),
                       pltpu.VMEM((SC_CHUNK, SC_LANES), F32),
                       pltpu.SemaphoreType.DMA, pltpu.SemaphoreType.DMA],
        compiler_params=_sc_params(),
        name="peer_expert_in_sc",
    )
    def body(tab_hbm, idx_hbm, h_hbm, a_hbm, idx_v, x_v, rows_v, a_v, part_v, sem0, sem1):
        base = (lax.axis_index("s") * 2 + lax.axis_index("c")) * per
        lanes = lax.iota(jnp.int32, SC_LANES)
        himask = jnp.full((SC_LANES,), -65536, jnp.int32)
        sems = (sem0, sem1)

        def gather(k, b):
            off = pl.multiple_of(k * SC_CHUNK, SC_CHUNK)
            return pltpu.make_async_copy(tab_hbm.at[idx_v.at[pl.ds(off, SC_CHUNK)]], rows_v.at[b], sems[b])

        def dots(k, b):
            tok = k // nchunk
            for q in range(2):
                first = q * nq
                xh = [x_v[tok, pl.ds((first + j) * SC_LANES, SC_LANES)] for j in range(nq)]
                xl = [x_v[tok, pl.ds(words + (first + j) * SC_LANES, SC_LANES)] for j in range(nq)]

                def row_body(r, carry):
                    prods = []
                    for j in range(nq):
                        wv = rows_v[b, r, pl.ds((first + j) * SC_LANES, SC_LANES)]
                        prods.append(plsc.bitcast(wv & himask, F32) * xh[j] + plsc.bitcast(wv << 16, F32) * xl[j])
                    acc = _tree_sum(prods)
                    if q == 0:
                        part_v[r, :] = acc
                    else:
                        part_v[r, :] = part_v[r, :] + acc
                    return carry
                lax.fori_loop(0, SC_CHUNK, row_body, 0)
            for g in range(SC_CHUNK // SC_LANES):
                def red_body(r, avec):
                    return jnp.where(lanes == r, jnp.sum(part_v[g * SC_LANES + r, :]), avec)
                avec = lax.fori_loop(0, SC_LANES, red_body, jnp.zeros((SC_LANES,), F32))
                a_v[pl.ds(pl.multiple_of(k * SC_CHUNK + g * SC_LANES, SC_LANES), SC_LANES)] = avec

        @pl.loop(0, per // SC_GROUP)
        def _(g):
            t0 = pl.multiple_of(base + g * SC_GROUP, SC_GROUP)
            pltpu.sync_copy(idx_hbm.at[pl.ds(t0 * SLOTS, SC_GROUP * SLOTS)], idx_v)
            pltpu.sync_copy(h_hbm.at[pl.ds(t0, SC_GROUP)], x_v)
            gather(0, 0).start()

            @pl.loop(0, group_chunks // 2)
            def _(kk):
                k0 = 2 * kk
                gather(k0 + 1, 1).start()
                gather(k0, 0).wait()
                dots(k0, 0)

                @pl.when(k0 + 2 < group_chunks)
                def _():
                    gather(k0 + 2, 0).start()
                gather(k0 + 1, 1).wait()
                dots(k0 + 1, 1)

            pltpu.sync_copy(a_v, a_hbm.at[pl.ds(t0 * SLOTS, SC_GROUP * SLOTS)])

    return body(table_words, idx, h)

def _sc_expert_out(table_words, idx, w, x, first_tok):
    d = x.shape[1]
    n_tok = w.shape[0] // SLOTS
    per = n_tok // SC_WORKERS
    words = d // 2
    nq = words // SC_LANES // 2
    nchunk = SLOTS // SC_CHUNK
    group_chunks = SC_GROUP * nchunk
    mesh = plsc.VectorSubcoreMesh(core_axis_name="c", subcore_axis_name="s")

    @functools.partial(
        pl.kernel, mesh=mesh,
        out_type=jax.ShapeDtypeStruct((n_tok, d), F32),
        scratch_types=[pltpu.VMEM((SC_GROUP * SLOTS,), jnp.int32), pltpu.VMEM((SC_GROUP * SLOTS,), F32),
                       pltpu.VMEM((2, SC_CHUNK, words), jnp.int32), pltpu.VMEM((SC_GROUP, d), F32),
                       pltpu.SemaphoreType.DMA, pltpu.SemaphoreType.DMA],
        compiler_params=_sc_params(),
        name="peer_expert_out_sc",
    )
    def body(tab_hbm, idx_hbm, w_hbm, x_hbm, o_hbm, idx_v, w_v, rows_v, y_v, sem0, sem1):
        base = (lax.axis_index("s") * 2 + lax.axis_index("c")) * per
        zero = jnp.zeros((SC_LANES,), jnp.int32)
        himask = jnp.full((SC_LANES,), -65536, jnp.int32)
        sems = (sem0, sem1)

        def gather(k, b):
            off = pl.multiple_of(k * SC_CHUNK, SC_CHUNK)
            return pltpu.make_async_copy(tab_hbm.at[idx_v.at[pl.ds(off, SC_CHUNK)]], rows_v.at[b], sems[b])

        def accumulate(k, b):
            tok = k // nchunk
            for q in range(2):
                first = q * nq
                acc0 = (tuple(y_v[tok, pl.ds((first + j) * SC_LANES, SC_LANES)] for j in range(nq))
                        + tuple(y_v[tok, pl.ds(words + (first + j) * SC_LANES, SC_LANES)] for j in range(nq)))

                def row_body(r, accs):
                    ws = plsc.load_gather(w_v, [zero + (k * SC_CHUNK + r)])
                    hi_acc, lo_acc = [], []
                    for j in range(nq):
                        wv = rows_v[b, r, pl.ds((first + j) * SC_LANES, SC_LANES)]
                        hi_acc.append(accs[j] + plsc.bitcast(wv & himask, F32) * ws)
                        lo_acc.append(accs[nq + j] + plsc.bitcast(wv << 16, F32) * ws)
                    return tuple(hi_acc) + tuple(lo_acc)

                accs = lax.fori_loop(0, SC_CHUNK, row_body, acc0)
                for j in range(nq):
                    y_v[tok, pl.ds((first + j) * SC_LANES, SC_LANES)] = accs[j]
                    y_v[tok, pl.ds(words + (first + j) * SC_LANES, SC_LANES)] = accs[nq + j]

        @pl.loop(0, per // SC_GROUP)
        def _(g):
            t0 = pl.multiple_of(base + g * SC_GROUP, SC_GROUP)
            pltpu.sync_copy(idx_hbm.at[pl.ds(t0 * SLOTS, SC_GROUP * SLOTS)], idx_v)
            pltpu.sync_copy(w_hbm.at[pl.ds(t0 * SLOTS, SC_GROUP * SLOTS)], w_v)
            pltpu.sync_copy(x_hbm.at[pl.ds(first_tok + t0, SC_GROUP)], y_v)
            gather(0, 0).start()

            @pl.loop(0, group_chunks // 2)
            def _(kk):
                k0 = 2 * kk
                gather(k0 + 1, 1).start()
                gather(k0, 0).wait()
                accumulate(k0, 0)

                @pl.when(k0 + 2 < group_chunks)
                def _():
                    gather(k0 + 2, 0).start()
                gather(k0 + 1, 1).wait()
                accumulate(k0 + 1, 1)

            pltpu.sync_copy(y_v, o_hbm.at[pl.ds(t0, SC_GROUP)])

    return body(table_words, idx, w, x)


def _gelu_gate_kernel(a_ref, g_ref, after_ref, w_ref):
    del after_ref
    a = a_ref[...]
    w_ref[...] = g_ref[...] * (0.5 * a * (1.0 + lax.erf(a * (1.0 / math.sqrt(2.0)))))


def _gelu_gate(a, gate, after):
    return pl.pallas_call(_gelu_gate_kernel, out_shape=jax.ShapeDtypeStruct(a.shape, F32),
                          compiler_params=pltpu.CompilerParams(vmem_limit_bytes=VMEM_LIMIT),
                          name="peer_gelu_gate")(a, gate, after)


def _peer(x, gain, w_query, sub_keys, expert_u, expert_v):
    t = x.shape[0]
    t_tc = t - SC_TOKENS
    wq, keys = w_query.astype(BF16), sub_keys.astype(BF16)
    u_packed, v_packed = _pack_table(expert_u), _pack_table(expert_v)
    u_rows = u_packed.reshape(expert_u.shape[0], ROW_WORDS * LANES)
    v_rows = v_packed.reshape(expert_v.shape[0], ROW_WORDS * LANES)

    h_sc, idx_sc, gate_sc = _route(x, gain, wq, keys, t_tc, SC_TOKENS)
    experts_sc = (idx_sc // ROW_WORDS).reshape(-1)
    a_sc = _sc_expert_in(u_rows, experts_sc, h_sc)

    h, idx, gate = _route(x, gain, wq, keys, 0, t_tc)
    idx_split = _split_offsets(idx)
    w = _expert_in(h, idx_split, gate, u_packed)

    w_sc = _gelu_gate(a_sc.reshape(SC_TOKENS, SLOTS), gate_sc, w[:SUBLANES])
    out_sc = _sc_expert_out(v_rows, experts_sc, w_sc.reshape(-1), x, t_tc)
    out_tc = _expert_out(x, idx_split, w, v_packed)
    return jnp.concatenate([out_tc, out_sc], axis=0)


def kernel(x, norm_mix, norm_ffn, attn_w_qkv, attn_q_norm, attn_k_norm, attn_sinks, attn_w_o, conv_w_in, conv_w, conv_w_out, peer_w_query, peer_sub_keys, peer_u, peer_v):
    batch, seq, d = x.shape
    xt = x.reshape(batch * seq, d)
    for i in range(norm_mix.shape[0]):
        j = i // 2
        if i % 2 == 0:
            qkv = _norm_matmul(xt, norm_mix[i], attn_w_qkv[j].astype(BF16))
            o = _attention(qkv, attn_q_norm[j], attn_k_norm[j], attn_sinks[j], batch, seq)
            xt = _matmul_residual(o, attn_w_o[j].astype(BF16), xt)
        else:
            xt = _conv_mixer(xt, norm_mix[i], conv_w_in[j], conv_w[j], conv_w_out[j], batch, seq)
        xt = _peer(xt, norm_ffn[i], peer_w_query[i], peer_sub_keys[i], peer_u[i], peer_v[i])
    return xt.reshape(batch, seq, d)
```

```python
import dataclasses
import functools
import math

import jax
import jax.numpy as jnp
from jax import lax
from jax.experimental import pallas as pl
from jax.experimental.pallas import tpu as pltpu
from jax.experimental.pallas import tpu_sc as plsc

D_MODEL = 1024
RMS_EPS = 1e-6

HEAD_DIM = 64
N_Q_HEADS = 16
N_KV_HEADS = 4
GROUP = N_Q_HEADS // N_KV_HEADS
WINDOW = 128
ROT_DIM = HEAD_DIM // 4
ROPE_THETA = 500000.0
Q_COLS = N_Q_HEADS * HEAD_DIM
KV_COLS = N_KV_HEADS * HEAD_DIM
NEG_INF = -1e30

CONV_WIDTH = 3

PEER_HEADS = 8
N_KEYS = 128
N_EXPERTS = N_KEYS * N_KEYS
PEER_TOPK = 16
QUERY_HALF = 128
SLOTS = PEER_HEADS * PEER_TOPK

LANES = 128
SUBLANES = 8
ROW_WORDS = D_MODEL // 2 // LANES
VMEM_LIMIT = 48 * 1024 * 1024

BF16 = jnp.bfloat16
F32 = jnp.float32


def _rms(x, gain):
    return x * lax.rsqrt(jnp.mean(x * x, axis=-1, keepdims=True) + RMS_EPS) * gain


def _norm_matmul_kernel(x_ref, g_ref, w_ref, o_ref):
    h = _rms(x_ref[...], g_ref[...])
    o_ref[...] = jnp.dot(h.astype(BF16), w_ref[...], preferred_element_type=F32)


def _norm_matmul(x, gain, w, tm=512):
    t, d = x.shape
    n = w.shape[1]
    return pl.pallas_call(
        _norm_matmul_kernel,
        grid=(t // tm,),
        in_specs=[pl.BlockSpec((tm, d), lambda i: (i, 0)),
                  pl.BlockSpec((1, d), lambda i: (0, 0)),
                  pl.BlockSpec((d, n), lambda i: (0, 0))],
        out_specs=pl.BlockSpec((tm, n), lambda i: (i, 0)),
        out_shape=jax.ShapeDtypeStruct((t, n), F32),
        compiler_params=pltpu.CompilerParams(dimension_semantics=("arbitrary",), vmem_limit_bytes=VMEM_LIMIT),
        name="norm_matmul",
    )(x, gain.reshape(1, d), w)


def _matmul_residual_kernel(a_ref, w_ref, r_ref, o_ref):
    o_ref[...] = r_ref[...] + jnp.dot(a_ref[...].astype(BF16), w_ref[...], preferred_element_type=F32)


def _matmul_residual(a, w, res, tm=512):
    t, k = a.shape
    n = w.shape[1]
    return pl.pallas_call(
        _matmul_residual_kernel,
        grid=(t // tm,),
        in_specs=[pl.BlockSpec((tm, k), lambda i: (i, 0)),
                  pl.BlockSpec((k, n), lambda i: (0, 0)),
                  pl.BlockSpec((tm, n), lambda i: (i, 0))],
        out_specs=pl.BlockSpec((tm, n), lambda i: (i, 0)),
        out_shape=jax.ShapeDtypeStruct((t, n), F32),
        compiler_params=pltpu.CompilerParams(dimension_semantics=("arbitrary",), vmem_limit_bytes=VMEM_LIMIT),
        name="matmul_residual",
    )(a, w, res)


def _rope_tables(seq):
    half = ROT_DIM // 2
    freqs = ROPE_THETA ** (-jnp.arange(0, ROT_DIM, 2, dtype=F32) / ROT_DIM)
    ang = jnp.arange(seq, dtype=F32)[:, None] * freqs[None, :]
    cos, sin = jnp.cos(ang), jnp.sin(ang)
    ones = jnp.ones((seq, HEAD_DIM - ROT_DIM), F32)
    zeros = jnp.zeros((seq, HEAD_DIM - ROT_DIM), F32)
    zh = jnp.zeros((seq, half), F32)
    c = jnp.concatenate([cos, cos, ones], axis=1)
    s_next = jnp.concatenate([-sin, zh, zeros], axis=1)
    s_prev = jnp.concatenate([zh, sin, zeros], axis=1)
    return jnp.stack([jnp.tile(c, (1, 2)), jnp.tile(s_next, (1, 2)), jnp.tile(s_prev, (1, 2))])


def _head_norm_rope(x, gain2, rope, lo):
    sq = x * x
    s_lo = jnp.sum(jnp.where(lo, sq, 0.0), axis=1, keepdims=True)
    s_hi = jnp.sum(jnp.where(lo, 0.0, sq), axis=1, keepdims=True)
    ms = jnp.where(lo, s_lo, s_hi) * (1.0 / HEAD_DIM)
    xn = x * lax.rsqrt(ms + RMS_EPS) * gain2
    half = ROT_DIM // 2
    return xn * rope[0] + pltpu.roll(xn, LANES - half, 1) * rope[1] + pltpu.roll(xn, half, 1) * rope[2]


def _attn_kernel(sinks_ref, q_ref, kc_ref, kp_ref, vc_ref, vp_ref, rc_ref, rp_ref, qg_ref, kg_ref, o_ref):
    n = pl.program_id(1)
    lo = lax.broadcasted_iota(jnp.int32, (WINDOW, LANES), 1) < HEAD_DIM
    rope_c = rc_ref[...]
    rope_p = rp_ref[...]
    qg = qg_ref[...]
    kg = kg_ref[...]

    rows = GROUP * WINDOW
    qi = lax.broadcasted_iota(jnp.int32, (rows, 2 * WINDOW), 0) & (WINDOW - 1)
    ki = lax.broadcasted_iota(jnp.int32, (rows, 2 * WINDOW), 1)
    rel = WINDOW + qi - ki
    valid = (rel >= 0) & (rel < WINDOW) & ((n > 0) | (ki >= WINDOW))
    head_of_row = lax.broadcasted_iota(jnp.int32, (rows, 1), 0) // WINDOW
    scale = 1.0 / math.sqrt(HEAD_DIM)

    q2 = [_head_norm_rope(q_ref[:, c * LANES:(c + 1) * LANES], qg, rope_c, lo).astype(BF16)
          for c in range(Q_COLS // LANES)]
    for c in range(KV_COLS // LANES):
        cols = slice(c * LANES, (c + 1) * LANES)
        kprev = _head_norm_rope(kp_ref[:, cols], kg, rope_p, lo)
        kcur = _head_norm_rope(kc_ref[:, cols], kg, rope_c, lo)
        kfull = jnp.concatenate([kprev, kcur], axis=0).astype(BF16)
        vfull = jnp.concatenate([vp_ref[:, cols], vc_ref[:, cols]], axis=0).astype(BF16)
        for hh in range(LANES // HEAD_DIM):
            h = (LANES // HEAD_DIM) * c + hh
            kh = kfull[:, hh * HEAD_DIM:(hh + 1) * HEAD_DIM]
            vh = vfull[:, hh * HEAD_DIM:(hh + 1) * HEAD_DIM]
            heads = [GROUP * h + g for g in range(GROUP)]
            q4 = jnp.concatenate([q2[j // 2][:, (j % 2) * HEAD_DIM:(j % 2 + 1) * HEAD_DIM] for j in heads], axis=0)
            s = lax.dot_general(q4, kh, (((1,), (1,)), ((), ())), preferred_element_type=F32) * scale
            s = jnp.where(valid, s, NEG_INF)
            sink = jnp.zeros((rows, 1), F32)
            for g, j in enumerate(heads):
                sink = jnp.where(head_of_row == g, sinks_ref[j], sink)
            m = jnp.maximum(jnp.max(s, axis=1, keepdims=True), sink)
            p = jnp.exp(s - m)
            denom = jnp.sum(p, axis=1, keepdims=True) + jnp.exp(sink - m)
            o = jnp.dot(p.astype(BF16), vh, preferred_element_type=F32) / denom
            for g, j in enumerate(heads):
                o_ref[:, j * HEAD_DIM:(j + 1) * HEAD_DIM] = o[g * WINDOW:(g + 1) * WINDOW]


def _attention(qkv, q_gain, k_gain, sinks, batch, seq):
    t = batch * seq
    nb = seq // WINDOW
    rope = _rope_tables(seq)
    kcol = Q_COLS // KV_COLS
    cur = lambda b, n: (b * nb + n, 0)
    kcur = lambda b, n: (b * nb + n, kcol)
    kprev = lambda b, n: (b * nb + jnp.maximum(n - 1, 0), kcol)
    vcur = lambda b, n: (b * nb + n, kcol + 1)
    vprev = lambda b, n: (b * nb + jnp.maximum(n - 1, 0), kcol + 1)
    return pl.pallas_call(
        _attn_kernel,
        grid=(batch, nb),
        in_specs=[pl.BlockSpec(memory_space=pltpu.SMEM),
                  pl.BlockSpec((WINDOW, Q_COLS), cur),
                  pl.BlockSpec((WINDOW, KV_COLS), kcur),
                  pl.BlockSpec((WINDOW, KV_COLS), kprev),
                  pl.BlockSpec((WINDOW, KV_COLS), vcur),
                  pl.BlockSpec((WINDOW, KV_COLS), vprev),
                  pl.BlockSpec((3, WINDOW, LANES), lambda b, n: (0, n, 0)),
                  pl.BlockSpec((3, WINDOW, LANES), lambda b, n: (0, jnp.maximum(n - 1, 0), 0)),
                  pl.BlockSpec((1, LANES), lambda b, n: (0, 0)),
                  pl.BlockSpec((1, LANES), lambda b, n: (0, 0))],
        out_specs=pl.BlockSpec((WINDOW, Q_COLS), cur),
        out_shape=jax.ShapeDtypeStruct((t, Q_COLS), F32),
        compiler_params=pltpu.CompilerParams(dimension_semantics=("arbitrary", "arbitrary"),
                                             vmem_limit_bytes=VMEM_LIMIT),
        name="swa_attention",
    )(sinks, qkv, qkv, qkv, qkv, qkv, rope, rope,
      jnp.tile(q_gain, 2).reshape(1, LANES), jnp.tile(k_gain, 2).reshape(1, LANES))


def _conv_kernel(x_ref, g_ref, win_ref, cw_ref, wout_ref, o_ref, zprev_ref):
    n = pl.program_id(1)
    d = D_MODEL

    @pl.when(n == 0)
    def _():
        zprev_ref[...] = jnp.zeros_like(zprev_ref)

    x = x_ref[...]
    h = _rms(x, g_ref[...])
    bcu = jnp.dot(h.astype(BF16), win_ref[...], preferred_element_type=F32)
    gate_b = bcu[:, :d]
    z = bcu[:, d:2 * d] * bcu[:, 2 * d:]
    tm = z.shape[0]
    row = lax.broadcasted_iota(jnp.int32, z.shape, 0)
    prev = zprev_ref[...]
    p_last = prev[SUBLANES - 1:SUBLANES, :]
    p_last2 = prev[SUBLANES - 2:SUBLANES - 1, :]
    z1 = jnp.where(row == 0, p_last, pltpu.roll(z, 1, 0))
    z2 = jnp.where(row == 0, p_last2, jnp.where(row == 1, p_last, pltpu.roll(z, 2, 0)))
    cw = cw_ref[...]
    conv = cw[0:1, :] * z2 + cw[1:2, :] * z1 + cw[2:3, :] * z
    zprev_ref[...] = z[tm - SUBLANES:, :]
    o_ref[...] = x + jnp.dot((gate_b * conv).astype(BF16), wout_ref[...], preferred_element_type=F32)


def _conv_mixer(x, gain, w_in, conv_w, w_out, batch, seq, tm=256):
    t, d = x.shape
    nblk = seq // tm
    blk = lambda b, n: (b * nblk + n, 0)
    const = lambda b, n: (0, 0)
    return pl.pallas_call(
        _conv_kernel,
        grid=(batch, nblk),
        in_specs=[pl.BlockSpec((tm, d), blk),
                  pl.BlockSpec((1, d), const),
                  pl.BlockSpec((d, 3 * d), const),
                  pl.BlockSpec((CONV_WIDTH, d), const),
                  pl.BlockSpec((d, d), const)],
        out_specs=pl.BlockSpec((tm, d), blk),
        out_shape=jax.ShapeDtypeStruct((t, d), F32),
        scratch_shapes=[pltpu.VMEM((SUBLANES, d), F32)],
        compiler_params=pltpu.CompilerParams(dimension_semantics=("arbitrary", "arbitrary"),
                                             vmem_limit_bytes=VMEM_LIMIT),
        name="conv_mixer",
    )(x, gain.reshape(1, d), w_in.astype(BF16), conv_w, w_out.astype(BF16))


def _topk_axis0(s, k, ids=None, payload=None):
    n, tm = s.shape
    if ids is None:
        ids = lax.broadcasted_iota(jnp.int32, (n, tm), 0)
    krow = lax.broadcasted_iota(jnp.int32, (k, tm), 0)
    vals = jnp.zeros((k, tm), F32)
    picks = jnp.zeros((k, tm), jnp.int32)
    for r in range(k):
        m = jnp.max(s, axis=0, keepdims=True)
        pos = jnp.min(jnp.where(s == m, ids, jnp.iinfo(jnp.int32).max), axis=0, keepdims=True)
        sel = ids == pos
        if payload is None:
            picked = pos
        else:
            picked = jnp.sum(jnp.where(sel, payload, 0), axis=0, keepdims=True)
        vals = jnp.where(krow == r, m, vals)
        picks = jnp.where(krow == r, picked, picks)
        s = jnp.where(sel, -jnp.inf, s)
    return vals, picks


def _pair_candidates(s1, i1, s2, i2):
    k, tm = s1.shape
    sub = lax.broadcasted_iota(jnp.int32, (SUBLANES, tm), 0)
    scores, flat, expert = [], [], []
    for i in range(k // 2):
        width = k if i == 0 else SUBLANES
        sc = s1[i:i + 1, :] + s2[0:width, :]
        ex = i1[i:i + 1, :] * N_KEYS + i2[0:width, :]
        fl = i * k + lax.broadcasted_iota(jnp.int32, (width, tm), 0)
        reach = k // (i + 1)
        if reach < width:
            sc = jnp.where(sub < reach, sc, -jnp.inf)
        scores.append(sc)
        flat.append(fl)
        expert.append(ex)
    scores.append(s1[k // 2:, :] + s2[0:1, :])
    expert.append(i1[k // 2:, :] * N_KEYS + i2[0:1, :])
    flat.append((k // 2 + sub) * k)
    return jnp.concatenate(scores, axis=0), jnp.concatenate(flat, axis=0), jnp.concatenate(expert, axis=0)


def _route_kernel(x_ref, g_ref, wq_ref, keys_ref, after_ref, h_ref, idx_ref, gate_ref):
    del after_ref
    h = _rms(x_ref[...], g_ref[...])
    h_ref[...] = h
    q = jnp.dot(h.astype(BF16), wq_ref[...], preferred_element_type=F32).astype(BF16)
    idx_rows, gate_rows = [], []
    for head in range(PEER_HEADS):
        tops = []
        for part in range(2):
            col = (head * 2 + part) * QUERY_HALF
            s = lax.dot_general(keys_ref[head, part], q[:, col:col + QUERY_HALF],
                                (((1,), (1,)), ((), ())), preferred_element_type=F32)
            tops.append(_topk_axis0(s, PEER_TOPK))
        (s1, i1), (s2, i2) = tops
        cand, flat_ids, cand_idx = _pair_candidates(s1, i1, s2, i2)
        g_s, e_idx = _topk_axis0(cand, PEER_TOPK, ids=flat_ids, payload=cand_idx)
        e = jnp.exp(g_s - jnp.max(g_s, axis=0, keepdims=True))
        gate_rows.append(e / jnp.sum(e, axis=0, keepdims=True))
        idx_rows.append(e_idx * ROW_WORDS)
    idx_ref[...] = jnp.concatenate(idx_rows, axis=0).T
    gate_ref[...] = jnp.concatenate(gate_rows, axis=0).T


def _route(x, gain, w_query, sub_keys, first_tok, n_tok, after, tm=128):
    t, d = x.shape
    nq = w_query.shape[1]
    first_blk = first_tok // tm
    return pl.pallas_call(
        _route_kernel,
        grid=(n_tok // tm,),
        in_specs=[pl.BlockSpec((tm, d), lambda i: (i + first_blk, 0)),
                  pl.BlockSpec((1, d), lambda i: (0, 0)),
                  pl.BlockSpec((d, nq), lambda i: (0, 0)),
                  pl.BlockSpec((PEER_HEADS, 2, N_KEYS, QUERY_HALF), lambda i: (0, 0, 0, 0)),
                  pl.BlockSpec((1, d), lambda i: (0, 0))],
        out_specs=[pl.BlockSpec((tm, d), lambda i: (i, 0)),
                   pl.BlockSpec((tm, SLOTS), lambda i: (i, 0)),
                   pl.BlockSpec((tm, SLOTS), lambda i: (i, 0))],
        out_shape=[jax.ShapeDtypeStruct((n_tok, d), F32),
                   jax.ShapeDtypeStruct((n_tok, SLOTS), jnp.int32),
                   jax.ShapeDtypeStruct((n_tok, SLOTS), F32)],
        compiler_params=pltpu.CompilerParams(dimension_semantics=("arbitrary",), vmem_limit_bytes=VMEM_LIMIT),
        name="peer_route",
    )(x, gain.reshape(1, d), w_query, sub_keys, after)


def _pack_table(tab):
    bits = lax.bitcast_convert_type(tab.astype(BF16), jnp.uint16).astype(jnp.uint32)
    half = D_MODEL // 2
    words = (bits[:, :half] << 16) | bits[:, half:]
    return lax.bitcast_convert_type(words, jnp.int32).reshape(tab.shape[0] * ROW_WORDS, LANES)


def _load_table_once(tab_hbm, tab, sem):
    @pl.when(pl.program_id(0) == 0)
    def _():
        cp = pltpu.make_async_copy(tab_hbm, tab, sem)
        cp.start()
        cp.wait()


def _gather_pair(tab, off_a, off_b):
    ra = tab[pl.ds(pl.multiple_of(off_a, ROW_WORDS), ROW_WORDS), :]
    rb = tab[pl.ds(pl.multiple_of(off_b, ROW_WORDS), ROW_WORDS), :]
    words = jnp.concatenate([ra, rb], axis=0)
    hi = pltpu.bitcast(words & jnp.int32(-65536), F32)
    lo = pltpu.bitcast(words << 16, F32)
    return hi, lo


TOKEN_UNROLL = 8


def _expert_in_kernel(*refs):
    idx_refs = refs[:TOKEN_UNROLL]
    h_ref, gate_ref, tab_hbm, w_ref, tab, sem, a_ref = refs[TOKEN_UNROLL:]
    _load_table_once(tab_hbm, tab, sem)
    sub = lax.broadcasted_iota(jnp.int32, (SUBLANES, LANES), 0)
    lane = lax.broadcasted_iota(jnp.int32, (SUBLANES, LANES), 1)
    own_half = (sub >= ROW_WORDS) == ((lane & 1) == 1)
    tb = h_ref.shape[0]

    def step(i, carry):
        xs = []
        for u in range(TOKEN_UNROLL):
            x = h_ref[i * TOKEN_UNROLL + u]
            xs.append((jnp.concatenate([x[0:ROW_WORDS], x[0:ROW_WORDS]], axis=0),
                       jnp.concatenate([x[ROW_WORDS:], x[ROW_WORDS:]], axis=0)))
        accs = [jnp.zeros((SUBLANES, LANES), F32) for _ in range(TOKEN_UNROLL)]
        for p in range(SLOTS // 2):
            for u in range(TOKEN_UNROLL):
                hi, lo = _gather_pair(tab, idx_refs[u][i, 2 * p], idx_refs[u][i, 2 * p + 1])
                part = jnp.sum(hi * xs[u][0] + lo * xs[u][1], axis=1, keepdims=True)
                accs[u] = jnp.where((lane >> 1) == p, part, accs[u])
        for u in range(TOKEN_UNROLL):
            a_ref[pl.ds(i * TOKEN_UNROLL + u, 1), :] = jnp.sum(jnp.where(own_half, accs[u], 0.0),
                                                                axis=0, keepdims=True)
        return carry

    lax.fori_loop(0, tb // TOKEN_UNROLL, step, 0)
    a = a_ref[...]
    w_ref[...] = gate_ref[...] * (0.5 * a * (1.0 + lax.erf(a * (1.0 / math.sqrt(2.0)))))


def _expert_out_kernel(*refs):
    idx_refs = refs[:TOKEN_UNROLL]
    w_ref, x_ref, tab_hbm, o_ref, tab, sem, wb_ref = refs[TOKEN_UNROLL:]
    _load_table_once(tab_hbm, tab, sem)
    sub = lax.broadcasted_iota(jnp.int32, (SUBLANES, LANES), 0)
    lower = sub < ROW_WORDS
    tb = x_ref.shape[0]

    def step(i, carry):
        for u in range(TOKEN_UNROLL):
            row = w_ref[pl.ds(i * TOKEN_UNROLL + u, 1), :]
            wb_ref[u] = jnp.broadcast_to(row, (SLOTS, LANES)).T
        acc_h = [jnp.zeros((SUBLANES, LANES), F32) for _ in range(TOKEN_UNROLL)]
        acc_l = [jnp.zeros((SUBLANES, LANES), F32) for _ in range(TOKEN_UNROLL)]
        for p in range(SLOTS // 2):
            ka, kb = 2 * p, 2 * p + 1
            for u in range(TOKEN_UNROLL):
                hi, lo = _gather_pair(tab, idx_refs[u][i, ka], idx_refs[u][i, kb])
                wa = jnp.broadcast_to(wb_ref[u, ka:ka + 1, :], (SUBLANES, LANES))
                wb = jnp.broadcast_to(wb_ref[u, kb:kb + 1, :], (SUBLANES, LANES))
                wt = jnp.where(lower, wa, wb)
                acc_h[u] = acc_h[u] + hi * wt
                acc_l[u] = acc_l[u] + lo * wt
        for u in range(TOKEN_UNROLL):
            t = i * TOKEN_UNROLL + u
            ah = acc_h[u] + pltpu.roll(acc_h[u], ROW_WORDS, 0)
            al = acc_l[u] + pltpu.roll(acc_l[u], ROW_WORDS, 0)
            o_ref[t] = x_ref[t] + jnp.where(lower, ah, al)
        return carry

    lax.fori_loop(0, tb // TOKEN_UNROLL, step, 0)


def _expert_specs(tb):
    un = TOKEN_UNROLL
    smem_blks = [pl.BlockSpec((None, tb // un, SLOTS), functools.partial(lambda u, i: (u, i, 0), u),
                              memory_space=pltpu.SMEM) for u in range(un)]
    vmem_blk = pl.BlockSpec((tb, SLOTS), lambda i: (i, 0))
    tok_blk = pl.BlockSpec((tb, D_MODEL // LANES, LANES), lambda i: (i, 0, 0))
    params = pltpu.CompilerParams(dimension_semantics=("arbitrary",), vmem_limit_bytes=VMEM_LIMIT)
    return smem_blks, vmem_blk, tok_blk, params


def _split_offsets(idx):
    n = idx.shape[0]
    return idx.reshape(n // TOKEN_UNROLL, TOKEN_UNROLL, SLOTS).transpose(1, 0, 2)


def _expert_in(h, idx_split, gate, u_packed, tb=128):
    n, d = h.shape
    smem_blks, vmem_blk, tok_blk, params = _expert_specs(tb)
    return pl.pallas_call(
        _expert_in_kernel,
        grid=(n // tb,),
        in_specs=smem_blks + [tok_blk, vmem_blk, pl.BlockSpec(memory_space=pl.ANY)],
        out_specs=vmem_blk,
        out_shape=jax.ShapeDtypeStruct((n, SLOTS), F32),
        scratch_shapes=[pltpu.VMEM(u_packed.shape, jnp.int32), pltpu.SemaphoreType.DMA, pltpu.VMEM((tb, SLOTS), F32)],
        compiler_params=params,
        name="peer_expert_in",
    )(*([idx_split] * TOKEN_UNROLL), h.reshape(n, d // LANES, LANES), gate, u_packed)


def _expert_out(x, idx_split, w, v_packed, tb=128):
    t, d = x.shape
    n = w.shape[0]
    smem_blks, vmem_blk, tok_blk, params = _expert_specs(tb)
    out = pl.pallas_call(
        _expert_out_kernel,
        grid=(n // tb,),
        in_specs=smem_blks + [vmem_blk, tok_blk, pl.BlockSpec(memory_space=pl.ANY)],
        out_specs=tok_blk,
        out_shape=jax.ShapeDtypeStruct((n, d // LANES, LANES), F32),
        scratch_shapes=[pltpu.VMEM(v_packed.shape, jnp.int32), pltpu.SemaphoreType.DMA,
                        pltpu.VMEM((TOKEN_UNROLL, SLOTS, LANES), F32)],
        compiler_params=params,
        name="peer_expert_out",
    )(*([idx_split] * TOKEN_UNROLL), w, x.reshape(t, d // LANES, LANES), v_packed)
    return out.reshape(n, d)


SC_LANES = 16
SC_WORKERS = 32
SC_CHUNK = 32
SC_GROUP = 8
SC_TOKENS = 12032


def _sc_params():
    cp = pltpu.CompilerParams()
    if "needs_layout_passes" in pltpu.CompilerParams.__dataclass_fields__:
        cp = dataclasses.replace(cp, needs_layout_passes=False)
    return cp


def _tree_sum(vals):
    while len(vals) > 1:
        vals = [vals[i] + vals[i + 1] for i in range(0, len(vals) - 1, 2)] + ([vals[-1]] if len(vals) % 2 else [])
    return vals[0]

def _sc_expert_in(table_words, idx, h):
    n_tok, d = h.shape
    per = n_tok // SC_WORKERS
    words = d // 2
    nq = words // SC_LANES // 2
    nchunk = SLOTS // SC_CHUNK
    group_chunks = SC_GROUP * nchunk
    mesh = plsc.VectorSubcoreMesh(core_axis_name="c", subcore_axis_name="s")

    @functools.partial(
        pl.kernel, mesh=mesh,
        out_type=jax.ShapeDtypeStruct((n_tok * SLOTS,), F32),
        scratch_types=[pltpu.VMEM((SC_GROUP * SLOTS,), jnp.int32), pltpu.VMEM((SC_GROUP, d), F32),
                       pltpu.VMEM((2, SC_CHUNK, words), jnp.int32), pltpu.VMEM((SC_GROUP * SLOTS,), F32),
                       pltpu.VMEM((SC_CHUNK, SC_LANES), F32),
                       pltpu.SemaphoreType.DMA, pltpu.SemaphoreType.DMA],
        compiler_params=_sc_params(),
        name="peer_expert_in_sc",
    )
    def body(tab_hbm, idx_hbm, h_hbm, a_hbm, idx_v, x_v, rows_v, a_v, part_v, sem0, sem1):
        base = (lax.axis_index("s") * 2 + lax.axis_index("c")) * per
        lanes = lax.iota(jnp.int32, SC_LANES)
        himask = jnp.full((SC_LANES,), -65536, jnp.int32)
        sems = (sem0, sem1)

        def gather(k, b):
            off = pl.multiple_of(k * SC_CHUNK, SC_CHUNK)
            return pltpu.make_async_copy(tab_hbm.at[idx_v.at[pl.ds(off, SC_CHUNK)]], rows_v.at[b], sems[b])

        def dots(k, b):
            tok = k // nchunk
            for q in range(2):
                first = q * nq
                xh = [x_v[tok, pl.ds((first + j) * SC_LANES, SC_LANES)] for j in range(nq)]
                xl = [x_v[tok, pl.ds(words + (first + j) * SC_LANES, SC_LANES)] for j in range(nq)]

                def row_body(r, carry):
                    prods = []
                    for j in range(nq):
                        wv = rows_v[b, r, pl.ds((first + j) * SC_LANES, SC_LANES)]
                        prods.append(plsc.bitcast(wv & himask, F32) * xh[j] + plsc.bitcast(wv << 16, F32) * xl[j])
                    acc = _tree_sum(prods)
                    if q == 0:
                        part_v[r, :] = acc
                    else:
                        part_v[r, :] = part_v[r, :] + acc
                    return carry
                lax.fori_loop(0, SC_CHUNK, row_body, 0)
            for g in range(SC_CHUNK // SC_LANES):
                def red_body(r, avec):
                    return jnp.where(lanes == r, jnp.sum(part_v[g * SC_LANES + r, :]), avec)
                avec = lax.fori_loop(0, SC_LANES, red_body, jnp.zeros((SC_LANES,), F32))
                a_v[pl.ds(pl.multiple_of(k * SC_CHUNK + g * SC_LANES, SC_LANES), SC_LANES)] = avec

        @pl.loop(0, per // SC_GROUP)
        def _(g):
            t0 = pl.multiple_of(base + g * SC_GROUP, SC_GROUP)
            pltpu.sync_copy(idx_hbm.at[pl.ds(t0 * SLOTS, SC_GROUP * SLOTS)], idx_v)
            pltpu.sync_copy(h_hbm.at[pl.ds(t0, SC_GROUP)], x_v)
            gather(0, 0).start()

            @pl.loop(0, group_chunks // 2)
            def _(kk):
                k0 = 2 * kk
                gather(k0 + 1, 1).start()
                gather(k0, 0).wait()
                dots(k0, 0)

                @pl.when(k0 + 2 < group_chunks)
                def _():
                    gather(k0 + 2, 0).start()
                gather(k0 + 1, 1).wait()
                dots(k0 + 1, 1)

            pltpu.sync_copy(a_v, a_hbm.at[pl.ds(t0 * SLOTS, SC_GROUP * SLOTS)])

    return body(table_words, idx, h)

def _sc_expert_out(table_words, idx, w, x, first_tok):
    d = x.shape[1]
    n_tok = w.shape[0] // SLOTS
    per = n_tok // SC_WORKERS
    words = d // 2
    nq = words // SC_LANES // 2
    nchunk = SLOTS // SC_CHUNK
    group_chunks = SC_GROUP * nchunk
    mesh = plsc.VectorSubcoreMesh(core_axis_name="c", subcore_axis_name="s")

    @functools.partial(
        pl.kernel, mesh=mesh,
        out_type=jax.ShapeDtypeStruct((n_tok, d), F32),
        scratch_types=[pltpu.VMEM((SC_GROUP * SLOTS,), jnp.int32), pltpu.VMEM((SC_GROUP * SLOTS,), F32),
                       pltpu.VMEM((2, SC_CHUNK, words), jnp.int32), pltpu.VMEM((SC_GROUP, d), F32),
                       pltpu.SemaphoreType.DMA, pltpu.SemaphoreType.DMA],
        compiler_params=_sc_params(),
        name="peer_expert_out_sc",
    )
    def body(tab_hbm, idx_hbm, w_hbm, x_hbm, o_hbm, idx_v, w_v, rows_v, y_v, sem0, sem1):
        base = (lax.axis_index("s") * 2 + lax.axis_index("c")) * per
        zero = jnp.zeros((SC_LANES,), jnp.int32)
        himask = jnp.full((SC_LANES,), -65536, jnp.int32)
        sems = (sem0, sem1)

        def gather(k, b):
            off = pl.multiple_of(k * SC_CHUNK, SC_CHUNK)
            return pltpu.make_async_copy(tab_hbm.at[idx_v.at[pl.ds(off, SC_CHUNK)]], rows_v.at[b], sems[b])

        def accumulate(k, b):
            tok = k // nchunk
            for q in range(2):
                first = q * nq
                acc0 = (tuple(y_v[tok, pl.ds((first + j) * SC_LANES, SC_LANES)] for j in range(nq))
                        + tuple(y_v[tok, pl.ds(words + (first + j) * SC_LANES, SC_LANES)] for j in range(nq)))

                def row_body(r, accs):
                    ws = plsc.load_gather(w_v, [zero + (k * SC_CHUNK + r)])
                    hi_acc, lo_acc = [], []
                    for j in range(nq):
                        wv = rows_v[b, r, pl.ds((first + j) * SC_LANES, SC_LANES)]
                        hi_acc.append(accs[j] + plsc.bitcast(wv & himask, F32) * ws)
                        lo_acc.append(accs[nq + j] + plsc.bitcast(wv << 16, F32) * ws)
                    return tuple(hi_acc) + tuple(lo_acc)

                accs = lax.fori_loop(0, SC_CHUNK, row_body, acc0)
                for j in range(nq):
                    y_v[tok, pl.ds((first + j) * SC_LANES, SC_LANES)] = accs[j]
                    y_v[tok, pl.ds(words + (first + j) * SC_LANES, SC_LANES)] = accs[nq + j]

        @pl.loop(0, per // SC_GROUP)
        def _(g):
            t0 = pl.multiple_of(base + g * SC_GROUP, SC_GROUP)
            pltpu.sync_copy(idx_hbm.at[pl.ds(t0 * SLOTS, SC_GROUP * SLOTS)], idx_v)
            pltpu.sync_copy(w_hbm.at[pl.ds(t0 * SLOTS, SC_GROUP * SLOTS)], w_v)
            pltpu.sync_copy(x_hbm.at[pl.ds(first_tok + t0, SC_GROUP)], y_v)
            gather(0, 0).start()

            @pl.loop(0, group_chunks // 2)
            def _(kk):
                k0 = 2 * kk
                gather(k0 + 1, 1).start()
                gather(k0, 0).wait()
                accumulate(k0, 0)

                @pl.when(k0 + 2 < group_chunks)
                def _():
                    gather(k0 + 2, 0).start()
                gather(k0 + 1, 1).wait()
                accumulate(k0 + 1, 1)

            pltpu.sync_copy(y_v, o_hbm.at[pl.ds(t0, SC_GROUP)])

    return body(table_words, idx, w, x)


def _gelu_gate_kernel(a_ref, g_ref, after_ref, w_ref):
    del after_ref
    a = a_ref[...]
    w_ref[...] = g_ref[...] * (0.5 * a * (1.0 + lax.erf(a * (1.0 / math.sqrt(2.0)))))


def _gelu_gate(a, gate, after):
    return pl.pallas_call(_gelu_gate_kernel, out_shape=jax.ShapeDtypeStruct(a.shape, F32),
                          compiler_params=pltpu.CompilerParams(vmem_limit_bytes=VMEM_LIMIT),
                          name="peer_gelu_gate")(a, gate, after)


def _peer(x, gain, w_query, sub_keys, expert_u, expert_v):
    t = x.shape[0]
    t_tc = t - SC_TOKENS
    wq, keys = w_query.astype(BF16), sub_keys.astype(BF16)
    u_packed, v_packed = _pack_table(expert_u), _pack_table(expert_v)
    u_rows = u_packed.reshape(expert_u.shape[0], ROW_WORDS * LANES)
    v_rows = v_packed.reshape(expert_v.shape[0], ROW_WORDS * LANES)

    h_sc, idx_sc, gate_sc = _route(x, gain, wq, keys, t_tc, SC_TOKENS, gain.reshape(1, -1))
    experts_sc = (idx_sc // ROW_WORDS).reshape(-1)
    a_sc = _sc_expert_in(u_rows, experts_sc, h_sc)

    h, idx, gate = _route(x, gain, wq, keys, 0, t_tc, h_sc[:1])
    idx_split = _split_offsets(idx)
    w = _expert_in(h, idx_split, gate, u_packed)

    w_sc = _gelu_gate(a_sc.reshape(SC_TOKENS, SLOTS), gate_sc, w[:SUBLANES])
    out_sc = _sc_expert_out(v_rows, experts_sc, w_sc.reshape(-1), x, t_tc)
    out_tc = _expert_out(x, idx_split, w, v_packed)
    return jnp.concatenate([out_tc, out_sc], axis=0)


def kernel(x, norm_mix, norm_ffn, attn_w_qkv, attn_q_norm, attn_k_norm, attn_sinks, attn_w_o, conv_w_in, conv_w, conv_w_out, peer_w_query, peer_sub_keys, peer_u, peer_v):
    batch, seq, d = x.shape
    xt = x.reshape(batch * seq, d)
    for i in range(norm_mix.shape[0]):
        j = i // 2
        if i % 2 == 0:
            qkv = _norm_matmul(xt, norm_mix[i], attn_w_qkv[j].astype(BF16))
            o = _attention(qkv, attn_q_norm[j], attn_k_norm[j], attn_sinks[j], batch, seq)
            xt = _matmul_residual(o, attn_w_o[j].astype(BF16), xt)
        else:
            xt = _conv_mixer(xt, norm_mix[i], conv_w_in[j], conv_w[j], conv_w_out[j], batch, seq)
        xt = _peer(xt, norm_ffn[i], peer_w_query[i], peer_sub_keys[i], peer_u[i], peer_v[i])
    return xt.reshape(batch, seq, d)
```

```python
import dataclasses
import functools
import math

import jax
import jax.numpy as jnp
from jax import lax
from jax.experimental import pallas as pl
from jax.experimental.pallas import tpu as pltpu
from jax.experimental.pallas import tpu_sc as plsc

D_MODEL = 1024
RMS_EPS = 1e-6

HEAD_DIM = 64
N_Q_HEADS = 16
N_KV_HEADS = 4
GROUP = N_Q_HEADS // N_KV_HEADS
WINDOW = 128
ROT_DIM = HEAD_DIM // 4
ROPE_THETA = 500000.0
Q_COLS = N_Q_HEADS * HEAD_DIM
KV_COLS = N_KV_HEADS * HEAD_DIM
NEG_INF = -1e30

CONV_WIDTH = 3

PEER_HEADS = 8
N_KEYS = 128
N_EXPERTS = N_KEYS * N_KEYS
PEER_TOPK = 16
QUERY_HALF = 128
SLOTS = PEER_HEADS * PEER_TOPK

LANES = 128
SUBLANES = 8
ROW_WORDS = D_MODEL // 2 // LANES
VMEM_LIMIT = 48 * 1024 * 1024

BF16 = jnp.bfloat16
F32 = jnp.float32


def _rms(x, gain):
    return x * lax.rsqrt(jnp.mean(x * x, axis=-1, keepdims=True) + RMS_EPS) * gain


def _norm_matmul_kernel(x_ref, g_ref, w_ref, o_ref):
    h = _rms(x_ref[...], g_ref[...])
    o_ref[...] = jnp.dot(h.astype(BF16), w_ref[...], preferred_element_type=F32)


def _norm_matmul(x, gain, w, tm=512):
    t, d = x.shape
    n = w.shape[1]
    return pl.pallas_call(
        _norm_matmul_kernel,
        grid=(t // tm,),
        in_specs=[pl.BlockSpec((tm, d), lambda i: (i, 0)),
                  pl.BlockSpec((1, d), lambda i: (0, 0)),
                  pl.BlockSpec((d, n), lambda i: (0, 0))],
        out_specs=pl.BlockSpec((tm, n), lambda i: (i, 0)),
        out_shape=jax.ShapeDtypeStruct((t, n), F32),
        compiler_params=pltpu.CompilerParams(dimension_semantics=("arbitrary",), vmem_limit_bytes=VMEM_LIMIT),
        name="norm_matmul",
    )(x, gain.reshape(1, d), w)


def _matmul_residual_kernel(a_ref, w_ref, r_ref, o_ref):
    o_ref[...] = r_ref[...] + jnp.dot(a_ref[...].astype(BF16), w_ref[...], preferred_element_type=F32)


def _matmul_residual(a, w, res, tm=512):
    t, k = a.shape
    n = w.shape[1]
    return pl.pallas_call(
        _matmul_residual_kernel,
        grid=(t // tm,),
        in_specs=[pl.BlockSpec((tm, k), lambda i: (i, 0)),
                  pl.BlockSpec((k, n), lambda i: (0, 0)),
                  pl.BlockSpec((tm, n), lambda i: (i, 0))],
        out_specs=pl.BlockSpec((tm, n), lambda i: (i, 0)),
        out_shape=jax.ShapeDtypeStruct((t, n), F32),
        compiler_params=pltpu.CompilerParams(dimension_semantics=("arbitrary",), vmem_limit_bytes=VMEM_LIMIT),
        name="matmul_residual",
    )(a, w, res)


def _rope_tables(seq):
    half = ROT_DIM // 2
    freqs = ROPE_THETA ** (-jnp.arange(0, ROT_DIM, 2, dtype=F32) / ROT_DIM)
    ang = jnp.arange(seq, dtype=F32)[:, None] * freqs[None, :]
    cos, sin = jnp.cos(ang), jnp.sin(ang)
    ones = jnp.ones((seq, HEAD_DIM - ROT_DIM), F32)
    zeros = jnp.zeros((seq, HEAD_DIM - ROT_DIM), F32)
    zh = jnp.zeros((seq, half), F32)
    c = jnp.concatenate([cos, cos, ones], axis=1)
    s_next = jnp.concatenate([-sin, zh, zeros], axis=1)
    s_prev = jnp.concatenate([zh, sin, zeros], axis=1)
    return jnp.stack([jnp.tile(c, (1, 2)), jnp.tile(s_next, (1, 2)), jnp.tile(s_prev, (1, 2))])


def _head_norm_rope(x, gain2, rope, lo):
    sq = x * x
    s_lo = jnp.sum(jnp.where(lo, sq, 0.0), axis=1, keepdims=True)
    s_hi = jnp.sum(jnp.where(lo, 0.0, sq), axis=1, keepdims=True)
    ms = jnp.where(lo, s_lo, s_hi) * (1.0 / HEAD_DIM)
    xn = x * lax.rsqrt(ms + RMS_EPS) * gain2
    half = ROT_DIM // 2
    return xn * rope[0] + pltpu.roll(xn, LANES - half, 1) * rope[1] + pltpu.roll(xn, half, 1) * rope[2]


def _attn_kernel(sinks_ref, q_ref, kc_ref, kp_ref, vc_ref, vp_ref, rc_ref, rp_ref, qg_ref, kg_ref, o_ref):
    n = pl.program_id(1)
    lo = lax.broadcasted_iota(jnp.int32, (WINDOW, LANES), 1) < HEAD_DIM
    rope_c = rc_ref[...]
    rope_p = rp_ref[...]
    qg = qg_ref[...]
    kg = kg_ref[...]

    rows = GROUP * WINDOW
    qi = lax.broadcasted_iota(jnp.int32, (rows, 2 * WINDOW), 0) & (WINDOW - 1)
    ki = lax.broadcasted_iota(jnp.int32, (rows, 2 * WINDOW), 1)
    rel = WINDOW + qi - ki
    valid = (rel >= 0) & (rel < WINDOW) & ((n > 0) | (ki >= WINDOW))
    head_of_row = lax.broadcasted_iota(jnp.int32, (rows, 1), 0) // WINDOW
    scale = 1.0 / math.sqrt(HEAD_DIM)

    q2 = [_head_norm_rope(q_ref[:, c * LANES:(c + 1) * LANES], qg, rope_c, lo).astype(BF16)
          for c in range(Q_COLS // LANES)]
    for c in range(KV_COLS // LANES):
        cols = slice(c * LANES, (c + 1) * LANES)
        kprev = _head_norm_rope(kp_ref[:, cols], kg, rope_p, lo)
        kcur = _head_norm_rope(kc_ref[:, cols], kg, rope_c, lo)
        kfull = jnp.concatenate([kprev, kcur], axis=0).astype(BF16)
        vfull = jnp.concatenate([vp_ref[:, cols], vc_ref[:, cols]], axis=0).astype(BF16)
        for hh in range(LANES // HEAD_DIM):
            h = (LANES // HEAD_DIM) * c + hh
            kh = kfull[:, hh * HEAD_DIM:(hh + 1) * HEAD_DIM]
            vh = vfull[:, hh * HEAD_DIM:(hh + 1) * HEAD_DIM]
            heads = [GROUP * h + g for g in range(GROUP)]
            q4 = jnp.concatenate([q2[j // 2][:, (j % 2) * HEAD_DIM:(j % 2 + 1) * HEAD_DIM] for j in heads], axis=0)
            s = lax.dot_general(q4, kh, (((1,), (1,)), ((), ())), preferred_element_type=F32) * scale
            s = jnp.where(valid, s, NEG_INF)
            sink = jnp.zeros((rows, 1), F32)
            for g, j in enumerate(heads):
                sink = jnp.where(head_of_row == g, sinks_ref[j], sink)
            m = jnp.maximum(jnp.max(s, axis=1, keepdims=True), sink)
            p = jnp.exp(s - m)
            denom = jnp.sum(p, axis=1, keepdims=True) + jnp.exp(sink - m)
            o = jnp.dot(p.astype(BF16), vh, preferred_element_type=F32) / denom
            for g, j in enumerate(heads):
                o_ref[:, j * HEAD_DIM:(j + 1) * HEAD_DIM] = o[g * WINDOW:(g + 1) * WINDOW]


def _attention(qkv, q_gain, k_gain, sinks, batch, seq):
    t = batch * seq
    nb = seq // WINDOW
    rope = _rope_tables(seq)
    kcol = Q_COLS // KV_COLS
    cur = lambda b, n: (b * nb + n, 0)
    kcur = lambda b, n: (b * nb + n, kcol)
    kprev = lambda b, n: (b * nb + jnp.maximum(n - 1, 0), kcol)
    vcur = lambda b, n: (b * nb + n, kcol + 1)
    vprev = lambda b, n: (b * nb + jnp.maximum(n - 1, 0), kcol + 1)
    return pl.pallas_call(
        _attn_kernel,
        grid=(batch, nb),
        in_specs=[pl.BlockSpec(memory_space=pltpu.SMEM),
                  pl.BlockSpec((WINDOW, Q_COLS), cur),
                  pl.BlockSpec((WINDOW, KV_COLS), kcur),
                  pl.BlockSpec((WINDOW, KV_COLS), kprev),
                  pl.BlockSpec((WINDOW, KV_COLS), vcur),
                  pl.BlockSpec((WINDOW, KV_COLS), vprev),
                  pl.BlockSpec((3, WINDOW, LANES), lambda b, n: (0, n, 0)),
                  pl.BlockSpec((3, WINDOW, LANES), lambda b, n: (0, jnp.maximum(n - 1, 0), 0)),
                  pl.BlockSpec((1, LANES), lambda b, n: (0, 0)),
                  pl.BlockSpec((1, LANES), lambda b, n: (0, 0))],
        out_specs=pl.BlockSpec((WINDOW, Q_COLS), cur),
        out_shape=jax.ShapeDtypeStruct((t, Q_COLS), F32),
        compiler_params=pltpu.CompilerParams(dimension_semantics=("arbitrary", "arbitrary"),
                                             vmem_limit_bytes=VMEM_LIMIT),
        name="swa_attention",
    )(sinks, qkv, qkv, qkv, qkv, qkv, rope, rope,
      jnp.tile(q_gain, 2).reshape(1, LANES), jnp.tile(k_gain, 2).reshape(1, LANES))


def _conv_kernel(x_ref, g_ref, win_ref, cw_ref, wout_ref, o_ref, zprev_ref):
    n = pl.program_id(1)
    d = D_MODEL

    @pl.when(n == 0)
    def _():
        zprev_ref[...] = jnp.zeros_like(zprev_ref)

    x = x_ref[...]
    h = _rms(x, g_ref[...])
    bcu = jnp.dot(h.astype(BF16), win_ref[...], preferred_element_type=F32)
    gate_b = bcu[:, :d]
    z = bcu[:, d:2 * d] * bcu[:, 2 * d:]
    tm = z.shape[0]
    row = lax.broadcasted_iota(jnp.int32, z.shape, 0)
    prev = zprev_ref[...]
    p_last = prev[SUBLANES - 1:SUBLANES, :]
    p_last2 = prev[SUBLANES - 2:SUBLANES - 1, :]
    z1 = jnp.where(row == 0, p_last, pltpu.roll(z, 1, 0))
    z2 = jnp.where(row == 0, p_last2, jnp.where(row == 1, p_last, pltpu.roll(z, 2, 0)))
    cw = cw_ref[...]
    conv = cw[0:1, :] * z2 + cw[1:2, :] * z1 + cw[2:3, :] * z
    zprev_ref[...] = z[tm - SUBLANES:, :]
    o_ref[...] = x + jnp.dot((gate_b * conv).astype(BF16), wout_ref[...], preferred_element_type=F32)


def _conv_mixer(x, gain, w_in, conv_w, w_out, batch, seq, tm=256):
    t, d = x.shape
    nblk = seq // tm
    blk = lambda b, n: (b * nblk + n, 0)
    const = lambda b, n: (0, 0)
    return pl.pallas_call(
        _conv_kernel,
        grid=(batch, nblk),
        in_specs=[pl.BlockSpec((tm, d), blk),
                  pl.BlockSpec((1, d), const),
                  pl.BlockSpec((d, 3 * d), const),
                  pl.BlockSpec((CONV_WIDTH, d), const),
                  pl.BlockSpec((d, d), const)],
        out_specs=pl.BlockSpec((tm, d), blk),
        out_shape=jax.ShapeDtypeStruct((t, d), F32),
        scratch_shapes=[pltpu.VMEM((SUBLANES, d), F32)],
        compiler_params=pltpu.CompilerParams(dimension_semantics=("arbitrary", "arbitrary"),
                                             vmem_limit_bytes=VMEM_LIMIT),
        name="conv_mixer",
    )(x, gain.reshape(1, d), w_in.astype(BF16), conv_w, w_out.astype(BF16))


def _topk_axis0(s, k, ids=None, payload=None):
    n, tm = s.shape
    if ids is None:
        ids = lax.broadcasted_iota(jnp.int32, (n, tm), 0)
    krow = lax.broadcasted_iota(jnp.int32, (k, tm), 0)
    vals = jnp.zeros((k, tm), F32)
    picks = jnp.zeros((k, tm), jnp.int32)
    for r in range(k):
        m = jnp.max(s, axis=0, keepdims=True)
        pos = jnp.min(jnp.where(s == m, ids, jnp.iinfo(jnp.int32).max), axis=0, keepdims=True)
        sel = ids == pos
        if payload is None:
            picked = pos
        else:
            picked = jnp.sum(jnp.where(sel, payload, 0), axis=0, keepdims=True)
        vals = jnp.where(krow == r, m, vals)
        picks = jnp.where(krow == r, picked, picks)
        s = jnp.where(sel, -jnp.inf, s)
    return vals, picks


def _pair_candidates(s1, i1, s2, i2):
    k, tm = s1.shape
    sub = lax.broadcasted_iota(jnp.int32, (SUBLANES, tm), 0)
    scores, flat, expert = [], [], []
    for i in range(k // 2):
        width = k if i == 0 else SUBLANES
        sc = s1[i:i + 1, :] + s2[0:width, :]
        ex = i1[i:i + 1, :] * N_KEYS + i2[0:width, :]
        fl = i * k + lax.broadcasted_iota(jnp.int32, (width, tm), 0)
        reach = k // (i + 1)
        if reach < width:
            sc = jnp.where(sub < reach, sc, -jnp.inf)
        scores.append(sc)
        flat.append(fl)
        expert.append(ex)
    scores.append(s1[k // 2:, :] + s2[0:1, :])
    expert.append(i1[k // 2:, :] * N_KEYS + i2[0:1, :])
    flat.append((k // 2 + sub) * k)
    return jnp.concatenate(scores, axis=0), jnp.concatenate(flat, axis=0), jnp.concatenate(expert, axis=0)


def _route_kernel(x_ref, g_ref, wq_ref, keys_ref, after_ref, h_ref, idx_ref, gate_ref):
    del after_ref
    h = _rms(x_ref[...], g_ref[...])
    h_ref[...] = h
    q = jnp.dot(h.astype(BF16), wq_ref[...], preferred_element_type=F32).astype(BF16)
    idx_rows, gate_rows = [], []
    for head in range(PEER_HEADS):
        tops = []
        for part in range(2):
            col = (head * 2 + part) * QUERY_HALF
            s = lax.dot_general(keys_ref[head, part], q[:, col:col + QUERY_HALF],
                                (((1,), (1,)), ((), ())), preferred_element_type=F32)
            tops.append(_topk_axis0(s, PEER_TOPK))
        (s1, i1), (s2, i2) = tops
        cand, flat_ids, cand_idx = _pair_candidates(s1, i1, s2, i2)
        g_s, e_idx = _topk_axis0(cand, PEER_TOPK, ids=flat_ids, payload=cand_idx)
        e = jnp.exp(g_s - jnp.max(g_s, axis=0, keepdims=True))
        gate_rows.append(e / jnp.sum(e, axis=0, keepdims=True))
        idx_rows.append(e_idx * ROW_WORDS)
    idx_ref[...] = jnp.concatenate(idx_rows, axis=0).T
    gate_ref[...] = jnp.concatenate(gate_rows, axis=0).T


def _route(x, gain, w_query, sub_keys, first_tok, n_tok, after, tm=128):
    t, d = x.shape
    nq = w_query.shape[1]
    first_blk = first_tok // tm
    return pl.pallas_call(
        _route_kernel,
        grid=(n_tok // tm,),
        in_specs=[pl.BlockSpec((tm, d), lambda i: (i + first_blk, 0)),
                  pl.BlockSpec((1, d), lambda i: (0, 0)),
                  pl.BlockSpec((d, nq), lambda i: (0, 0)),
                  pl.BlockSpec((PEER_HEADS, 2, N_KEYS, QUERY_HALF), lambda i: (0, 0, 0, 0)),
                  pl.BlockSpec(after.shape, lambda i: (0, 0))],
        out_specs=[pl.BlockSpec((tm, d), lambda i: (i, 0)),
                   pl.BlockSpec((tm, SLOTS), lambda i: (i, 0)),
                   pl.BlockSpec((tm, SLOTS), lambda i: (i, 0))],
        out_shape=[jax.ShapeDtypeStruct((n_tok, d), F32),
                   jax.ShapeDtypeStruct((n_tok, SLOTS), jnp.int32),
                   jax.ShapeDtypeStruct((n_tok, SLOTS), F32)],
        compiler_params=pltpu.CompilerParams(dimension_semantics=("arbitrary",), vmem_limit_bytes=VMEM_LIMIT),
        name="peer_route",
    )(x, gain.reshape(1, d), w_query, sub_keys, after)


def _pack_table(tab):
    bits = lax.bitcast_convert_type(tab.astype(BF16), jnp.uint16).astype(jnp.uint32)
    half = D_MODEL // 2
    words = (bits[:, :half] << 16) | bits[:, half:]
    return lax.bitcast_convert_type(words, jnp.int32).reshape(tab.shape[0] * ROW_WORDS, LANES)


def _load_table_once(tab_hbm, tab, sem):
    @pl.when(pl.program_id(0) == 0)
    def _():
        cp = pltpu.make_async_copy(tab_hbm, tab, sem)
        cp.start()
        cp.wait()


def _gather_pair(tab, off_a, off_b):
    ra = tab[pl.ds(pl.multiple_of(off_a, ROW_WORDS), ROW_WORDS), :]
    rb = tab[pl.ds(pl.multiple_of(off_b, ROW_WORDS), ROW_WORDS), :]
    words = jnp.concatenate([ra, rb], axis=0)
    hi = pltpu.bitcast(words & jnp.int32(-65536), F32)
    lo = pltpu.bitcast(words << 16, F32)
    return hi, lo


TOKEN_UNROLL = 8


def _expert_in_kernel(*refs):
    idx_refs = refs[:TOKEN_UNROLL]
    h_ref, gate_ref, tab_hbm, w_ref, tab, sem, a_ref = refs[TOKEN_UNROLL:]
    _load_table_once(tab_hbm, tab, sem)
    sub = lax.broadcasted_iota(jnp.int32, (SUBLANES, LANES), 0)
    lane = lax.broadcasted_iota(jnp.int32, (SUBLANES, LANES), 1)
    own_half = (sub >= ROW_WORDS) == ((lane & 1) == 1)
    tb = h_ref.shape[0]

    def step(i, carry):
        xs = []
        for u in range(TOKEN_UNROLL):
            x = h_ref[i * TOKEN_UNROLL + u]
            xs.append((jnp.concatenate([x[0:ROW_WORDS], x[0:ROW_WORDS]], axis=0),
                       jnp.concatenate([x[ROW_WORDS:], x[ROW_WORDS:]], axis=0)))
        accs = [jnp.zeros((SUBLANES, LANES), F32) for _ in range(TOKEN_UNROLL)]
        for p in range(SLOTS // 2):
            for u in range(TOKEN_UNROLL):
                hi, lo = _gather_pair(tab, idx_refs[u][i, 2 * p], idx_refs[u][i, 2 * p + 1])
                part = jnp.sum(hi * xs[u][0] + lo * xs[u][1], axis=1, keepdims=True)
                accs[u] = jnp.where((lane >> 1) == p, part, accs[u])
        for u in range(TOKEN_UNROLL):
            a_ref[pl.ds(i * TOKEN_UNROLL + u, 1), :] = jnp.sum(jnp.where(own_half, accs[u], 0.0),
                                                                axis=0, keepdims=True)
        return carry

    lax.fori_loop(0, tb // TOKEN_UNROLL, step, 0)
    a = a_ref[...]
    w_ref[...] = gate_ref[...] * (0.5 * a * (1.0 + lax.erf(a * (1.0 / math.sqrt(2.0)))))


def _expert_out_kernel(*refs):
    idx_refs = refs[:TOKEN_UNROLL]
    w_ref, x_ref, tab_hbm, o_ref, tab, sem, wb_ref = refs[TOKEN_UNROLL:]
    _load_table_once(tab_hbm, tab, sem)
    sub = lax.broadcasted_iota(jnp.int32, (SUBLANES, LANES), 0)
    lower = sub < ROW_WORDS
    tb = x_ref.shape[0]

    def step(i, carry):
        for u in range(TOKEN_UNROLL):
            row = w_ref[pl.ds(i * TOKEN_UNROLL + u, 1), :]
            wb_ref[u] = jnp.broadcast_to(row, (SLOTS, LANES)).T
        acc_h = [jnp.zeros((SUBLANES, LANES), F32) for _ in range(TOKEN_UNROLL)]
        acc_l = [jnp.zeros((SUBLANES, LANES), F32) for _ in range(TOKEN_UNROLL)]
        for p in range(SLOTS // 2):
            ka, kb = 2 * p, 2 * p + 1
            for u in range(TOKEN_UNROLL):
                hi, lo = _gather_pair(tab, idx_refs[u][i, ka], idx_refs[u][i, kb])
                wa = jnp.broadcast_to(wb_ref[u, ka:ka + 1, :], (SUBLANES, LANES))
                wb = jnp.broadcast_to(wb_ref[u, kb:kb + 1, :], (SUBLANES, LANES))
                wt = jnp.where(lower, wa, wb)
                acc_h[u] = acc_h[u] + hi * wt
                acc_l[u] = acc_l[u] + lo * wt
        for u in range(TOKEN_UNROLL):
            t = i * TOKEN_UNROLL + u
            ah = acc_h[u] + pltpu.roll(acc_h[u], ROW_WORDS, 0)
            al = acc_l[u] + pltpu.roll(acc_l[u], ROW_WORDS, 0)
            o_ref[t] = x_ref[t] + jnp.where(lower, ah, al)
        return carry

    lax.fori_loop(0, tb // TOKEN_UNROLL, step, 0)


def _expert_specs(tb):
    un = TOKEN_UNROLL
    smem_blks = [pl.BlockSpec((None, tb // un, SLOTS), functools.partial(lambda u, i: (u, i, 0), u),
                              memory_space=pltpu.SMEM) for u in range(un)]
    vmem_blk = pl.BlockSpec((tb, SLOTS), lambda i: (i, 0))
    tok_blk = pl.BlockSpec((tb, D_MODEL // LANES, LANES), lambda i: (i, 0, 0))
    params = pltpu.CompilerParams(dimension_semantics=("arbitrary",), vmem_limit_bytes=VMEM_LIMIT)
    return smem_blks, vmem_blk, tok_blk, params


def _split_offsets(idx):
    n = idx.shape[0]
    return idx.reshape(n // TOKEN_UNROLL, TOKEN_UNROLL, SLOTS).transpose(1, 0, 2)


def _expert_in(h, idx_split, gate, u_packed, tb=128):
    n, d = h.shape
    smem_blks, vmem_blk, tok_blk, params = _expert_specs(tb)
    return pl.pallas_call(
        _expert_in_kernel,
        grid=(n // tb,),
        in_specs=smem_blks + [tok_blk, vmem_blk, pl.BlockSpec(memory_space=pl.ANY)],
        out_specs=vmem_blk,
        out_shape=jax.ShapeDtypeStruct((n, SLOTS), F32),
        scratch_shapes=[pltpu.VMEM(u_packed.shape, jnp.int32), pltpu.SemaphoreType.DMA, pltpu.VMEM((tb, SLOTS), F32)],
        compiler_params=params,
        name="peer_expert_in",
    )(*([idx_split] * TOKEN_UNROLL), h.reshape(n, d // LANES, LANES), gate, u_packed)


def _expert_out(x, idx_split, w, v_packed, tb=128):
    t, d = x.shape
    n = w.shape[0]
    smem_blks, vmem_blk, tok_blk, params = _expert_specs(tb)
    out = pl.pallas_call(
        _expert_out_kernel,
        grid=(n // tb,),
        in_specs=smem_blks + [vmem_blk, tok_blk, pl.BlockSpec(memory_space=pl.ANY)],
        out_specs=tok_blk,
        out_shape=jax.ShapeDtypeStruct((n, d // LANES, LANES), F32),
        scratch_shapes=[pltpu.VMEM(v_packed.shape, jnp.int32), pltpu.SemaphoreType.DMA,
                        pltpu.VMEM((TOKEN_UNROLL, SLOTS, LANES), F32)],
        compiler_params=params,
        name="peer_expert_out",
    )(*([idx_split] * TOKEN_UNROLL), w, x.reshape(t, d // LANES, LANES), v_packed)
    return out.reshape(n, d)


SC_LANES = 16
SC_WORKERS = 32
SC_CHUNK = 32
SC_GROUP = 8
SC_TOKENS = 12032


def _sc_params():
    cp = pltpu.CompilerParams()
    if "needs_layout_passes" in pltpu.CompilerParams.__dataclass_fields__:
        cp = dataclasses.replace(cp, needs_layout_passes=False)
    return cp


def _tree_sum(vals):
    while len(vals) > 1:
        vals = [vals[i] + vals[i + 1] for i in range(0, len(vals) - 1, 2)] + ([vals[-1]] if len(vals) % 2 else [])
    return vals[0]

def _sc_expert_in(table_words, idx, h):
    n_tok, d = h.shape
    per = n_tok // SC_WORKERS
    words = d // 2
    nq = words // SC_LANES // 2
    nchunk = SLOTS // SC_CHUNK
    group_chunks = SC_GROUP * nchunk
    mesh = plsc.VectorSubcoreMesh(core_axis_name="c", subcore_axis_name="s")

    @functools.partial(
        pl.kernel, mesh=mesh,
        out_type=jax.ShapeDtypeStruct((n_tok * SLOTS,), F32),
        scratch_types=[pltpu.VMEM((SC_GROUP * SLOTS,), jnp.int32), pltpu.VMEM((SC_GROUP, d), F32),
                       pltpu.VMEM((2, SC_CHUNK, words), jnp.int32), pltpu.VMEM((SC_GROUP * SLOTS,), F32),
                       pltpu.VMEM((SC_CHUNK, SC_LANES), F32),
                       pltpu.SemaphoreType.DMA, pltpu.SemaphoreType.DMA],
        compiler_params=_sc_params(),
        name="peer_expert_in_sc",
    )
    def body(tab_hbm, idx_hbm, h_hbm, a_hbm, idx_v, x_v, rows_v, a_v, part_v, sem0, sem1):
        base = (lax.axis_index("s") * 2 + lax.axis_index("c")) * per
        lanes = lax.iota(jnp.int32, SC_LANES)
        himask = jnp.full((SC_LANES,), -65536, jnp.int32)
        sems = (sem0, sem1)

        def gather(k, b):
            off = pl.multiple_of(k * SC_CHUNK, SC_CHUNK)
            return pltpu.make_async_copy(tab_hbm.at[idx_v.at[pl.ds(off, SC_CHUNK)]], rows_v.at[b], sems[b])

        def dots(k, b):
            tok = k // nchunk
            for q in range(2):
                first = q * nq
                xh = [x_v[tok, pl.ds((first + j) * SC_LANES, SC_LANES)] for j in range(nq)]
                xl = [x_v[tok, pl.ds(words + (first + j) * SC_LANES, SC_LANES)] for j in range(nq)]

                def row_body(r, carry):
                    prods = []
                    for j in range(nq):
                        wv = rows_v[b, r, pl.ds((first + j) * SC_LANES, SC_LANES)]
                        prods.append(plsc.bitcast(wv & himask, F32) * xh[j] + plsc.bitcast(wv << 16, F32) * xl[j])
                    acc = _tree_sum(prods)
                    if q == 0:
                        part_v[r, :] = acc
                    else:
                        part_v[r, :] = part_v[r, :] + acc
                    return carry
                lax.fori_loop(0, SC_CHUNK, row_body, 0)
            for g in range(SC_CHUNK // SC_LANES):
                def red_body(r, avec):
                    return jnp.where(lanes == r, jnp.sum(part_v[g * SC_LANES + r, :]), avec)
                avec = lax.fori_loop(0, SC_LANES, red_body, jnp.zeros((SC_LANES,), F32))
                a_v[pl.ds(pl.multiple_of(k * SC_CHUNK + g * SC_LANES, SC_LANES), SC_LANES)] = avec

        @pl.loop(0, per // SC_GROUP)
        def _(g):
            t0 = pl.multiple_of(base + g * SC_GROUP, SC_GROUP)
            pltpu.sync_copy(idx_hbm.at[pl.ds(t0 * SLOTS, SC_GROUP * SLOTS)], idx_v)
            pltpu.sync_copy(h_hbm.at[pl.ds(t0, SC_GROUP)], x_v)
            gather(0, 0).start()

            @pl.loop(0, group_chunks // 2)
            def _(kk):
                k0 = 2 * kk
                gather(k0 + 1, 1).start()
                gather(k0, 0).wait()
                dots(k0, 0)

                @pl.when(k0 + 2 < group_chunks)
                def _():
                    gather(k0 + 2, 0).start()
                gather(k0 + 1, 1).wait()
                dots(k0 + 1, 1)

            pltpu.sync_copy(a_v, a_hbm.at[pl.ds(t0 * SLOTS, SC_GROUP * SLOTS)])

    return body(table_words, idx, h)

def _sc_expert_out(table_words, idx, w, x, first_tok):
    d = x.shape[1]
    n_tok = w.shape[0] // SLOTS
    per = n_tok // SC_WORKERS
    words = d // 2
    nq = words // SC_LANES // 2
    nchunk = SLOTS // SC_CHUNK
    group_chunks = SC_GROUP * nchunk
    mesh = plsc.VectorSubcoreMesh(core_axis_name="c", subcore_axis_name="s")

    @functools.partial(
        pl.kernel, mesh=mesh,
        out_type=jax.ShapeDtypeStruct((n_tok, d), F32),
        scratch_types=[pltpu.VMEM((SC_GROUP * SLOTS,), jnp.int32), pltpu.VMEM((SC_GROUP * SLOTS,), F32),
                       pltpu.VMEM((2, SC_CHUNK, words), jnp.int32), pltpu.VMEM((SC_GROUP, d), F32),
                       pltpu.SemaphoreType.DMA, pltpu.SemaphoreType.DMA],
        compiler_params=_sc_params(),
        name="peer_expert_out_sc",
    )
    def body(tab_hbm, idx_hbm, w_hbm, x_hbm, o_hbm, idx_v, w_v, rows_v, y_v, sem0, sem1):
        base = (lax.axis_index("s") * 2 + lax.axis_index("c")) * per
        zero = jnp.zeros((SC_LANES,), jnp.int32)
        himask = jnp.full((SC_LANES,), -65536, jnp.int32)
        sems = (sem0, sem1)

        def gather(k, b):
            off = pl.multiple_of(k * SC_CHUNK, SC_CHUNK)
            return pltpu.make_async_copy(tab_hbm.at[idx_v.at[pl.ds(off, SC_CHUNK)]], rows_v.at[b], sems[b])

        def accumulate(k, b):
            tok = k // nchunk
            for q in range(2):
                first = q * nq
                acc0 = (tuple(y_v[tok, pl.ds((first + j) * SC_LANES, SC_LANES)] for j in range(nq))
                        + tuple(y_v[tok, pl.ds(words + (first + j) * SC_LANES, SC_LANES)] for j in range(nq)))

                def row_body(r, accs):
                    ws = plsc.load_gather(w_v, [zero + (k * SC_CHUNK + r)])
                    hi_acc, lo_acc = [], []
                    for j in range(nq):
                        wv = rows_v[b, r, pl.ds((first + j) * SC_LANES, SC_LANES)]
                        hi_acc.append(accs[j] + plsc.bitcast(wv & himask, F32) * ws)
                        lo_acc.append(accs[nq + j] + plsc.bitcast(wv << 16, F32) * ws)
                    return tuple(hi_acc) + tuple(lo_acc)

                accs = lax.fori_loop(0, SC_CHUNK, row_body, acc0)
                for j in range(nq):
                    y_v[tok, pl.ds((first + j) * SC_LANES, SC_LANES)] = accs[j]
                    y_v[tok, pl.ds(words + (first + j) * SC_LANES, SC_LANES)] = accs[nq + j]

        @pl.loop(0, per // SC_GROUP)
        def _(g):
            t0 = pl.multiple_of(base + g * SC_GROUP, SC_GROUP)
            pltpu.sync_copy(idx_hbm.at[pl.ds(t0 * SLOTS, SC_GROUP * SLOTS)], idx_v)
            pltpu.sync_copy(w_hbm.at[pl.ds(t0 * SLOTS, SC_GROUP * SLOTS)], w_v)
            pltpu.sync_copy(x_hbm.at[pl.ds(first_tok + t0, SC_GROUP)], y_v)
            gather(0, 0).start()

            @pl.loop(0, group_chunks // 2)
            def _(kk):
                k0 = 2 * kk
                gather(k0 + 1, 1).start()
                gather(k0, 0).wait()
                accumulate(k0, 0)

                @pl.when(k0 + 2 < group_chunks)
                def _():
                    gather(k0 + 2, 0).start()
                gather(k0 + 1, 1).wait()
                accumulate(k0 + 1, 1)

            pltpu.sync_copy(y_v, o_hbm.at[pl.ds(t0, SC_GROUP)])

    return body(table_words, idx, w, x)


def _gelu_gate_kernel(a_ref, g_ref, after_ref, w_ref):
    del after_ref
    a = a_ref[...]
    w_ref[...] = g_ref[...] * (0.5 * a * (1.0 + lax.erf(a * (1.0 / math.sqrt(2.0)))))


def _gelu_gate(a, gate, after):
    return pl.pallas_call(_gelu_gate_kernel, out_shape=jax.ShapeDtypeStruct(a.shape, F32),
                          compiler_params=pltpu.CompilerParams(vmem_limit_bytes=VMEM_LIMIT),
                          name="peer_gelu_gate")(a, gate, after)


def _peer(x, gain, w_query, sub_keys, expert_u, expert_v):
    t = x.shape[0]
    t_tc = t - SC_TOKENS
    wq, keys = w_query.astype(BF16), sub_keys.astype(BF16)
    u_packed, v_packed = _pack_table(expert_u), _pack_table(expert_v)
    u_rows = u_packed.reshape(expert_u.shape[0], ROW_WORDS * LANES)
    v_rows = v_packed.reshape(expert_v.shape[0], ROW_WORDS * LANES)

    h_sc, idx_sc, gate_sc = _route(x, gain, wq, keys, t_tc, SC_TOKENS, u_rows[:SUBLANES])
    experts_sc = (idx_sc // ROW_WORDS).reshape(-1)
    a_sc = _sc_expert_in(u_rows, experts_sc, h_sc)

    h, idx, gate = _route(x, gain, wq, keys, 0, t_tc, experts_sc[:SUBLANES * SLOTS].reshape(SUBLANES, SLOTS))
    idx_split = _split_offsets(idx)
    w = _expert_in(h, idx_split, gate, u_packed)

    w_sc = _gelu_gate(a_sc.reshape(SC_TOKENS, SLOTS), gate_sc, w[:SUBLANES])
    out_sc = _sc_expert_out(v_rows, experts_sc, w_sc.reshape(-1), x, t_tc)
    out_tc = _expert_out(x, idx_split, w, v_packed)
    return jnp.concatenate([out_tc, out_sc], axis=0)


def kernel(x, norm_mix, norm_ffn, attn_w_qkv, attn_q_norm, attn_k_norm, attn_sinks, attn_w_o, conv_w_in, conv_w, conv_w_out, peer_w_query, peer_sub_keys, peer_u, peer_v):
    batch, seq, d = x.shape
    xt = x.reshape(batch * seq, d)
    for i in range(norm_mix.shape[0]):
        j = i // 2
        if i % 2 == 0:
            qkv = _norm_matmul(xt, norm_mix[i], attn_w_qkv[j].astype(BF16))
            o = _attention(qkv, attn_q_norm[j], attn_k_norm[j], attn_sinks[j], batch, seq)
            xt = _matmul_residual(o, attn_w_o[j].astype(BF16), xt)
        else:
            xt = _conv_mixer(xt, norm_mix[i], conv_w_in[j], conv_w[j], conv_w_out[j], batch, seq)
        xt = _peer(xt, norm_ffn[i], peer_w_query[i], peer_sub_keys[i], peer_u[i], peer_v[i])
    return xt.reshape(batch, seq, d)
```

```python
import dataclasses
import functools
import math

import jax
import jax.numpy as jnp
from jax import lax
from jax.experimental import pallas as pl
from jax.experimental.pallas import tpu as pltpu
from jax.experimental.pallas import tpu_sc as plsc

D_MODEL = 1024
RMS_EPS = 1e-6

HEAD_DIM = 64
N_Q_HEADS = 16
N_KV_HEADS = 4
GROUP = N_Q_HEADS // N_KV_HEADS
WINDOW = 128
ROT_DIM = HEAD_DIM // 4
ROPE_THETA = 500000.0
Q_COLS = N_Q_HEADS * HEAD_DIM
KV_COLS = N_KV_HEADS * HEAD_DIM
NEG_INF = -1e30

CONV_WIDTH = 3

PEER_HEADS = 8
N_KEYS = 128
N_EXPERTS = N_KEYS * N_KEYS
PEER_TOPK = 16
QUERY_HALF = 128
SLOTS = PEER_HEADS * PEER_TOPK

LANES = 128
SUBLANES = 8
ROW_WORDS = D_MODEL // 2 // LANES
VMEM_LIMIT = 48 * 1024 * 1024

BF16 = jnp.bfloat16
F32 = jnp.float32


def _rms(x, gain):
    return x * lax.rsqrt(jnp.mean(x * x, axis=-1, keepdims=True) + RMS_EPS) * gain


def _norm_matmul_kernel(x_ref, g_ref, w_ref, o_ref):
    h = _rms(x_ref[...], g_ref[...])
    o_ref[...] = jnp.dot(h.astype(BF16), w_ref[...], preferred_element_type=F32)


def _norm_matmul(x, gain, w, tm=512):
    t, d = x.shape
    n = w.shape[1]
    return pl.pallas_call(
        _norm_matmul_kernel,
        grid=(t // tm,),
        in_specs=[pl.BlockSpec((tm, d), lambda i: (i, 0)),
                  pl.BlockSpec((1, d), lambda i: (0, 0)),
                  pl.BlockSpec((d, n), lambda i: (0, 0))],
        out_specs=pl.BlockSpec((tm, n), lambda i: (i, 0)),
        out_shape=jax.ShapeDtypeStruct((t, n), F32),
        compiler_params=pltpu.CompilerParams(dimension_semantics=("arbitrary",), vmem_limit_bytes=VMEM_LIMIT),
        name="norm_matmul",
    )(x, gain.reshape(1, d), w)


def _matmul_residual_kernel(a_ref, w_ref, r_ref, o_ref):
    o_ref[...] = r_ref[...] + jnp.dot(a_ref[...].astype(BF16), w_ref[...], preferred_element_type=F32)


def _matmul_residual(a, w, res, tm=512):
    t, k = a.shape
    n = w.shape[1]
    return pl.pallas_call(
        _matmul_residual_kernel,
        grid=(t // tm,),
        in_specs=[pl.BlockSpec((tm, k), lambda i: (i, 0)),
                  pl.BlockSpec((k, n), lambda i: (0, 0)),
                  pl.BlockSpec((tm, n), lambda i: (i, 0))],
        out_specs=pl.BlockSpec((tm, n), lambda i: (i, 0)),
        out_shape=jax.ShapeDtypeStruct((t, n), F32),
        compiler_params=pltpu.CompilerParams(dimension_semantics=("arbitrary",), vmem_limit_bytes=VMEM_LIMIT),
        name="matmul_residual",
    )(a, w, res)


def _rope_tables(seq):
    half = ROT_DIM // 2
    freqs = ROPE_THETA ** (-jnp.arange(0, ROT_DIM, 2, dtype=F32) / ROT_DIM)
    ang = jnp.arange(seq, dtype=F32)[:, None] * freqs[None, :]
    cos, sin = jnp.cos(ang), jnp.sin(ang)
    ones = jnp.ones((seq, HEAD_DIM - ROT_DIM), F32)
    zeros = jnp.zeros((seq, HEAD_DIM - ROT_DIM), F32)
    zh = jnp.zeros((seq, half), F32)
    c = jnp.concatenate([cos, cos, ones], axis=1)
    s_next = jnp.concatenate([-sin, zh, zeros], axis=1)
    s_prev = jnp.concatenate([zh, sin, zeros], axis=1)
    return jnp.stack([jnp.tile(c, (1, 2)), jnp.tile(s_next, (1, 2)), jnp.tile(s_prev, (1, 2))])


def _head_norm_rope(x, gain2, rope, lo):
    sq = x * x
    s_lo = jnp.sum(jnp.where(lo, sq, 0.0), axis=1, keepdims=True)
    s_hi = jnp.sum(jnp.where(lo, 0.0, sq), axis=1, keepdims=True)
    ms = jnp.where(lo, s_lo, s_hi) * (1.0 / HEAD_DIM)
    xn = x * lax.rsqrt(ms + RMS_EPS) * gain2
    half = ROT_DIM // 2
    return xn * rope[0] + pltpu.roll(xn, LANES - half, 1) * rope[1] + pltpu.roll(xn, half, 1) * rope[2]


def _attn_kernel(sinks_ref, q_ref, kc_ref, kp_ref, vc_ref, vp_ref, rc_ref, rp_ref, qg_ref, kg_ref, o_ref):
    n = pl.program_id(1)
    lo = lax.broadcasted_iota(jnp.int32, (WINDOW, LANES), 1) < HEAD_DIM
    rope_c = rc_ref[...]
    rope_p = rp_ref[...]
    qg = qg_ref[...]
    kg = kg_ref[...]

    rows = GROUP * WINDOW
    qi = lax.broadcasted_iota(jnp.int32, (rows, 2 * WINDOW), 0) & (WINDOW - 1)
    ki = lax.broadcasted_iota(jnp.int32, (rows, 2 * WINDOW), 1)
    rel = WINDOW + qi - ki
    valid = (rel >= 0) & (rel < WINDOW) & ((n > 0) | (ki >= WINDOW))
    head_of_row = lax.broadcasted_iota(jnp.int32, (rows, 1), 0) // WINDOW
    scale = 1.0 / math.sqrt(HEAD_DIM)

    q2 = [_head_norm_rope(q_ref[:, c * LANES:(c + 1) * LANES], qg, rope_c, lo).astype(BF16)
          for c in range(Q_COLS // LANES)]
    for c in range(KV_COLS // LANES):
        cols = slice(c * LANES, (c + 1) * LANES)
        kprev = _head_norm_rope(kp_ref[:, cols], kg, rope_p, lo)
        kcur = _head_norm_rope(kc_ref[:, cols], kg, rope_c, lo)
        kfull = jnp.concatenate([kprev, kcur], axis=0).astype(BF16)
        vfull = jnp.concatenate([vp_ref[:, cols], vc_ref[:, cols]], axis=0).astype(BF16)
        for hh in range(LANES // HEAD_DIM):
            h = (LANES // HEAD_DIM) * c + hh
            kh = kfull[:, hh * HEAD_DIM:(hh + 1) * HEAD_DIM]
            vh = vfull[:, hh * HEAD_DIM:(hh + 1) * HEAD_DIM]
            heads = [GROUP * h + g for g in range(GROUP)]
            q4 = jnp.concatenate([q2[j // 2][:, (j % 2) * HEAD_DIM:(j % 2 + 1) * HEAD_DIM] for j in heads], axis=0)
            s = lax.dot_general(q4, kh, (((1,), (1,)), ((), ())), preferred_element_type=F32) * scale
            s = jnp.where(valid, s, NEG_INF)
            sink = jnp.zeros((rows, 1), F32)
            for g, j in enumerate(heads):
                sink = jnp.where(head_of_row == g, sinks_ref[j], sink)
            m = jnp.maximum(jnp.max(s, axis=1, keepdims=True), sink)
            p = jnp.exp(s - m)
            denom = jnp.sum(p, axis=1, keepdims=True) + jnp.exp(sink - m)
            o = jnp.dot(p.astype(BF16), vh, preferred_element_type=F32) / denom
            for g, j in enumerate(heads):
                o_ref[:, j * HEAD_DIM:(j + 1) * HEAD_DIM] = o[g * WINDOW:(g + 1) * WINDOW]


def _attention(qkv, q_gain, k_gain, sinks, batch, seq):
    t = batch * seq
    nb = seq // WINDOW
    rope = _rope_tables(seq)
    kcol = Q_COLS // KV_COLS
    cur = lambda b, n: (b * nb + n, 0)
    kcur = lambda b, n: (b * nb + n, kcol)
    kprev = lambda b, n: (b * nb + jnp.maximum(n - 1, 0), kcol)
    vcur = lambda b, n: (b * nb + n, kcol + 1)
    vprev = lambda b, n: (b * nb + jnp.maximum(n - 1, 0), kcol + 1)
    return pl.pallas_call(
        _attn_kernel,
        grid=(batch, nb),
        in_specs=[pl.BlockSpec(memory_space=pltpu.SMEM),
                  pl.BlockSpec((WINDOW, Q_COLS), cur),
                  pl.BlockSpec((WINDOW, KV_COLS), kcur),
                  pl.BlockSpec((WINDOW, KV_COLS), kprev),
                  pl.BlockSpec((WINDOW, KV_COLS), vcur),
                  pl.BlockSpec((WINDOW, KV_COLS), vprev),
                  pl.BlockSpec((3, WINDOW, LANES), lambda b, n: (0, n, 0)),
                  pl.BlockSpec((3, WINDOW, LANES), lambda b, n: (0, jnp.maximum(n - 1, 0), 0)),
                  pl.BlockSpec((1, LANES), lambda b, n: (0, 0)),
                  pl.BlockSpec((1, LANES), lambda b, n: (0, 0))],
        out_specs=pl.BlockSpec((WINDOW, Q_COLS), cur),
        out_shape=jax.ShapeDtypeStruct((t, Q_COLS), F32),
        compiler_params=pltpu.CompilerParams(dimension_semantics=("arbitrary", "arbitrary"),
                                             vmem_limit_bytes=VMEM_LIMIT),
        name="swa_attention",
    )(sinks, qkv, qkv, qkv, qkv, qkv, rope, rope,
      jnp.tile(q_gain, 2).reshape(1, LANES), jnp.tile(k_gain, 2).reshape(1, LANES))


def _conv_kernel(x_ref, g_ref, win_ref, cw_ref, wout_ref, o_ref, zprev_ref):
    n = pl.program_id(1)
    d = D_MODEL

    @pl.when(n == 0)
    def _():
        zprev_ref[...] = jnp.zeros_like(zprev_ref)

    x = x_ref[...]
    h = _rms(x, g_ref[...])
    bcu = jnp.dot(h.astype(BF16), win_ref[...], preferred_element_type=F32)
    gate_b = bcu[:, :d]
    z = bcu[:, d:2 * d] * bcu[:, 2 * d:]
    tm = z.shape[0]
    row = lax.broadcasted_iota(jnp.int32, z.shape, 0)
    prev = zprev_ref[...]
    p_last = prev[SUBLANES - 1:SUBLANES, :]
    p_last2 = prev[SUBLANES - 2:SUBLANES - 1, :]
    z1 = jnp.where(row == 0, p_last, pltpu.roll(z, 1, 0))
    z2 = jnp.where(row == 0, p_last2, jnp.where(row == 1, p_last, pltpu.roll(z, 2, 0)))
    cw = cw_ref[...]
    conv = cw[0:1, :] * z2 + cw[1:2, :] * z1 + cw[2:3, :] * z
    zprev_ref[...] = z[tm - SUBLANES:, :]
    o_ref[...] = x + jnp.dot((gate_b * conv).astype(BF16), wout_ref[...], preferred_element_type=F32)


def _conv_mixer(x, gain, w_in, conv_w, w_out, batch, seq, tm=256):
    t, d = x.shape
    nblk = seq // tm
    blk = lambda b, n: (b * nblk + n, 0)
    const = lambda b, n: (0, 0)
    return pl.pallas_call(
        _conv_kernel,
        grid=(batch, nblk),
        in_specs=[pl.BlockSpec((tm, d), blk),
                  pl.BlockSpec((1, d), const),
                  pl.BlockSpec((d, 3 * d), const),
                  pl.BlockSpec((CONV_WIDTH, d), const),
                  pl.BlockSpec((d, d), const)],
        out_specs=pl.BlockSpec((tm, d), blk),
        out_shape=jax.ShapeDtypeStruct((t, d), F32),
        scratch_shapes=[pltpu.VMEM((SUBLANES, d), F32)],
        compiler_params=pltpu.CompilerParams(dimension_semantics=("arbitrary", "arbitrary"),
                                             vmem_limit_bytes=VMEM_LIMIT),
        name="conv_mixer",
    )(x, gain.reshape(1, d), w_in.astype(BF16), conv_w, w_out.astype(BF16))


def _topk_axis0(s, k, ids=None, payload=None):
    n, tm = s.shape
    if ids is None:
        ids = lax.broadcasted_iota(jnp.int32, (n, tm), 0)
    krow = lax.broadcasted_iota(jnp.int32, (k, tm), 0)
    vals = jnp.zeros((k, tm), F32)
    picks = jnp.zeros((k, tm), jnp.int32)
    for r in range(k):
        m = jnp.max(s, axis=0, keepdims=True)
        pos = jnp.min(jnp.where(s == m, ids, jnp.iinfo(jnp.int32).max), axis=0, keepdims=True)
        sel = ids == pos
        if payload is None:
            picked = pos
        else:
            picked = jnp.sum(jnp.where(sel, payload, 0), axis=0, keepdims=True)
        vals = jnp.where(krow == r, m, vals)
        picks = jnp.where(krow == r, picked, picks)
        s = jnp.where(sel, -jnp.inf, s)
    return vals, picks


def _pair_candidates(s1, i1, s2, i2):
    k, tm = s1.shape
    sub = lax.broadcasted_iota(jnp.int32, (SUBLANES, tm), 0)
    scores, flat, expert = [], [], []
    for i in range(k // 2):
        width = k if i == 0 else SUBLANES
        sc = s1[i:i + 1, :] + s2[0:width, :]
        ex = i1[i:i + 1, :] * N_KEYS + i2[0:width, :]
        fl = i * k + lax.broadcasted_iota(jnp.int32, (width, tm), 0)
        reach = k // (i + 1)
        if reach < width:
            sc = jnp.where(sub < reach, sc, -jnp.inf)
        scores.append(sc)
        flat.append(fl)
        expert.append(ex)
    scores.append(s1[k // 2:, :] + s2[0:1, :])
    expert.append(i1[k // 2:, :] * N_KEYS + i2[0:1, :])
    flat.append((k // 2 + sub) * k)
    return jnp.concatenate(scores, axis=0), jnp.concatenate(flat, axis=0), jnp.concatenate(expert, axis=0)


def _route_kernel(x_ref, g_ref, wq_ref, keys_ref, after_ref, h_ref, idx_ref, gate_ref):
    del after_ref
    h = _rms(x_ref[...], g_ref[...])
    h_ref[...] = h
    q = jnp.dot(h.astype(BF16), wq_ref[...], preferred_element_type=F32).astype(BF16)
    idx_rows, gate_rows = [], []
    for head in range(PEER_HEADS):
        tops = []
        for part in range(2):
            col = (head * 2 + part) * QUERY_HALF
            s = lax.dot_general(keys_ref[head, part], q[:, col:col + QUERY_HALF],
                                (((1,), (1,)), ((), ())), preferred_element_type=F32)
            tops.append(_topk_axis0(s, PEER_TOPK))
        (s1, i1), (s2, i2) = tops
        cand, flat_ids, cand_idx = _pair_candidates(s1, i1, s2, i2)
        g_s, e_idx = _topk_axis0(cand, PEER_TOPK, ids=flat_ids, payload=cand_idx)
        e = jnp.exp(g_s - jnp.max(g_s, axis=0, keepdims=True))
        gate_rows.append(e / jnp.sum(e, axis=0, keepdims=True))
        idx_rows.append(e_idx * ROW_WORDS)
    idx_ref[...] = jnp.concatenate(idx_rows, axis=0).T
    gate_ref[...] = jnp.concatenate(gate_rows, axis=0).T


def _route(x, gain, w_query, sub_keys, first_tok, n_tok, after, tm=128):
    t, d = x.shape
    nq = w_query.shape[1]
    first_blk = first_tok // tm
    return pl.pallas_call(
        _route_kernel,
        grid=(n_tok // tm,),
        in_specs=[pl.BlockSpec((tm, d), lambda i: (i + first_blk, 0)),
                  pl.BlockSpec((1, d), lambda i: (0, 0)),
                  pl.BlockSpec((d, nq), lambda i: (0, 0)),
                  pl.BlockSpec((PEER_HEADS, 2, N_KEYS, QUERY_HALF), lambda i: (0, 0, 0, 0)),
                  pl.BlockSpec(after.shape, lambda i: (0, 0))],
        out_specs=[pl.BlockSpec((tm, d), lambda i: (i, 0)),
                   pl.BlockSpec((tm, SLOTS), lambda i: (i, 0)),
                   pl.BlockSpec((tm, SLOTS), lambda i: (i, 0))],
        out_shape=[jax.ShapeDtypeStruct((n_tok, d), F32),
                   jax.ShapeDtypeStruct((n_tok, SLOTS), jnp.int32),
                   jax.ShapeDtypeStruct((n_tok, SLOTS), F32)],
        compiler_params=pltpu.CompilerParams(dimension_semantics=("arbitrary",), vmem_limit_bytes=VMEM_LIMIT),
        name="peer_route",
    )(x, gain.reshape(1, d), w_query, sub_keys, after)


def _pack_table(tab):
    bits = lax.bitcast_convert_type(tab.astype(BF16), jnp.uint16).astype(jnp.uint32)
    half = D_MODEL // 2
    words = (bits[:, :half] << 16) | bits[:, half:]
    return lax.bitcast_convert_type(words, jnp.int32).reshape(tab.shape[0] * ROW_WORDS, LANES)


def _load_table_once(tab_hbm, tab, sem):
    @pl.when(pl.program_id(0) == 0)
    def _():
        cp = pltpu.make_async_copy(tab_hbm, tab, sem)
        cp.start()
        cp.wait()


def _gather_pair(tab, off_a, off_b):
    ra = tab[pl.ds(pl.multiple_of(off_a, ROW_WORDS), ROW_WORDS), :]
    rb = tab[pl.ds(pl.multiple_of(off_b, ROW_WORDS), ROW_WORDS), :]
    words = jnp.concatenate([ra, rb], axis=0)
    hi = pltpu.bitcast(words & jnp.int32(-65536), F32)
    lo = pltpu.bitcast(words << 16, F32)
    return hi, lo


TOKEN_UNROLL = 8


def _expert_in_kernel(*refs):
    idx_refs = refs[:TOKEN_UNROLL]
    h_ref, gate_ref, tab_hbm, w_ref, tab, sem, a_ref = refs[TOKEN_UNROLL:]
    _load_table_once(tab_hbm, tab, sem)
    sub = lax.broadcasted_iota(jnp.int32, (SUBLANES, LANES), 0)
    lane = lax.broadcasted_iota(jnp.int32, (SUBLANES, LANES), 1)
    own_half = (sub >= ROW_WORDS) == ((lane & 1) == 1)
    tb = h_ref.shape[0]

    def step(i, carry):
        xs = []
        for u in range(TOKEN_UNROLL):
            x = h_ref[i * TOKEN_UNROLL + u]
            xs.append((jnp.concatenate([x[0:ROW_WORDS], x[0:ROW_WORDS]], axis=0),
                       jnp.concatenate([x[ROW_WORDS:], x[ROW_WORDS:]], axis=0)))
        accs = [jnp.zeros((SUBLANES, LANES), F32) for _ in range(TOKEN_UNROLL)]
        for p in range(SLOTS // 2):
            for u in range(TOKEN_UNROLL):
                hi, lo = _gather_pair(tab, idx_refs[u][i, 2 * p], idx_refs[u][i, 2 * p + 1])
                part = jnp.sum(hi * xs[u][0] + lo * xs[u][1], axis=1, keepdims=True)
                accs[u] = jnp.where((lane >> 1) == p, part, accs[u])
        for u in range(TOKEN_UNROLL):
            a_ref[pl.ds(i * TOKEN_UNROLL + u, 1), :] = jnp.sum(jnp.where(own_half, accs[u], 0.0),
                                                                axis=0, keepdims=True)
        return carry

    lax.fori_loop(0, tb // TOKEN_UNROLL, step, 0)
    a = a_ref[...]
    w_ref[...] = gate_ref[...] * (0.5 * a * (1.0 + lax.erf(a * (1.0 / math.sqrt(2.0)))))


def _expert_out_kernel(*refs):
    idx_refs = refs[:TOKEN_UNROLL]
    w_ref, x_ref, tab_hbm, o_ref, tab, sem, wb_ref = refs[TOKEN_UNROLL:]
    _load_table_once(tab_hbm, tab, sem)
    sub = lax.broadcasted_iota(jnp.int32, (SUBLANES, LANES), 0)
    lower = sub < ROW_WORDS
    tb = x_ref.shape[0]

    def step(i, carry):
        for u in range(TOKEN_UNROLL):
            row = w_ref[pl.ds(i * TOKEN_UNROLL + u, 1), :]
            wb_ref[u] = jnp.broadcast_to(row, (SLOTS, LANES)).T
        acc_h = [jnp.zeros((SUBLANES, LANES), F32) for _ in range(TOKEN_UNROLL)]
        acc_l = [jnp.zeros((SUBLANES, LANES), F32) for _ in range(TOKEN_UNROLL)]
        for p in range(SLOTS // 2):
            ka, kb = 2 * p, 2 * p + 1
            for u in range(TOKEN_UNROLL):
                hi, lo = _gather_pair(tab, idx_refs[u][i, ka], idx_refs[u][i, kb])
                wa = jnp.broadcast_to(wb_ref[u, ka:ka + 1, :], (SUBLANES, LANES))
                wb = jnp.broadcast_to(wb_ref[u, kb:kb + 1, :], (SUBLANES, LANES))
                wt = jnp.where(lower, wa, wb)
                acc_h[u] = acc_h[u] + hi * wt
                acc_l[u] = acc_l[u] + lo * wt
        for u in range(TOKEN_UNROLL):
            t = i * TOKEN_UNROLL + u
            ah = acc_h[u] + pltpu.roll(acc_h[u], ROW_WORDS, 0)
            al = acc_l[u] + pltpu.roll(acc_l[u], ROW_WORDS, 0)
            o_ref[t] = x_ref[t] + jnp.where(lower, ah, al)
        return carry

    lax.fori_loop(0, tb // TOKEN_UNROLL, step, 0)


def _expert_specs(tb):
    un = TOKEN_UNROLL
    smem_blks = [pl.BlockSpec((None, tb // un, SLOTS), functools.partial(lambda u, i: (u, i, 0), u),
                              memory_space=pltpu.SMEM) for u in range(un)]
    vmem_blk = pl.BlockSpec((tb, SLOTS), lambda i: (i, 0))
    tok_blk = pl.BlockSpec((tb, D_MODEL // LANES, LANES), lambda i: (i, 0, 0))
    params = pltpu.CompilerParams(dimension_semantics=("arbitrary",), vmem_limit_bytes=VMEM_LIMIT)
    return smem_blks, vmem_blk, tok_blk, params


def _split_offsets(idx):
    n = idx.shape[0]
    return idx.reshape(n // TOKEN_UNROLL, TOKEN_UNROLL, SLOTS).transpose(1, 0, 2)


def _expert_in(h, idx_split, gate, u_packed, tb=128):
    n, d = h.shape
    smem_blks, vmem_blk, tok_blk, params = _expert_specs(tb)
    return pl.pallas_call(
        _expert_in_kernel,
        grid=(n // tb,),
        in_specs=smem_blks + [tok_blk, vmem_blk, pl.BlockSpec(memory_space=pl.ANY)],
        out_specs=vmem_blk,
        out_shape=jax.ShapeDtypeStruct((n, SLOTS), F32),
        scratch_shapes=[pltpu.VMEM(u_packed.shape, jnp.int32), pltpu.SemaphoreType.DMA, pltpu.VMEM((tb, SLOTS), F32)],
        compiler_params=params,
        name="peer_expert_in",
    )(*([idx_split] * TOKEN_UNROLL), h.reshape(n, d // LANES, LANES), gate, u_packed)


def _expert_out(x, idx_split, w, v_packed, tb=128):
    t, d = x.shape
    n = w.shape[0]
    smem_blks, vmem_blk, tok_blk, params = _expert_specs(tb)
    out = pl.pallas_call(
        _expert_out_kernel,
        grid=(n // tb,),
        in_specs=smem_blks + [vmem_blk, tok_blk, pl.BlockSpec(memory_space=pl.ANY)],
        out_specs=tok_blk,
        out_shape=jax.ShapeDtypeStruct((n, d // LANES, LANES), F32),
        scratch_shapes=[pltpu.VMEM(v_packed.shape, jnp.int32), pltpu.SemaphoreType.DMA,
                        pltpu.VMEM((TOKEN_UNROLL, SLOTS, LANES), F32)],
        compiler_params=params,
        name="peer_expert_out",
    )(*([idx_split] * TOKEN_UNROLL), w, x.reshape(t, d // LANES, LANES), v_packed)
    return out.reshape(n, d)


SC_LANES = 16
SC_WORKERS = 32
SC_CHUNK = 32
SC_GROUP = 8
SC_TOKENS = 17408


def _sc_params():
    cp = pltpu.CompilerParams()
    if "needs_layout_passes" in pltpu.CompilerParams.__dataclass_fields__:
        cp = dataclasses.replace(cp, needs_layout_passes=False)
    return cp


def _sc_expert_out(table_words, idx, w, x, first_tok):
    d = x.shape[1]
    n_tok = w.shape[0] // SLOTS
    per = n_tok // SC_WORKERS
    words = d // 2
    nq = words // SC_LANES // 2
    nchunk = SLOTS // SC_CHUNK
    group_chunks = SC_GROUP * nchunk
    mesh = plsc.VectorSubcoreMesh(core_axis_name="c", subcore_axis_name="s")

    @functools.partial(
        pl.kernel, mesh=mesh,
        out_type=jax.ShapeDtypeStruct((n_tok, d), F32),
        scratch_types=[pltpu.VMEM((SC_GROUP * SLOTS,), jnp.int32), pltpu.VMEM((SC_GROUP * SLOTS,), F32),
                       pltpu.VMEM((2, SC_CHUNK, words), jnp.int32), pltpu.VMEM((SC_GROUP, d), F32),
                       pltpu.SemaphoreType.DMA, pltpu.SemaphoreType.DMA],
        compiler_params=_sc_params(),
        name="peer_expert_out_sc",
    )
    def body(tab_hbm, idx_hbm, w_hbm, x_hbm, o_hbm, idx_v, w_v, rows_v, y_v, sem0, sem1):
        base = (lax.axis_index("s") * 2 + lax.axis_index("c")) * per
        zero = jnp.zeros((SC_LANES,), jnp.int32)
        himask = jnp.full((SC_LANES,), -65536, jnp.int32)
        sems = (sem0, sem1)

        def gather(k, b):
            off = pl.multiple_of(k * SC_CHUNK, SC_CHUNK)
            return pltpu.make_async_copy(tab_hbm.at[idx_v.at[pl.ds(off, SC_CHUNK)]], rows_v.at[b], sems[b])

        def accumulate(k, b):
            tok = k // nchunk
            for q in range(2):
                first = q * nq
                acc0 = (tuple(y_v[tok, pl.ds((first + j) * SC_LANES, SC_LANES)] for j in range(nq))
                        + tuple(y_v[tok, pl.ds(words + (first + j) * SC_LANES, SC_LANES)] for j in range(nq)))

                def row_body(r, accs):
                    ws = plsc.load_gather(w_v, [zero + (k * SC_CHUNK + r)])
                    hi_acc, lo_acc = [], []
                    for j in range(nq):
                        wv = rows_v[b, r, pl.ds((first + j) * SC_LANES, SC_LANES)]
                        hi_acc.append(accs[j] + plsc.bitcast(wv & himask, F32) * ws)
                        lo_acc.append(accs[nq + j] + plsc.bitcast(wv << 16, F32) * ws)
                    return tuple(hi_acc) + tuple(lo_acc)

                accs = lax.fori_loop(0, SC_CHUNK, row_body, acc0)
                for j in range(nq):
                    y_v[tok, pl.ds((first + j) * SC_LANES, SC_LANES)] = accs[j]
                    y_v[tok, pl.ds(words + (first + j) * SC_LANES, SC_LANES)] = accs[nq + j]

        @pl.loop(0, per // SC_GROUP)
        def _(g):
            t0 = pl.multiple_of(base + g * SC_GROUP, SC_GROUP)
            pltpu.sync_copy(idx_hbm.at[pl.ds(t0 * SLOTS, SC_GROUP * SLOTS)], idx_v)
            pltpu.sync_copy(w_hbm.at[pl.ds(t0 * SLOTS, SC_GROUP * SLOTS)], w_v)
            pltpu.sync_copy(x_hbm.at[pl.ds(first_tok + t0, SC_GROUP)], y_v)
            gather(0, 0).start()

            @pl.loop(0, group_chunks // 2)
            def _(kk):
                k0 = 2 * kk
                gather(k0 + 1, 1).start()
                gather(k0, 0).wait()
                accumulate(k0, 0)

                @pl.when(k0 + 2 < group_chunks)
                def _():
                    gather(k0 + 2, 0).start()
                gather(k0 + 1, 1).wait()
                accumulate(k0 + 1, 1)

            pltpu.sync_copy(y_v, o_hbm.at[pl.ds(t0, SC_GROUP)])

    return body(table_words, idx, w, x)


def _peer(x, gain, w_query, sub_keys, expert_u, expert_v):
    t = x.shape[0]
    t_tc = t - SC_TOKENS
    wq, keys = w_query.astype(BF16), sub_keys.astype(BF16)
    u_packed, v_packed = _pack_table(expert_u), _pack_table(expert_v)
    v_rows = v_packed.reshape(expert_v.shape[0], ROW_WORDS * LANES)

    h, idx, gate = _route(x, gain, wq, keys, t_tc, SC_TOKENS, v_rows[:SUBLANES])
    w_sc = _expert_in(h, _split_offsets(idx), gate, u_packed)
    out_sc = _sc_expert_out(v_rows, (idx // ROW_WORDS).reshape(-1), w_sc.reshape(-1), x, t_tc)

    h, idx, gate = _route(x, gain, wq, keys, 0, t_tc, w_sc[:SUBLANES])
    idx_split = _split_offsets(idx)
    w = _expert_in(h, idx_split, gate, u_packed)
    out_tc = _expert_out(x, idx_split, w, v_packed)
    return jnp.concatenate([out_tc, out_sc], axis=0)


def kernel(x, norm_mix, norm_ffn, attn_w_qkv, attn_q_norm, attn_k_norm, attn_sinks, attn_w_o, conv_w_in, conv_w, conv_w_out, peer_w_query, peer_sub_keys, peer_u, peer_v):
    batch, seq, d = x.shape
    xt = x.reshape(batch * seq, d)
    for i in range(norm_mix.shape[0]):
        j = i // 2
        if i % 2 == 0:
            qkv = _norm_matmul(xt, norm_mix[i], attn_w_qkv[j].astype(BF16))
            o = _attention(qkv, attn_q_norm[j], attn_k_norm[j], attn_sinks[j], batch, seq)
            xt = _matmul_residual(o, attn_w_o[j].astype(BF16), xt)
        else:
            xt = _conv_mixer(xt, norm_mix[i], conv_w_in[j], conv_w[j], conv_w_out[j], batch, seq)
        xt = _peer(xt, norm_ffn[i], peer_w_query[i], peer_sub_keys[i], peer_u[i], peer_v[i])
    return xt.reshape(batch, seq, d)
```

```python
import dataclasses
import functools
import math

import jax
import jax.numpy as jnp
from jax import lax
from jax.experimental import pallas as pl
from jax.experimental.pallas import tpu as pltpu
from jax.experimental.pallas import tpu_sc as plsc

D_MODEL = 1024
RMS_EPS = 1e-6

HEAD_DIM = 64
N_Q_HEADS = 16
N_KV_HEADS = 4
GROUP = N_Q_HEADS // N_KV_HEADS
WINDOW = 128
ROT_DIM = HEAD_DIM // 4
ROPE_THETA = 500000.0
Q_COLS = N_Q_HEADS * HEAD_DIM
KV_COLS = N_KV_HEADS * HEAD_DIM
NEG_INF = -1e30

CONV_WIDTH = 3

PEER_HEADS = 8
N_KEYS = 128
N_EXPERTS = N_KEYS * N_KEYS
PEER_TOPK = 16
QUERY_HALF = 128
SLOTS = PEER_HEADS * PEER_TOPK

LANES = 128
SUBLANES = 8
ROW_WORDS = D_MODEL // 2 // LANES
VMEM_LIMIT = 48 * 1024 * 1024

BF16 = jnp.bfloat16
F32 = jnp.float32


def _rms(x, gain):
    return x * lax.rsqrt(jnp.mean(x * x, axis=-1, keepdims=True) + RMS_EPS) * gain


def _norm_matmul_kernel(x_ref, g_ref, w_ref, o_ref):
    h = _rms(x_ref[...], g_ref[...])
    o_ref[...] = jnp.dot(h.astype(BF16), w_ref[...], preferred_element_type=F32)


def _norm_matmul(x, gain, w, tm=512):
    t, d = x.shape
    n = w.shape[1]
    return pl.pallas_call(
        _norm_matmul_kernel,
        grid=(t // tm,),
        in_specs=[pl.BlockSpec((tm, d), lambda i: (i, 0)),
                  pl.BlockSpec((1, d), lambda i: (0, 0)),
                  pl.BlockSpec((d, n), lambda i: (0, 0))],
        out_specs=pl.BlockSpec((tm, n), lambda i: (i, 0)),
        out_shape=jax.ShapeDtypeStruct((t, n), F32),
        compiler_params=pltpu.CompilerParams(dimension_semantics=("arbitrary",), vmem_limit_bytes=VMEM_LIMIT),
        name="norm_matmul",
    )(x, gain.reshape(1, d), w)


def _matmul_residual_kernel(a_ref, w_ref, r_ref, o_ref):
    o_ref[...] = r_ref[...] + jnp.dot(a_ref[...].astype(BF16), w_ref[...], preferred_element_type=F32)


def _matmul_residual(a, w, res, tm=512):
    t, k = a.shape
    n = w.shape[1]
    return pl.pallas_call(
        _matmul_residual_kernel,
        grid=(t // tm,),
        in_specs=[pl.BlockSpec((tm, k), lambda i: (i, 0)),
                  pl.BlockSpec((k, n), lambda i: (0, 0)),
                  pl.BlockSpec((tm, n), lambda i: (i, 0))],
        out_specs=pl.BlockSpec((tm, n), lambda i: (i, 0)),
        out_shape=jax.ShapeDtypeStruct((t, n), F32),
        compiler_params=pltpu.CompilerParams(dimension_semantics=("arbitrary",), vmem_limit_bytes=VMEM_LIMIT),
        name="matmul_residual",
    )(a, w, res)


def _rope_tables(seq):
    half = ROT_DIM // 2
    freqs = ROPE_THETA ** (-jnp.arange(0, ROT_DIM, 2, dtype=F32) / ROT_DIM)
    ang = jnp.arange(seq, dtype=F32)[:, None] * freqs[None, :]
    cos, sin = jnp.cos(ang), jnp.sin(ang)
    ones = jnp.ones((seq, HEAD_DIM - ROT_DIM), F32)
    zeros = jnp.zeros((seq, HEAD_DIM - ROT_DIM), F32)
    zh = jnp.zeros((seq, half), F32)
    c = jnp.concatenate([cos, cos, ones], axis=1)
    s_next = jnp.concatenate([-sin, zh, zeros], axis=1)
    s_prev = jnp.concatenate([zh, sin, zeros], axis=1)
    return jnp.stack([jnp.tile(c, (1, 2)), jnp.tile(s_next, (1, 2)), jnp.tile(s_prev, (1, 2))])


def _head_norm_rope(x, gain2, rope, lo):
    sq = x * x
    s_lo = jnp.sum(jnp.where(lo, sq, 0.0), axis=1, keepdims=True)
    s_hi = jnp.sum(jnp.where(lo, 0.0, sq), axis=1, keepdims=True)
    ms = jnp.where(lo, s_lo, s_hi) * (1.0 / HEAD_DIM)
    xn = x * lax.rsqrt(ms + RMS_EPS) * gain2
    half = ROT_DIM // 2
    return xn * rope[0] + pltpu.roll(xn, LANES - half, 1) * rope[1] + pltpu.roll(xn, half, 1) * rope[2]


def _attn_kernel(sinks_ref, q_ref, kc_ref, kp_ref, vc_ref, vp_ref, rc_ref, rp_ref, qg_ref, kg_ref, o_ref):
    n = pl.program_id(1)
    lo = lax.broadcasted_iota(jnp.int32, (WINDOW, LANES), 1) < HEAD_DIM
    rope_c = rc_ref[...]
    rope_p = rp_ref[...]
    qg = qg_ref[...]
    kg = kg_ref[...]

    rows = GROUP * WINDOW
    qi = lax.broadcasted_iota(jnp.int32, (rows, 2 * WINDOW), 0) & (WINDOW - 1)
    ki = lax.broadcasted_iota(jnp.int32, (rows, 2 * WINDOW), 1)
    rel = WINDOW + qi - ki
    valid = (rel >= 0) & (rel < WINDOW) & ((n > 0) | (ki >= WINDOW))
    head_of_row = lax.broadcasted_iota(jnp.int32, (rows, 1), 0) // WINDOW
    scale = 1.0 / math.sqrt(HEAD_DIM)

    q2 = [_head_norm_rope(q_ref[:, c * LANES:(c + 1) * LANES], qg, rope_c, lo).astype(BF16)
          for c in range(Q_COLS // LANES)]
    for c in range(KV_COLS // LANES):
        cols = slice(c * LANES, (c + 1) * LANES)
        kprev = _head_norm_rope(kp_ref[:, cols], kg, rope_p, lo)
        kcur = _head_norm_rope(kc_ref[:, cols], kg, rope_c, lo)
        kfull = jnp.concatenate([kprev, kcur], axis=0).astype(BF16)
        vfull = jnp.concatenate([vp_ref[:, cols], vc_ref[:, cols]], axis=0).astype(BF16)
        for hh in range(LANES // HEAD_DIM):
            h = (LANES // HEAD_DIM) * c + hh
            kh = kfull[:, hh * HEAD_DIM:(hh + 1) * HEAD_DIM]
            vh = vfull[:, hh * HEAD_DIM:(hh + 1) * HEAD_DIM]
            heads = [GROUP * h + g for g in range(GROUP)]
            q4 = jnp.concatenate([q2[j // 2][:, (j % 2) * HEAD_DIM:(j % 2 + 1) * HEAD_DIM] for j in heads], axis=0)
            s = lax.dot_general(q4, kh, (((1,), (1,)), ((), ())), preferred_element_type=F32) * scale
            s = jnp.where(valid, s, NEG_INF)
            sink = jnp.zeros((rows, 1), F32)
            for g, j in enumerate(heads):
                sink = jnp.where(head_of_row == g, sinks_ref[j], sink)
            m = jnp.maximum(jnp.max(s, axis=1, keepdims=True), sink)
            p = jnp.exp(s - m)
            denom = jnp.sum(p, axis=1, keepdims=True) + jnp.exp(sink - m)
            o = jnp.dot(p.astype(BF16), vh, preferred_element_type=F32) / denom
            for g, j in enumerate(heads):
                o_ref[:, j * HEAD_DIM:(j + 1) * HEAD_DIM] = o[g * WINDOW:(g + 1) * WINDOW]


def _attention(qkv, q_gain, k_gain, sinks, batch, seq):
    t = batch * seq
    nb = seq // WINDOW
    rope = _rope_tables(seq)
    kcol = Q_COLS // KV_COLS
    cur = lambda b, n: (b * nb + n, 0)
    kcur = lambda b, n: (b * nb + n, kcol)
    kprev = lambda b, n: (b * nb + jnp.maximum(n - 1, 0), kcol)
    vcur = lambda b, n: (b * nb + n, kcol + 1)
    vprev = lambda b, n: (b * nb + jnp.maximum(n - 1, 0), kcol + 1)
    return pl.pallas_call(
        _attn_kernel,
        grid=(batch, nb),
        in_specs=[pl.BlockSpec(memory_space=pltpu.SMEM),
                  pl.BlockSpec((WINDOW, Q_COLS), cur),
                  pl.BlockSpec((WINDOW, KV_COLS), kcur),
                  pl.BlockSpec((WINDOW, KV_COLS), kprev),
                  pl.BlockSpec((WINDOW, KV_COLS), vcur),
                  pl.BlockSpec((WINDOW, KV_COLS), vprev),
                  pl.BlockSpec((3, WINDOW, LANES), lambda b, n: (0, n, 0)),
                  pl.BlockSpec((3, WINDOW, LANES), lambda b, n: (0, jnp.maximum(n - 1, 0), 0)),
                  pl.BlockSpec((1, LANES), lambda b, n: (0, 0)),
                  pl.BlockSpec((1, LANES), lambda b, n: (0, 0))],
        out_specs=pl.BlockSpec((WINDOW, Q_COLS), cur),
        out_shape=jax.ShapeDtypeStruct((t, Q_COLS), F32),
        compiler_params=pltpu.CompilerParams(dimension_semantics=("arbitrary", "arbitrary"),
                                             vmem_limit_bytes=VMEM_LIMIT),
        name="swa_attention",
    )(sinks, qkv, qkv, qkv, qkv, qkv, rope, rope,
      jnp.tile(q_gain, 2).reshape(1, LANES), jnp.tile(k_gain, 2).reshape(1, LANES))


def _conv_kernel(x_ref, g_ref, win_ref, cw_ref, wout_ref, o_ref, zprev_ref):
    n = pl.program_id(1)
    d = D_MODEL

    @pl.when(n == 0)
    def _():
        zprev_ref[...] = jnp.zeros_like(zprev_ref)

    x = x_ref[...]
    h = _rms(x, g_ref[...])
    bcu = jnp.dot(h.astype(BF16), win_ref[...], preferred_element_type=F32)
    gate_b = bcu[:, :d]
    z = bcu[:, d:2 * d] * bcu[:, 2 * d:]
    tm = z.shape[0]
    row = lax.broadcasted_iota(jnp.int32, z.shape, 0)
    prev = zprev_ref[...]
    p_last = prev[SUBLANES - 1:SUBLANES, :]
    p_last2 = prev[SUBLANES - 2:SUBLANES - 1, :]
    z1 = jnp.where(row == 0, p_last, pltpu.roll(z, 1, 0))
    z2 = jnp.where(row == 0, p_last2, jnp.where(row == 1, p_last, pltpu.roll(z, 2, 0)))
    cw = cw_ref[...]
    conv = cw[0:1, :] * z2 + cw[1:2, :] * z1 + cw[2:3, :] * z
    zprev_ref[...] = z[tm - SUBLANES:, :]
    o_ref[...] = x + jnp.dot((gate_b * conv).astype(BF16), wout_ref[...], preferred_element_type=F32)


def _conv_mixer(x, gain, w_in, conv_w, w_out, batch, seq, tm=256):
    t, d = x.shape
    nblk = seq // tm
    blk = lambda b, n: (b * nblk + n, 0)
    const = lambda b, n: (0, 0)
    return pl.pallas_call(
        _conv_kernel,
        grid=(batch, nblk),
        in_specs=[pl.BlockSpec((tm, d), blk),
                  pl.BlockSpec((1, d), const),
                  pl.BlockSpec((d, 3 * d), const),
                  pl.BlockSpec((CONV_WIDTH, d), const),
                  pl.BlockSpec((d, d), const)],
        out_specs=pl.BlockSpec((tm, d), blk),
        out_shape=jax.ShapeDtypeStruct((t, d), F32),
        scratch_shapes=[pltpu.VMEM((SUBLANES, d), F32)],
        compiler_params=pltpu.CompilerParams(dimension_semantics=("arbitrary", "arbitrary"),
                                             vmem_limit_bytes=VMEM_LIMIT),
        name="conv_mixer",
    )(x, gain.reshape(1, d), w_in.astype(BF16), conv_w, w_out.astype(BF16))


def _topk_axis0(s, k, ids=None, payload=None):
    n, tm = s.shape
    if ids is None:
        ids = lax.broadcasted_iota(jnp.int32, (n, tm), 0)
    krow = lax.broadcasted_iota(jnp.int32, (k, tm), 0)
    vals = jnp.zeros((k, tm), F32)
    picks = jnp.zeros((k, tm), jnp.int32)
    for r in range(k):
        m = jnp.max(s, axis=0, keepdims=True)
        pos = jnp.min(jnp.where(s == m, ids, jnp.iinfo(jnp.int32).max), axis=0, keepdims=True)
        sel = ids == pos
        if payload is None:
            picked = pos
        else:
            picked = jnp.sum(jnp.where(sel, payload, 0), axis=0, keepdims=True)
        vals = jnp.where(krow == r, m, vals)
        picks = jnp.where(krow == r, picked, picks)
        s = jnp.where(sel, -jnp.inf, s)
    return vals, picks


def _pair_candidates(s1, i1, s2, i2):
    k, tm = s1.shape
    sub = lax.broadcasted_iota(jnp.int32, (SUBLANES, tm), 0)
    scores, flat, expert = [], [], []
    for i in range(k // 2):
        width = k if i == 0 else SUBLANES
        sc = s1[i:i + 1, :] + s2[0:width, :]
        ex = i1[i:i + 1, :] * N_KEYS + i2[0:width, :]
        fl = i * k + lax.broadcasted_iota(jnp.int32, (width, tm), 0)
        reach = k // (i + 1)
        if reach < width:
            sc = jnp.where(sub < reach, sc, -jnp.inf)
        scores.append(sc)
        flat.append(fl)
        expert.append(ex)
    scores.append(s1[k // 2:, :] + s2[0:1, :])
    expert.append(i1[k // 2:, :] * N_KEYS + i2[0:1, :])
    flat.append((k // 2 + sub) * k)
    return jnp.concatenate(scores, axis=0), jnp.concatenate(flat, axis=0), jnp.concatenate(expert, axis=0)


def _route_kernel(x_ref, g_ref, wq_ref, keys_ref, after_ref, h_ref, idx_ref, gate_ref):
    del after_ref
    h = _rms(x_ref[...], g_ref[...])
    h_ref[...] = h
    q = jnp.dot(h.astype(BF16), wq_ref[...], preferred_element_type=F32).astype(BF16)
    idx_rows, gate_rows = [], []
    for head in range(PEER_HEADS):
        tops = []
        for part in range(2):
            col = (head * 2 + part) * QUERY_HALF
            s = lax.dot_general(keys_ref[head, part], q[:, col:col + QUERY_HALF],
                                (((1,), (1,)), ((), ())), preferred_element_type=F32)
            tops.append(_topk_axis0(s, PEER_TOPK))
        (s1, i1), (s2, i2) = tops
        cand, flat_ids, cand_idx = _pair_candidates(s1, i1, s2, i2)
        g_s, e_idx = _topk_axis0(cand, PEER_TOPK, ids=flat_ids, payload=cand_idx)
        e = jnp.exp(g_s - jnp.max(g_s, axis=0, keepdims=True))
        gate_rows.append(e / jnp.sum(e, axis=0, keepdims=True))
        idx_rows.append(e_idx * ROW_WORDS)
    idx_ref[...] = jnp.concatenate(idx_rows, axis=0).T
    gate_ref[...] = jnp.concatenate(gate_rows, axis=0).T


def _route(x, gain, w_query, sub_keys, first_tok, n_tok, after, tm=128):
    t, d = x.shape
    nq = w_query.shape[1]
    first_blk = first_tok // tm
    return pl.pallas_call(
        _route_kernel,
        grid=(n_tok // tm,),
        in_specs=[pl.BlockSpec((tm, d), lambda i: (i + first_blk, 0)),
                  pl.BlockSpec((1, d), lambda i: (0, 0)),
                  pl.BlockSpec((d, nq), lambda i: (0, 0)),
                  pl.BlockSpec((PEER_HEADS, 2, N_KEYS, QUERY_HALF), lambda i: (0, 0, 0, 0)),
                  pl.BlockSpec(after.shape, lambda i: (0, 0))],
        out_specs=[pl.BlockSpec((tm, d), lambda i: (i, 0)),
                   pl.BlockSpec((tm, SLOTS), lambda i: (i, 0)),
                   pl.BlockSpec((tm, SLOTS), lambda i: (i, 0))],
        out_shape=[jax.ShapeDtypeStruct((n_tok, d), F32),
                   jax.ShapeDtypeStruct((n_tok, SLOTS), jnp.int32),
                   jax.ShapeDtypeStruct((n_tok, SLOTS), F32)],
        compiler_params=pltpu.CompilerParams(dimension_semantics=("arbitrary",), vmem_limit_bytes=VMEM_LIMIT),
        name="peer_route",
    )(x, gain.reshape(1, d), w_query, sub_keys, after)


def _pack_table(tab):
    bits = lax.bitcast_convert_type(tab.astype(BF16), jnp.uint16).astype(jnp.uint32)
    half = D_MODEL // 2
    words = (bits[:, :half] << 16) | bits[:, half:]
    return lax.bitcast_convert_type(words, jnp.int32).reshape(tab.shape[0] * ROW_WORDS, LANES)


def _load_table_once(tab_hbm, tab, sem):
    @pl.when(pl.program_id(0) == 0)
    def _():
        cp = pltpu.make_async_copy(tab_hbm, tab, sem)
        cp.start()
        cp.wait()


def _gather_pair(tab, off_a, off_b):
    ra = tab[pl.ds(pl.multiple_of(off_a, ROW_WORDS), ROW_WORDS), :]
    rb = tab[pl.ds(pl.multiple_of(off_b, ROW_WORDS), ROW_WORDS), :]
    words = jnp.concatenate([ra, rb], axis=0)
    hi = pltpu.bitcast(words & jnp.int32(-65536), F32)
    lo = pltpu.bitcast(words << 16, F32)
    return hi, lo


TOKEN_UNROLL = 8


def _expert_in_kernel(*refs):
    idx_refs = refs[:TOKEN_UNROLL]
    h_ref, gate_ref, tab_hbm, w_ref, tab, sem, a_ref = refs[TOKEN_UNROLL:]
    _load_table_once(tab_hbm, tab, sem)
    sub = lax.broadcasted_iota(jnp.int32, (SUBLANES, LANES), 0)
    lane = lax.broadcasted_iota(jnp.int32, (SUBLANES, LANES), 1)
    own_half = (sub >= ROW_WORDS) == ((lane & 1) == 1)
    tb = h_ref.shape[0]

    def step(i, carry):
        xs = []
        for u in range(TOKEN_UNROLL):
            x = h_ref[i * TOKEN_UNROLL + u]
            xs.append((jnp.concatenate([x[0:ROW_WORDS], x[0:ROW_WORDS]], axis=0),
                       jnp.concatenate([x[ROW_WORDS:], x[ROW_WORDS:]], axis=0)))
        accs = [jnp.zeros((SUBLANES, LANES), F32) for _ in range(TOKEN_UNROLL)]
        for p in range(SLOTS // 2):
            for u in range(TOKEN_UNROLL):
                hi, lo = _gather_pair(tab, idx_refs[u][i, 2 * p], idx_refs[u][i, 2 * p + 1])
                part = jnp.sum(hi * xs[u][0] + lo * xs[u][1], axis=1, keepdims=True)
                accs[u] = jnp.where((lane >> 1) == p, part, accs[u])
        for u in range(TOKEN_UNROLL):
            a_ref[pl.ds(i * TOKEN_UNROLL + u, 1), :] = jnp.sum(jnp.where(own_half, accs[u], 0.0),
                                                                axis=0, keepdims=True)
        return carry

    lax.fori_loop(0, tb // TOKEN_UNROLL, step, 0)
    a = a_ref[...]
    w_ref[...] = gate_ref[...] * (0.5 * a * (1.0 + lax.erf(a * (1.0 / math.sqrt(2.0)))))


def _expert_out_kernel(*refs):
    idx_refs = refs[:TOKEN_UNROLL]
    w_ref, x_ref, tab_hbm, o_ref, tab, sem, wb_ref = refs[TOKEN_UNROLL:]
    _load_table_once(tab_hbm, tab, sem)
    sub = lax.broadcasted_iota(jnp.int32, (SUBLANES, LANES), 0)
    lower = sub < ROW_WORDS
    tb = x_ref.shape[0]

    def step(i, carry):
        for u in range(TOKEN_UNROLL):
            row = w_ref[pl.ds(i * TOKEN_UNROLL + u, 1), :]
            wb_ref[u] = jnp.broadcast_to(row, (SLOTS, LANES)).T
        acc_h = [jnp.zeros((SUBLANES, LANES), F32) for _ in range(TOKEN_UNROLL)]
        acc_l = [jnp.zeros((SUBLANES, LANES), F32) for _ in range(TOKEN_UNROLL)]
        for p in range(SLOTS // 2):
            ka, kb = 2 * p, 2 * p + 1
            for u in range(TOKEN_UNROLL):
                hi, lo = _gather_pair(tab, idx_refs[u][i, ka], idx_refs[u][i, kb])
                wa = jnp.broadcast_to(wb_ref[u, ka:ka + 1, :], (SUBLANES, LANES))
                wb = jnp.broadcast_to(wb_ref[u, kb:kb + 1, :], (SUBLANES, LANES))
                wt = jnp.where(lower, wa, wb)
                acc_h[u] = acc_h[u] + hi * wt
                acc_l[u] = acc_l[u] + lo * wt
        for u in range(TOKEN_UNROLL):
            t = i * TOKEN_UNROLL + u
            ah = acc_h[u] + pltpu.roll(acc_h[u], ROW_WORDS, 0)
            al = acc_l[u] + pltpu.roll(acc_l[u], ROW_WORDS, 0)
            o_ref[t] = x_ref[t] + jnp.where(lower, ah, al)
        return carry

    lax.fori_loop(0, tb // TOKEN_UNROLL, step, 0)


def _expert_specs(tb):
    un = TOKEN_UNROLL
    smem_blks = [pl.BlockSpec((None, tb // un, SLOTS), functools.partial(lambda u, i: (u, i, 0), u),
                              memory_space=pltpu.SMEM) for u in range(un)]
    vmem_blk = pl.BlockSpec((tb, SLOTS), lambda i: (i, 0))
    tok_blk = pl.BlockSpec((tb, D_MODEL // LANES, LANES), lambda i: (i, 0, 0))
    params = pltpu.CompilerParams(dimension_semantics=("arbitrary",), vmem_limit_bytes=VMEM_LIMIT)
    return smem_blks, vmem_blk, tok_blk, params


def _split_offsets(idx):
    n = idx.shape[0]
    return idx.reshape(n // TOKEN_UNROLL, TOKEN_UNROLL, SLOTS).transpose(1, 0, 2)


def _expert_in(h, idx_split, gate, u_packed, tb=128):
    n, d = h.shape
    smem_blks, vmem_blk, tok_blk, params = _expert_specs(tb)
    return pl.pallas_call(
        _expert_in_kernel,
        grid=(n // tb,),
        in_specs=smem_blks + [tok_blk, vmem_blk, pl.BlockSpec(memory_space=pl.ANY)],
        out_specs=vmem_blk,
        out_shape=jax.ShapeDtypeStruct((n, SLOTS), F32),
        scratch_shapes=[pltpu.VMEM(u_packed.shape, jnp.int32), pltpu.SemaphoreType.DMA, pltpu.VMEM((tb, SLOTS), F32)],
        compiler_params=params,
        name="peer_expert_in",
    )(*([idx_split] * TOKEN_UNROLL), h.reshape(n, d // LANES, LANES), gate, u_packed)


def _expert_out(x, idx_split, w, v_packed, tb=128):
    t, d = x.shape
    n = w.shape[0]
    smem_blks, vmem_blk, tok_blk, params = _expert_specs(tb)
    out = pl.pallas_call(
        _expert_out_kernel,
        grid=(n // tb,),
        in_specs=smem_blks + [vmem_blk, tok_blk, pl.BlockSpec(memory_space=pl.ANY)],
        out_specs=tok_blk,
        out_shape=jax.ShapeDtypeStruct((n, d // LANES, LANES), F32),
        scratch_shapes=[pltpu.VMEM(v_packed.shape, jnp.int32), pltpu.SemaphoreType.DMA,
                        pltpu.VMEM((TOKEN_UNROLL, SLOTS, LANES), F32)],
        compiler_params=params,
        name="peer_expert_out",
    )(*([idx_split] * TOKEN_UNROLL), w, x.reshape(t, d // LANES, LANES), v_packed)
    return out.reshape(n, d)


SC_LANES = 16
SC_WORKERS = 32
SC_CHUNK = 32
SC_GROUP = 8
SC_TOKENS = 17408


def _sc_params():
    cp = pltpu.CompilerParams()
    if "needs_layout_passes" in pltpu.CompilerParams.__dataclass_fields__:
        cp = dataclasses.replace(cp, needs_layout_passes=False)
    return cp


def _sc_expert_out(table_words, idx, w, x, first_tok):
    d = x.shape[1]
    n_tok = w.shape[0] // SLOTS
    per = n_tok // SC_WORKERS
    words = d // 2
    nq = words // SC_LANES // 2
    nchunk = SLOTS // SC_CHUNK
    group_chunks = SC_GROUP * nchunk
    mesh = plsc.VectorSubcoreMesh(core_axis_name="c", subcore_axis_name="s")

    @functools.partial(
        pl.kernel, mesh=mesh,
        out_type=jax.ShapeDtypeStruct((n_tok, d), F32),
        scratch_types=[pltpu.VMEM((SC_GROUP * SLOTS,), jnp.int32), pltpu.VMEM((SC_GROUP * SLOTS,), F32),
                       pltpu.VMEM((2, SC_CHUNK, words), jnp.int32), pltpu.VMEM((SC_GROUP, d), F32),
                       pltpu.SemaphoreType.DMA, pltpu.SemaphoreType.DMA],
        compiler_params=_sc_params(),
        name="peer_expert_out_sc",
    )
    def body(tab_hbm, idx_hbm, w_hbm, x_hbm, o_hbm, idx_v, w_v, rows_v, y_v, sem0, sem1):
        base = (lax.axis_index("s") * 2 + lax.axis_index("c")) * per
        zero = jnp.zeros((SC_LANES,), jnp.int32)
        himask = jnp.full((SC_LANES,), -65536, jnp.int32)
        sems = (sem0, sem1)

        def gather(k, b):
            off = pl.multiple_of(k * SC_CHUNK, SC_CHUNK)
            return pltpu.make_async_copy(tab_hbm.at[idx_v.at[pl.ds(off, SC_CHUNK)]], rows_v.at[b], sems[b])

        def accumulate(k, b):
            tok = k // nchunk
            for q in range(2):
                first = q * nq
                acc0 = (tuple(y_v[tok, pl.ds((first + j) * SC_LANES, SC_LANES)] for j in range(nq))
                        + tuple(y_v[tok, pl.ds(words + (first + j) * SC_LANES, SC_LANES)] for j in range(nq)))

                def row_body(r, accs):
                    ws = plsc.load_gather(w_v, [zero + (k * SC_CHUNK + r)])
                    hi_acc, lo_acc = [], []
                    for j in range(nq):
                        wv = rows_v[b, r, pl.ds((first + j) * SC_LANES, SC_LANES)]
                        hi_acc.append(accs[j] + plsc.bitcast(wv & himask, F32) * ws)
                        lo_acc.append(accs[nq + j] + plsc.bitcast(wv << 16, F32) * ws)
                    return tuple(hi_acc) + tuple(lo_acc)

                accs = plsc.parallel_loop(0, SC_CHUNK, unroll=2, carry=acc0)(row_body)
                for j in range(nq):
                    y_v[tok, pl.ds((first + j) * SC_LANES, SC_LANES)] = accs[j]
                    y_v[tok, pl.ds(words + (first + j) * SC_LANES, SC_LANES)] = accs[nq + j]

        @pl.loop(0, per // SC_GROUP)
        def _(g):
            t0 = pl.multiple_of(base + g * SC_GROUP, SC_GROUP)
            pltpu.sync_copy(idx_hbm.at[pl.ds(t0 * SLOTS, SC_GROUP * SLOTS)], idx_v)
            pltpu.sync_copy(w_hbm.at[pl.ds(t0 * SLOTS, SC_GROUP * SLOTS)], w_v)
            pltpu.sync_copy(x_hbm.at[pl.ds(first_tok + t0, SC_GROUP)], y_v)
            gather(0, 0).start()

            @pl.loop(0, group_chunks // 2)
            def _(kk):
                k0 = 2 * kk
                gather(k0 + 1, 1).start()
                gather(k0, 0).wait()
                accumulate(k0, 0)

                @pl.when(k0 + 2 < group_chunks)
                def _():
                    gather(k0 + 2, 0).start()
                gather(k0 + 1, 1).wait()
                accumulate(k0 + 1, 1)

            pltpu.sync_copy(y_v, o_hbm.at[pl.ds(t0, SC_GROUP)])

    return body(table_words, idx, w, x)


def _peer(x, gain, w_query, sub_keys, expert_u, expert_v):
    t = x.shape[0]
    t_tc = t - SC_TOKENS
    wq, keys = w_query.astype(BF16), sub_keys.astype(BF16)
    u_packed, v_packed = _pack_table(expert_u), _pack_table(expert_v)
    v_rows = v_packed.reshape(expert_v.shape[0], ROW_WORDS * LANES)

    h, idx, gate = _route(x, gain, wq, keys, t_tc, SC_TOKENS, v_rows[:SUBLANES])
    w_sc = _expert_in(h, _split_offsets(idx), gate, u_packed)
    out_sc = _sc_expert_out(v_rows, (idx // ROW_WORDS).reshape(-1), w_sc.reshape(-1), x, t_tc)

    h, idx, gate = _route(x, gain, wq, keys, 0, t_tc, w_sc[:SUBLANES])
    idx_split = _split_offsets(idx)
    w = _expert_in(h, idx_split, gate, u_packed)
    out_tc = _expert_out(x, idx_split, w, v_packed)
    return jnp.concatenate([out_tc, out_sc], axis=0)


def kernel(x, norm_mix, norm_ffn, attn_w_qkv, attn_q_norm, attn_k_norm, attn_sinks, attn_w_o, conv_w_in, conv_w, conv_w_out, peer_w_query, peer_sub_keys, peer_u, peer_v):
    batch, seq, d = x.shape
    xt = x.reshape(batch * seq, d)
    for i in range(norm_mix.shape[0]):
        j = i // 2
        if i % 2 == 0:
            qkv = _norm_matmul(xt, norm_mix[i], attn_w_qkv[j].astype(BF16))
            o = _attention(qkv, attn_q_norm[j], attn_k_norm[j], attn_sinks[j], batch, seq)
            xt = _matmul_residual(o, attn_w_o[j].astype(BF16), xt)
        else:
            xt = _conv_mixer(xt, norm_mix[i], conv_w_in[j], conv_w[j], conv_w_out[j], batch, seq)
        xt = _peer(xt, norm_ffn[i], peer_w_query[i], peer_sub_keys[i], peer_u[i], peer_v[i])
    return xt.reshape(batch, seq, d)
```

```python
import dataclasses
import functools
import math

import jax
import jax.numpy as jnp
from jax import lax
from jax.experimental import pallas as pl
from jax.experimental.pallas import tpu as pltpu
from jax.experimental.pallas import tpu_sc as plsc

D_MODEL = 1024
RMS_EPS = 1e-6

HEAD_DIM = 64
N_Q_HEADS = 16
N_KV_HEADS = 4
GROUP = N_Q_HEADS // N_KV_HEADS
WINDOW = 128
ROT_DIM = HEAD_DIM // 4
ROPE_THETA = 500000.0
Q_COLS = N_Q_HEADS * HEAD_DIM
KV_COLS = N_KV_HEADS * HEAD_DIM
NEG_INF = -1e30

CONV_WIDTH = 3

PEER_HEADS = 8
N_KEYS = 128
N_EXPERTS = N_KEYS * N_KEYS
PEER_TOPK = 16
QUERY_HALF = 128
SLOTS = PEER_HEADS * PEER_TOPK

LANES = 128
SUBLANES = 8
ROW_WORDS = D_MODEL // 2 // LANES
VMEM_LIMIT = 48 * 1024 * 1024

BF16 = jnp.bfloat16
F32 = jnp.float32


def _rms(x, gain):
    return x * lax.rsqrt(jnp.mean(x * x, axis=-1, keepdims=True) + RMS_EPS) * gain


def _norm_matmul_kernel(x_ref, g_ref, w_ref, o_ref):
    h = _rms(x_ref[...], g_ref[...])
    o_ref[...] = jnp.dot(h.astype(BF16), w_ref[...], preferred_element_type=F32)


def _norm_matmul(x, gain, w, tm=512):
    t, d = x.shape
    n = w.shape[1]
    return pl.pallas_call(
        _norm_matmul_kernel,
        grid=(t // tm,),
        in_specs=[pl.BlockSpec((tm, d), lambda i: (i, 0)),
                  pl.BlockSpec((1, d), lambda i: (0, 0)),
                  pl.BlockSpec((d, n), lambda i: (0, 0))],
        out_specs=pl.BlockSpec((tm, n), lambda i: (i, 0)),
        out_shape=jax.ShapeDtypeStruct((t, n), F32),
        compiler_params=pltpu.CompilerParams(dimension_semantics=("arbitrary",), vmem_limit_bytes=VMEM_LIMIT),
        name="norm_matmul",
    )(x, gain.reshape(1, d), w)


def _matmul_residual_kernel(a_ref, w_ref, r_ref, o_ref):
    o_ref[...] = r_ref[...] + jnp.dot(a_ref[...].astype(BF16), w_ref[...], preferred_element_type=F32)


def _matmul_residual(a, w, res, tm=512):
    t, k = a.shape
    n = w.shape[1]
    return pl.pallas_call(
        _matmul_residual_kernel,
        grid=(t // tm,),
        in_specs=[pl.BlockSpec((tm, k), lambda i: (i, 0)),
                  pl.BlockSpec((k, n), lambda i: (0, 0)),
                  pl.BlockSpec((tm, n), lambda i: (i, 0))],
        out_specs=pl.BlockSpec((tm, n), lambda i: (i, 0)),
        out_shape=jax.ShapeDtypeStruct((t, n), F32),
        compiler_params=pltpu.CompilerParams(dimension_semantics=("arbitrary",), vmem_limit_bytes=VMEM_LIMIT),
        name="matmul_residual",
    )(a, w, res)


def _rope_tables(seq):
    half = ROT_DIM // 2
    freqs = ROPE_THETA ** (-jnp.arange(0, ROT_DIM, 2, dtype=F32) / ROT_DIM)
    ang = jnp.arange(seq, dtype=F32)[:, None] * freqs[None, :]
    cos, sin = jnp.cos(ang), jnp.sin(ang)
    ones = jnp.ones((seq, HEAD_DIM - ROT_DIM), F32)
    zeros = jnp.zeros((seq, HEAD_DIM - ROT_DIM), F32)
    zh = jnp.zeros((seq, half), F32)
    c = jnp.concatenate([cos, cos, ones], axis=1)
    s_next = jnp.concatenate([-sin, zh, zeros], axis=1)
    s_prev = jnp.concatenate([zh, sin, zeros], axis=1)
    return jnp.stack([jnp.tile(c, (1, 2)), jnp.tile(s_next, (1, 2)), jnp.tile(s_prev, (1, 2))])


def _head_norm_rope(x, gain2, rope, lo):
    sq = x * x
    s_lo = jnp.sum(jnp.where(lo, sq, 0.0), axis=1, keepdims=True)
    s_hi = jnp.sum(jnp.where(lo, 0.0, sq), axis=1, keepdims=True)
    ms = jnp.where(lo, s_lo, s_hi) * (1.0 / HEAD_DIM)
    xn = x * lax.rsqrt(ms + RMS_EPS) * gain2
    half = ROT_DIM // 2
    return xn * rope[0] + pltpu.roll(xn, LANES - half, 1) * rope[1] + pltpu.roll(xn, half, 1) * rope[2]


def _attn_kernel(sinks_ref, q_ref, kc_ref, kp_ref, vc_ref, vp_ref, rc_ref, rp_ref, qg_ref, kg_ref, o_ref):
    n = pl.program_id(1)
    lo = lax.broadcasted_iota(jnp.int32, (WINDOW, LANES), 1) < HEAD_DIM
    rope_c = rc_ref[...]
    rope_p = rp_ref[...]
    qg = qg_ref[...]
    kg = kg_ref[...]

    rows = GROUP * WINDOW
    qi = lax.broadcasted_iota(jnp.int32, (rows, 2 * WINDOW), 0) & (WINDOW - 1)
    ki = lax.broadcasted_iota(jnp.int32, (rows, 2 * WINDOW), 1)
    rel = WINDOW + qi - ki
    valid = (rel >= 0) & (rel < WINDOW) & ((n > 0) | (ki >= WINDOW))
    head_of_row = lax.broadcasted_iota(jnp.int32, (rows, 1), 0) // WINDOW
    scale = 1.0 / math.sqrt(HEAD_DIM)

    q2 = [_head_norm_rope(q_ref[:, c * LANES:(c + 1) * LANES], qg, rope_c, lo).astype(BF16)
          for c in range(Q_COLS // LANES)]
    for c in range(KV_COLS // LANES):
        cols = slice(c * LANES, (c + 1) * LANES)
        kprev = _head_norm_rope(kp_ref[:, cols], kg, rope_p, lo)
        kcur = _head_norm_rope(kc_ref[:, cols], kg, rope_c, lo)
        kfull = jnp.concatenate([kprev, kcur], axis=0).astype(BF16)
        vfull = jnp.concatenate([vp_ref[:, cols], vc_ref[:, cols]], axis=0).astype(BF16)
        for hh in range(LANES // HEAD_DIM):
            h = (LANES // HEAD_DIM) * c + hh
            kh = kfull[:, hh * HEAD_DIM:(hh + 1) * HEAD_DIM]
            vh = vfull[:, hh * HEAD_DIM:(hh + 1) * HEAD_DIM]
            heads = [GROUP * h + g for g in range(GROUP)]
            q4 = jnp.concatenate([q2[j // 2][:, (j % 2) * HEAD_DIM:(j % 2 + 1) * HEAD_DIM] for j in heads], axis=0)
            s = lax.dot_general(q4, kh, (((1,), (1,)), ((), ())), preferred_element_type=F32) * scale
            s = jnp.where(valid, s, NEG_INF)
            sink = jnp.zeros((rows, 1), F32)
            for g, j in enumerate(heads):
                sink = jnp.where(head_of_row == g, sinks_ref[j], sink)
            m = jnp.maximum(jnp.max(s, axis=1, keepdims=True), sink)
            p = jnp.exp(s - m)
            denom = jnp.sum(p, axis=1, keepdims=True) + jnp.exp(sink - m)
            o = jnp.dot(p.astype(BF16), vh, preferred_element_type=F32) / denom
            for g, j in enumerate(heads):
                o_ref[:, j * HEAD_DIM:(j + 1) * HEAD_DIM] = o[g * WINDOW:(g + 1) * WINDOW]


def _attention(qkv, q_gain, k_gain, sinks, batch, seq):
    t = batch * seq
    nb = seq // WINDOW
    rope = _rope_tables(seq)
    kcol = Q_COLS // KV_COLS
    cur = lambda b, n: (b * nb + n, 0)
    kcur = lambda b, n: (b * nb + n, kcol)
    kprev = lambda b, n: (b * nb + jnp.maximum(n - 1, 0), kcol)
    vcur = lambda b, n: (b * nb + n, kcol + 1)
    vprev = lambda b, n: (b * nb + jnp.maximum(n - 1, 0), kcol + 1)
    return pl.pallas_call(
        _attn_kernel,
        grid=(batch, nb),
        in_specs=[pl.BlockSpec(memory_space=pltpu.SMEM),
                  pl.BlockSpec((WINDOW, Q_COLS), cur),
                  pl.BlockSpec((WINDOW, KV_COLS), kcur),
                  pl.BlockSpec((WINDOW, KV_COLS), kprev),
                  pl.BlockSpec((WINDOW, KV_COLS), vcur),
                  pl.BlockSpec((WINDOW, KV_COLS), vprev),
                  pl.BlockSpec((3, WINDOW, LANES), lambda b, n: (0, n, 0)),
                  pl.BlockSpec((3, WINDOW, LANES), lambda b, n: (0, jnp.maximum(n - 1, 0), 0)),
                  pl.BlockSpec((1, LANES), lambda b, n: (0, 0)),
                  pl.BlockSpec((1, LANES), lambda b, n: (0, 0))],
        out_specs=pl.BlockSpec((WINDOW, Q_COLS), cur),
        out_shape=jax.ShapeDtypeStruct((t, Q_COLS), F32),
        compiler_params=pltpu.CompilerParams(dimension_semantics=("arbitrary", "arbitrary"),
                                             vmem_limit_bytes=VMEM_LIMIT),
        name="swa_attention",
    )(sinks, qkv, qkv, qkv, qkv, qkv, rope, rope,
      jnp.tile(q_gain, 2).reshape(1, LANES), jnp.tile(k_gain, 2).reshape(1, LANES))


def _conv_kernel(x_ref, g_ref, win_ref, cw_ref, wout_ref, o_ref, zprev_ref):
    n = pl.program_id(1)
    d = D_MODEL

    @pl.when(n == 0)
    def _():
        zprev_ref[...] = jnp.zeros_like(zprev_ref)

    x = x_ref[...]
    h = _rms(x, g_ref[...])
    bcu = jnp.dot(h.astype(BF16), win_ref[...], preferred_element_type=F32)
    gate_b = bcu[:, :d]
    z = bcu[:, d:2 * d] * bcu[:, 2 * d:]
    tm = z.shape[0]
    row = lax.broadcasted_iota(jnp.int32, z.shape, 0)
    prev = zprev_ref[...]
    p_last = prev[SUBLANES - 1:SUBLANES, :]
    p_last2 = prev[SUBLANES - 2:SUBLANES - 1, :]
    z1 = jnp.where(row == 0, p_last, pltpu.roll(z, 1, 0))
    z2 = jnp.where(row == 0, p_last2, jnp.where(row == 1, p_last, pltpu.roll(z, 2, 0)))
    cw = cw_ref[...]
    conv = cw[0:1, :] * z2 + cw[1:2, :] * z1 + cw[2:3, :] * z
    zprev_ref[...] = z[tm - SUBLANES:, :]
    o_ref[...] = x + jnp.dot((gate_b * conv).astype(BF16), wout_ref[...], preferred_element_type=F32)


def _conv_mixer(x, gain, w_in, conv_w, w_out, batch, seq, tm=256):
    t, d = x.shape
    nblk = seq // tm
    blk = lambda b, n: (b * nblk + n, 0)
    const = lambda b, n: (0, 0)
    return pl.pallas_call(
        _conv_kernel,
        grid=(batch, nblk),
        in_specs=[pl.BlockSpec((tm, d), blk),
                  pl.BlockSpec((1, d), const),
                  pl.BlockSpec((d, 3 * d), const),
                  pl.BlockSpec((CONV_WIDTH, d), const),
                  pl.BlockSpec((d, d), const)],
        out_specs=pl.BlockSpec((tm, d), blk),
        out_shape=jax.ShapeDtypeStruct((t, d), F32),
        scratch_shapes=[pltpu.VMEM((SUBLANES, d), F32)],
        compiler_params=pltpu.CompilerParams(dimension_semantics=("arbitrary", "arbitrary"),
                                             vmem_limit_bytes=VMEM_LIMIT),
        name="conv_mixer",
    )(x, gain.reshape(1, d), w_in.astype(BF16), conv_w, w_out.astype(BF16))


def _topk_axis0(s, k, ids=None, payload=None):
    n, tm = s.shape
    if ids is None:
        ids = lax.broadcasted_iota(jnp.int32, (n, tm), 0)
    krow = lax.broadcasted_iota(jnp.int32, (k, tm), 0)
    vals = jnp.zeros((k, tm), F32)
    picks = jnp.zeros((k, tm), jnp.int32)
    for r in range(k):
        m = jnp.max(s, axis=0, keepdims=True)
        pos = jnp.min(jnp.where(s == m, ids, jnp.iinfo(jnp.int32).max), axis=0, keepdims=True)
        sel = ids == pos
        if payload is None:
            picked = pos
        else:
            picked = jnp.sum(jnp.where(sel, payload, 0), axis=0, keepdims=True)
        vals = jnp.where(krow == r, m, vals)
        picks = jnp.where(krow == r, picked, picks)
        s = jnp.where(sel, -jnp.inf, s)
    return vals, picks


def _pair_candidates(s1, i1, s2, i2):
    k, tm = s1.shape
    sub = lax.broadcasted_iota(jnp.int32, (SUBLANES, tm), 0)
    scores, flat, expert = [], [], []
    for i in range(k // 2):
        width = k if i == 0 else SUBLANES
        sc = s1[i:i + 1, :] + s2[0:width, :]
        ex = i1[i:i + 1, :] * N_KEYS + i2[0:width, :]
        fl = i * k + lax.broadcasted_iota(jnp.int32, (width, tm), 0)
        reach = k // (i + 1)
        if reach < width:
            sc = jnp.where(sub < reach, sc, -jnp.inf)
        scores.append(sc)
        flat.append(fl)
        expert.append(ex)
    scores.append(s1[k // 2:, :] + s2[0:1, :])
    expert.append(i1[k // 2:, :] * N_KEYS + i2[0:1, :])
    flat.append((k // 2 + sub) * k)
    return jnp.concatenate(scores, axis=0), jnp.concatenate(flat, axis=0), jnp.concatenate(expert, axis=0)


def _route_kernel(x_ref, g_ref, wq_ref, keys_ref, after_ref, h_ref, idx_ref, gate_ref):
    del after_ref
    h = _rms(x_ref[...], g_ref[...])
    h_ref[...] = h
    q = jnp.dot(h.astype(BF16), wq_ref[...], preferred_element_type=F32).astype(BF16)
    idx_rows, gate_rows = [], []
    for head in range(PEER_HEADS):
        tops = []
        for part in range(2):
            col = (head * 2 + part) * QUERY_HALF
            s = lax.dot_general(keys_ref[head, part], q[:, col:col + QUERY_HALF],
                                (((1,), (1,)), ((), ())), preferred_element_type=F32)
            tops.append(_topk_axis0(s, PEER_TOPK))
        (s1, i1), (s2, i2) = tops
        cand, flat_ids, cand_idx = _pair_candidates(s1, i1, s2, i2)
        g_s, e_idx = _topk_axis0(cand, PEER_TOPK, ids=flat_ids, payload=cand_idx)
        e = jnp.exp(g_s - jnp.max(g_s, axis=0, keepdims=True))
        gate_rows.append(e / jnp.sum(e, axis=0, keepdims=True))
        idx_rows.append(e_idx * ROW_WORDS)
    idx_ref[...] = jnp.concatenate(idx_rows, axis=0).T
    gate_ref[...] = jnp.concatenate(gate_rows, axis=0).T


def _route(x, gain, w_query, sub_keys, first_tok, n_tok, after, tm=128):
    t, d = x.shape
    nq = w_query.shape[1]
    first_blk = first_tok // tm
    return pl.pallas_call(
        _route_kernel,
        grid=(n_tok // tm,),
        in_specs=[pl.BlockSpec((tm, d), lambda i: (i + first_blk, 0)),
                  pl.BlockSpec((1, d), lambda i: (0, 0)),
                  pl.BlockSpec((d, nq), lambda i: (0, 0)),
                  pl.BlockSpec((PEER_HEADS, 2, N_KEYS, QUERY_HALF), lambda i: (0, 0, 0, 0)),
                  pl.BlockSpec(after.shape, lambda i: (0, 0))],
        out_specs=[pl.BlockSpec((tm, d), lambda i: (i, 0)),
                   pl.BlockSpec((tm, SLOTS), lambda i: (i, 0)),
                   pl.BlockSpec((tm, SLOTS), lambda i: (i, 0))],
        out_shape=[jax.ShapeDtypeStruct((n_tok, d), F32),
                   jax.ShapeDtypeStruct((n_tok, SLOTS), jnp.int32),
                   jax.ShapeDtypeStruct((n_tok, SLOTS), F32)],
        compiler_params=pltpu.CompilerParams(dimension_semantics=("arbitrary",), vmem_limit_bytes=VMEM_LIMIT),
        name="peer_route",
    )(x, gain.reshape(1, d), w_query, sub_keys, after)


def _pack_table(tab):
    bits = lax.bitcast_convert_type(tab.astype(BF16), jnp.uint16).astype(jnp.uint32)
    half = D_MODEL // 2
    words = (bits[:, :half] << 16) | bits[:, half:]
    return lax.bitcast_convert_type(words, jnp.int32).reshape(tab.shape[0] * ROW_WORDS, LANES)


def _load_table_once(tab_hbm, tab, sem):
    @pl.when(pl.program_id(0) == 0)
    def _():
        cp = pltpu.make_async_copy(tab_hbm, tab, sem)
        cp.start()
        cp.wait()


def _gather_pair(tab, off_a, off_b):
    ra = tab[pl.ds(pl.multiple_of(off_a, ROW_WORDS), ROW_WORDS), :]
    rb = tab[pl.ds(pl.multiple_of(off_b, ROW_WORDS), ROW_WORDS), :]
    words = jnp.concatenate([ra, rb], axis=0)
    hi = pltpu.bitcast(words & jnp.int32(-65536), F32)
    lo = pltpu.bitcast(words << 16, F32)
    return hi, lo


TOKEN_UNROLL = 8


def _expert_in_kernel(*refs):
    idx_refs = refs[:TOKEN_UNROLL]
    h_ref, gate_ref, tab_hbm, w_ref, tab, sem, a_ref = refs[TOKEN_UNROLL:]
    _load_table_once(tab_hbm, tab, sem)
    sub = lax.broadcasted_iota(jnp.int32, (SUBLANES, LANES), 0)
    lane = lax.broadcasted_iota(jnp.int32, (SUBLANES, LANES), 1)
    own_half = (sub >= ROW_WORDS) == ((lane & 1) == 1)
    tb = h_ref.shape[0]

    def step(i, carry):
        xs = []
        for u in range(TOKEN_UNROLL):
            x = h_ref[i * TOKEN_UNROLL + u]
            xs.append((jnp.concatenate([x[0:ROW_WORDS], x[0:ROW_WORDS]], axis=0),
                       jnp.concatenate([x[ROW_WORDS:], x[ROW_WORDS:]], axis=0)))
        accs = [jnp.zeros((SUBLANES, LANES), F32) for _ in range(TOKEN_UNROLL)]
        for p in range(SLOTS // 2):
            for u in range(TOKEN_UNROLL):
                hi, lo = _gather_pair(tab, idx_refs[u][i, 2 * p], idx_refs[u][i, 2 * p + 1])
                part = jnp.sum(hi * xs[u][0] + lo * xs[u][1], axis=1, keepdims=True)
                accs[u] = jnp.where((lane >> 1) == p, part, accs[u])
        for u in range(TOKEN_UNROLL):
            a_ref[pl.ds(i * TOKEN_UNROLL + u, 1), :] = jnp.sum(jnp.where(own_half, accs[u], 0.0),
                                                                axis=0, keepdims=True)
        return carry

    lax.fori_loop(0, tb // TOKEN_UNROLL, step, 0)
    a = a_ref[...]
    w_ref[...] = gate_ref[...] * (0.5 * a * (1.0 + lax.erf(a * (1.0 / math.sqrt(2.0)))))


def _expert_out_kernel(*refs):
    idx_refs = refs[:TOKEN_UNROLL]
    w_ref, x_ref, tab_hbm, o_ref, tab, sem, wb_ref = refs[TOKEN_UNROLL:]
    _load_table_once(tab_hbm, tab, sem)
    sub = lax.broadcasted_iota(jnp.int32, (SUBLANES, LANES), 0)
    lower = sub < ROW_WORDS
    tb = x_ref.shape[0]

    def step(i, carry):
        for u in range(TOKEN_UNROLL):
            row = w_ref[pl.ds(i * TOKEN_UNROLL + u, 1), :]
            wb_ref[u] = jnp.broadcast_to(row, (SLOTS, LANES)).T
        acc_h = [jnp.zeros((SUBLANES, LANES), F32) for _ in range(TOKEN_UNROLL)]
        acc_l = [jnp.zeros((SUBLANES, LANES), F32) for _ in range(TOKEN_UNROLL)]
        for p in range(SLOTS // 2):
            ka, kb = 2 * p, 2 * p + 1
            for u in range(TOKEN_UNROLL):
                hi, lo = _gather_pair(tab, idx_refs[u][i, ka], idx_refs[u][i, kb])
                wa = jnp.broadcast_to(wb_ref[u, ka:ka + 1, :], (SUBLANES, LANES))
                wb = jnp.broadcast_to(wb_ref[u, kb:kb + 1, :], (SUBLANES, LANES))
                wt = jnp.where(lower, wa, wb)
                acc_h[u] = acc_h[u] + hi * wt
                acc_l[u] = acc_l[u] + lo * wt
        for u in range(TOKEN_UNROLL):
            t = i * TOKEN_UNROLL + u
            ah = acc_h[u] + pltpu.roll(acc_h[u], ROW_WORDS, 0)
            al = acc_l[u] + pltpu.roll(acc_l[u], ROW_WORDS, 0)
            o_ref[t] = x_ref[t] + jnp.where(lower, ah, al)
        return carry

    lax.fori_loop(0, tb // TOKEN_UNROLL, step, 0)


def _expert_specs(tb):
    un = TOKEN_UNROLL
    smem_blks = [pl.BlockSpec((None, tb // un, SLOTS), functools.partial(lambda u, i: (u, i, 0), u),
                              memory_space=pltpu.SMEM) for u in range(un)]
    vmem_blk = pl.BlockSpec((tb, SLOTS), lambda i: (i, 0))
    tok_blk = pl.BlockSpec((tb, D_MODEL // LANES, LANES), lambda i: (i, 0, 0))
    params = pltpu.CompilerParams(dimension_semantics=("arbitrary",), vmem_limit_bytes=VMEM_LIMIT)
    return smem_blks, vmem_blk, tok_blk, params


def _split_offsets(idx):
    n = idx.shape[0]
    return idx.reshape(n // TOKEN_UNROLL, TOKEN_UNROLL, SLOTS).transpose(1, 0, 2)


def _expert_in(h, idx_split, gate, u_packed, tb=128):
    n, d = h.shape
    smem_blks, vmem_blk, tok_blk, params = _expert_specs(tb)
    return pl.pallas_call(
        _expert_in_kernel,
        grid=(n // tb,),
        in_specs=smem_blks + [tok_blk, vmem_blk, pl.BlockSpec(memory_space=pl.ANY)],
        out_specs=vmem_blk,
        out_shape=jax.ShapeDtypeStruct((n, SLOTS), F32),
        scratch_shapes=[pltpu.VMEM(u_packed.shape, jnp.int32), pltpu.SemaphoreType.DMA, pltpu.VMEM((tb, SLOTS), F32)],
        compiler_params=params,
        name="peer_expert_in",
    )(*([idx_split] * TOKEN_UNROLL), h.reshape(n, d // LANES, LANES), gate, u_packed)


def _expert_out(x, idx_split, w, v_packed, tb=128):
    t, d = x.shape
    n = w.shape[0]
    smem_blks, vmem_blk, tok_blk, params = _expert_specs(tb)
    out = pl.pallas_call(
        _expert_out_kernel,
        grid=(n // tb,),
        in_specs=smem_blks + [vmem_blk, tok_blk, pl.BlockSpec(memory_space=pl.ANY)],
        out_specs=tok_blk,
        out_shape=jax.ShapeDtypeStruct((n, d // LANES, LANES), F32),
        scratch_shapes=[pltpu.VMEM(v_packed.shape, jnp.int32), pltpu.SemaphoreType.DMA,
                        pltpu.VMEM((TOKEN_UNROLL, SLOTS, LANES), F32)],
        compiler_params=params,
        name="peer_expert_out",
    )(*([idx_split] * TOKEN_UNROLL), w, x.reshape(t, d // LANES, LANES), v_packed)
    return out.reshape(n, d)


SC_LANES = 16
SC_WORKERS = 32
SC_CHUNK = 32
SC_GROUP = 8
SC_TOKENS = 17408


def _sc_params():
    cp = pltpu.CompilerParams()
    if "needs_layout_passes" in pltpu.CompilerParams.__dataclass_fields__:
        cp = dataclasses.replace(cp, needs_layout_passes=False)
    return cp


def _sc_expert_out(table_words, idx, w, x, first_tok):
    d = x.shape[1]
    n_tok = w.shape[0]
    per = n_tok // SC_WORKERS
    words = d // 2
    nq = words // SC_LANES // 2
    nchunk = SLOTS // SC_CHUNK
    group_chunks = SC_GROUP * nchunk
    mesh = plsc.VectorSubcoreMesh(core_axis_name="c", subcore_axis_name="s")

    @functools.partial(
        pl.kernel, mesh=mesh,
        out_type=jax.ShapeDtypeStruct((n_tok, d), F32),
        scratch_types=[pltpu.VMEM((SC_GROUP, SLOTS), jnp.int32), pltpu.VMEM((SC_GROUP, SLOTS), F32),
                       pltpu.VMEM((2, SC_CHUNK, words), jnp.int32), pltpu.VMEM((SC_GROUP, d), F32),
                       pltpu.SemaphoreType.DMA, pltpu.SemaphoreType.DMA],
        compiler_params=_sc_params(),
        name="peer_expert_out_sc",
    )
    def body(tab_hbm, idx_hbm, w_hbm, x_hbm, o_hbm, idx_v, w_v, rows_v, y_v, sem0, sem1):
        base = (lax.axis_index("s") * 2 + lax.axis_index("c")) * per
        zero = jnp.zeros((SC_LANES,), jnp.int32)
        himask = jnp.full((SC_LANES,), -65536, jnp.int32)
        sems = (sem0, sem1)

        def gather(k, b):
            off = pl.multiple_of((k % nchunk) * SC_CHUNK, SC_CHUNK)
            return pltpu.make_async_copy(tab_hbm.at[idx_v.at[k // nchunk, pl.ds(off, SC_CHUNK)]],
                                         rows_v.at[b], sems[b])

        def accumulate(k, b):
            tok = k // nchunk
            for q in range(2):
                first = q * nq
                acc0 = (tuple(y_v[tok, pl.ds((first + j) * SC_LANES, SC_LANES)] for j in range(nq))
                        + tuple(y_v[tok, pl.ds(words + (first + j) * SC_LANES, SC_LANES)] for j in range(nq)))

                def row_body(r, accs):
                    ws = plsc.load_gather(w_v, [zero + tok, zero + ((k % nchunk) * SC_CHUNK + r)])
                    hi_acc, lo_acc = [], []
                    for j in range(nq):
                        wv = rows_v[b, r, pl.ds((first + j) * SC_LANES, SC_LANES)]
                        hi_acc.append(accs[j] + plsc.bitcast(wv & himask, F32) * ws)
                        lo_acc.append(accs[nq + j] + plsc.bitcast(wv << 16, F32) * ws)
                    return tuple(hi_acc) + tuple(lo_acc)

                accs = lax.fori_loop(0, SC_CHUNK, row_body, acc0)
                for j in range(nq):
                    y_v[tok, pl.ds((first + j) * SC_LANES, SC_LANES)] = accs[j]
                    y_v[tok, pl.ds(words + (first + j) * SC_LANES, SC_LANES)] = accs[nq + j]

        @pl.loop(0, per // SC_GROUP)
        def _(g):
            t0 = pl.multiple_of(base + g * SC_GROUP, SC_GROUP)
            pltpu.sync_copy(idx_hbm.at[pl.ds(t0, SC_GROUP)], idx_v)
            pltpu.sync_copy(w_hbm.at[pl.ds(t0, SC_GROUP)], w_v)
            pltpu.sync_copy(x_hbm.at[pl.ds(first_tok + t0, SC_GROUP)], y_v)
            gather(0, 0).start()

            @pl.loop(0, group_chunks // 2)
            def _(kk):
                k0 = 2 * kk
                gather(k0 + 1, 1).start()
                gather(k0, 0).wait()
                accumulate(k0, 0)

                @pl.when(k0 + 2 < group_chunks)
                def _():
                    gather(k0 + 2, 0).start()
                gather(k0 + 1, 1).wait()
                accumulate(k0 + 1, 1)

            pltpu.sync_copy(y_v, o_hbm.at[pl.ds(t0, SC_GROUP)])

    return body(table_words, idx, w, x)


def _peer(x, gain, w_query, sub_keys, expert_u, expert_v):
    t = x.shape[0]
    t_tc = t - SC_TOKENS
    wq, keys = w_query.astype(BF16), sub_keys.astype(BF16)
    u_packed, v_packed = _pack_table(expert_u), _pack_table(expert_v)
    v_rows = v_packed.reshape(expert_v.shape[0], ROW_WORDS * LANES)

    h, idx, gate = _route(x, gain, wq, keys, t_tc, SC_TOKENS, v_rows[:SUBLANES])
    w_sc = _expert_in(h, _split_offsets(idx), gate, u_packed)
    out_sc = _sc_expert_out(v_rows, idx // ROW_WORDS, w_sc, x, t_tc)

    h, idx, gate = _route(x, gain, wq, keys, 0, t_tc, w_sc[:SUBLANES])
    idx_split = _split_offsets(idx)
    w = _expert_in(h, idx_split, gate, u_packed)
    out_tc = _expert_out(x, idx_split, w, v_packed)
    return jnp.concatenate([out_tc, out_sc], axis=0)


def kernel(x, norm_mix, norm_ffn, attn_w_qkv, attn_q_norm, attn_k_norm, attn_sinks, attn_w_o, conv_w_in, conv_w, conv_w_out, peer_w_query, peer_sub_keys, peer_u, peer_v):
    batch, seq, d = x.shape
    xt = x.reshape(batch * seq, d)
    for i in range(norm_mix.shape[0]):
        j = i // 2
        if i % 2 == 0:
            qkv = _norm_matmul(xt, norm_mix[i], attn_w_qkv[j].astype(BF16))
            o = _attention(qkv, attn_q_norm[j], attn_k_norm[j], attn_sinks[j], batch, seq)
            xt = _matmul_residual(o, attn_w_o[j].astype(BF16), xt)
        else:
            xt = _conv_mixer(xt, norm_mix[i], conv_w_in[j], conv_w[j], conv_w_out[j], batch, seq)
        xt = _peer(xt, norm_ffn[i], peer_w_query[i], peer_sub_keys[i], peer_u[i], peer_v[i])
    return xt.reshape(batch, seq, d)
```

```python
import dataclasses
import functools
import math

import jax
import jax.numpy as jnp
from jax import lax
from jax.experimental import pallas as pl
from jax.experimental.pallas import tpu as pltpu
from jax.experimental.pallas import tpu_sc as plsc

D_MODEL = 1024
RMS_EPS = 1e-6

HEAD_DIM = 64
N_Q_HEADS = 16
N_KV_HEADS = 4
GROUP = N_Q_HEADS // N_KV_HEADS
WINDOW = 128
ROT_DIM = HEAD_DIM // 4
ROPE_THETA = 500000.0
Q_COLS = N_Q_HEADS * HEAD_DIM
KV_COLS = N_KV_HEADS * HEAD_DIM
NEG_INF = -1e30

CONV_WIDTH = 3

PEER_HEADS = 8
N_KEYS = 128
N_EXPERTS = N_KEYS * N_KEYS
PEER_TOPK = 16
QUERY_HALF = 128
SLOTS = PEER_HEADS * PEER_TOPK

LANES = 128
SUBLANES = 8
ROW_WORDS = D_MODEL // 2 // LANES
VMEM_LIMIT = 48 * 1024 * 1024

BF16 = jnp.bfloat16
F32 = jnp.float32


def _rms(x, gain):
    return x * lax.rsqrt(jnp.mean(x * x, axis=-1, keepdims=True) + RMS_EPS) * gain


def _norm_matmul_kernel(x_ref, g_ref, w_ref, o_ref):
    h = _rms(x_ref[...], g_ref[...])
    o_ref[...] = jnp.dot(h.astype(BF16), w_ref[...], preferred_element_type=F32)


def _norm_matmul(x, gain, w, tm=512):
    t, d = x.shape
    n = w.shape[1]
    return pl.pallas_call(
        _norm_matmul_kernel,
        grid=(t // tm,),
        in_specs=[pl.BlockSpec((tm, d), lambda i: (i, 0)),
                  pl.BlockSpec((1, d), lambda i: (0, 0)),
                  pl.BlockSpec((d, n), lambda i: (0, 0))],
        out_specs=pl.BlockSpec((tm, n), lambda i: (i, 0)),
        out_shape=jax.ShapeDtypeStruct((t, n), F32),
        compiler_params=pltpu.CompilerParams(dimension_semantics=("arbitrary",), vmem_limit_bytes=VMEM_LIMIT),
        name="norm_matmul",
    )(x, gain.reshape(1, d), w)


def _matmul_residual_kernel(a_ref, w_ref, r_ref, o_ref):
    o_ref[...] = r_ref[...] + jnp.dot(a_ref[...].astype(BF16), w_ref[...], preferred_element_type=F32)


def _matmul_residual(a, w, res, tm=512):
    t, k = a.shape
    n = w.shape[1]
    return pl.pallas_call(
        _matmul_residual_kernel,
        grid=(t // tm,),
        in_specs=[pl.BlockSpec((tm, k), lambda i: (i, 0)),
                  pl.BlockSpec((k, n), lambda i: (0, 0)),
                  pl.BlockSpec((tm, n), lambda i: (i, 0))],
        out_specs=pl.BlockSpec((tm, n), lambda i: (i, 0)),
        out_shape=jax.ShapeDtypeStruct((t, n), F32),
        compiler_params=pltpu.CompilerParams(dimension_semantics=("arbitrary",), vmem_limit_bytes=VMEM_LIMIT),
        name="matmul_residual",
    )(a, w, res)


def _rope_tables(seq):
    half = ROT_DIM // 2
    freqs = ROPE_THETA ** (-jnp.arange(0, ROT_DIM, 2, dtype=F32) / ROT_DIM)
    ang = jnp.arange(seq, dtype=F32)[:, None] * freqs[None, :]
    cos, sin = jnp.cos(ang), jnp.sin(ang)
    ones = jnp.ones((seq, HEAD_DIM - ROT_DIM), F32)
    zeros = jnp.zeros((seq, HEAD_DIM - ROT_DIM), F32)
    zh = jnp.zeros((seq, half), F32)
    c = jnp.concatenate([cos, cos, ones], axis=1)
    s_next = jnp.concatenate([-sin, zh, zeros], axis=1)
    s_prev = jnp.concatenate([zh, sin, zeros], axis=1)
    return jnp.stack([jnp.tile(c, (1, 2)), jnp.tile(s_next, (1, 2)), jnp.tile(s_prev, (1, 2))])


def _head_norm_rope(x, gain2, rope, lo):
    sq = x * x
    s_lo = jnp.sum(jnp.where(lo, sq, 0.0), axis=1, keepdims=True)
    s_hi = jnp.sum(jnp.where(lo, 0.0, sq), axis=1, keepdims=True)
    ms = jnp.where(lo, s_lo, s_hi) * (1.0 / HEAD_DIM)
    xn = x * lax.rsqrt(ms + RMS_EPS) * gain2
    half = ROT_DIM // 2
    return xn * rope[0] + pltpu.roll(xn, LANES - half, 1) * rope[1] + pltpu.roll(xn, half, 1) * rope[2]


def _attn_kernel(sinks_ref, q_ref, kc_ref, kp_ref, vc_ref, vp_ref, rc_ref, rp_ref, qg_ref, kg_ref, o_ref):
    n = pl.program_id(1)
    lo = lax.broadcasted_iota(jnp.int32, (WINDOW, LANES), 1) < HEAD_DIM
    rope_c = rc_ref[...]
    rope_p = rp_ref[...]
    qg = qg_ref[...]
    kg = kg_ref[...]

    rows = GROUP * WINDOW
    qi = lax.broadcasted_iota(jnp.int32, (rows, 2 * WINDOW), 0) & (WINDOW - 1)
    ki = lax.broadcasted_iota(jnp.int32, (rows, 2 * WINDOW), 1)
    rel = WINDOW + qi - ki
    valid = (rel >= 0) & (rel < WINDOW) & ((n > 0) | (ki >= WINDOW))
    head_of_row = lax.broadcasted_iota(jnp.int32, (rows, 1), 0) // WINDOW
    scale = 1.0 / math.sqrt(HEAD_DIM)

    q2 = [_head_norm_rope(q_ref[:, c * LANES:(c + 1) * LANES], qg, rope_c, lo).astype(BF16)
          for c in range(Q_COLS // LANES)]
    for c in range(KV_COLS // LANES):
        cols = slice(c * LANES, (c + 1) * LANES)
        kprev = _head_norm_rope(kp_ref[:, cols], kg, rope_p, lo)
        kcur = _head_norm_rope(kc_ref[:, cols], kg, rope_c, lo)
        kfull = jnp.concatenate([kprev, kcur], axis=0).astype(BF16)
        vfull = jnp.concatenate([vp_ref[:, cols], vc_ref[:, cols]], axis=0).astype(BF16)
        for hh in range(LANES // HEAD_DIM):
            h = (LANES // HEAD_DIM) * c + hh
            kh = kfull[:, hh * HEAD_DIM:(hh + 1) * HEAD_DIM]
            vh = vfull[:, hh * HEAD_DIM:(hh + 1) * HEAD_DIM]
            heads = [GROUP * h + g for g in range(GROUP)]
            q4 = jnp.concatenate([q2[j // 2][:, (j % 2) * HEAD_DIM:(j % 2 + 1) * HEAD_DIM] for j in heads], axis=0)
            s = lax.dot_general(q4, kh, (((1,), (1,)), ((), ())), preferred_element_type=F32) * scale
            s = jnp.where(valid, s, NEG_INF)
            sink = jnp.zeros((rows, 1), F32)
            for g, j in enumerate(heads):
                sink = jnp.where(head_of_row == g, sinks_ref[j], sink)
            m = jnp.maximum(jnp.max(s, axis=1, keepdims=True), sink)
            p = jnp.exp(s - m)
            denom = jnp.sum(p, axis=1, keepdims=True) + jnp.exp(sink - m)
            o = jnp.dot(p.astype(BF16), vh, preferred_element_type=F32) / denom
            for g, j in enumerate(heads):
                o_ref[:, j * HEAD_DIM:(j + 1) * HEAD_DIM] = o[g * WINDOW:(g + 1) * WINDOW]


def _attention(qkv, q_gain, k_gain, sinks, batch, seq):
    t = batch * seq
    nb = seq // WINDOW
    rope = _rope_tables(seq)
    kcol = Q_COLS // KV_COLS
    cur = lambda b, n: (b * nb + n, 0)
    kcur = lambda b, n: (b * nb + n, kcol)
    kprev = lambda b, n: (b * nb + jnp.maximum(n - 1, 0), kcol)
    vcur = lambda b, n: (b * nb + n, kcol + 1)
    vprev = lambda b, n: (b * nb + jnp.maximum(n - 1, 0), kcol + 1)
    return pl.pallas_call(
        _attn_kernel,
        grid=(batch, nb),
        in_specs=[pl.BlockSpec(memory_space=pltpu.SMEM),
                  pl.BlockSpec((WINDOW, Q_COLS), cur),
                  pl.BlockSpec((WINDOW, KV_COLS), kcur),
                  pl.BlockSpec((WINDOW, KV_COLS), kprev),
                  pl.BlockSpec((WINDOW, KV_COLS), vcur),
                  pl.BlockSpec((WINDOW, KV_COLS), vprev),
                  pl.BlockSpec((3, WINDOW, LANES), lambda b, n: (0, n, 0)),
                  pl.BlockSpec((3, WINDOW, LANES), lambda b, n: (0, jnp.maximum(n - 1, 0), 0)),
                  pl.BlockSpec((1, LANES), lambda b, n: (0, 0)),
                  pl.BlockSpec((1, LANES), lambda b, n: (0, 0))],
        out_specs=pl.BlockSpec((WINDOW, Q_COLS), cur),
        out_shape=jax.ShapeDtypeStruct((t, Q_COLS), F32),
        compiler_params=pltpu.CompilerParams(dimension_semantics=("arbitrary", "arbitrary"),
                                             vmem_limit_bytes=VMEM_LIMIT),
        name="swa_attention",
    )(sinks, qkv, qkv, qkv, qkv, qkv, rope, rope,
      jnp.tile(q_gain, 2).reshape(1, LANES), jnp.tile(k_gain, 2).reshape(1, LANES))


def _conv_kernel(x_ref, g_ref, win_ref, cw_ref, wout_ref, o_ref, zprev_ref):
    n = pl.program_id(1)
    d = D_MODEL

    @pl.when(n == 0)
    def _():
        zprev_ref[...] = jnp.zeros_like(zprev_ref)

    x = x_ref[...]
    h = _rms(x, g_ref[...])
    bcu = jnp.dot(h.astype(BF16), win_ref[...], preferred_element_type=F32)
    gate_b = bcu[:, :d]
    z = bcu[:, d:2 * d] * bcu[:, 2 * d:]
    tm = z.shape[0]
    row = lax.broadcasted_iota(jnp.int32, z.shape, 0)
    prev = zprev_ref[...]
    p_last = prev[SUBLANES - 1:SUBLANES, :]
    p_last2 = prev[SUBLANES - 2:SUBLANES - 1, :]
    z1 = jnp.where(row == 0, p_last, pltpu.roll(z, 1, 0))
    z2 = jnp.where(row == 0, p_last2, jnp.where(row == 1, p_last, pltpu.roll(z, 2, 0)))
    cw = cw_ref[...]
    conv = cw[0:1, :] * z2 + cw[1:2, :] * z1 + cw[2:3, :] * z
    zprev_ref[...] = z[tm - SUBLANES:, :]
    o_ref[...] = x + jnp.dot((gate_b * conv).astype(BF16), wout_ref[...], preferred_element_type=F32)


def _conv_mixer(x, gain, w_in, conv_w, w_out, batch, seq, tm=256):
    t, d = x.shape
    nblk = seq // tm
    blk = lambda b, n: (b * nblk + n, 0)
    const = lambda b, n: (0, 0)
    return pl.pallas_call(
        _conv_kernel,
        grid=(batch, nblk),
        in_specs=[pl.BlockSpec((tm, d), blk),
                  pl.BlockSpec((1, d), const),
                  pl.BlockSpec((d, 3 * d), const),
                  pl.BlockSpec((CONV_WIDTH, d), const),
                  pl.BlockSpec((d, d), const)],
        out_specs=pl.BlockSpec((tm, d), blk),
        out_shape=jax.ShapeDtypeStruct((t, d), F32),
        scratch_shapes=[pltpu.VMEM((SUBLANES, d), F32)],
        compiler_params=pltpu.CompilerParams(dimension_semantics=("arbitrary", "arbitrary"),
                                             vmem_limit_bytes=VMEM_LIMIT),
        name="conv_mixer",
    )(x, gain.reshape(1, d), w_in.astype(BF16), conv_w, w_out.astype(BF16))


def _topk_axis0(s, k, ids=None, payload=None):
    n, tm = s.shape
    if ids is None:
        ids = lax.broadcasted_iota(jnp.int32, (n, tm), 0)
    krow = lax.broadcasted_iota(jnp.int32, (k, tm), 0)
    vals = jnp.zeros((k, tm), F32)
    picks = jnp.zeros((k, tm), jnp.int32)
    for r in range(k):
        m = jnp.max(s, axis=0, keepdims=True)
        pos = jnp.min(jnp.where(s == m, ids, jnp.iinfo(jnp.int32).max), axis=0, keepdims=True)
        sel = ids == pos
        if payload is None:
            picked = pos
        else:
            picked = jnp.sum(jnp.where(sel, payload, 0), axis=0, keepdims=True)
        vals = jnp.where(krow == r, m, vals)
        picks = jnp.where(krow == r, picked, picks)
        s = jnp.where(sel, -jnp.inf, s)
    return vals, picks


def _pair_candidates(s1, i1, s2, i2):
    k, tm = s1.shape
    sub = lax.broadcasted_iota(jnp.int32, (SUBLANES, tm), 0)
    scores, flat, expert = [], [], []
    for i in range(k // 2):
        width = k if i == 0 else SUBLANES
        sc = s1[i:i + 1, :] + s2[0:width, :]
        ex = i1[i:i + 1, :] * N_KEYS + i2[0:width, :]
        fl = i * k + lax.broadcasted_iota(jnp.int32, (width, tm), 0)
        reach = k // (i + 1)
        if reach < width:
            sc = jnp.where(sub < reach, sc, -jnp.inf)
        scores.append(sc)
        flat.append(fl)
        expert.append(ex)
    scores.append(s1[k // 2:, :] + s2[0:1, :])
    expert.append(i1[k // 2:, :] * N_KEYS + i2[0:1, :])
    flat.append((k // 2 + sub) * k)
    return jnp.concatenate(scores, axis=0), jnp.concatenate(flat, axis=0), jnp.concatenate(expert, axis=0)


def _route_kernel(x_ref, g_ref, wq_ref, keys_ref, after_ref, h_ref, idx_ref, gate_ref):
    del after_ref
    h = _rms(x_ref[...], g_ref[...])
    h_ref[...] = h
    q = jnp.dot(h.astype(BF16), wq_ref[...], preferred_element_type=F32).astype(BF16)
    idx_rows, gate_rows = [], []
    for head in range(PEER_HEADS):
        tops = []
        for part in range(2):
            col = (head * 2 + part) * QUERY_HALF
            s = lax.dot_general(keys_ref[head, part], q[:, col:col + QUERY_HALF],
                                (((1,), (1,)), ((), ())), preferred_element_type=F32)
            tops.append(_topk_axis0(s, PEER_TOPK))
        (s1, i1), (s2, i2) = tops
        cand, flat_ids, cand_idx = _pair_candidates(s1, i1, s2, i2)
        g_s, e_idx = _topk_axis0(cand, PEER_TOPK, ids=flat_ids, payload=cand_idx)
        e = jnp.exp(g_s - jnp.max(g_s, axis=0, keepdims=True))
        gate_rows.append(e / jnp.sum(e, axis=0, keepdims=True))
        idx_rows.append(e_idx * ROW_WORDS)
    idx_ref[...] = jnp.concatenate(idx_rows, axis=0).T
    gate_ref[...] = jnp.concatenate(gate_rows, axis=0).T


def _route(x, gain, w_query, sub_keys, first_tok, n_tok, after, tm=128):
    t, d = x.shape
    nq = w_query.shape[1]
    first_blk = first_tok // tm
    return pl.pallas_call(
        _route_kernel,
        grid=(n_tok // tm,),
        in_specs=[pl.BlockSpec((tm, d), lambda i: (i + first_blk, 0)),
                  pl.BlockSpec((1, d), lambda i: (0, 0)),
                  pl.BlockSpec((d, nq), lambda i: (0, 0)),
                  pl.BlockSpec((PEER_HEADS, 2, N_KEYS, QUERY_HALF), lambda i: (0, 0, 0, 0)),
                  pl.BlockSpec(after.shape, lambda i: (0, 0))],
        out_specs=[pl.BlockSpec((tm, d), lambda i: (i, 0)),
                   pl.BlockSpec((tm, SLOTS), lambda i: (i, 0)),
                   pl.BlockSpec((tm, SLOTS), lambda i: (i, 0))],
        out_shape=[jax.ShapeDtypeStruct((n_tok, d), F32),
                   jax.ShapeDtypeStruct((n_tok, SLOTS), jnp.int32),
                   jax.ShapeDtypeStruct((n_tok, SLOTS), F32)],
        compiler_params=pltpu.CompilerParams(dimension_semantics=("arbitrary",), vmem_limit_bytes=VMEM_LIMIT),
        name="peer_route",
    )(x, gain.reshape(1, d), w_query, sub_keys, after)


def _pack_kernel(t_ref, tiles_ref, rows_ref):
    x = t_ref[...]
    half = x.shape[1] // 2
    hi = pltpu.bitcast(x[:, :half].astype(BF16).astype(F32), jnp.int32)
    lo = pltpu.bitcast(x[:, half:].astype(BF16).astype(F32), jnp.int32)
    words = hi | lax.shift_right_logical(lo, 16)
    rows_ref[...] = words
    for s in range(ROW_WORDS):
        tiles_ref[pl.ds(s, x.shape[0], stride=ROW_WORDS), :] = words[:, s * LANES:(s + 1) * LANES]


def _pack_table(tab, rows_per_step=256):
    ne, d = tab.shape
    return pl.pallas_call(
        _pack_kernel,
        grid=(ne // rows_per_step,),
        in_specs=[pl.BlockSpec((rows_per_step, d), lambda i: (i, 0))],
        out_specs=[pl.BlockSpec((rows_per_step * ROW_WORDS, LANES), lambda i: (i, 0)),
                   pl.BlockSpec((rows_per_step, d // 2), lambda i: (i, 0))],
        out_shape=[jax.ShapeDtypeStruct((ne * ROW_WORDS, LANES), jnp.int32),
                   jax.ShapeDtypeStruct((ne, d // 2), jnp.int32)],
        compiler_params=pltpu.CompilerParams(dimension_semantics=("arbitrary",), vmem_limit_bytes=VMEM_LIMIT),
        name="pack_table",
    )(tab)


def _load_table_once(tab_hbm, tab, sem):
    @pl.when(pl.program_id(0) == 0)
    def _():
        cp = pltpu.make_async_copy(tab_hbm, tab, sem)
        cp.start()
        cp.wait()


def _gather_pair(tab, off_a, off_b):
    ra = tab[pl.ds(pl.multiple_of(off_a, ROW_WORDS), ROW_WORDS), :]
    rb = tab[pl.ds(pl.multiple_of(off_b, ROW_WORDS), ROW_WORDS), :]
    words = jnp.concatenate([ra, rb], axis=0)
    hi = pltpu.bitcast(words & jnp.int32(-65536), F32)
    lo = pltpu.bitcast(words << 16, F32)
    return hi, lo


TOKEN_UNROLL = 8


def _expert_in_kernel(*refs):
    idx_refs = refs[:TOKEN_UNROLL]
    h_ref, gate_ref, tab_hbm, w_ref, tab, sem, a_ref = refs[TOKEN_UNROLL:]
    _load_table_once(tab_hbm, tab, sem)
    sub = lax.broadcasted_iota(jnp.int32, (SUBLANES, LANES), 0)
    lane = lax.broadcasted_iota(jnp.int32, (SUBLANES, LANES), 1)
    own_half = (sub >= ROW_WORDS) == ((lane & 1) == 1)
    tb = h_ref.shape[0]

    def step(i, carry):
        xs = []
        for u in range(TOKEN_UNROLL):
            x = h_ref[i * TOKEN_UNROLL + u]
            xs.append((jnp.concatenate([x[0:ROW_WORDS], x[0:ROW_WORDS]], axis=0),
                       jnp.concatenate([x[ROW_WORDS:], x[ROW_WORDS:]], axis=0)))
        accs = [jnp.zeros((SUBLANES, LANES), F32) for _ in range(TOKEN_UNROLL)]
        for p in range(SLOTS // 2):
            for u in range(TOKEN_UNROLL):
                hi, lo = _gather_pair(tab, idx_refs[u][i, 2 * p], idx_refs[u][i, 2 * p + 1])
                part = jnp.sum(hi * xs[u][0] + lo * xs[u][1], axis=1, keepdims=True)
                accs[u] = jnp.where((lane >> 1) == p, part, accs[u])
        for u in range(TOKEN_UNROLL):
            a_ref[pl.ds(i * TOKEN_UNROLL + u, 1), :] = jnp.sum(jnp.where(own_half, accs[u], 0.0),
                                                                axis=0, keepdims=True)
        return carry

    lax.fori_loop(0, tb // TOKEN_UNROLL, step, 0)
    a = a_ref[...]
    w_ref[...] = gate_ref[...] * (0.5 * a * (1.0 + lax.erf(a * (1.0 / math.sqrt(2.0)))))


def _expert_out_kernel(*refs):
    idx_refs = refs[:TOKEN_UNROLL]
    w_ref, x_ref, tab_hbm, o_ref, tab, sem, wb_ref = refs[TOKEN_UNROLL:]
    _load_table_once(tab_hbm, tab, sem)
    sub = lax.broadcasted_iota(jnp.int32, (SUBLANES, LANES), 0)
    lower = sub < ROW_WORDS
    tb = x_ref.shape[0]

    def step(i, carry):
        for u in range(TOKEN_UNROLL):
            row = w_ref[pl.ds(i * TOKEN_UNROLL + u, 1), :]
            wb_ref[u] = jnp.broadcast_to(row, (SLOTS, LANES)).T
        acc_h = [jnp.zeros((SUBLANES, LANES), F32) for _ in range(TOKEN_UNROLL)]
        acc_l = [jnp.zeros((SUBLANES, LANES), F32) for _ in range(TOKEN_UNROLL)]
        for p in range(SLOTS // 2):
            ka, kb = 2 * p, 2 * p + 1
            for u in range(TOKEN_UNROLL):
                hi, lo = _gather_pair(tab, idx_refs[u][i, ka], idx_refs[u][i, kb])
                wa = jnp.broadcast_to(wb_ref[u, ka:ka + 1, :], (SUBLANES, LANES))
                wb = jnp.broadcast_to(wb_ref[u, kb:kb + 1, :], (SUBLANES, LANES))
                wt = jnp.where(lower, wa, wb)
                acc_h[u] = acc_h[u] + hi * wt
                acc_l[u] = acc_l[u] + lo * wt
        for u in range(TOKEN_UNROLL):
            t = i * TOKEN_UNROLL + u
            ah = acc_h[u] + pltpu.roll(acc_h[u], ROW_WORDS, 0)
            al = acc_l[u] + pltpu.roll(acc_l[u], ROW_WORDS, 0)
            o_ref[t] = x_ref[t] + jnp.where(lower, ah, al)
        return carry

    lax.fori_loop(0, tb // TOKEN_UNROLL, step, 0)


def _expert_specs(tb):
    un = TOKEN_UNROLL
    smem_blks = [pl.BlockSpec((None, tb // un, SLOTS), functools.partial(lambda u, i: (u, i, 0), u),
                              memory_space=pltpu.SMEM) for u in range(un)]
    vmem_blk = pl.BlockSpec((tb, SLOTS), lambda i: (i, 0))
    tok_blk = pl.BlockSpec((tb, D_MODEL // LANES, LANES), lambda i: (i, 0, 0))
    params = pltpu.CompilerParams(dimension_semantics=("arbitrary",), vmem_limit_bytes=VMEM_LIMIT)
    return smem_blks, vmem_blk, tok_blk, params


def _split_offsets(idx):
    n = idx.shape[0]
    return idx.reshape(n // TOKEN_UNROLL, TOKEN_UNROLL, SLOTS).transpose(1, 0, 2)


def _expert_in(h, idx_split, gate, u_packed, tb=128):
    n, d = h.shape
    smem_blks, vmem_blk, tok_blk, params = _expert_specs(tb)
    return pl.pallas_call(
        _expert_in_kernel,
        grid=(n // tb,),
        in_specs=smem_blks + [tok_blk, vmem_blk, pl.BlockSpec(memory_space=pl.ANY)],
        out_specs=vmem_blk,
        out_shape=jax.ShapeDtypeStruct((n, SLOTS), F32),
        scratch_shapes=[pltpu.VMEM(u_packed.shape, jnp.int32), pltpu.SemaphoreType.DMA, pltpu.VMEM((tb, SLOTS), F32)],
        compiler_params=params,
        name="peer_expert_in",
    )(*([idx_split] * TOKEN_UNROLL), h.reshape(n, d // LANES, LANES), gate, u_packed)


def _expert_out(x, idx_split, w, v_packed, tb=128):
    t, d = x.shape
    n = w.shape[0]
    smem_blks, vmem_blk, tok_blk, params = _expert_specs(tb)
    out = pl.pallas_call(
        _expert_out_kernel,
        grid=(n // tb,),
        in_specs=smem_blks + [vmem_blk, tok_blk, pl.BlockSpec(memory_space=pl.ANY)],
        out_specs=tok_blk,
        out_shape=jax.ShapeDtypeStruct((n, d // LANES, LANES), F32),
        scratch_shapes=[pltpu.VMEM(v_packed.shape, jnp.int32), pltpu.SemaphoreType.DMA,
                        pltpu.VMEM((TOKEN_UNROLL, SLOTS, LANES), F32)],
        compiler_params=params,
        name="peer_expert_out",
    )(*([idx_split] * TOKEN_UNROLL), w, x.reshape(t, d // LANES, LANES), v_packed)
    return out.reshape(n, d)


SC_LANES = 16
SC_WORKERS = 32
SC_CHUNK = 32
SC_GROUP = 8
SC_TOKENS = 17408


def _sc_params():
    cp = pltpu.CompilerParams()
    if "needs_layout_passes" in pltpu.CompilerParams.__dataclass_fields__:
        cp = dataclasses.replace(cp, needs_layout_passes=False)
    return cp


def _sc_expert_out(table_words, idx, w, x, first_tok):
    d = x.shape[1]
    n_tok = w.shape[0] // SLOTS
    per = n_tok // SC_WORKERS
    words = d // 2
    nq = words // SC_LANES // 2
    nchunk = SLOTS // SC_CHUNK
    group_chunks = SC_GROUP * nchunk
    mesh = plsc.VectorSubcoreMesh(core_axis_name="c", subcore_axis_name="s")

    @functools.partial(
        pl.kernel, mesh=mesh,
        out_type=jax.ShapeDtypeStruct((n_tok, d), F32),
        scratch_types=[pltpu.VMEM((SC_GROUP * SLOTS,), jnp.int32), pltpu.VMEM((SC_GROUP * SLOTS,), F32),
                       pltpu.VMEM((2, SC_CHUNK, words), jnp.int32), pltpu.VMEM((SC_GROUP, d), F32),
                       pltpu.SemaphoreType.DMA, pltpu.SemaphoreType.DMA],
        compiler_params=_sc_params(),
        name="peer_expert_out_sc",
    )
    def body(tab_hbm, idx_hbm, w_hbm, x_hbm, o_hbm, idx_v, w_v, rows_v, y_v, sem0, sem1):
        base = (lax.axis_index("s") * 2 + lax.axis_index("c")) * per
        zero = jnp.zeros((SC_LANES,), jnp.int32)
        himask = jnp.full((SC_LANES,), -65536, jnp.int32)
        sems = (sem0, sem1)

        def gather(k, b):
            off = pl.multiple_of(k * SC_CHUNK, SC_CHUNK)
            return pltpu.make_async_copy(tab_hbm.at[idx_v.at[pl.ds(off, SC_CHUNK)]], rows_v.at[b], sems[b])

        def accumulate(k, b):
            tok = k // nchunk
            for q in range(2):
                first = q * nq
                acc0 = (tuple(y_v[tok, pl.ds((first + j) * SC_LANES, SC_LANES)] for j in range(nq))
                        + tuple(y_v[tok, pl.ds(words + (first + j) * SC_LANES, SC_LANES)] for j in range(nq)))

                def row_body(r, accs):
                    ws = plsc.load_gather(w_v, [zero + (k * SC_CHUNK + r)])
                    hi_acc, lo_acc = [], []
                    for j in range(nq):
                        wv = rows_v[b, r, pl.ds((first + j) * SC_LANES, SC_LANES)]
                        hi_acc.append(accs[j] + plsc.bitcast(wv & himask, F32) * ws)
                        lo_acc.append(accs[nq + j] + plsc.bitcast(wv << 16, F32) * ws)
                    return tuple(hi_acc) + tuple(lo_acc)

                accs = lax.fori_loop(0, SC_CHUNK, row_body, acc0)
                for j in range(nq):
                    y_v[tok, pl.ds((first + j) * SC_LANES, SC_LANES)] = accs[j]
                    y_v[tok, pl.ds(words + (first + j) * SC_LANES, SC_LANES)] = accs[nq + j]

        @pl.loop(0, per // SC_GROUP)
        def _(g):
            t0 = pl.multiple_of(base + g * SC_GROUP, SC_GROUP)
            pltpu.sync_copy(idx_hbm.at[pl.ds(t0 * SLOTS, SC_GROUP * SLOTS)], idx_v)
            pltpu.sync_copy(w_hbm.at[pl.ds(t0 * SLOTS, SC_GROUP * SLOTS)], w_v)
            pltpu.sync_copy(x_hbm.at[pl.ds(first_tok + t0, SC_GROUP)], y_v)
            gather(0, 0).start()

            @pl.loop(0, group_chunks // 2)
            def _(kk):
                k0 = 2 * kk
                gather(k0 + 1, 1).start()
                gather(k0, 0).wait()
                accumulate(k0, 0)

                @pl.when(k0 + 2 < group_chunks)
                def _():
                    gather(k0 + 2, 0).start()
                gather(k0 + 1, 1).wait()
                accumulate(k0 + 1, 1)

            pltpu.sync_copy(y_v, o_hbm.at[pl.ds(t0, SC_GROUP)])

    return body(table_words, idx, w, x)


def _peer(x, gain, w_query, sub_keys, expert_u, expert_v):
    t = x.shape[0]
    t_tc = t - SC_TOKENS
    wq, keys = w_query.astype(BF16), sub_keys.astype(BF16)
    u_packed, _ = _pack_table(expert_u)
    v_packed, v_rows = _pack_table(expert_v)

    h, idx, gate = _route(x, gain, wq, keys, t_tc, SC_TOKENS, v_rows[:SUBLANES])
    w_sc = _expert_in(h, _split_offsets(idx), gate, u_packed)
    out_sc = _sc_expert_out(v_rows, (idx // ROW_WORDS).reshape(-1), w_sc.reshape(-1), x, t_tc)

    h, idx, gate = _route(x, gain, wq, keys, 0, t_tc, w_sc[:SUBLANES])
    idx_split = _split_offsets(idx)
    w = _expert_in(h, idx_split, gate, u_packed)
    out_tc = _expert_out(x, idx_split, w, v_packed)
    return jnp.concatenate([out_tc, out_sc], axis=0)


def kernel(x, norm_mix, norm_ffn, attn_w_qkv, attn_q_norm, attn_k_norm, attn_sinks, attn_w_o, conv_w_in, conv_w, conv_w_out, peer_w_query, peer_sub_keys, peer_u, peer_v):
    batch, seq, d = x.shape
    xt = x.reshape(batch * seq, d)
    for i in range(norm_mix.shape[0]):
        j = i // 2
        if i % 2 == 0:
            qkv = _norm_matmul(xt, norm_mix[i], attn_w_qkv[j].astype(BF16))
            o = _attention(qkv, attn_q_norm[j], attn_k_norm[j], attn_sinks[j], batch, seq)
            xt = _matmul_residual(o, attn_w_o[j].astype(BF16), xt)
        else:
            xt = _conv_mixer(xt, norm_mix[i], conv_w_in[j], conv_w[j], conv_w_out[j], batch, seq)
        xt = _peer(xt, norm_ffn[i], peer_w_query[i], peer_sub_keys[i], peer_u[i], peer_v[i])
    return xt.reshape(batch, seq, d)
```

```python
import dataclasses
import functools
import math

import jax
import jax.numpy as jnp
from jax import lax
from jax.experimental import pallas as pl
from jax.experimental.pallas import tpu as pltpu
from jax.experimental.pallas import tpu_sc as plsc

D_MODEL = 1024
RMS_EPS = 1e-6

HEAD_DIM = 64
N_Q_HEADS = 16
N_KV_HEADS = 4
GROUP = N_Q_HEADS // N_KV_HEADS
WINDOW = 128
ROT_DIM = HEAD_DIM // 4
ROPE_THETA = 500000.0
Q_COLS = N_Q_HEADS * HEAD_DIM
KV_COLS = N_KV_HEADS * HEAD_DIM
NEG_INF = -1e30

CONV_WIDTH = 3

PEER_HEADS = 8
N_KEYS = 128
N_EXPERTS = N_KEYS * N_KEYS
PEER_TOPK = 16
QUERY_HALF = 128
SLOTS = PEER_HEADS * PEER_TOPK

LANES = 128
SUBLANES = 8
ROW_WORDS = D_MODEL // 2 // LANES
VMEM_LIMIT = 48 * 1024 * 1024

BF16 = jnp.bfloat16
F32 = jnp.float32


def _rms(x, gain):
    return x * lax.rsqrt(jnp.mean(x * x, axis=-1, keepdims=True) + RMS_EPS) * gain


def _norm_matmul_kernel(x_ref, g_ref, w_ref, o_ref):
    h = _rms(x_ref[...], g_ref[...])
    o_ref[...] = jnp.dot(h.astype(BF16), w_ref[...], preferred_element_type=F32)


def _norm_matmul(x, gain, w, tm=512):
    t, d = x.shape
    n = w.shape[1]
    return pl.pallas_call(
        _norm_matmul_kernel,
        grid=(t // tm,),
        in_specs=[pl.BlockSpec((tm, d), lambda i: (i, 0)),
                  pl.BlockSpec((1, d), lambda i: (0, 0)),
                  pl.BlockSpec((d, n), lambda i: (0, 0))],
        out_specs=pl.BlockSpec((tm, n), lambda i: (i, 0)),
        out_shape=jax.ShapeDtypeStruct((t, n), F32),
        compiler_params=pltpu.CompilerParams(dimension_semantics=("arbitrary",), vmem_limit_bytes=VMEM_LIMIT),
        name="norm_matmul",
    )(x, gain.reshape(1, d), w)


def _matmul_residual_kernel(a_ref, w_ref, r_ref, o_ref):
    o_ref[...] = r_ref[...] + jnp.dot(a_ref[...].astype(BF16), w_ref[...], preferred_element_type=F32)


def _matmul_residual(a, w, res, tm=512):
    t, k = a.shape
    n = w.shape[1]
    return pl.pallas_call(
        _matmul_residual_kernel,
        grid=(t // tm,),
        in_specs=[pl.BlockSpec((tm, k), lambda i: (i, 0)),
                  pl.BlockSpec((k, n), lambda i: (0, 0)),
                  pl.BlockSpec((tm, n), lambda i: (i, 0))],
        out_specs=pl.BlockSpec((tm, n), lambda i: (i, 0)),
        out_shape=jax.ShapeDtypeStruct((t, n), F32),
        compiler_params=pltpu.CompilerParams(dimension_semantics=("arbitrary",), vmem_limit_bytes=VMEM_LIMIT),
        name="matmul_residual",
    )(a, w, res)


def _rope_tables(seq):
    half = ROT_DIM // 2
    freqs = ROPE_THETA ** (-jnp.arange(0, ROT_DIM, 2, dtype=F32) / ROT_DIM)
    ang = jnp.arange(seq, dtype=F32)[:, None] * freqs[None, :]
    cos, sin = jnp.cos(ang), jnp.sin(ang)
    ones = jnp.ones((seq, HEAD_DIM - ROT_DIM), F32)
    zeros = jnp.zeros((seq, HEAD_DIM - ROT_DIM), F32)
    zh = jnp.zeros((seq, half), F32)
    c = jnp.concatenate([cos, cos, ones], axis=1)
    s_next = jnp.concatenate([-sin, zh, zeros], axis=1)
    s_prev = jnp.concatenate([zh, sin, zeros], axis=1)
    return jnp.stack([jnp.tile(c, (1, 2)), jnp.tile(s_next, (1, 2)), jnp.tile(s_prev, (1, 2))])


def _head_norm_rope(x, gain2, rope, lo):
    sq = x * x
    s_lo = jnp.sum(jnp.where(lo, sq, 0.0), axis=1, keepdims=True)
    s_hi = jnp.sum(jnp.where(lo, 0.0, sq), axis=1, keepdims=True)
    ms = jnp.where(lo, s_lo, s_hi) * (1.0 / HEAD_DIM)
    xn = x * lax.rsqrt(ms + RMS_EPS) * gain2
    half = ROT_DIM // 2
    return xn * rope[0] + pltpu.roll(xn, LANES - half, 1) * rope[1] + pltpu.roll(xn, half, 1) * rope[2]


def _attn_kernel(sinks_ref, q_ref, kc_ref, kp_ref, vc_ref, vp_ref, rc_ref, rp_ref, qg_ref, kg_ref, o_ref):
    n = pl.program_id(1)
    lo = lax.broadcasted_iota(jnp.int32, (WINDOW, LANES), 1) < HEAD_DIM
    rope_c = rc_ref[...]
    rope_p = rp_ref[...]
    qg = qg_ref[...]
    kg = kg_ref[...]

    rows = GROUP * WINDOW
    qi = lax.broadcasted_iota(jnp.int32, (rows, 2 * WINDOW), 0) & (WINDOW - 1)
    ki = lax.broadcasted_iota(jnp.int32, (rows, 2 * WINDOW), 1)
    rel = WINDOW + qi - ki
    valid = (rel >= 0) & (rel < WINDOW) & ((n > 0) | (ki >= WINDOW))
    head_of_row = lax.broadcasted_iota(jnp.int32, (rows, 1), 0) // WINDOW
    scale = 1.0 / math.sqrt(HEAD_DIM)

    q2 = [_head_norm_rope(q_ref[:, c * LANES:(c + 1) * LANES], qg, rope_c, lo).astype(BF16)
          for c in range(Q_COLS // LANES)]
    for c in range(KV_COLS // LANES):
        cols = slice(c * LANES, (c + 1) * LANES)
        kprev = _head_norm_rope(kp_ref[:, cols], kg, rope_p, lo)
        kcur = _head_norm_rope(kc_ref[:, cols], kg, rope_c, lo)
        kfull = jnp.concatenate([kprev, kcur], axis=0).astype(BF16)
        vfull = jnp.concatenate([vp_ref[:, cols], vc_ref[:, cols]], axis=0).astype(BF16)
        for hh in range(LANES // HEAD_DIM):
            h = (LANES // HEAD_DIM) * c + hh
            kh = kfull[:, hh * HEAD_DIM:(hh + 1) * HEAD_DIM]
            vh = vfull[:, hh * HEAD_DIM:(hh + 1) * HEAD_DIM]
            heads = [GROUP * h + g for g in range(GROUP)]
            q4 = jnp.concatenate([q2[j // 2][:, (j % 2) * HEAD_DIM:(j % 2 + 1) * HEAD_DIM] for j in heads], axis=0)
            s = lax.dot_general(q4, kh, (((1,), (1,)), ((), ())), preferred_element_type=F32) * scale
            s = jnp.where(valid, s, NEG_INF)
            sink = jnp.zeros((rows, 1), F32)
            for g, j in enumerate(heads):
                sink = jnp.where(head_of_row == g, sinks_ref[j], sink)
            m = jnp.maximum(jnp.max(s, axis=1, keepdims=True), sink)
            p = jnp.exp(s - m)
            denom = jnp.sum(p, axis=1, keepdims=True) + jnp.exp(sink - m)
            o = jnp.dot(p.astype(BF16), vh, preferred_element_type=F32) / denom
            for g, j in enumerate(heads):
                o_ref[:, j * HEAD_DIM:(j + 1) * HEAD_DIM] = o[g * WINDOW:(g + 1) * WINDOW]


def _attention(qkv, q_gain, k_gain, sinks, batch, seq):
    t = batch * seq
    nb = seq // WINDOW
    rope = _rope_tables(seq)
    kcol = Q_COLS // KV_COLS
    cur = lambda b, n: (b * nb + n, 0)
    kcur = lambda b, n: (b * nb + n, kcol)
    kprev = lambda b, n: (b * nb + jnp.maximum(n - 1, 0), kcol)
    vcur = lambda b, n: (b * nb + n, kcol + 1)
    vprev = lambda b, n: (b * nb + jnp.maximum(n - 1, 0), kcol + 1)
    return pl.pallas_call(
        _attn_kernel,
        grid=(batch, nb),
        in_specs=[pl.BlockSpec(memory_space=pltpu.SMEM),
                  pl.BlockSpec((WINDOW, Q_COLS), cur),
                  pl.BlockSpec((WINDOW, KV_COLS), kcur),
                  pl.BlockSpec((WINDOW, KV_COLS), kprev),
                  pl.BlockSpec((WINDOW, KV_COLS), vcur),
                  pl.BlockSpec((WINDOW, KV_COLS), vprev),
                  pl.BlockSpec((3, WINDOW, LANES), lambda b, n: (0, n, 0)),
                  pl.BlockSpec((3, WINDOW, LANES), lambda b, n: (0, jnp.maximum(n - 1, 0), 0)),
                  pl.BlockSpec((1, LANES), lambda b, n: (0, 0)),
                  pl.BlockSpec((1, LANES), lambda b, n: (0, 0))],
        out_specs=pl.BlockSpec((WINDOW, Q_COLS), cur),
        out_shape=jax.ShapeDtypeStruct((t, Q_COLS), F32),
        compiler_params=pltpu.CompilerParams(dimension_semantics=("arbitrary", "arbitrary"),
                                             vmem_limit_bytes=VMEM_LIMIT),
        name="swa_attention",
    )(sinks, qkv, qkv, qkv, qkv, qkv, rope, rope,
      jnp.tile(q_gain, 2).reshape(1, LANES), jnp.tile(k_gain, 2).reshape(1, LANES))


def _conv_kernel(x_ref, g_ref, win_ref, cw_ref, wout_ref, o_ref, zprev_ref):
    n = pl.program_id(1)
    d = D_MODEL

    @pl.when(n == 0)
    def _():
        zprev_ref[...] = jnp.zeros_like(zprev_ref)

    x = x_ref[...]
    h = _rms(x, g_ref[...])
    bcu = jnp.dot(h.astype(BF16), win_ref[...], preferred_element_type=F32)
    gate_b = bcu[:, :d]
    z = bcu[:, d:2 * d] * bcu[:, 2 * d:]
    tm = z.shape[0]
    row = lax.broadcasted_iota(jnp.int32, z.shape, 0)
    prev = zprev_ref[...]
    p_last = prev[SUBLANES - 1:SUBLANES, :]
    p_last2 = prev[SUBLANES - 2:SUBLANES - 1, :]
    z1 = jnp.where(row == 0, p_last, pltpu.roll(z, 1, 0))
    z2 = jnp.where(row == 0, p_last2, jnp.where(row == 1, p_last, pltpu.roll(z, 2, 0)))
    cw = cw_ref[...]
    conv = cw[0:1, :] * z2 + cw[1:2, :] * z1 + cw[2:3, :] * z
    zprev_ref[...] = z[tm - SUBLANES:, :]
    o_ref[...] = x + jnp.dot((gate_b * conv).astype(BF16), wout_ref[...], preferred_element_type=F32)


def _conv_mixer(x, gain, w_in, conv_w, w_out, batch, seq, tm=256):
    t, d = x.shape
    nblk = seq // tm
    blk = lambda b, n: (b * nblk + n, 0)
    const = lambda b, n: (0, 0)
    return pl.pallas_call(
        _conv_kernel,
        grid=(batch, nblk),
        in_specs=[pl.BlockSpec((tm, d), blk),
                  pl.BlockSpec((1, d), const),
                  pl.BlockSpec((d, 3 * d), const),
                  pl.BlockSpec((CONV_WIDTH, d), const),
                  pl.BlockSpec((d, d), const)],
        out_specs=pl.BlockSpec((tm, d), blk),
        out_shape=jax.ShapeDtypeStruct((t, d), F32),
        scratch_shapes=[pltpu.VMEM((SUBLANES, d), F32)],
        compiler_params=pltpu.CompilerParams(dimension_semantics=("arbitrary", "arbitrary"),
                                             vmem_limit_bytes=VMEM_LIMIT),
        name="conv_mixer",
    )(x, gain.reshape(1, d), w_in.astype(BF16), conv_w, w_out.astype(BF16))


def _topk_axis0(s, k, ids=None, payload=None):
    n, tm = s.shape
    if ids is None:
        ids = lax.broadcasted_iota(jnp.int32, (n, tm), 0)
    krow = lax.broadcasted_iota(jnp.int32, (k, tm), 0)
    vals = jnp.zeros((k, tm), F32)
    picks = jnp.zeros((k, tm), jnp.int32)
    for r in range(k):
        m = jnp.max(s, axis=0, keepdims=True)
        pos = jnp.min(jnp.where(s == m, ids, jnp.iinfo(jnp.int32).max), axis=0, keepdims=True)
        sel = ids == pos
        if payload is None:
            picked = pos
        else:
            picked = jnp.sum(jnp.where(sel, payload, 0), axis=0, keepdims=True)
        vals = jnp.where(krow == r, m, vals)
        picks = jnp.where(krow == r, picked, picks)
        s = jnp.where(sel, -jnp.inf, s)
    return vals, picks


def _pair_candidates(s1, i1, s2, i2):
    k, tm = s1.shape
    sub = lax.broadcasted_iota(jnp.int32, (SUBLANES, tm), 0)
    scores, flat, expert = [], [], []
    for i in range(k // 2):
        width = k if i == 0 else SUBLANES
        sc = s1[i:i + 1, :] + s2[0:width, :]
        ex = i1[i:i + 1, :] * N_KEYS + i2[0:width, :]
        fl = i * k + lax.broadcasted_iota(jnp.int32, (width, tm), 0)
        reach = k // (i + 1)
        if reach < width:
            sc = jnp.where(sub < reach, sc, -jnp.inf)
        scores.append(sc)
        flat.append(fl)
        expert.append(ex)
    scores.append(s1[k // 2:, :] + s2[0:1, :])
    expert.append(i1[k // 2:, :] * N_KEYS + i2[0:1, :])
    flat.append((k // 2 + sub) * k)
    return jnp.concatenate(scores, axis=0), jnp.concatenate(flat, axis=0), jnp.concatenate(expert, axis=0)


def _route_kernel(x_ref, g_ref, wq_ref, keys_ref, after_ref, h_ref, idx_ref, gate_ref):
    del after_ref
    h = _rms(x_ref[...], g_ref[...])
    h_ref[...] = h
    q = jnp.dot(h.astype(BF16), wq_ref[...], preferred_element_type=F32).astype(BF16)
    idx_rows, gate_rows = [], []
    for head in range(PEER_HEADS):
        tops = []
        for part in range(2):
            col = (head * 2 + part) * QUERY_HALF
            s = lax.dot_general(keys_ref[head, part], q[:, col:col + QUERY_HALF],
                                (((1,), (1,)), ((), ())), preferred_element_type=F32)
            tops.append(_topk_axis0(s, PEER_TOPK))
        (s1, i1), (s2, i2) = tops
        cand, flat_ids, cand_idx = _pair_candidates(s1, i1, s2, i2)
        g_s, e_idx = _topk_axis0(cand, PEER_TOPK, ids=flat_ids, payload=cand_idx)
        e = jnp.exp(g_s - jnp.max(g_s, axis=0, keepdims=True))
        gate_rows.append(e / jnp.sum(e, axis=0, keepdims=True))
        idx_rows.append(e_idx * ROW_WORDS)
    idx_ref[...] = jnp.concatenate(idx_rows, axis=0).T
    gate_ref[...] = jnp.concatenate(gate_rows, axis=0).T


def _route(x, gain, w_query, sub_keys, first_tok, n_tok, after, tm=128):
    t, d = x.shape
    nq = w_query.shape[1]
    first_blk = first_tok // tm
    return pl.pallas_call(
        _route_kernel,
        grid=(n_tok // tm,),
        in_specs=[pl.BlockSpec((tm, d), lambda i: (i + first_blk, 0)),
                  pl.BlockSpec((1, d), lambda i: (0, 0)),
                  pl.BlockSpec((d, nq), lambda i: (0, 0)),
                  pl.BlockSpec((PEER_HEADS, 2, N_KEYS, QUERY_HALF), lambda i: (0, 0, 0, 0)),
                  pl.BlockSpec(after.shape, lambda i: (0, 0))],
        out_specs=[pl.BlockSpec((tm, d), lambda i: (i, 0)),
                   pl.BlockSpec((tm, SLOTS), lambda i: (i, 0)),
                   pl.BlockSpec((tm, SLOTS), lambda i: (i, 0))],
        out_shape=[jax.ShapeDtypeStruct((n_tok, d), F32),
                   jax.ShapeDtypeStruct((n_tok, SLOTS), jnp.int32),
                   jax.ShapeDtypeStruct((n_tok, SLOTS), F32)],
        compiler_params=pltpu.CompilerParams(dimension_semantics=("arbitrary",), vmem_limit_bytes=VMEM_LIMIT),
        name="peer_route",
    )(x, gain.reshape(1, d), w_query, sub_keys, after)


def _pack_kernel(t_ref, tiles_ref, rows_ref):
    x = t_ref[...]
    half = x.shape[1] // 2
    hi = pltpu.bitcast(x[:, :half].astype(BF16).astype(F32), jnp.int32)
    lo = pltpu.bitcast(x[:, half:].astype(BF16).astype(F32), jnp.int32)
    words = hi | lax.shift_right_logical(lo, 16)
    rows_ref[...] = words
    for s in range(ROW_WORDS):
        tiles_ref[pl.ds(s, x.shape[0], stride=ROW_WORDS), :] = words[:, s * LANES:(s + 1) * LANES]


def _pack_table(tabs, layer, rows_per_step=256):
    _, ne, d = tabs.shape
    return pl.pallas_call(
        _pack_kernel,
        grid=(ne // rows_per_step,),
        in_specs=[pl.BlockSpec((None, rows_per_step, d), lambda i: (layer, i, 0))],
        out_specs=[pl.BlockSpec((rows_per_step * ROW_WORDS, LANES), lambda i: (i, 0)),
                   pl.BlockSpec((rows_per_step, d // 2), lambda i: (i, 0))],
        out_shape=[jax.ShapeDtypeStruct((ne * ROW_WORDS, LANES), jnp.int32),
                   jax.ShapeDtypeStruct((ne, d // 2), jnp.int32)],
        compiler_params=pltpu.CompilerParams(dimension_semantics=("arbitrary",), vmem_limit_bytes=VMEM_LIMIT),
        name="pack_table",
    )(tabs)


def _load_table_once(tab_hbm, tab, sem):
    @pl.when(pl.program_id(0) == 0)
    def _():
        cp = pltpu.make_async_copy(tab_hbm, tab, sem)
        cp.start()
        cp.wait()


def _gather_pair(tab, off_a, off_b):
    ra = tab[pl.ds(pl.multiple_of(off_a, ROW_WORDS), ROW_WORDS), :]
    rb = tab[pl.ds(pl.multiple_of(off_b, ROW_WORDS), ROW_WORDS), :]
    words = jnp.concatenate([ra, rb], axis=0)
    hi = pltpu.bitcast(words & jnp.int32(-65536), F32)
    lo = pltpu.bitcast(words << 16, F32)
    return hi, lo


TOKEN_UNROLL = 8


def _expert_in_kernel(*refs):
    idx_refs = refs[:TOKEN_UNROLL]
    h_ref, gate_ref, tab_hbm, w_ref, tab, sem, a_ref = refs[TOKEN_UNROLL:]
    _load_table_once(tab_hbm, tab, sem)
    sub = lax.broadcasted_iota(jnp.int32, (SUBLANES, LANES), 0)
    lane = lax.broadcasted_iota(jnp.int32, (SUBLANES, LANES), 1)
    own_half = (sub >= ROW_WORDS) == ((lane & 1) == 1)
    tb = h_ref.shape[0]

    def step(i, carry):
        xs = []
        for u in range(TOKEN_UNROLL):
            x = h_ref[i * TOKEN_UNROLL + u]
            xs.append((jnp.concatenate([x[0:ROW_WORDS], x[0:ROW_WORDS]], axis=0),
                       jnp.concatenate([x[ROW_WORDS:], x[ROW_WORDS:]], axis=0)))
        accs = [jnp.zeros((SUBLANES, LANES), F32) for _ in range(TOKEN_UNROLL)]
        for p in range(SLOTS // 2):
            for u in range(TOKEN_UNROLL):
                hi, lo = _gather_pair(tab, idx_refs[u][i, 2 * p], idx_refs[u][i, 2 * p + 1])
                part = jnp.sum(hi * xs[u][0] + lo * xs[u][1], axis=1, keepdims=True)
                accs[u] = jnp.where((lane >> 1) == p, part, accs[u])
        for u in range(TOKEN_UNROLL):
            a_ref[pl.ds(i * TOKEN_UNROLL + u, 1), :] = jnp.sum(jnp.where(own_half, accs[u], 0.0),
                                                                axis=0, keepdims=True)
        return carry

    lax.fori_loop(0, tb // TOKEN_UNROLL, step, 0)
    a = a_ref[...]
    w_ref[...] = gate_ref[...] * (0.5 * a * (1.0 + lax.erf(a * (1.0 / math.sqrt(2.0)))))


def _expert_out_kernel(*refs):
    idx_refs = refs[:TOKEN_UNROLL]
    w_ref, x_ref, tab_hbm, o_ref, tab, sem, wb_ref = refs[TOKEN_UNROLL:]
    _load_table_once(tab_hbm, tab, sem)
    sub = lax.broadcasted_iota(jnp.int32, (SUBLANES, LANES), 0)
    lower = sub < ROW_WORDS
    tb = x_ref.shape[0]

    def step(i, carry):
        for u in range(TOKEN_UNROLL):
            row = w_ref[pl.ds(i * TOKEN_UNROLL + u, 1), :]
            wb_ref[u] = jnp.broadcast_to(row, (SLOTS, LANES)).T
        acc_h = [jnp.zeros((SUBLANES, LANES), F32) for _ in range(TOKEN_UNROLL)]
        acc_l = [jnp.zeros((SUBLANES, LANES), F32) for _ in range(TOKEN_UNROLL)]
        for p in range(SLOTS // 2):
            ka, kb = 2 * p, 2 * p + 1
            for u in range(TOKEN_UNROLL):
                hi, lo = _gather_pair(tab, idx_refs[u][i, ka], idx_refs[u][i, kb])
                wa = jnp.broadcast_to(wb_ref[u, ka:ka + 1, :], (SUBLANES, LANES))
                wb = jnp.broadcast_to(wb_ref[u, kb:kb + 1, :], (SUBLANES, LANES))
                wt = jnp.where(lower, wa, wb)
                acc_h[u] = acc_h[u] + hi * wt
                acc_l[u] = acc_l[u] + lo * wt
        for u in range(TOKEN_UNROLL):
            t = i * TOKEN_UNROLL + u
            ah = acc_h[u] + pltpu.roll(acc_h[u], ROW_WORDS, 0)
            al = acc_l[u] + pltpu.roll(acc_l[u], ROW_WORDS, 0)
            o_ref[t] = x_ref[t] + jnp.where(lower, ah, al)
        return carry

    lax.fori_loop(0, tb // TOKEN_UNROLL, step, 0)


def _expert_specs(tb):
    un = TOKEN_UNROLL
    smem_blks = [pl.BlockSpec((None, tb // un, SLOTS), functools.partial(lambda u, i: (u, i, 0), u),
                              memory_space=pltpu.SMEM) for u in range(un)]
    vmem_blk = pl.BlockSpec((tb, SLOTS), lambda i: (i, 0))
    tok_blk = pl.BlockSpec((tb, D_MODEL // LANES, LANES), lambda i: (i, 0, 0))
    params = pltpu.CompilerParams(dimension_semantics=("arbitrary",), vmem_limit_bytes=VMEM_LIMIT)
    return smem_blks, vmem_blk, tok_blk, params


def _split_offsets(idx):
    n = idx.shape[0]
    return idx.reshape(n // TOKEN_UNROLL, TOKEN_UNROLL, SLOTS).transpose(1, 0, 2)


def _expert_in(h, idx_split, gate, u_packed, tb=128):
    n, d = h.shape
    smem_blks, vmem_blk, tok_blk, params = _expert_specs(tb)
    return pl.pallas_call(
        _expert_in_kernel,
        grid=(n // tb,),
        in_specs=smem_blks + [tok_blk, vmem_blk, pl.BlockSpec(memory_space=pl.ANY)],
        out_specs=vmem_blk,
        out_shape=jax.ShapeDtypeStruct((n, SLOTS), F32),
        scratch_shapes=[pltpu.VMEM(u_packed.shape, jnp.int32), pltpu.SemaphoreType.DMA, pltpu.VMEM((tb, SLOTS), F32)],
        compiler_params=params,
        name="peer_expert_in",
    )(*([idx_split] * TOKEN_UNROLL), h.reshape(n, d // LANES, LANES), gate, u_packed)


def _expert_out(x, idx_split, w, v_packed, tb=128):
    t, d = x.shape
    n = w.shape[0]
    smem_blks, vmem_blk, tok_blk, params = _expert_specs(tb)
    out = pl.pallas_call(
        _expert_out_kernel,
        grid=(n // tb,),
        in_specs=smem_blks + [vmem_blk, tok_blk, pl.BlockSpec(memory_space=pl.ANY)],
        out_specs=tok_blk,
        out_shape=jax.ShapeDtypeStruct((n, d // LANES, LANES), F32),
        scratch_shapes=[pltpu.VMEM(v_packed.shape, jnp.int32), pltpu.SemaphoreType.DMA,
                        pltpu.VMEM((TOKEN_UNROLL, SLOTS, LANES), F32)],
        compiler_params=params,
        name="peer_expert_out",
    )(*([idx_split] * TOKEN_UNROLL), w, x.reshape(t, d // LANES, LANES), v_packed)
    return out.reshape(n, d)


SC_LANES = 16
SC_WORKERS = 32
SC_CHUNK = 32
SC_GROUP = 8
SC_TOKENS = 17408


def _sc_params():
    cp = pltpu.CompilerParams()
    if "needs_layout_passes" in pltpu.CompilerParams.__dataclass_fields__:
        cp = dataclasses.replace(cp, needs_layout_passes=False)
    return cp


def _sc_expert_out(table_words, idx, w, x, first_tok):
    d = x.shape[1]
    n_tok = w.shape[0] // SLOTS
    per = n_tok // SC_WORKERS
    words = d // 2
    nq = words // SC_LANES // 2
    nchunk = SLOTS // SC_CHUNK
    group_chunks = SC_GROUP * nchunk
    mesh = plsc.VectorSubcoreMesh(core_axis_name="c", subcore_axis_name="s")

    @functools.partial(
        pl.kernel, mesh=mesh,
        out_type=jax.ShapeDtypeStruct((n_tok, d), F32),
        scratch_types=[pltpu.VMEM((SC_GROUP * SLOTS,), jnp.int32), pltpu.VMEM((SC_GROUP * SLOTS,), F32),
                       pltpu.VMEM((2, SC_CHUNK, words), jnp.int32), pltpu.VMEM((SC_GROUP, d), F32),
                       pltpu.SemaphoreType.DMA, pltpu.SemaphoreType.DMA],
        compiler_params=_sc_params(),
        name="peer_expert_out_sc",
    )
    def body(tab_hbm, idx_hbm, w_hbm, x_hbm, o_hbm, idx_v, w_v, rows_v, y_v, sem0, sem1):
        base = (lax.axis_index("s") * 2 + lax.axis_index("c")) * per
        zero = jnp.zeros((SC_LANES,), jnp.int32)
        himask = jnp.full((SC_LANES,), -65536, jnp.int32)
        sems = (sem0, sem1)

        def gather(k, b):
            off = pl.multiple_of(k * SC_CHUNK, SC_CHUNK)
            return pltpu.make_async_copy(tab_hbm.at[idx_v.at[pl.ds(off, SC_CHUNK)]], rows_v.at[b], sems[b])

        def accumulate(k, b):
            tok = k // nchunk
            for q in range(2):
                first = q * nq
                acc0 = (tuple(y_v[tok, pl.ds((first + j) * SC_LANES, SC_LANES)] for j in range(nq))
                        + tuple(y_v[tok, pl.ds(words + (first + j) * SC_LANES, SC_LANES)] for j in range(nq)))

                def row_body(r, accs):
                    ws = plsc.load_gather(w_v, [zero + (k * SC_CHUNK + r)])
                    hi_acc, lo_acc = [], []
                    for j in range(nq):
                        wv = rows_v[b, r, pl.ds((first + j) * SC_LANES, SC_LANES)]
                        hi_acc.append(accs[j] + plsc.bitcast(wv & himask, F32) * ws)
                        lo_acc.append(accs[nq + j] + plsc.bitcast(wv << 16, F32) * ws)
                    return tuple(hi_acc) + tuple(lo_acc)

                accs = lax.fori_loop(0, SC_CHUNK, row_body, acc0)
                for j in range(nq):
                    y_v[tok, pl.ds((first + j) * SC_LANES, SC_LANES)] = accs[j]
                    y_v[tok, pl.ds(words + (first + j) * SC_LANES, SC_LANES)] = accs[nq + j]

        @pl.loop(0, per // SC_GROUP)
        def _(g):
            t0 = pl.multiple_of(base + g * SC_GROUP, SC_GROUP)
            pltpu.sync_copy(idx_hbm.at[pl.ds(t0 * SLOTS, SC_GROUP * SLOTS)], idx_v)
            pltpu.sync_copy(w_hbm.at[pl.ds(t0 * SLOTS, SC_GROUP * SLOTS)], w_v)
            pltpu.sync_copy(x_hbm.at[pl.ds(first_tok + t0, SC_GROUP)], y_v)
            gather(0, 0).start()

            @pl.loop(0, group_chunks // 2)
            def _(kk):
                k0 = 2 * kk
                gather(k0 + 1, 1).start()
                gather(k0, 0).wait()
                accumulate(k0, 0)

                @pl.when(k0 + 2 < group_chunks)
                def _():
                    gather(k0 + 2, 0).start()
                gather(k0 + 1, 1).wait()
                accumulate(k0 + 1, 1)

            pltpu.sync_copy(y_v, o_hbm.at[pl.ds(t0, SC_GROUP)])

    return body(table_words, idx, w, x)


def _peer(x, gain, w_query, sub_keys, experts_u, experts_v, layer):
    t = x.shape[0]
    t_tc = t - SC_TOKENS
    wq, keys = w_query.astype(BF16), sub_keys.astype(BF16)
    u_packed, _ = _pack_table(experts_u, layer)
    v_packed, v_rows = _pack_table(experts_v, layer)

    h, idx, gate = _route(x, gain, wq, keys, t_tc, SC_TOKENS, v_rows[:SUBLANES])
    w_sc = _expert_in(h, _split_offsets(idx), gate, u_packed)
    out_sc = _sc_expert_out(v_rows, (idx // ROW_WORDS).reshape(-1), w_sc.reshape(-1), x, t_tc)

    h, idx, gate = _route(x, gain, wq, keys, 0, t_tc, w_sc[:SUBLANES])
    idx_split = _split_offsets(idx)
    w = _expert_in(h, idx_split, gate, u_packed)
    out_tc = _expert_out(x, idx_split, w, v_packed)
    return jnp.concatenate([out_tc, out_sc], axis=0)


def kernel(x, norm_mix, norm_ffn, attn_w_qkv, attn_q_norm, attn_k_norm, attn_sinks, attn_w_o, conv_w_in, conv_w, conv_w_out, peer_w_query, peer_sub_keys, peer_u, peer_v):
    batch, seq, d = x.shape
    xt = x.reshape(batch * seq, d)
    for i in range(norm_mix.shape[0]):
        j = i // 2
        if i % 2 == 0:
            qkv = _norm_matmul(xt, norm_mix[i], attn_w_qkv[j].astype(BF16))
            o = _attention(qkv, attn_q_norm[j], attn_k_norm[j], attn_sinks[j], batch, seq)
            xt = _matmul_residual(o, attn_w_o[j].astype(BF16), xt)
        else:
            xt = _conv_mixer(xt, norm_mix[i], conv_w_in[j], conv_w[j], conv_w_out[j], batch, seq)
        xt = _peer(xt, norm_ffn[i], peer_w_query[i], peer_sub_keys[i], peer_u, peer_v, i)
    return xt.reshape(batch, seq, d)
```

```python
import dataclasses
import functools
import math

import jax
import jax.numpy as jnp
from jax import lax
from jax.experimental import pallas as pl
from jax.experimental.pallas import tpu as pltpu
from jax.experimental.pallas import tpu_sc as plsc

D_MODEL = 1024
RMS_EPS = 1e-6

HEAD_DIM = 64
N_Q_HEADS = 16
N_KV_HEADS = 4
GROUP = N_Q_HEADS // N_KV_HEADS
WINDOW = 128
ROT_DIM = HEAD_DIM // 4
ROPE_THETA = 500000.0
Q_COLS = N_Q_HEADS * HEAD_DIM
KV_COLS = N_KV_HEADS * HEAD_DIM
NEG_INF = -1e30

CONV_WIDTH = 3

PEER_HEADS = 8
N_KEYS = 128
N_EXPERTS = N_KEYS * N_KEYS
PEER_TOPK = 16
QUERY_HALF = 128
SLOTS = PEER_HEADS * PEER_TOPK

LANES = 128
SUBLANES = 8
ROW_WORDS = D_MODEL // 2 // LANES
VMEM_LIMIT = 48 * 1024 * 1024

BF16 = jnp.bfloat16
F32 = jnp.float32


def _rms(x, gain):
    return x * lax.rsqrt(jnp.mean(x * x, axis=-1, keepdims=True) + RMS_EPS) * gain


def _norm_matmul_kernel(x_ref, g_ref, w_ref, o_ref):
    h = _rms(x_ref[...], g_ref[...])
    o_ref[...] = jnp.dot(h.astype(BF16), w_ref[...], preferred_element_type=F32)


def _norm_matmul(x, gain, w, tm=512):
    t, d = x.shape
    n = w.shape[1]
    return pl.pallas_call(
        _norm_matmul_kernel,
        grid=(t // tm,),
        in_specs=[pl.BlockSpec((tm, d), lambda i: (i, 0)),
                  pl.BlockSpec((1, d), lambda i: (0, 0)),
                  pl.BlockSpec((d, n), lambda i: (0, 0))],
        out_specs=pl.BlockSpec((tm, n), lambda i: (i, 0)),
        out_shape=jax.ShapeDtypeStruct((t, n), F32),
        compiler_params=pltpu.CompilerParams(dimension_semantics=("arbitrary",), vmem_limit_bytes=VMEM_LIMIT),
        name="norm_matmul",
    )(x, gain.reshape(1, d), w)


def _matmul_residual_kernel(a_ref, w_ref, r_ref, o_ref):
    o_ref[...] = r_ref[...] + jnp.dot(a_ref[...].astype(BF16), w_ref[...], preferred_element_type=F32)


def _matmul_residual(a, w, res, tm=512):
    t, k = a.shape
    n = w.shape[1]
    return pl.pallas_call(
        _matmul_residual_kernel,
        grid=(t // tm,),
        in_specs=[pl.BlockSpec((tm, k), lambda i: (i, 0)),
                  pl.BlockSpec((k, n), lambda i: (0, 0)),
                  pl.BlockSpec((tm, n), lambda i: (i, 0))],
        out_specs=pl.BlockSpec((tm, n), lambda i: (i, 0)),
        out_shape=jax.ShapeDtypeStruct((t, n), F32),
        compiler_params=pltpu.CompilerParams(dimension_semantics=("arbitrary",), vmem_limit_bytes=VMEM_LIMIT),
        name="matmul_residual",
    )(a, w, res)


def _rope_tables(seq):
    half = ROT_DIM // 2
    freqs = ROPE_THETA ** (-jnp.arange(0, ROT_DIM, 2, dtype=F32) / ROT_DIM)
    ang = jnp.arange(seq, dtype=F32)[:, None] * freqs[None, :]
    cos, sin = jnp.cos(ang), jnp.sin(ang)
    ones = jnp.ones((seq, HEAD_DIM - ROT_DIM), F32)
    zeros = jnp.zeros((seq, HEAD_DIM - ROT_DIM), F32)
    zh = jnp.zeros((seq, half), F32)
    c = jnp.concatenate([cos, cos, ones], axis=1)
    s_next = jnp.concatenate([-sin, zh, zeros], axis=1)
    s_prev = jnp.concatenate([zh, sin, zeros], axis=1)
    return jnp.stack([jnp.tile(c, (1, 2)), jnp.tile(s_next, (1, 2)), jnp.tile(s_prev, (1, 2))])


def _head_norm_rope(x, gain2, rope, lo):
    sq = x * x
    s_lo = jnp.sum(jnp.where(lo, sq, 0.0), axis=1, keepdims=True)
    s_hi = jnp.sum(jnp.where(lo, 0.0, sq), axis=1, keepdims=True)
    ms = jnp.where(lo, s_lo, s_hi) * (1.0 / HEAD_DIM)
    xn = x * lax.rsqrt(ms + RMS_EPS) * gain2
    half = ROT_DIM // 2
    return xn * rope[0] + pltpu.roll(xn, LANES - half, 1) * rope[1] + pltpu.roll(xn, half, 1) * rope[2]


def _attn_kernel(sinks_ref, q_ref, kc_ref, kp_ref, vc_ref, vp_ref, rc_ref, rp_ref, qg_ref, kg_ref, o_ref):
    n = pl.program_id(1)
    lo = lax.broadcasted_iota(jnp.int32, (WINDOW, LANES), 1) < HEAD_DIM
    rope_c = rc_ref[...]
    rope_p = rp_ref[...]
    qg = qg_ref[...]
    kg = kg_ref[...]

    rows = GROUP * WINDOW
    qi = lax.broadcasted_iota(jnp.int32, (rows, 2 * WINDOW), 0) & (WINDOW - 1)
    ki = lax.broadcasted_iota(jnp.int32, (rows, 2 * WINDOW), 1)
    rel = WINDOW + qi - ki
    valid = (rel >= 0) & (rel < WINDOW) & ((n > 0) | (ki >= WINDOW))
    head_of_row = lax.broadcasted_iota(jnp.int32, (rows, 1), 0) // WINDOW
    scale = 1.0 / math.sqrt(HEAD_DIM)

    q2 = [_head_norm_rope(q_ref[:, c * LANES:(c + 1) * LANES], qg, rope_c, lo).astype(BF16)
          for c in range(Q_COLS // LANES)]
    for c in range(KV_COLS // LANES):
        cols = slice(c * LANES, (c + 1) * LANES)
        kprev = _head_norm_rope(kp_ref[:, cols], kg, rope_p, lo)
        kcur = _head_norm_rope(kc_ref[:, cols], kg, rope_c, lo)
        kfull = jnp.concatenate([kprev, kcur], axis=0).astype(BF16)
        vfull = jnp.concatenate([vp_ref[:, cols], vc_ref[:, cols]], axis=0).astype(BF16)
        for hh in range(LANES // HEAD_DIM):
            h = (LANES // HEAD_DIM) * c + hh
            kh = kfull[:, hh * HEAD_DIM:(hh + 1) * HEAD_DIM]
            vh = vfull[:, hh * HEAD_DIM:(hh + 1) * HEAD_DIM]
            heads = [GROUP * h + g for g in range(GROUP)]
            q4 = jnp.concatenate([q2[j // 2][:, (j % 2) * HEAD_DIM:(j % 2 + 1) * HEAD_DIM] for j in heads], axis=0)
            s = lax.dot_general(q4, kh, (((1,), (1,)), ((), ())), preferred_element_type=F32) * scale
            s = jnp.where(valid, s, NEG_INF)
            sink = jnp.zeros((rows, 1), F32)
            for g, j in enumerate(heads):
                sink = jnp.where(head_of_row == g, sinks_ref[j], sink)
            m = jnp.maximum(jnp.max(s, axis=1, keepdims=True), sink)
            p = jnp.exp(s - m)
            denom = jnp.sum(p, axis=1, keepdims=True) + jnp.exp(sink - m)
            o = jnp.dot(p.astype(BF16), vh, preferred_element_type=F32) / denom
            for g, j in enumerate(heads):
                o_ref[:, j * HEAD_DIM:(j + 1) * HEAD_DIM] = o[g * WINDOW:(g + 1) * WINDOW]


def _attention(qkv, q_gain, k_gain, sinks, batch, seq):
    t = batch * seq
    nb = seq // WINDOW
    rope = _rope_tables(seq)
    kcol = Q_COLS // KV_COLS
    cur = lambda b, n: (b * nb + n, 0)
    kcur = lambda b, n: (b * nb + n, kcol)
    kprev = lambda b, n: (b * nb + jnp.maximum(n - 1, 0), kcol)
    vcur = lambda b, n: (b * nb + n, kcol + 1)
    vprev = lambda b, n: (b * nb + jnp.maximum(n - 1, 0), kcol + 1)
    return pl.pallas_call(
        _attn_kernel,
        grid=(batch, nb),
        in_specs=[pl.BlockSpec(memory_space=pltpu.SMEM),
                  pl.BlockSpec((WINDOW, Q_COLS), cur),
                  pl.BlockSpec((WINDOW, KV_COLS), kcur),
                  pl.BlockSpec((WINDOW, KV_COLS), kprev),
                  pl.BlockSpec((WINDOW, KV_COLS), vcur),
                  pl.BlockSpec((WINDOW, KV_COLS), vprev),
                  pl.BlockSpec((3, WINDOW, LANES), lambda b, n: (0, n, 0)),
                  pl.BlockSpec((3, WINDOW, LANES), lambda b, n: (0, jnp.maximum(n - 1, 0), 0)),
                  pl.BlockSpec((1, LANES), lambda b, n: (0, 0)),
                  pl.BlockSpec((1, LANES), lambda b, n: (0, 0))],
        out_specs=pl.BlockSpec((WINDOW, Q_COLS), cur),
        out_shape=jax.ShapeDtypeStruct((t, Q_COLS), F32),
        compiler_params=pltpu.CompilerParams(dimension_semantics=("arbitrary", "arbitrary"),
                                             vmem_limit_bytes=VMEM_LIMIT),
        name="swa_attention",
    )(sinks, qkv, qkv, qkv, qkv, qkv, rope, rope,
      jnp.tile(q_gain, 2).reshape(1, LANES), jnp.tile(k_gain, 2).reshape(1, LANES))


def _conv_kernel(x_ref, g_ref, win_ref, cw_ref, wout_ref, o_ref, zprev_ref):
    n = pl.program_id(1)
    d = D_MODEL

    @pl.when(n == 0)
    def _():
        zprev_ref[...] = jnp.zeros_like(zprev_ref)

    x = x_ref[...]
    h = _rms(x, g_ref[...])
    bcu = jnp.dot(h.astype(BF16), win_ref[...], preferred_element_type=F32)
    gate_b = bcu[:, :d]
    z = bcu[:, d:2 * d] * bcu[:, 2 * d:]
    tm = z.shape[0]
    row = lax.broadcasted_iota(jnp.int32, z.shape, 0)
    prev = zprev_ref[...]
    p_last = prev[SUBLANES - 1:SUBLANES, :]
    p_last2 = prev[SUBLANES - 2:SUBLANES - 1, :]
    z1 = jnp.where(row == 0, p_last, pltpu.roll(z, 1, 0))
    z2 = jnp.where(row == 0, p_last2, jnp.where(row == 1, p_last, pltpu.roll(z, 2, 0)))
    cw = cw_ref[...]
    conv = cw[0:1, :] * z2 + cw[1:2, :] * z1 + cw[2:3, :] * z
    zprev_ref[...] = z[tm - SUBLANES:, :]
    o_ref[...] = x + jnp.dot((gate_b * conv).astype(BF16), wout_ref[...], preferred_element_type=F32)


def _conv_mixer(x, gain, w_in, conv_w, w_out, batch, seq, tm=256):
    t, d = x.shape
    nblk = seq // tm
    blk = lambda b, n: (b * nblk + n, 0)
    const = lambda b, n: (0, 0)
    return pl.pallas_call(
        _conv_kernel,
        grid=(batch, nblk),
        in_specs=[pl.BlockSpec((tm, d), blk),
                  pl.BlockSpec((1, d), const),
                  pl.BlockSpec((d, 3 * d), const),
                  pl.BlockSpec((CONV_WIDTH, d), const),
                  pl.BlockSpec((d, d), const)],
        out_specs=pl.BlockSpec((tm, d), blk),
        out_shape=jax.ShapeDtypeStruct((t, d), F32),
        scratch_shapes=[pltpu.VMEM((SUBLANES, d), F32)],
        compiler_params=pltpu.CompilerParams(dimension_semantics=("arbitrary", "arbitrary"),
                                             vmem_limit_bytes=VMEM_LIMIT),
        name="conv_mixer",
    )(x, gain.reshape(1, d), w_in.astype(BF16), conv_w, w_out.astype(BF16))


def _topk_axis0(s, k, ids=None, payload=None):
    n, tm = s.shape
    if ids is None:
        ids = lax.broadcasted_iota(jnp.int32, (n, tm), 0)
    krow = lax.broadcasted_iota(jnp.int32, (k, tm), 0)
    vals = jnp.zeros((k, tm), F32)
    picks = jnp.zeros((k, tm), jnp.int32)
    for r in range(k):
        m = jnp.max(s, axis=0, keepdims=True)
        pos = jnp.min(jnp.where(s == m, ids, jnp.iinfo(jnp.int32).max), axis=0, keepdims=True)
        sel = ids == pos
        if payload is None:
            picked = pos
        else:
            picked = jnp.sum(jnp.where(sel, payload, 0), axis=0, keepdims=True)
        vals = jnp.where(krow == r, m, vals)
        picks = jnp.where(krow == r, picked, picks)
        s = jnp.where(sel, -jnp.inf, s)
    return vals, picks


def _pair_candidates(s1, i1, s2, i2):
    k, tm = s1.shape
    sub = lax.broadcasted_iota(jnp.int32, (SUBLANES, tm), 0)
    scores, flat, expert = [], [], []
    for i in range(k // 2):
        width = k if i == 0 else SUBLANES
        sc = s1[i:i + 1, :] + s2[0:width, :]
        ex = i1[i:i + 1, :] * N_KEYS + i2[0:width, :]
        fl = i * k + lax.broadcasted_iota(jnp.int32, (width, tm), 0)
        reach = k // (i + 1)
        if reach < width:
            sc = jnp.where(sub < reach, sc, -jnp.inf)
        scores.append(sc)
        flat.append(fl)
        expert.append(ex)
    scores.append(s1[k // 2:, :] + s2[0:1, :])
    expert.append(i1[k // 2:, :] * N_KEYS + i2[0:1, :])
    flat.append((k // 2 + sub) * k)
    return jnp.concatenate(scores, axis=0), jnp.concatenate(flat, axis=0), jnp.concatenate(expert, axis=0)


def _route_kernel(x_ref, g_ref, wq_ref, keys_ref, after_ref, h_ref, idx_ref, gate_ref):
    del after_ref
    h = _rms(x_ref[...], g_ref[...])
    h_ref[...] = h
    q = jnp.dot(h.astype(BF16), wq_ref[...], preferred_element_type=F32).astype(BF16)
    idx_rows, gate_rows = [], []
    for head in range(PEER_HEADS):
        tops = []
        for part in range(2):
            col = (head * 2 + part) * QUERY_HALF
            s = lax.dot_general(keys_ref[head, part], q[:, col:col + QUERY_HALF],
                                (((1,), (1,)), ((), ())), preferred_element_type=F32)
            tops.append(_topk_axis0(s, PEER_TOPK))
        (s1, i1), (s2, i2) = tops
        cand, flat_ids, cand_idx = _pair_candidates(s1, i1, s2, i2)
        g_s, e_idx = _topk_axis0(cand, PEER_TOPK, ids=flat_ids, payload=cand_idx)
        e = jnp.exp(g_s - jnp.max(g_s, axis=0, keepdims=True))
        gate_rows.append(e / jnp.sum(e, axis=0, keepdims=True))
        idx_rows.append(e_idx * ROW_WORDS)
    idx_ref[...] = jnp.concatenate(idx_rows, axis=0).T
    gate_ref[...] = jnp.concatenate(gate_rows, axis=0).T


def _route(x, gain, w_query, sub_keys, first_tok, n_tok, after, tm=128):
    t, d = x.shape
    nq = w_query.shape[1]
    first_blk = first_tok // tm
    return pl.pallas_call(
        _route_kernel,
        grid=(n_tok // tm,),
        in_specs=[pl.BlockSpec((tm, d), lambda i: (i + first_blk, 0)),
                  pl.BlockSpec((1, d), lambda i: (0, 0)),
                  pl.BlockSpec((d, nq), lambda i: (0, 0)),
                  pl.BlockSpec((PEER_HEADS, 2, N_KEYS, QUERY_HALF), lambda i: (0, 0, 0, 0)),
                  pl.BlockSpec(after.shape, lambda i: (0, 0))],
        out_specs=[pl.BlockSpec((tm, d), lambda i: (i, 0)),
                   pl.BlockSpec((tm, SLOTS), lambda i: (i, 0)),
                   pl.BlockSpec((tm, SLOTS), lambda i: (i, 0))],
        out_shape=[jax.ShapeDtypeStruct((n_tok, d), F32),
                   jax.ShapeDtypeStruct((n_tok, SLOTS), jnp.int32),
                   jax.ShapeDtypeStruct((n_tok, SLOTS), F32)],
        compiler_params=pltpu.CompilerParams(dimension_semantics=("arbitrary",), vmem_limit_bytes=VMEM_LIMIT),
        name="peer_route",
    )(x, gain.reshape(1, d), w_query, sub_keys, after)


def _pack_kernel(t_ref, tiles_ref, *rows_ref):
    x = t_ref[...]
    half = x.shape[1] // 2
    hi = pltpu.bitcast(x[:, :half].astype(BF16).astype(F32), jnp.int32)
    lo = pltpu.bitcast(x[:, half:].astype(BF16).astype(F32), jnp.int32)
    words = hi | lax.shift_right_logical(lo, 16)
    for ref in rows_ref:
        ref[...] = words
    for s in range(ROW_WORDS):
        tiles_ref[pl.ds(s, x.shape[0], stride=ROW_WORDS), :] = words[:, s * LANES:(s + 1) * LANES]


def _pack_table(tabs, layer, with_rows, rows_per_step=256):
    _, ne, d = tabs.shape
    out_specs = [pl.BlockSpec((rows_per_step * ROW_WORDS, LANES), lambda i: (i, 0))]
    out_shape = [jax.ShapeDtypeStruct((ne * ROW_WORDS, LANES), jnp.int32)]
    if with_rows:
        out_specs.append(pl.BlockSpec((rows_per_step, d // 2), lambda i: (i, 0)))
        out_shape.append(jax.ShapeDtypeStruct((ne, d // 2), jnp.int32))
    return pl.pallas_call(
        _pack_kernel,
        grid=(ne // rows_per_step,),
        in_specs=[pl.BlockSpec((None, rows_per_step, d), lambda i: (layer, i, 0))],
        out_specs=out_specs,
        out_shape=out_shape,
        compiler_params=pltpu.CompilerParams(dimension_semantics=("arbitrary",), vmem_limit_bytes=VMEM_LIMIT),
        name="pack_table",
    )(tabs)


def _load_table_once(tab_hbm, tab, sem):
    @pl.when(pl.program_id(0) == 0)
    def _():
        cp = pltpu.make_async_copy(tab_hbm, tab, sem)
        cp.start()
        cp.wait()


def _gather_pair(tab, off_a, off_b):
    ra = tab[pl.ds(pl.multiple_of(off_a, ROW_WORDS), ROW_WORDS), :]
    rb = tab[pl.ds(pl.multiple_of(off_b, ROW_WORDS), ROW_WORDS), :]
    words = jnp.concatenate([ra, rb], axis=0)
    hi = pltpu.bitcast(words & jnp.int32(-65536), F32)
    lo = pltpu.bitcast(words << 16, F32)
    return hi, lo


TOKEN_UNROLL = 8


def _expert_in_kernel(*refs):
    idx_refs = refs[:TOKEN_UNROLL]
    h_ref, gate_ref, tab_hbm, w_ref, tab, sem, a_ref = refs[TOKEN_UNROLL:]
    _load_table_once(tab_hbm, tab, sem)
    sub = lax.broadcasted_iota(jnp.int32, (SUBLANES, LANES), 0)
    lane = lax.broadcasted_iota(jnp.int32, (SUBLANES, LANES), 1)
    own_half = (sub >= ROW_WORDS) == ((lane & 1) == 1)
    tb = h_ref.shape[0]

    def step(i, carry):
        xs = []
        for u in range(TOKEN_UNROLL):
            x = h_ref[i * TOKEN_UNROLL + u]
            xs.append((jnp.concatenate([x[0:ROW_WORDS], x[0:ROW_WORDS]], axis=0),
                       jnp.concatenate([x[ROW_WORDS:], x[ROW_WORDS:]], axis=0)))
        accs = [jnp.zeros((SUBLANES, LANES), F32) for _ in range(TOKEN_UNROLL)]
        for p in range(SLOTS // 2):
            for u in range(TOKEN_UNROLL):
                hi, lo = _gather_pair(tab, idx_refs[u][i, 2 * p], idx_refs[u][i, 2 * p + 1])
                part = jnp.sum(hi * xs[u][0] + lo * xs[u][1], axis=1, keepdims=True)
                accs[u] = jnp.where((lane >> 1) == p, part, accs[u])
        for u in range(TOKEN_UNROLL):
            a_ref[pl.ds(i * TOKEN_UNROLL + u, 1), :] = jnp.sum(jnp.where(own_half, accs[u], 0.0),
                                                                axis=0, keepdims=True)
        return carry

    lax.fori_loop(0, tb // TOKEN_UNROLL, step, 0)
    a = a_ref[...]
    w_ref[...] = gate_ref[...] * (0.5 * a * (1.0 + lax.erf(a * (1.0 / math.sqrt(2.0)))))


def _expert_out_kernel(*refs):
    idx_refs = refs[:TOKEN_UNROLL]
    w_ref, x_ref, tab_hbm, o_ref, tab, sem, wb_ref = refs[TOKEN_UNROLL:]
    _load_table_once(tab_hbm, tab, sem)
    sub = lax.broadcasted_iota(jnp.int32, (SUBLANES, LANES), 0)
    lower = sub < ROW_WORDS
    tb = x_ref.shape[0]

    def step(i, carry):
        for u in range(TOKEN_UNROLL):
            row = w_ref[pl.ds(i * TOKEN_UNROLL + u, 1), :]
            wb_ref[u] = jnp.broadcast_to(row, (SLOTS, LANES)).T
        acc_h = [jnp.zeros((SUBLANES, LANES), F32) for _ in range(TOKEN_UNROLL)]
        acc_l = [jnp.zeros((SUBLANES, LANES), F32) for _ in range(TOKEN_UNROLL)]
        for p in range(SLOTS // 2):
            ka, kb = 2 * p, 2 * p + 1
            for u in range(TOKEN_UNROLL):
                hi, lo = _gather_pair(tab, idx_refs[u][i, ka], idx_refs[u][i, kb])
                wa = jnp.broadcast_to(wb_ref[u, ka:ka + 1, :], (SUBLANES, LANES))
                wb = jnp.broadcast_to(wb_ref[u, kb:kb + 1, :], (SUBLANES, LANES))
                wt = jnp.where(lower, wa, wb)
                acc_h[u] = acc_h[u] + hi * wt
                acc_l[u] = acc_l[u] + lo * wt
        for u in range(TOKEN_UNROLL):
            t = i * TOKEN_UNROLL + u
            ah = acc_h[u] + pltpu.roll(acc_h[u], ROW_WORDS, 0)
            al = acc_l[u] + pltpu.roll(acc_l[u], ROW_WORDS, 0)
            o_ref[t] = x_ref[t] + jnp.where(lower, ah, al)
        return carry

    lax.fori_loop(0, tb // TOKEN_UNROLL, step, 0)


def _expert_specs(tb):
    un = TOKEN_UNROLL
    smem_blks = [pl.BlockSpec((None, tb // un, SLOTS), functools.partial(lambda u, i: (u, i, 0), u),
                              memory_space=pltpu.SMEM) for u in range(un)]
    vmem_blk = pl.BlockSpec((tb, SLOTS), lambda i: (i, 0))
    tok_blk = pl.BlockSpec((tb, D_MODEL // LANES, LANES), lambda i: (i, 0, 0))
    params = pltpu.CompilerParams(dimension_semantics=("arbitrary",), vmem_limit_bytes=VMEM_LIMIT)
    return smem_blks, vmem_blk, tok_blk, params


def _split_offsets(idx):
    n = idx.shape[0]
    return idx.reshape(n // TOKEN_UNROLL, TOKEN_UNROLL, SLOTS).transpose(1, 0, 2)


def _expert_in(h, idx_split, gate, u_packed, tb=128):
    n, d = h.shape
    smem_blks, vmem_blk, tok_blk, params = _expert_specs(tb)
    return pl.pallas_call(
        _expert_in_kernel,
        grid=(n // tb,),
        in_specs=smem_blks + [tok_blk, vmem_blk, pl.BlockSpec(memory_space=pl.ANY)],
        out_specs=vmem_blk,
        out_shape=jax.ShapeDtypeStruct((n, SLOTS), F32),
        scratch_shapes=[pltpu.VMEM(u_packed.shape, jnp.int32), pltpu.SemaphoreType.DMA, pltpu.VMEM((tb, SLOTS), F32)],
        compiler_params=params,
        name="peer_expert_in",
    )(*([idx_split] * TOKEN_UNROLL), h.reshape(n, d // LANES, LANES), gate, u_packed)


def _expert_out(x, idx_split, w, v_packed, tb=128):
    t, d = x.shape
    n = w.shape[0]
    smem_blks, vmem_blk, tok_blk, params = _expert_specs(tb)
    out = pl.pallas_call(
        _expert_out_kernel,
        grid=(n // tb,),
        in_specs=smem_blks + [vmem_blk, tok_blk, pl.BlockSpec(memory_space=pl.ANY)],
        out_specs=tok_blk,
        out_shape=jax.ShapeDtypeStruct((n, d // LANES, LANES), F32),
        scratch_shapes=[pltpu.VMEM(v_packed.shape, jnp.int32), pltpu.SemaphoreType.DMA,
                        pltpu.VMEM((TOKEN_UNROLL, SLOTS, LANES), F32)],
        compiler_params=params,
        name="peer_expert_out",
    )(*([idx_split] * TOKEN_UNROLL), w, x.reshape(t, d // LANES, LANES), v_packed)
    return out.reshape(n, d)


SC_LANES = 16
SC_WORKERS = 32
SC_CHUNK = 32
SC_GROUP = 8
SC_TOKENS = 16640


def _sc_params():
    cp = pltpu.CompilerParams()
    if "needs_layout_passes" in pltpu.CompilerParams.__dataclass_fields__:
        cp = dataclasses.replace(cp, needs_layout_passes=False)
    return cp


def _sc_expert_out(table_words, idx, w, x, first_tok):
    d = x.shape[1]
    n_tok = w.shape[0] // SLOTS
    per = n_tok // SC_WORKERS
    words = d // 2
    nq = words // SC_LANES // 2
    nchunk = SLOTS // SC_CHUNK
    group_chunks = SC_GROUP * nchunk
    mesh = plsc.VectorSubcoreMesh(core_axis_name="c", subcore_axis_name="s")

    @functools.partial(
        pl.kernel, mesh=mesh,
        out_type=jax.ShapeDtypeStruct((n_tok, d), F32),
        scratch_types=[pltpu.VMEM((SC_GROUP * SLOTS,), jnp.int32), pltpu.VMEM((SC_GROUP * SLOTS,), F32),
                       pltpu.VMEM((2, SC_CHUNK, words), jnp.int32), pltpu.VMEM((SC_GROUP, d), F32),
                       pltpu.SemaphoreType.DMA, pltpu.SemaphoreType.DMA],
        compiler_params=_sc_params(),
        name="peer_expert_out_sc",
    )
    def body(tab_hbm, idx_hbm, w_hbm, x_hbm, o_hbm, idx_v, w_v, rows_v, y_v, sem0, sem1):
        base = (lax.axis_index("s") * 2 + lax.axis_index("c")) * per
        zero = jnp.zeros((SC_LANES,), jnp.int32)
        himask = jnp.full((SC_LANES,), -65536, jnp.int32)
        sems = (sem0, sem1)

        def gather(k, b):
            off = pl.multiple_of(k * SC_CHUNK, SC_CHUNK)
            return pltpu.make_async_copy(tab_hbm.at[idx_v.at[pl.ds(off, SC_CHUNK)]], rows_v.at[b], sems[b])

        def accumulate(k, b):
            tok = k // nchunk
            for q in range(2):
                first = q * nq
                acc0 = (tuple(y_v[tok, pl.ds((first + j) * SC_LANES, SC_LANES)] for j in range(nq))
                        + tuple(y_v[tok, pl.ds(words + (first + j) * SC_LANES, SC_LANES)] for j in range(nq)))

                def row_body(r, accs):
                    ws = plsc.load_gather(w_v, [zero + (k * SC_CHUNK + r)])
                    hi_acc, lo_acc = [], []
                    for j in range(nq):
                        wv = rows_v[b, r, pl.ds((first + j) * SC_LANES, SC_LANES)]
                        hi_acc.append(accs[j] + plsc.bitcast(wv & himask, F32) * ws)
                        lo_acc.append(accs[nq + j] + plsc.bitcast(wv << 16, F32) * ws)
                    return tuple(hi_acc) + tuple(lo_acc)

                accs = lax.fori_loop(0, SC_CHUNK, row_body, acc0)
                for j in range(nq):
                    y_v[tok, pl.ds((first + j) * SC_LANES, SC_LANES)] = accs[j]
                    y_v[tok, pl.ds(words + (first + j) * SC_LANES, SC_LANES)] = accs[nq + j]

        @pl.loop(0, per // SC_GROUP)
        def _(g):
            t0 = pl.multiple_of(base + g * SC_GROUP, SC_GROUP)
            pltpu.sync_copy(idx_hbm.at[pl.ds(t0 * SLOTS, SC_GROUP * SLOTS)], idx_v)
            pltpu.sync_copy(w_hbm.at[pl.ds(t0 * SLOTS, SC_GROUP * SLOTS)], w_v)
            pltpu.sync_copy(x_hbm.at[pl.ds(first_tok + t0, SC_GROUP)], y_v)
            gather(0, 0).start()

            @pl.loop(0, group_chunks // 2)
            def _(kk):
                k0 = 2 * kk
                gather(k0 + 1, 1).start()
                gather(k0, 0).wait()
                accumulate(k0, 0)

                @pl.when(k0 + 2 < group_chunks)
                def _():
                    gather(k0 + 2, 0).start()
                gather(k0 + 1, 1).wait()
                accumulate(k0 + 1, 1)

            pltpu.sync_copy(y_v, o_hbm.at[pl.ds(t0, SC_GROUP)])

    return body(table_words, idx, w, x)


def _peer(x, gain, w_query, sub_keys, experts_u, experts_v, layer):
    t = x.shape[0]
    t_tc = t - SC_TOKENS
    wq, keys = w_query.astype(BF16), sub_keys.astype(BF16)
    (u_packed,) = _pack_table(experts_u, layer, with_rows=False)
    v_packed, v_rows = _pack_table(experts_v, layer, with_rows=True)

    h, idx, gate = _route(x, gain, wq, keys, t_tc, SC_TOKENS, v_rows[:SUBLANES])
    w_sc = _expert_in(h, _split_offsets(idx), gate, u_packed)
    out_sc = _sc_expert_out(v_rows, (idx // ROW_WORDS).reshape(-1), w_sc.reshape(-1), x, t_tc)

    h, idx, gate = _route(x, gain, wq, keys, 0, t_tc, w_sc[:SUBLANES])
    idx_split = _split_offsets(idx)
    w = _expert_in(h, idx_split, gate, u_packed)
    out_tc = _expert_out(x, idx_split, w, v_packed)
    return jnp.concatenate([out_tc, out_sc], axis=0)


def kernel(x, norm_mix, norm_ffn, attn_w_qkv, attn_q_norm, attn_k_norm, attn_sinks, attn_w_o, conv_w_in, conv_w, conv_w_out, peer_w_query, peer_sub_keys, peer_u, peer_v):
    batch, seq, d = x.shape
    xt = x.reshape(batch * seq, d)
    for i in range(norm_mix.shape[0]):
        j = i // 2
        if i % 2 == 0:
            qkv = _norm_matmul(xt, norm_mix[i], attn_w_qkv[j].astype(BF16))
            o = _attention(qkv, attn_q_norm[j], attn_k_norm[j], attn_sinks[j], batch, seq)
            xt = _matmul_residual(o, attn_w_o[j].astype(BF16), xt)
        else:
            xt = _conv_mixer(xt, norm_mix[i], conv_w_in[j], conv_w[j], conv_w_out[j], batch, seq)
        xt = _peer(xt, norm_ffn[i], peer_w_query[i], peer_sub_keys[i], peer_u, peer_v, i)
    return xt.reshape(batch, seq, d)
```

```python
import dataclasses
import functools
import math

import jax
import jax.numpy as jnp
from jax import lax
from jax.experimental import pallas as pl
from jax.experimental.pallas import tpu as pltpu
from jax.experimental.pallas import tpu_sc as plsc

D_MODEL = 1024
RMS_EPS = 1e-6

HEAD_DIM = 64
N_Q_HEADS = 16
N_KV_HEADS = 4
GROUP = N_Q_HEADS // N_KV_HEADS
WINDOW = 128
ROT_DIM = HEAD_DIM // 4
ROPE_THETA = 500000.0
Q_COLS = N_Q_HEADS * HEAD_DIM
KV_COLS = N_KV_HEADS * HEAD_DIM
NEG_INF = -1e30

CONV_WIDTH = 3

PEER_HEADS = 8
N_KEYS = 128
N_EXPERTS = N_KEYS * N_KEYS
PEER_TOPK = 16
QUERY_HALF = 128
SLOTS = PEER_HEADS * PEER_TOPK

LANES = 128
SUBLANES = 8
ROW_WORDS = D_MODEL // 2 // LANES
VMEM_LIMIT = 48 * 1024 * 1024

BF16 = jnp.bfloat16
F32 = jnp.float32


def _rms(x, gain):
    return x * lax.rsqrt(jnp.mean(x * x, axis=-1, keepdims=True) + RMS_EPS) * gain


def _norm_matmul_kernel(x_ref, g_ref, w_ref, o_ref):
    h = _rms(x_ref[...], g_ref[...])
    o_ref[...] = jnp.dot(h.astype(BF16), w_ref[...], preferred_element_type=F32)


def _norm_matmul(x, gain, w, tm=512):
    t, d = x.shape
    n = w.shape[1]
    return pl.pallas_call(
        _norm_matmul_kernel,
        grid=(t // tm,),
        in_specs=[pl.BlockSpec((tm, d), lambda i: (i, 0)),
                  pl.BlockSpec((1, d), lambda i: (0, 0)),
                  pl.BlockSpec((d, n), lambda i: (0, 0))],
        out_specs=pl.BlockSpec((tm, n), lambda i: (i, 0)),
        out_shape=jax.ShapeDtypeStruct((t, n), F32),
        compiler_params=pltpu.CompilerParams(dimension_semantics=("arbitrary",), vmem_limit_bytes=VMEM_LIMIT),
        name="norm_matmul",
    )(x, gain.reshape(1, d), w)


def _matmul_residual_kernel(a_ref, w_ref, r_ref, o_ref):
    o_ref[...] = r_ref[...] + jnp.dot(a_ref[...].astype(BF16), w_ref[...], preferred_element_type=F32)


def _matmul_residual(a, w, res, tm=512):
    t, k = a.shape
    n = w.shape[1]
    return pl.pallas_call(
        _matmul_residual_kernel,
        grid=(t // tm,),
        in_specs=[pl.BlockSpec((tm, k), lambda i: (i, 0)),
                  pl.BlockSpec((k, n), lambda i: (0, 0)),
                  pl.BlockSpec((tm, n), lambda i: (i, 0))],
        out_specs=pl.BlockSpec((tm, n), lambda i: (i, 0)),
        out_shape=jax.ShapeDtypeStruct((t, n), F32),
        compiler_params=pltpu.CompilerParams(dimension_semantics=("arbitrary",), vmem_limit_bytes=VMEM_LIMIT),
        name="matmul_residual",
    )(a, w, res)


def _rope_tables(seq):
    half = ROT_DIM // 2
    freqs = ROPE_THETA ** (-jnp.arange(0, ROT_DIM, 2, dtype=F32) / ROT_DIM)
    ang = jnp.arange(seq, dtype=F32)[:, None] * freqs[None, :]
    cos, sin = jnp.cos(ang), jnp.sin(ang)
    ones = jnp.ones((seq, HEAD_DIM - ROT_DIM), F32)
    zeros = jnp.zeros((seq, HEAD_DIM - ROT_DIM), F32)
    zh = jnp.zeros((seq, half), F32)
    c = jnp.concatenate([cos, cos, ones], axis=1)
    s_next = jnp.concatenate([-sin, zh, zeros], axis=1)
    s_prev = jnp.concatenate([zh, sin, zeros], axis=1)
    return jnp.stack([jnp.tile(c, (1, 2)), jnp.tile(s_next, (1, 2)), jnp.tile(s_prev, (1, 2))])


def _head_norm_rope(x, gain2, rope, lo):
    sq = x * x
    s_lo = jnp.sum(jnp.where(lo, sq, 0.0), axis=1, keepdims=True)
    s_hi = jnp.sum(jnp.where(lo, 0.0, sq), axis=1, keepdims=True)
    ms = jnp.where(lo, s_lo, s_hi) * (1.0 / HEAD_DIM)
    xn = x * lax.rsqrt(ms + RMS_EPS) * gain2
    half = ROT_DIM // 2
    return xn * rope[0] + pltpu.roll(xn, LANES - half, 1) * rope[1] + pltpu.roll(xn, half, 1) * rope[2]


def _attn_kernel(sinks_ref, q_ref, kc_ref, kp_ref, vc_ref, vp_ref, rc_ref, rp_ref, qg_ref, kg_ref, o_ref):
    n = pl.program_id(1)
    lo = lax.broadcasted_iota(jnp.int32, (WINDOW, LANES), 1) < HEAD_DIM
    rope_c = rc_ref[...]
    rope_p = rp_ref[...]
    qg = qg_ref[...]
    kg = kg_ref[...]

    rows = GROUP * WINDOW
    qi = lax.broadcasted_iota(jnp.int32, (rows, 2 * WINDOW), 0) & (WINDOW - 1)
    ki = lax.broadcasted_iota(jnp.int32, (rows, 2 * WINDOW), 1)
    rel = WINDOW + qi - ki
    valid = (rel >= 0) & (rel < WINDOW) & ((n > 0) | (ki >= WINDOW))
    head_of_row = lax.broadcasted_iota(jnp.int32, (rows, 1), 0) // WINDOW
    scale = 1.0 / math.sqrt(HEAD_DIM)

    q2 = [_head_norm_rope(q_ref[:, c * LANES:(c + 1) * LANES], qg, rope_c, lo).astype(BF16)
          for c in range(Q_COLS // LANES)]
    for c in range(KV_COLS // LANES):
        cols = slice(c * LANES, (c + 1) * LANES)
        kprev = _head_norm_rope(kp_ref[:, cols], kg, rope_p, lo)
        kcur = _head_norm_rope(kc_ref[:, cols], kg, rope_c, lo)
        kfull = jnp.concatenate([kprev, kcur], axis=0).astype(BF16)
        vfull = jnp.concatenate([vp_ref[:, cols], vc_ref[:, cols]], axis=0).astype(BF16)
        for hh in range(LANES // HEAD_DIM):
            h = (LANES // HEAD_DIM) * c + hh
            kh = kfull[:, hh * HEAD_DIM:(hh + 1) * HEAD_DIM]
            vh = vfull[:, hh * HEAD_DIM:(hh + 1) * HEAD_DIM]
            heads = [GROUP * h + g for g in range(GROUP)]
            q4 = jnp.concatenate([q2[j // 2][:, (j % 2) * HEAD_DIM:(j % 2 + 1) * HEAD_DIM] for j in heads], axis=0)
            s = lax.dot_general(q4, kh, (((1,), (1,)), ((), ())), preferred_element_type=F32) * scale
            s = jnp.where(valid, s, NEG_INF)
            sink = jnp.zeros((rows, 1), F32)
            for g, j in enumerate(heads):
                sink = jnp.where(head_of_row == g, sinks_ref[j], sink)
            m = jnp.maximum(jnp.max(s, axis=1, keepdims=True), sink)
            p = jnp.exp(s - m)
            denom = jnp.sum(p, axis=1, keepdims=True) + jnp.exp(sink - m)
            o = jnp.dot(p.astype(BF16), vh, preferred_element_type=F32) / denom
            for g, j in enumerate(heads):
                o_ref[:, j * HEAD_DIM:(j + 1) * HEAD_DIM] = o[g * WINDOW:(g + 1) * WINDOW]


def _attention(qkv, q_gain, k_gain, sinks, batch, seq):
    t = batch * seq
    nb = seq // WINDOW
    rope = _rope_tables(seq)
    kcol = Q_COLS // KV_COLS
    cur = lambda b, n: (b * nb + n, 0)
    kcur = lambda b, n: (b * nb + n, kcol)
    kprev = lambda b, n: (b * nb + jnp.maximum(n - 1, 0), kcol)
    vcur = lambda b, n: (b * nb + n, kcol + 1)
    vprev = lambda b, n: (b * nb + jnp.maximum(n - 1, 0), kcol + 1)
    return pl.pallas_call(
        _attn_kernel,
        grid=(batch, nb),
        in_specs=[pl.BlockSpec(memory_space=pltpu.SMEM),
                  pl.BlockSpec((WINDOW, Q_COLS), cur),
                  pl.BlockSpec((WINDOW, KV_COLS), kcur),
                  pl.BlockSpec((WINDOW, KV_COLS), kprev),
                  pl.BlockSpec((WINDOW, KV_COLS), vcur),
                  pl.BlockSpec((WINDOW, KV_COLS), vprev),
                  pl.BlockSpec((3, WINDOW, LANES), lambda b, n: (0, n, 0)),
                  pl.BlockSpec((3, WINDOW, LANES), lambda b, n: (0, jnp.maximum(n - 1, 0), 0)),
                  pl.BlockSpec((1, LANES), lambda b, n: (0, 0)),
                  pl.BlockSpec((1, LANES), lambda b, n: (0, 0))],
        out_specs=pl.BlockSpec((WINDOW, Q_COLS), cur),
        out_shape=jax.ShapeDtypeStruct((t, Q_COLS), F32),
        compiler_params=pltpu.CompilerParams(dimension_semantics=("arbitrary", "arbitrary"),
                                             vmem_limit_bytes=VMEM_LIMIT),
        name="swa_attention",
    )(sinks, qkv, qkv, qkv, qkv, qkv, rope, rope,
      jnp.tile(q_gain, 2).reshape(1, LANES), jnp.tile(k_gain, 2).reshape(1, LANES))


def _conv_kernel(x_ref, g_ref, win_ref, cw_ref, wout_ref, o_ref, zprev_ref):
    n = pl.program_id(1)
    d = D_MODEL

    @pl.when(n == 0)
    def _():
        zprev_ref[...] = jnp.zeros_like(zprev_ref)

    x = x_ref[...]
    h = _rms(x, g_ref[...])
    bcu = jnp.dot(h.astype(BF16), win_ref[...], preferred_element_type=F32)
    gate_b = bcu[:, :d]
    z = bcu[:, d:2 * d] * bcu[:, 2 * d:]
    tm = z.shape[0]
    row = lax.broadcasted_iota(jnp.int32, z.shape, 0)
    prev = zprev_ref[...]
    p_last = prev[SUBLANES - 1:SUBLANES, :]
    p_last2 = prev[SUBLANES - 2:SUBLANES - 1, :]
    z1 = jnp.where(row == 0, p_last, pltpu.roll(z, 1, 0))
    z2 = jnp.where(row == 0, p_last2, jnp.where(row == 1, p_last, pltpu.roll(z, 2, 0)))
    cw = cw_ref[...]
    conv = cw[0:1, :] * z2 + cw[1:2, :] * z1 + cw[2:3, :] * z
    zprev_ref[...] = z[tm - SUBLANES:, :]
    o_ref[...] = x + jnp.dot((gate_b * conv).astype(BF16), wout_ref[...], preferred_element_type=F32)


def _conv_mixer(x, gain, w_in, conv_w, w_out, batch, seq, tm=256):
    t, d = x.shape
    nblk = seq // tm
    blk = lambda b, n: (b * nblk + n, 0)
    const = lambda b, n: (0, 0)
    return pl.pallas_call(
        _conv_kernel,
        grid=(batch, nblk),
        in_specs=[pl.BlockSpec((tm, d), blk),
                  pl.BlockSpec((1, d), const),
                  pl.BlockSpec((d, 3 * d), const),
                  pl.BlockSpec((CONV_WIDTH, d), const),
                  pl.BlockSpec((d, d), const)],
        out_specs=pl.BlockSpec((tm, d), blk),
        out_shape=jax.ShapeDtypeStruct((t, d), F32),
        scratch_shapes=[pltpu.VMEM((SUBLANES, d), F32)],
        compiler_params=pltpu.CompilerParams(dimension_semantics=("arbitrary", "arbitrary"),
                                             vmem_limit_bytes=VMEM_LIMIT),
        name="conv_mixer",
    )(x, gain.reshape(1, d), w_in.astype(BF16), conv_w, w_out.astype(BF16))


def _topk_axis0(s, k, ids=None, payload=None):
    n, tm = s.shape
    if ids is None:
        ids = lax.broadcasted_iota(jnp.int32, (n, tm), 0)
    krow = lax.broadcasted_iota(jnp.int32, (k, tm), 0)
    vals = jnp.zeros((k, tm), F32)
    picks = jnp.zeros((k, tm), jnp.int32)
    for r in range(k):
        m = jnp.max(s, axis=0, keepdims=True)
        pos = jnp.min(jnp.where(s == m, ids, jnp.iinfo(jnp.int32).max), axis=0, keepdims=True)
        sel = ids == pos
        if payload is None:
            picked = pos
        else:
            picked = jnp.sum(jnp.where(sel, payload, 0), axis=0, keepdims=True)
        vals = jnp.where(krow == r, m, vals)
        picks = jnp.where(krow == r, picked, picks)
        s = jnp.where(sel, -jnp.inf, s)
    return vals, picks


def _pair_candidates(s1, i1, s2, i2):
    k, tm = s1.shape
    sub = lax.broadcasted_iota(jnp.int32, (SUBLANES, tm), 0)
    scores, flat, expert = [], [], []
    for i in range(k // 2):
        width = k if i == 0 else SUBLANES
        sc = s1[i:i + 1, :] + s2[0:width, :]
        ex = i1[i:i + 1, :] * N_KEYS + i2[0:width, :]
        fl = i * k + lax.broadcasted_iota(jnp.int32, (width, tm), 0)
        reach = k // (i + 1)
        if reach < width:
            sc = jnp.where(sub < reach, sc, -jnp.inf)
        scores.append(sc)
        flat.append(fl)
        expert.append(ex)
    scores.append(s1[k // 2:, :] + s2[0:1, :])
    expert.append(i1[k // 2:, :] * N_KEYS + i2[0:1, :])
    flat.append((k // 2 + sub) * k)
    return jnp.concatenate(scores, axis=0), jnp.concatenate(flat, axis=0), jnp.concatenate(expert, axis=0)


def _route_kernel(x_ref, g_ref, wq_ref, keys_ref, after_ref, h_ref, idx_ref, gate_ref):
    del after_ref
    h = _rms(x_ref[...], g_ref[...])
    h_ref[...] = h
    q = jnp.dot(h.astype(BF16), wq_ref[...], preferred_element_type=F32).astype(BF16)
    idx_rows, gate_rows = [], []
    for head in range(PEER_HEADS):
        tops = []
        for part in range(2):
            col = (head * 2 + part) * QUERY_HALF
            s = lax.dot_general(keys_ref[head, part], q[:, col:col + QUERY_HALF],
                                (((1,), (1,)), ((), ())), preferred_element_type=F32)
            tops.append(_topk_axis0(s, PEER_TOPK))
        (s1, i1), (s2, i2) = tops
        cand, flat_ids, cand_idx = _pair_candidates(s1, i1, s2, i2)
        g_s, e_idx = _topk_axis0(cand, PEER_TOPK, ids=flat_ids, payload=cand_idx)
        e = jnp.exp(g_s - jnp.max(g_s, axis=0, keepdims=True))
        gate_rows.append(e / jnp.sum(e, axis=0, keepdims=True))
        idx_rows.append(e_idx * ROW_WORDS)
    idx_ref[...] = jnp.concatenate(idx_rows, axis=0).T
    gate_ref[...] = jnp.concatenate(gate_rows, axis=0).T


def _route(x, gain, w_query, sub_keys, first_tok, n_tok, after, tm=128):
    t, d = x.shape
    nq = w_query.shape[1]
    first_blk = first_tok // tm
    return pl.pallas_call(
        _route_kernel,
        grid=(n_tok // tm,),
        in_specs=[pl.BlockSpec((tm, d), lambda i: (i + first_blk, 0)),
                  pl.BlockSpec((1, d), lambda i: (0, 0)),
                  pl.BlockSpec((d, nq), lambda i: (0, 0)),
                  pl.BlockSpec((PEER_HEADS, 2, N_KEYS, QUERY_HALF), lambda i: (0, 0, 0, 0)),
                  pl.BlockSpec(after.shape, lambda i: (0, 0))],
        out_specs=[pl.BlockSpec((tm, d), lambda i: (i, 0)),
                   pl.BlockSpec((tm, SLOTS), lambda i: (i, 0)),
                   pl.BlockSpec((tm, SLOTS), lambda i: (i, 0))],
        out_shape=[jax.ShapeDtypeStruct((n_tok, d), F32),
                   jax.ShapeDtypeStruct((n_tok, SLOTS), jnp.int32),
                   jax.ShapeDtypeStruct((n_tok, SLOTS), F32)],
        compiler_params=pltpu.CompilerParams(dimension_semantics=("arbitrary",), vmem_limit_bytes=VMEM_LIMIT),
        name="peer_route",
    )(x, gain.reshape(1, d), w_query, sub_keys, after)


def _pack_kernel(t_ref, tiles_ref, *rows_ref):
    x = t_ref[...]
    half = x.shape[1] // 2
    hi = pltpu.bitcast(x[:, :half].astype(BF16).astype(F32), jnp.int32)
    lo = pltpu.bitcast(x[:, half:].astype(BF16).astype(F32), jnp.int32)
    words = hi | lax.shift_right_logical(lo, 16)
    for ref in rows_ref:
        ref[...] = words
    for s in range(ROW_WORDS):
        tiles_ref[pl.ds(s, x.shape[0], stride=ROW_WORDS), :] = words[:, s * LANES:(s + 1) * LANES]


def _pack_table(tabs, layer, with_rows, rows_per_step=256):
    _, ne, d = tabs.shape
    out_specs = [pl.BlockSpec((rows_per_step * ROW_WORDS, LANES), lambda i: (i, 0))]
    out_shape = [jax.ShapeDtypeStruct((ne * ROW_WORDS, LANES), jnp.int32)]
    if with_rows:
        out_specs.append(pl.BlockSpec((rows_per_step, d // 2), lambda i: (i, 0)))
        out_shape.append(jax.ShapeDtypeStruct((ne, d // 2), jnp.int32))
    return pl.pallas_call(
        _pack_kernel,
        grid=(ne // rows_per_step,),
        in_specs=[pl.BlockSpec((None, rows_per_step, d), lambda i: (layer, i, 0))],
        out_specs=out_specs,
        out_shape=out_shape,
        compiler_params=pltpu.CompilerParams(dimension_semantics=("arbitrary",), vmem_limit_bytes=VMEM_LIMIT),
        name="pack_table",
    )(tabs)


def _load_table_once(tab_hbm, tab, sem):
    @pl.when(pl.program_id(0) == 0)
    def _():
        cp = pltpu.make_async_copy(tab_hbm, tab, sem)
        cp.start()
        cp.wait()


def _gather_pair(tab, off_a, off_b):
    ra = tab[pl.ds(pl.multiple_of(off_a, ROW_WORDS), ROW_WORDS), :]
    rb = tab[pl.ds(pl.multiple_of(off_b, ROW_WORDS), ROW_WORDS), :]
    words = jnp.concatenate([ra, rb], axis=0)
    hi = pltpu.bitcast(words & jnp.int32(-65536), F32)
    lo = pltpu.bitcast(words << 16, F32)
    return hi, lo


TOKEN_UNROLL = 8


def _expert_in_kernel(*refs):
    idx_refs = refs[:TOKEN_UNROLL]
    h_ref, gate_ref, tab_hbm, w_ref, tab, sem, a_ref = refs[TOKEN_UNROLL:]
    _load_table_once(tab_hbm, tab, sem)
    sub = lax.broadcasted_iota(jnp.int32, (SUBLANES, LANES), 0)
    lane = lax.broadcasted_iota(jnp.int32, (SUBLANES, LANES), 1)
    own_half = (sub >= ROW_WORDS) == ((lane & 1) == 1)
    tb = h_ref.shape[0]

    def step(i, carry):
        xs = []
        for u in range(TOKEN_UNROLL):
            x = h_ref[i * TOKEN_UNROLL + u]
            xs.append((jnp.concatenate([x[0:ROW_WORDS], x[0:ROW_WORDS]], axis=0),
                       jnp.concatenate([x[ROW_WORDS:], x[ROW_WORDS:]], axis=0)))
        accs = [jnp.zeros((SUBLANES, LANES), F32) for _ in range(TOKEN_UNROLL)]
        for p in range(SLOTS // 2):
            for u in range(TOKEN_UNROLL):
                hi, lo = _gather_pair(tab, idx_refs[u][i, 2 * p], idx_refs[u][i, 2 * p + 1])
                part = jnp.sum(hi * xs[u][0] + lo * xs[u][1], axis=1, keepdims=True)
                accs[u] = jnp.where((lane >> 1) == p, part, accs[u])
        for u in range(TOKEN_UNROLL):
            a_ref[pl.ds(i * TOKEN_UNROLL + u, 1), :] = jnp.sum(jnp.where(own_half, accs[u], 0.0),
                                                                axis=0, keepdims=True)
        return carry

    lax.fori_loop(0, tb // TOKEN_UNROLL, step, 0)
    a = a_ref[...]
    w_ref[...] = gate_ref[...] * (0.5 * a * (1.0 + lax.erf(a * (1.0 / math.sqrt(2.0)))))


def _expert_out_kernel(*refs):
    idx_refs = refs[:TOKEN_UNROLL]
    w_ref, x_ref, tab_hbm, o_ref, tab, sem, wb_ref = refs[TOKEN_UNROLL:]
    _load_table_once(tab_hbm, tab, sem)
    sub = lax.broadcasted_iota(jnp.int32, (SUBLANES, LANES), 0)
    lower = sub < ROW_WORDS
    tb = x_ref.shape[0]

    def step(i, carry):
        for u in range(TOKEN_UNROLL):
            row = w_ref[pl.ds(i * TOKEN_UNROLL + u, 1), :]
            wb_ref[u] = jnp.broadcast_to(row, (SLOTS, LANES)).T
        acc_h = [jnp.zeros((SUBLANES, LANES), F32) for _ in range(TOKEN_UNROLL)]
        acc_l = [jnp.zeros((SUBLANES, LANES), F32) for _ in range(TOKEN_UNROLL)]
        for p in range(SLOTS // 2):
            ka, kb = 2 * p, 2 * p + 1
            for u in range(TOKEN_UNROLL):
                hi, lo = _gather_pair(tab, idx_refs[u][i, ka], idx_refs[u][i, kb])
                wa = jnp.broadcast_to(wb_ref[u, ka:ka + 1, :], (SUBLANES, LANES))
                wb = jnp.broadcast_to(wb_ref[u, kb:kb + 1, :], (SUBLANES, LANES))
                wt = jnp.where(lower, wa, wb)
                acc_h[u] = acc_h[u] + hi * wt
                acc_l[u] = acc_l[u] + lo * wt
        for u in range(TOKEN_UNROLL):
            t = i * TOKEN_UNROLL + u
            ah = acc_h[u] + pltpu.roll(acc_h[u], ROW_WORDS, 0)
            al = acc_l[u] + pltpu.roll(acc_l[u], ROW_WORDS, 0)
            o_ref[t] = x_ref[t] + jnp.where(lower, ah, al)
        return carry

    lax.fori_loop(0, tb // TOKEN_UNROLL, step, 0)


def _expert_specs(tb):
    un = TOKEN_UNROLL
    smem_blks = [pl.BlockSpec((None, tb // un, SLOTS), functools.partial(lambda u, i: (u, i, 0), u),
                              memory_space=pltpu.SMEM) for u in range(un)]
    vmem_blk = pl.BlockSpec((tb, SLOTS), lambda i: (i, 0))
    tok_blk = pl.BlockSpec((tb, D_MODEL // LANES, LANES), lambda i: (i, 0, 0))
    params = pltpu.CompilerParams(dimension_semantics=("arbitrary",), vmem_limit_bytes=VMEM_LIMIT)
    return smem_blks, vmem_blk, tok_blk, params


def _split_offsets(idx):
    n = idx.shape[0]
    return idx.reshape(n // TOKEN_UNROLL, TOKEN_UNROLL, SLOTS).transpose(1, 0, 2)


def _expert_in(h, idx_split, gate, u_packed, tb=256):
    n, d = h.shape
    smem_blks, vmem_blk, tok_blk, params = _expert_specs(tb)
    return pl.pallas_call(
        _expert_in_kernel,
        grid=(n // tb,),
        in_specs=smem_blks + [tok_blk, vmem_blk, pl.BlockSpec(memory_space=pl.ANY)],
        out_specs=vmem_blk,
        out_shape=jax.ShapeDtypeStruct((n, SLOTS), F32),
        scratch_shapes=[pltpu.VMEM(u_packed.shape, jnp.int32), pltpu.SemaphoreType.DMA, pltpu.VMEM((tb, SLOTS), F32)],
        compiler_params=params,
        name="peer_expert_in",
    )(*([idx_split] * TOKEN_UNROLL), h.reshape(n, d // LANES, LANES), gate, u_packed)


def _expert_out(x, idx_split, w, v_packed, tb=256):
    t, d = x.shape
    n = w.shape[0]
    smem_blks, vmem_blk, tok_blk, params = _expert_specs(tb)
    out = pl.pallas_call(
        _expert_out_kernel,
        grid=(n // tb,),
        in_specs=smem_blks + [vmem_blk, tok_blk, pl.BlockSpec(memory_space=pl.ANY)],
        out_specs=tok_blk,
        out_shape=jax.ShapeDtypeStruct((n, d // LANES, LANES), F32),
        scratch_shapes=[pltpu.VMEM(v_packed.shape, jnp.int32), pltpu.SemaphoreType.DMA,
                        pltpu.VMEM((TOKEN_UNROLL, SLOTS, LANES), F32)],
        compiler_params=params,
        name="peer_expert_out",
    )(*([idx_split] * TOKEN_UNROLL), w, x.reshape(t, d // LANES, LANES), v_packed)
    return out.reshape(n, d)


SC_LANES = 16
SC_WORKERS = 32
SC_CHUNK = 32
SC_GROUP = 8
SC_TOKENS = 16640


def _sc_params():
    cp = pltpu.CompilerParams()
    if "needs_layout_passes" in pltpu.CompilerParams.__dataclass_fields__:
        cp = dataclasses.replace(cp, needs_layout_passes=False)
    return cp


def _sc_expert_out(table_words, idx, w, x, first_tok):
    d = x.shape[1]
    n_tok = w.shape[0] // SLOTS
    per = n_tok // SC_WORKERS
    words = d // 2
    nq = words // SC_LANES // 2
    nchunk = SLOTS // SC_CHUNK
    group_chunks = SC_GROUP * nchunk
    mesh = plsc.VectorSubcoreMesh(core_axis_name="c", subcore_axis_name="s")

    @functools.partial(
        pl.kernel, mesh=mesh,
        out_type=jax.ShapeDtypeStruct((n_tok, d), F32),
        scratch_types=[pltpu.VMEM((SC_GROUP * SLOTS,), jnp.int32), pltpu.VMEM((SC_GROUP * SLOTS,), F32),
                       pltpu.VMEM((2, SC_CHUNK, words), jnp.int32), pltpu.VMEM((SC_GROUP, d), F32),
                       pltpu.SemaphoreType.DMA, pltpu.SemaphoreType.DMA],
        compiler_params=_sc_params(),
        name="peer_expert_out_sc",
    )
    def body(tab_hbm, idx_hbm, w_hbm, x_hbm, o_hbm, idx_v, w_v, rows_v, y_v, sem0, sem1):
        base = (lax.axis_index("s") * 2 + lax.axis_index("c")) * per
        zero = jnp.zeros((SC_LANES,), jnp.int32)
        himask = jnp.full((SC_LANES,), -65536, jnp.int32)
        sems = (sem0, sem1)

        def gather(k, b):
            off = pl.multiple_of(k * SC_CHUNK, SC_CHUNK)
            return pltpu.make_async_copy(tab_hbm.at[idx_v.at[pl.ds(off, SC_CHUNK)]], rows_v.at[b], sems[b])

        def accumulate(k, b):
            tok = k // nchunk
            for q in range(2):
                first = q * nq
                acc0 = (tuple(y_v[tok, pl.ds((first + j) * SC_LANES, SC_LANES)] for j in range(nq))
                        + tuple(y_v[tok, pl.ds(words + (first + j) * SC_LANES, SC_LANES)] for j in range(nq)))

                def row_body(r, accs):
                    ws = plsc.load_gather(w_v, [zero + (k * SC_CHUNK + r)])
                    hi_acc, lo_acc = [], []
                    for j in range(nq):
                        wv = rows_v[b, r, pl.ds((first + j) * SC_LANES, SC_LANES)]
                        hi_acc.append(accs[j] + plsc.bitcast(wv & himask, F32) * ws)
                        lo_acc.append(accs[nq + j] + plsc.bitcast(wv << 16, F32) * ws)
                    return tuple(hi_acc) + tuple(lo_acc)

                accs = lax.fori_loop(0, SC_CHUNK, row_body, acc0)
                for j in range(nq):
                    y_v[tok, pl.ds((first + j) * SC_LANES, SC_LANES)] = accs[j]
                    y_v[tok, pl.ds(words + (first + j) * SC_LANES, SC_LANES)] = accs[nq + j]

        @pl.loop(0, per // SC_GROUP)
        def _(g):
            t0 = pl.multiple_of(base + g * SC_GROUP, SC_GROUP)
            pltpu.sync_copy(idx_hbm.at[pl.ds(t0 * SLOTS, SC_GROUP * SLOTS)], idx_v)
            pltpu.sync_copy(w_hbm.at[pl.ds(t0 * SLOTS, SC_GROUP * SLOTS)], w_v)
            pltpu.sync_copy(x_hbm.at[pl.ds(first_tok + t0, SC_GROUP)], y_v)
            gather(0, 0).start()

            @pl.loop(0, group_chunks // 2)
            def _(kk):
                k0 = 2 * kk
                gather(k0 + 1, 1).start()
                gather(k0, 0).wait()
                accumulate(k0, 0)

                @pl.when(k0 + 2 < group_chunks)
                def _():
                    gather(k0 + 2, 0).start()
                gather(k0 + 1, 1).wait()
                accumulate(k0 + 1, 1)

            pltpu.sync_copy(y_v, o_hbm.at[pl.ds(t0, SC_GROUP)])

    return body(table_words, idx, w, x)


def _peer(x, gain, w_query, sub_keys, experts_u, experts_v, layer):
    t = x.shape[0]
    t_tc = t - SC_TOKENS
    wq, keys = w_query.astype(BF16), sub_keys.astype(BF16)
    (u_packed,) = _pack_table(experts_u, layer, with_rows=False)
    v_packed, v_rows = _pack_table(experts_v, layer, with_rows=True)

    h, idx, gate = _route(x, gain, wq, keys, t_tc, SC_TOKENS, v_rows[:SUBLANES])
    w_sc = _expert_in(h, _split_offsets(idx), gate, u_packed)
    out_sc = _sc_expert_out(v_rows, (idx // ROW_WORDS).reshape(-1), w_sc.reshape(-1), x, t_tc)

    h, idx, gate = _route(x, gain, wq, keys, 0, t_tc, w_sc[:SUBLANES])
    idx_split = _split_offsets(idx)
    w = _expert_in(h, idx_split, gate, u_packed)
    out_tc = _expert_out(x, idx_split, w, v_packed)
    return jnp.concatenate([out_tc, out_sc], axis=0)


def kernel(x, norm_mix, norm_ffn, attn_w_qkv, attn_q_norm, attn_k_norm, attn_sinks, attn_w_o, conv_w_in, conv_w, conv_w_out, peer_w_query, peer_sub_keys, peer_u, peer_v):
    batch, seq, d = x.shape
    xt = x.reshape(batch * seq, d)
    for i in range(norm_mix.shape[0]):
        j = i // 2
        if i % 2 == 0:
            qkv = _norm_matmul(xt, norm_mix[i], attn_w_qkv[j].astype(BF16))
            o = _attention(qkv, attn_q_norm[j], attn_k_norm[j], attn_sinks[j], batch, seq)
            xt = _matmul_residual(o, attn_w_o[j].astype(BF16), xt)
        else:
            xt = _conv_mixer(xt, norm_mix[i], conv_w_in[j], conv_w[j], conv_w_out[j], batch, seq)
        xt = _peer(xt, norm_ffn[i], peer_w_query[i], peer_sub_keys[i], peer_u, peer_v, i)
    return xt.reshape(batch, seq, d)
```

```python
import dataclasses
import functools
import math

import jax
import jax.numpy as jnp
from jax import lax
from jax.experimental import pallas as pl
from jax.experimental.pallas import tpu as pltpu
from jax.experimental.pallas import tpu_sc as plsc

D_MODEL = 1024
RMS_EPS = 1e-6

HEAD_DIM = 64
N_Q_HEADS = 16
N_KV_HEADS = 4
GROUP = N_Q_HEADS // N_KV_HEADS
WINDOW = 128
ROT_DIM = HEAD_DIM // 4
ROPE_THETA = 500000.0
Q_COLS = N_Q_HEADS * HEAD_DIM
KV_COLS = N_KV_HEADS * HEAD_DIM
NEG_INF = -1e30

CONV_WIDTH = 3

PEER_HEADS = 8
N_KEYS = 128
N_EXPERTS = N_KEYS * N_KEYS
PEER_TOPK = 16
QUERY_HALF = 128
SLOTS = PEER_HEADS * PEER_TOPK

LANES = 128
SUBLANES = 8
ROW_WORDS = D_MODEL // 2 // LANES
VMEM_LIMIT = 48 * 1024 * 1024

BF16 = jnp.bfloat16
F32 = jnp.float32


def _rms(x, gain):
    return x * lax.rsqrt(jnp.mean(x * x, axis=-1, keepdims=True) + RMS_EPS) * gain


def _norm_matmul_kernel(x_ref, g_ref, w_ref, o_ref):
    h = _rms(x_ref[...], g_ref[...])
    o_ref[...] = jnp.dot(h.astype(BF16), w_ref[...], preferred_element_type=F32)


def _norm_matmul(x, gain, w, tm=512):
    t, d = x.shape
    n = w.shape[1]
    return pl.pallas_call(
        _norm_matmul_kernel,
        grid=(t // tm,),
        in_specs=[pl.BlockSpec((tm, d), lambda i: (i, 0)),
                  pl.BlockSpec((1, d), lambda i: (0, 0)),
                  pl.BlockSpec((d, n), lambda i: (0, 0))],
        out_specs=pl.BlockSpec((tm, n), lambda i: (i, 0)),
        out_shape=jax.ShapeDtypeStruct((t, n), F32),
        compiler_params=pltpu.CompilerParams(dimension_semantics=("arbitrary",), vmem_limit_bytes=VMEM_LIMIT),
        name="norm_matmul",
    )(x, gain.reshape(1, d), w)


def _matmul_residual_kernel(a_ref, w_ref, r_ref, o_ref):
    o_ref[...] = r_ref[...] + jnp.dot(a_ref[...].astype(BF16), w_ref[...], preferred_element_type=F32)


def _matmul_residual(a, w, res, tm=512):
    t, k = a.shape
    n = w.shape[1]
    return pl.pallas_call(
        _matmul_residual_kernel,
        grid=(t // tm,),
        in_specs=[pl.BlockSpec((tm, k), lambda i: (i, 0)),
                  pl.BlockSpec((k, n), lambda i: (0, 0)),
                  pl.BlockSpec((tm, n), lambda i: (i, 0))],
        out_specs=pl.BlockSpec((tm, n), lambda i: (i, 0)),
        out_shape=jax.ShapeDtypeStruct((t, n), F32),
        compiler_params=pltpu.CompilerParams(dimension_semantics=("arbitrary",), vmem_limit_bytes=VMEM_LIMIT),
        name="matmul_residual",
    )(a, w, res)


def _rope_tables(seq):
    half = ROT_DIM // 2
    freqs = ROPE_THETA ** (-jnp.arange(0, ROT_DIM, 2, dtype=F32) / ROT_DIM)
    ang = jnp.arange(seq, dtype=F32)[:, None] * freqs[None, :]
    cos, sin = jnp.cos(ang), jnp.sin(ang)
    ones = jnp.ones((seq, HEAD_DIM - ROT_DIM), F32)
    zeros = jnp.zeros((seq, HEAD_DIM - ROT_DIM), F32)
    zh = jnp.zeros((seq, half), F32)
    c = jnp.concatenate([cos, cos, ones], axis=1)
    s_next = jnp.concatenate([-sin, zh, zeros], axis=1)
    s_prev = jnp.concatenate([zh, sin, zeros], axis=1)
    return jnp.stack([jnp.tile(c, (1, 2)), jnp.tile(s_next, (1, 2)), jnp.tile(s_prev, (1, 2))])


def _head_norm_rope(x, gain2, rope, lo):
    sq = x * x
    s_lo = jnp.sum(jnp.where(lo, sq, 0.0), axis=1, keepdims=True)
    s_hi = jnp.sum(jnp.where(lo, 0.0, sq), axis=1, keepdims=True)
    ms = jnp.where(lo, s_lo, s_hi) * (1.0 / HEAD_DIM)
    xn = x * lax.rsqrt(ms + RMS_EPS) * gain2
    half = ROT_DIM // 2
    return xn * rope[0] + pltpu.roll(xn, LANES - half, 1) * rope[1] + pltpu.roll(xn, half, 1) * rope[2]


def _attn_kernel(sinks_ref, q_ref, kc_ref, kp_ref, vc_ref, vp_ref, rc_ref, rp_ref, qg_ref, kg_ref, o_ref):
    n = pl.program_id(1)
    lo = lax.broadcasted_iota(jnp.int32, (WINDOW, LANES), 1) < HEAD_DIM
    rope_c = rc_ref[...]
    rope_p = rp_ref[...]
    qg = qg_ref[...]
    kg = kg_ref[...]

    rows = GROUP * WINDOW
    qi = lax.broadcasted_iota(jnp.int32, (rows, 2 * WINDOW), 0) & (WINDOW - 1)
    ki = lax.broadcasted_iota(jnp.int32, (rows, 2 * WINDOW), 1)
    rel = WINDOW + qi - ki
    valid = (rel >= 0) & (rel < WINDOW) & ((n > 0) | (ki >= WINDOW))
    head_of_row = lax.broadcasted_iota(jnp.int32, (rows, 1), 0) // WINDOW
    scale = 1.0 / math.sqrt(HEAD_DIM)

    q2 = [_head_norm_rope(q_ref[:, c * LANES:(c + 1) * LANES], qg, rope_c, lo).astype(BF16)
          for c in range(Q_COLS // LANES)]
    for c in range(KV_COLS // LANES):
        cols = slice(c * LANES, (c + 1) * LANES)
        kprev = _head_norm_rope(kp_ref[:, cols], kg, rope_p, lo)
        kcur = _head_norm_rope(kc_ref[:, cols], kg, rope_c, lo)
        kfull = jnp.concatenate([kprev, kcur], axis=0).astype(BF16)
        vfull = jnp.concatenate([vp_ref[:, cols], vc_ref[:, cols]], axis=0).astype(BF16)
        for hh in range(LANES // HEAD_DIM):
            h = (LANES // HEAD_DIM) * c + hh
            kh = kfull[:, hh * HEAD_DIM:(hh + 1) * HEAD_DIM]
            vh = vfull[:, hh * HEAD_DIM:(hh + 1) * HEAD_DIM]
            heads = [GROUP * h + g for g in range(GROUP)]
            q4 = jnp.concatenate([q2[j // 2][:, (j % 2) * HEAD_DIM:(j % 2 + 1) * HEAD_DIM] for j in heads], axis=0)
            s = lax.dot_general(q4, kh, (((1,), (1,)), ((), ())), preferred_element_type=F32) * scale
            s = jnp.where(valid, s, NEG_INF)
            sink = jnp.zeros((rows, 1), F32)
            for g, j in enumerate(heads):
                sink = jnp.where(head_of_row == g, sinks_ref[j], sink)
            m = jnp.maximum(jnp.max(s, axis=1, keepdims=True), sink)
            p = jnp.exp(s - m)
            denom = jnp.sum(p, axis=1, keepdims=True) + jnp.exp(sink - m)
            o = jnp.dot(p.astype(BF16), vh, preferred_element_type=F32) / denom
            for g, j in enumerate(heads):
                o_ref[:, j * HEAD_DIM:(j + 1) * HEAD_DIM] = o[g * WINDOW:(g + 1) * WINDOW]


def _attention(qkv, q_gain, k_gain, sinks, batch, seq):
    t = batch * seq
    nb = seq // WINDOW
    rope = _rope_tables(seq)
    kcol = Q_COLS // KV_COLS
    cur = lambda b, n: (b * nb + n, 0)
    kcur = lambda b, n: (b * nb + n, kcol)
    kprev = lambda b, n: (b * nb + jnp.maximum(n - 1, 0), kcol)
    vcur = lambda b, n: (b * nb + n, kcol + 1)
    vprev = lambda b, n: (b * nb + jnp.maximum(n - 1, 0), kcol + 1)
    return pl.pallas_call(
        _attn_kernel,
        grid=(batch, nb),
        in_specs=[pl.BlockSpec(memory_space=pltpu.SMEM),
                  pl.BlockSpec((WINDOW, Q_COLS), cur),
                  pl.BlockSpec((WINDOW, KV_COLS), kcur),
                  pl.BlockSpec((WINDOW, KV_COLS), kprev),
                  pl.BlockSpec((WINDOW, KV_COLS), vcur),
                  pl.BlockSpec((WINDOW, KV_COLS), vprev),
                  pl.BlockSpec((3, WINDOW, LANES), lambda b, n: (0, n, 0)),
                  pl.BlockSpec((3, WINDOW, LANES), lambda b, n: (0, jnp.maximum(n - 1, 0), 0)),
                  pl.BlockSpec((1, LANES), lambda b, n: (0, 0)),
                  pl.BlockSpec((1, LANES), lambda b, n: (0, 0))],
        out_specs=pl.BlockSpec((WINDOW, Q_COLS), cur),
        out_shape=jax.ShapeDtypeStruct((t, Q_COLS), F32),
        compiler_params=pltpu.CompilerParams(dimension_semantics=("arbitrary", "arbitrary"),
                                             vmem_limit_bytes=VMEM_LIMIT),
        name="swa_attention",
    )(sinks, qkv, qkv, qkv, qkv, qkv, rope, rope,
      jnp.tile(q_gain, 2).reshape(1, LANES), jnp.tile(k_gain, 2).reshape(1, LANES))


def _conv_kernel(x_ref, g_ref, win_ref, cw_ref, wout_ref, o_ref, zprev_ref):
    n = pl.program_id(1)
    d = D_MODEL

    @pl.when(n == 0)
    def _():
        zprev_ref[...] = jnp.zeros_like(zprev_ref)

    x = x_ref[...]
    h = _rms(x, g_ref[...])
    bcu = jnp.dot(h.astype(BF16), win_ref[...], preferred_element_type=F32)
    gate_b = bcu[:, :d]
    z = bcu[:, d:2 * d] * bcu[:, 2 * d:]
    tm = z.shape[0]
    row = lax.broadcasted_iota(jnp.int32, z.shape, 0)
    prev = zprev_ref[...]
    p_last = prev[SUBLANES - 1:SUBLANES, :]
    p_last2 = prev[SUBLANES - 2:SUBLANES - 1, :]
    z1 = jnp.where(row == 0, p_last, pltpu.roll(z, 1, 0))
    z2 = jnp.where(row == 0, p_last2, jnp.where(row == 1, p_last, pltpu.roll(z, 2, 0)))
    cw = cw_ref[...]
    conv = cw[0:1, :] * z2 + cw[1:2, :] * z1 + cw[2:3, :] * z
    zprev_ref[...] = z[tm - SUBLANES:, :]
    o_ref[...] = x + jnp.dot((gate_b * conv).astype(BF16), wout_ref[...], preferred_element_type=F32)


def _conv_mixer(x, gain, w_in, conv_w, w_out, batch, seq, tm=256):
    t, d = x.shape
    nblk = seq // tm
    blk = lambda b, n: (b * nblk + n, 0)
    const = lambda b, n: (0, 0)
    return pl.pallas_call(
        _conv_kernel,
        grid=(batch, nblk),
        in_specs=[pl.BlockSpec((tm, d), blk),
                  pl.BlockSpec((1, d), const),
                  pl.BlockSpec((d, 3 * d), const),
                  pl.BlockSpec((CONV_WIDTH, d), const),
                  pl.BlockSpec((d, d), const)],
        out_specs=pl.BlockSpec((tm, d), blk),
        out_shape=jax.ShapeDtypeStruct((t, d), F32),
        scratch_shapes=[pltpu.VMEM((SUBLANES, d), F32)],
        compiler_params=pltpu.CompilerParams(dimension_semantics=("arbitrary", "arbitrary"),
                                             vmem_limit_bytes=VMEM_LIMIT),
        name="conv_mixer",
    )(x, gain.reshape(1, d), w_in.astype(BF16), conv_w, w_out.astype(BF16))


def _topk_axis0(s, k, ids=None, payload=None):
    n, tm = s.shape
    if ids is None:
        ids = lax.broadcasted_iota(jnp.int32, (n, tm), 0)
    krow = lax.broadcasted_iota(jnp.int32, (k, tm), 0)
    vals = jnp.zeros((k, tm), F32)
    picks = jnp.zeros((k, tm), jnp.int32)
    for r in range(k):
        m = jnp.max(s, axis=0, keepdims=True)
        pos = jnp.min(jnp.where(s == m, ids, jnp.iinfo(jnp.int32).max), axis=0, keepdims=True)
        sel = ids == pos
        if payload is None:
            picked = pos
        else:
            picked = jnp.sum(jnp.where(sel, payload, 0), axis=0, keepdims=True)
        vals = jnp.where(krow == r, m, vals)
        picks = jnp.where(krow == r, picked, picks)
        s = jnp.where(sel, -jnp.inf, s)
    return vals, picks


def _pair_candidates(s1, i1, s2, i2):
    k, tm = s1.shape
    sub = lax.broadcasted_iota(jnp.int32, (SUBLANES, tm), 0)
    scores, flat, expert = [], [], []
    for i in range(k // 2):
        width = k if i == 0 else SUBLANES
        sc = s1[i:i + 1, :] + s2[0:width, :]
        ex = i1[i:i + 1, :] * N_KEYS + i2[0:width, :]
        fl = i * k + lax.broadcasted_iota(jnp.int32, (width, tm), 0)
        reach = k // (i + 1)
        if reach < width:
            sc = jnp.where(sub < reach, sc, -jnp.inf)
        scores.append(sc)
        flat.append(fl)
        expert.append(ex)
    scores.append(s1[k // 2:, :] + s2[0:1, :])
    expert.append(i1[k // 2:, :] * N_KEYS + i2[0:1, :])
    flat.append((k // 2 + sub) * k)
    return jnp.concatenate(scores, axis=0), jnp.concatenate(flat, axis=0), jnp.concatenate(expert, axis=0)


def _route_kernel(x_ref, g_ref, wq_ref, keys_ref, after_ref, h_ref, idx_ref, gate_ref):
    del after_ref
    h = _rms(x_ref[...], g_ref[...])
    h_ref[...] = h
    q = jnp.dot(h.astype(BF16), wq_ref[...], preferred_element_type=F32).astype(BF16)
    idx_rows, gate_rows = [], []
    for head in range(PEER_HEADS):
        tops = []
        for part in range(2):
            col = (head * 2 + part) * QUERY_HALF
            s = lax.dot_general(keys_ref[head, part], q[:, col:col + QUERY_HALF],
                                (((1,), (1,)), ((), ())), preferred_element_type=F32)
            tops.append(_topk_axis0(s, PEER_TOPK))
        (s1, i1), (s2, i2) = tops
        cand, flat_ids, cand_idx = _pair_candidates(s1, i1, s2, i2)
        g_s, e_idx = _topk_axis0(cand, PEER_TOPK, ids=flat_ids, payload=cand_idx)
        e = jnp.exp(g_s - jnp.max(g_s, axis=0, keepdims=True))
        gate_rows.append(e / jnp.sum(e, axis=0, keepdims=True))
        idx_rows.append(e_idx * ROW_WORDS)
    idx_ref[...] = jnp.concatenate(idx_rows, axis=0).T
    gate_ref[...] = jnp.concatenate(gate_rows, axis=0).T


def _route(x, gain, w_query, sub_keys, first_tok, n_tok, after, tm=128):
    t, d = x.shape
    nq = w_query.shape[1]
    first_blk = first_tok // tm
    return pl.pallas_call(
        _route_kernel,
        grid=(n_tok // tm,),
        in_specs=[pl.BlockSpec((tm, d), lambda i: (i + first_blk, 0)),
                  pl.BlockSpec((1, d), lambda i: (0, 0)),
                  pl.BlockSpec((d, nq), lambda i: (0, 0)),
                  pl.BlockSpec((PEER_HEADS, 2, N_KEYS, QUERY_HALF), lambda i: (0, 0, 0, 0)),
                  pl.BlockSpec(after.shape, lambda i: (0, 0))],
        out_specs=[pl.BlockSpec((tm, d), lambda i: (i, 0)),
                   pl.BlockSpec((tm, SLOTS), lambda i: (i, 0)),
                   pl.BlockSpec((tm, SLOTS), lambda i: (i, 0))],
        out_shape=[jax.ShapeDtypeStruct((n_tok, d), F32),
                   jax.ShapeDtypeStruct((n_tok, SLOTS), jnp.int32),
                   jax.ShapeDtypeStruct((n_tok, SLOTS), F32)],
        compiler_params=pltpu.CompilerParams(dimension_semantics=("arbitrary",), vmem_limit_bytes=VMEM_LIMIT),
        name="peer_route",
    )(x, gain.reshape(1, d), w_query, sub_keys, after)


def _pack_kernel(t_ref, tiles_ref, *rows_ref):
    x = t_ref[...]
    half = x.shape[1] // 2
    hi = pltpu.bitcast(x[:, :half].astype(BF16).astype(F32), jnp.int32)
    lo = pltpu.bitcast(x[:, half:].astype(BF16).astype(F32), jnp.int32)
    words = hi | lax.shift_right_logical(lo, 16)
    for ref in rows_ref:
        ref[...] = words
    for s in range(ROW_WORDS):
        tiles_ref[pl.ds(s, x.shape[0], stride=ROW_WORDS), :] = words[:, s * LANES:(s + 1) * LANES]


def _pack_table(tabs, layer, with_rows, rows_per_step=256):
    _, ne, d = tabs.shape
    out_specs = [pl.BlockSpec((rows_per_step * ROW_WORDS, LANES), lambda i: (i, 0))]
    out_shape = [jax.ShapeDtypeStruct((ne * ROW_WORDS, LANES), jnp.int32)]
    if with_rows:
        out_specs.append(pl.BlockSpec((rows_per_step, d // 2), lambda i: (i, 0)))
        out_shape.append(jax.ShapeDtypeStruct((ne, d // 2), jnp.int32))
    return pl.pallas_call(
        _pack_kernel,
        grid=(ne // rows_per_step,),
        in_specs=[pl.BlockSpec((None, rows_per_step, d), lambda i: (layer, i, 0))],
        out_specs=out_specs,
        out_shape=out_shape,
        compiler_params=pltpu.CompilerParams(dimension_semantics=("arbitrary",), vmem_limit_bytes=VMEM_LIMIT),
        name="pack_table",
    )(tabs)


def _load_table_once(tab_hbm, tab, sem):
    @pl.when(pl.program_id(0) == 0)
    def _():
        cp = pltpu.make_async_copy(tab_hbm, tab, sem)
        cp.start()
        cp.wait()


def _gather_pair(tab, off_a, off_b):
    ra = tab[pl.ds(pl.multiple_of(off_a, ROW_WORDS), ROW_WORDS), :]
    rb = tab[pl.ds(pl.multiple_of(off_b, ROW_WORDS), ROW_WORDS), :]
    words = jnp.concatenate([ra, rb], axis=0)
    hi = pltpu.bitcast(words & jnp.int32(-65536), F32)
    lo = pltpu.bitcast(words << 16, F32)
    return hi, lo


TOKEN_UNROLL = 8


def _expert_in_kernel(*refs):
    idx_refs = refs[:TOKEN_UNROLL]
    h_ref, gate_ref, tab_hbm, w_ref, tab, sem, a_ref = refs[TOKEN_UNROLL:]
    _load_table_once(tab_hbm, tab, sem)
    sub = lax.broadcasted_iota(jnp.int32, (SUBLANES, LANES), 0)
    lane = lax.broadcasted_iota(jnp.int32, (SUBLANES, LANES), 1)
    own_half = (sub >= ROW_WORDS) == ((lane & 1) == 1)
    tb = h_ref.shape[0]

    def step(i, carry):
        xs = []
        for u in range(TOKEN_UNROLL):
            x = h_ref[i * TOKEN_UNROLL + u]
            xs.append((jnp.concatenate([x[0:ROW_WORDS], x[0:ROW_WORDS]], axis=0),
                       jnp.concatenate([x[ROW_WORDS:], x[ROW_WORDS:]], axis=0)))
        accs = [jnp.zeros((SUBLANES, LANES), F32) for _ in range(TOKEN_UNROLL)]
        for p in range(SLOTS // 2):
            for u in range(TOKEN_UNROLL):
                hi, lo = _gather_pair(tab, idx_refs[u][i, 2 * p], idx_refs[u][i, 2 * p + 1])
                part = jnp.sum(hi * xs[u][0] + lo * xs[u][1], axis=1, keepdims=True)
                accs[u] = jnp.where((lane >> 1) == p, part, accs[u])
        for u in range(TOKEN_UNROLL):
            a_ref[pl.ds(i * TOKEN_UNROLL + u, 1), :] = jnp.sum(jnp.where(own_half, accs[u], 0.0),
                                                                axis=0, keepdims=True)
        return carry

    lax.fori_loop(0, tb // TOKEN_UNROLL, step, 0)
    a = a_ref[...]
    w_ref[...] = gate_ref[...] * (0.5 * a * (1.0 + lax.erf(a * (1.0 / math.sqrt(2.0)))))


def _expert_out_kernel(*refs):
    idx_refs = refs[:TOKEN_UNROLL]
    w_ref, x_ref, tab_hbm, o_ref, tab, sem, wb_ref = refs[TOKEN_UNROLL:]
    _load_table_once(tab_hbm, tab, sem)
    sub = lax.broadcasted_iota(jnp.int32, (SUBLANES, LANES), 0)
    lower = sub < ROW_WORDS
    tb = x_ref.shape[0]

    def step(i, carry):
        for u in range(TOKEN_UNROLL):
            row = w_ref[pl.ds(i * TOKEN_UNROLL + u, 1), :]
            wb_ref[u] = jnp.broadcast_to(row, (SLOTS, LANES)).T
        acc_h = [jnp.zeros((SUBLANES, LANES), F32) for _ in range(TOKEN_UNROLL)]
        acc_l = [jnp.zeros((SUBLANES, LANES), F32) for _ in range(TOKEN_UNROLL)]
        for p in range(SLOTS // 2):
            ka, kb = 2 * p, 2 * p + 1
            for u in range(TOKEN_UNROLL):
                hi, lo = _gather_pair(tab, idx_refs[u][i, ka], idx_refs[u][i, kb])
                wa = jnp.broadcast_to(wb_ref[u, ka:ka + 1, :], (SUBLANES, LANES))
                wb = jnp.broadcast_to(wb_ref[u, kb:kb + 1, :], (SUBLANES, LANES))
                wt = jnp.where(lower, wa, wb)
                acc_h[u] = acc_h[u] + hi * wt
                acc_l[u] = acc_l[u] + lo * wt
        for u in range(TOKEN_UNROLL):
            t = i * TOKEN_UNROLL + u
            ah = acc_h[u] + pltpu.roll(acc_h[u], ROW_WORDS, 0)
            al = acc_l[u] + pltpu.roll(acc_l[u], ROW_WORDS, 0)
            o_ref[t] = x_ref[t] + jnp.where(lower, ah, al)
        return carry

    lax.fori_loop(0, tb // TOKEN_UNROLL, step, 0)


def _expert_specs(tb):
    un = TOKEN_UNROLL
    smem_blks = [pl.BlockSpec((None, tb // un, SLOTS), functools.partial(lambda u, i: (u, i, 0), u),
                              memory_space=pltpu.SMEM) for u in range(un)]
    vmem_blk = pl.BlockSpec((tb, SLOTS), lambda i: (i, 0))
    tok_blk = pl.BlockSpec((tb, D_MODEL // LANES, LANES), lambda i: (i, 0, 0))
    params = pltpu.CompilerParams(dimension_semantics=("arbitrary",), vmem_limit_bytes=VMEM_LIMIT)
    return smem_blks, vmem_blk, tok_blk, params


def _split_offsets(idx):
    n = idx.shape[0]
    return idx.reshape(n // TOKEN_UNROLL, TOKEN_UNROLL, SLOTS).transpose(1, 0, 2)


def _expert_in(h, idx_split, gate, u_packed, tb=256):
    n, d = h.shape
    smem_blks, vmem_blk, tok_blk, params = _expert_specs(tb)
    return pl.pallas_call(
        _expert_in_kernel,
        grid=(n // tb,),
        in_specs=smem_blks + [tok_blk, vmem_blk, pl.BlockSpec(memory_space=pl.ANY)],
        out_specs=vmem_blk,
        out_shape=jax.ShapeDtypeStruct((n, SLOTS), F32),
        scratch_shapes=[pltpu.VMEM(u_packed.shape, jnp.int32), pltpu.SemaphoreType.DMA, pltpu.VMEM((tb, SLOTS), F32)],
        compiler_params=params,
        name="peer_expert_in",
    )(*([idx_split] * TOKEN_UNROLL), h.reshape(n, d // LANES, LANES), gate, u_packed)


def _expert_out(x, idx_split, w, v_packed, tb=256):
    t, d = x.shape
    n = w.shape[0]
    smem_blks, vmem_blk, tok_blk, params = _expert_specs(tb)
    out = pl.pallas_call(
        _expert_out_kernel,
        grid=(n // tb,),
        in_specs=smem_blks + [vmem_blk, tok_blk, pl.BlockSpec(memory_space=pl.ANY)],
        out_specs=tok_blk,
        out_shape=jax.ShapeDtypeStruct((n, d // LANES, LANES), F32),
        scratch_shapes=[pltpu.VMEM(v_packed.shape, jnp.int32), pltpu.SemaphoreType.DMA,
                        pltpu.VMEM((TOKEN_UNROLL, SLOTS, LANES), F32)],
        compiler_params=params,
        name="peer_expert_out",
    )(*([idx_split] * TOKEN_UNROLL), w, x.reshape(t, d // LANES, LANES), v_packed)
    return out.reshape(n, d)


SC_LANES = 16
SC_WORKERS = 32
SC_CHUNK = 32
SC_GROUP = 8
SC_TOKENS = 16640


def _sc_params():
    cp = pltpu.CompilerParams()
    if "needs_layout_passes" in pltpu.CompilerParams.__dataclass_fields__:
        cp = dataclasses.replace(cp, needs_layout_passes=False)
    return cp


def _sc_expert_out(table_words, idx, w, x, first_tok):
    d = x.shape[1]
    n_tok = w.shape[0] // SLOTS
    per = n_tok // SC_WORKERS
    words = d // 2
    nq = words // SC_LANES // 2
    nchunk = SLOTS // SC_CHUNK
    group_chunks = SC_GROUP * nchunk
    mesh = plsc.VectorSubcoreMesh(core_axis_name="c", subcore_axis_name="s")

    @functools.partial(
        pl.kernel, mesh=mesh,
        out_type=jax.ShapeDtypeStruct((n_tok, d), F32),
        scratch_types=[pltpu.VMEM((SC_GROUP * SLOTS,), jnp.int32), pltpu.VMEM((SC_GROUP * SLOTS,), F32),
                       pltpu.VMEM((2, SC_CHUNK, words), jnp.int32), pltpu.VMEM((SC_GROUP, d), F32),
                       pltpu.SemaphoreType.DMA, pltpu.SemaphoreType.DMA],
        compiler_params=_sc_params(),
        name="peer_expert_out_sc",
    )
    def body(tab_hbm, idx_hbm, w_hbm, x_hbm, o_hbm, idx_v, w_v, rows_v, y_v, sem0, sem1):
        base = (lax.axis_index("s") * 2 + lax.axis_index("c")) * per
        zero = jnp.zeros((SC_LANES,), jnp.int32)
        sems = (sem0, sem1)

        def gather(k, b):
            off = pl.multiple_of(k * SC_CHUNK, SC_CHUNK)
            return pltpu.make_async_copy(tab_hbm.at[idx_v.at[pl.ds(off, SC_CHUNK)]], rows_v.at[b], sems[b])

        def accumulate(k, b):
            tok = k // nchunk
            for q in range(2):
                first = q * nq
                acc0 = (tuple(y_v[tok, pl.ds((first + j) * SC_LANES, SC_LANES)] for j in range(nq))
                        + tuple(y_v[tok, pl.ds(words + (first + j) * SC_LANES, SC_LANES)] for j in range(nq)))

                def row_body(r, accs):
                    ws = plsc.load_gather(w_v, [zero + (k * SC_CHUNK + r)])
                    hi_acc, lo_acc = [], []
                    for j in range(nq):
                        wv = rows_v[b, r, pl.ds((first + j) * SC_LANES, SC_LANES)]
                        lo, hi = plsc.unpack(plsc.bitcast(wv, BF16), format=plsc.PackFormat.INTERLEAVED,
                                             preferred_element_type=F32)
                        hi_acc.append(accs[j] + hi * ws)
                        lo_acc.append(accs[nq + j] + lo * ws)
                    return tuple(hi_acc) + tuple(lo_acc)

                accs = lax.fori_loop(0, SC_CHUNK, row_body, acc0)
                for j in range(nq):
                    y_v[tok, pl.ds((first + j) * SC_LANES, SC_LANES)] = accs[j]
                    y_v[tok, pl.ds(words + (first + j) * SC_LANES, SC_LANES)] = accs[nq + j]

        @pl.loop(0, per // SC_GROUP)
        def _(g):
            t0 = pl.multiple_of(base + g * SC_GROUP, SC_GROUP)
            pltpu.sync_copy(idx_hbm.at[pl.ds(t0 * SLOTS, SC_GROUP * SLOTS)], idx_v)
            pltpu.sync_copy(w_hbm.at[pl.ds(t0 * SLOTS, SC_GROUP * SLOTS)], w_v)
            pltpu.sync_copy(x_hbm.at[pl.ds(first_tok + t0, SC_GROUP)], y_v)
            gather(0, 0).start()

            @pl.loop(0, group_chunks // 2)
            def _(kk):
                k0 = 2 * kk
                gather(k0 + 1, 1).start()
                gather(k0, 0).wait()
                accumulate(k0, 0)

                @pl.when(k0 + 2 < group_chunks)
                def _():
                    gather(k0 + 2, 0).start()
                gather(k0 + 1, 1).wait()
                accumulate(k0 + 1, 1)

            pltpu.sync_copy(y_v, o_hbm.at[pl.ds(t0, SC_GROUP)])

    return body(table_words, idx, w, x)


def _peer(x, gain, w_query, sub_keys, experts_u, experts_v, layer):
    t = x.shape[0]
    t_tc = t - SC_TOKENS
    wq, keys = w_query.astype(BF16), sub_keys.astype(BF16)
    (u_packed,) = _pack_table(experts_u, layer, with_rows=False)
    v_packed, v_rows = _pack_table(experts_v, layer, with_rows=True)

    h, idx, gate = _route(x, gain, wq, keys, t_tc, SC_TOKENS, v_rows[:SUBLANES])
    w_sc = _expert_in(h, _split_offsets(idx), gate, u_packed)
    out_sc = _sc_expert_out(v_rows, (idx // ROW_WORDS).reshape(-1), w_sc.reshape(-1), x, t_tc)

    h, idx, gate = _route(x, gain, wq, keys, 0, t_tc, w_sc[:SUBLANES])
    idx_split = _split_offsets(idx)
    w = _expert_in(h, idx_split, gate, u_packed)
    out_tc = _expert_out(x, idx_split, w, v_packed)
    return jnp.concatenate([out_tc, out_sc], axis=0)


def kernel(x, norm_mix, norm_ffn, attn_w_qkv, attn_q_norm, attn_k_norm, attn_sinks, attn_w_o, conv_w_in, conv_w, conv_w_out, peer_w_query, peer_sub_keys, peer_u, peer_v):
    batch, seq, d = x.shape
    xt = x.reshape(batch * seq, d)
    for i in range(norm_mix.shape[0]):
        j = i // 2
        if i % 2 == 0:
            qkv = _norm_matmul(xt, norm_mix[i], attn_w_qkv[j].astype(BF16))
            o = _attention(qkv, attn_q_norm[j], attn_k_norm[j], attn_sinks[j], batch, seq)
            xt = _matmul_residual(o, attn_w_o[j].astype(BF16), xt)
        else:
            xt = _conv_mixer(xt, norm_mix[i], conv_w_in[j], conv_w[j], conv_w_out[j], batch, seq)
        xt = _peer(xt, norm_ffn[i], peer_w_query[i], peer_sub_keys[i], peer_u, peer_v, i)
    return xt.reshape(batch, seq, d)
```

```python
import dataclasses
import functools
import math

import jax
import jax.numpy as jnp
from jax import lax
from jax.experimental import pallas as pl
from jax.experimental.pallas import tpu as pltpu
from jax.experimental.pallas import tpu_sc as plsc

D_MODEL = 1024
RMS_EPS = 1e-6

HEAD_DIM = 64
N_Q_HEADS = 16
N_KV_HEADS = 4
GROUP = N_Q_HEADS // N_KV_HEADS
WINDOW = 128
ROT_DIM = HEAD_DIM // 4
ROPE_THETA = 500000.0
Q_COLS = N_Q_HEADS * HEAD_DIM
KV_COLS = N_KV_HEADS * HEAD_DIM
NEG_INF = -1e30

CONV_WIDTH = 3

PEER_HEADS = 8
N_KEYS = 128
N_EXPERTS = N_KEYS * N_KEYS
PEER_TOPK = 16
QUERY_HALF = 128
SLOTS = PEER_HEADS * PEER_TOPK

LANES = 128
SUBLANES = 8
ROW_WORDS = D_MODEL // 2 // LANES
VMEM_LIMIT = 48 * 1024 * 1024

BF16 = jnp.bfloat16
F32 = jnp.float32


def _rms(x, gain):
    return x * lax.rsqrt(jnp.mean(x * x, axis=-1, keepdims=True) + RMS_EPS) * gain


def _norm_matmul_kernel(x_ref, g_ref, w_ref, o_ref):
    h = _rms(x_ref[...], g_ref[...])
    o_ref[...] = jnp.dot(h.astype(BF16), w_ref[...], preferred_element_type=F32)


def _norm_matmul(x, gain, w, tm=512):
    t, d = x.shape
    n = w.shape[1]
    return pl.pallas_call(
        _norm_matmul_kernel,
        grid=(t // tm,),
        in_specs=[pl.BlockSpec((tm, d), lambda i: (i, 0)),
                  pl.BlockSpec((1, d), lambda i: (0, 0)),
                  pl.BlockSpec((d, n), lambda i: (0, 0))],
        out_specs=pl.BlockSpec((tm, n), lambda i: (i, 0)),
        out_shape=jax.ShapeDtypeStruct((t, n), F32),
        compiler_params=pltpu.CompilerParams(dimension_semantics=("arbitrary",), vmem_limit_bytes=VMEM_LIMIT),
        name="norm_matmul",
    )(x, gain.reshape(1, d), w)


def _matmul_residual_kernel(a_ref, w_ref, r_ref, o_ref):
    o_ref[...] = r_ref[...] + jnp.dot(a_ref[...].astype(BF16), w_ref[...], preferred_element_type=F32)


def _matmul_residual(a, w, res, tm=512):
    t, k = a.shape
    n = w.shape[1]
    return pl.pallas_call(
        _matmul_residual_kernel,
        grid=(t // tm,),
        in_specs=[pl.BlockSpec((tm, k), lambda i: (i, 0)),
                  pl.BlockSpec((k, n), lambda i: (0, 0)),
                  pl.BlockSpec((tm, n), lambda i: (i, 0))],
        out_specs=pl.BlockSpec((tm, n), lambda i: (i, 0)),
        out_shape=jax.ShapeDtypeStruct((t, n), F32),
        compiler_params=pltpu.CompilerParams(dimension_semantics=("arbitrary",), vmem_limit_bytes=VMEM_LIMIT),
        name="matmul_residual",
    )(a, w, res)


def _rope_tables(seq):
    half = ROT_DIM // 2
    freqs = ROPE_THETA ** (-jnp.arange(0, ROT_DIM, 2, dtype=F32) / ROT_DIM)
    ang = jnp.arange(seq, dtype=F32)[:, None] * freqs[None, :]
    cos, sin = jnp.cos(ang), jnp.sin(ang)
    ones = jnp.ones((seq, HEAD_DIM - ROT_DIM), F32)
    zeros = jnp.zeros((seq, HEAD_DIM - ROT_DIM), F32)
    zh = jnp.zeros((seq, half), F32)
    c = jnp.concatenate([cos, cos, ones], axis=1)
    s_next = jnp.concatenate([-sin, zh, zeros], axis=1)
    s_prev = jnp.concatenate([zh, sin, zeros], axis=1)
    return jnp.stack([jnp.tile(c, (1, 2)), jnp.tile(s_next, (1, 2)), jnp.tile(s_prev, (1, 2))])


def _head_norm_rope(x, gain2, rope, lo):
    sq = x * x
    s_lo = jnp.sum(jnp.where(lo, sq, 0.0), axis=1, keepdims=True)
    s_hi = jnp.sum(jnp.where(lo, 0.0, sq), axis=1, keepdims=True)
    ms = jnp.where(lo, s_lo, s_hi) * (1.0 / HEAD_DIM)
    xn = x * lax.rsqrt(ms + RMS_EPS) * gain2
    half = ROT_DIM // 2
    return xn * rope[0] + pltpu.roll(xn, LANES - half, 1) * rope[1] + pltpu.roll(xn, half, 1) * rope[2]


def _attn_kernel(sinks_ref, q_ref, kc_ref, kp_ref, vc_ref, vp_ref, rc_ref, rp_ref, qg_ref, kg_ref, o_ref):
    n = pl.program_id(1)
    lo = lax.broadcasted_iota(jnp.int32, (WINDOW, LANES), 1) < HEAD_DIM
    rope_c = rc_ref[...]
    rope_p = rp_ref[...]
    qg = qg_ref[...]
    kg = kg_ref[...]

    rows = GROUP * WINDOW
    qi = lax.broadcasted_iota(jnp.int32, (rows, 2 * WINDOW), 0) & (WINDOW - 1)
    ki = lax.broadcasted_iota(jnp.int32, (rows, 2 * WINDOW), 1)
    rel = WINDOW + qi - ki
    valid = (rel >= 0) & (rel < WINDOW) & ((n > 0) | (ki >= WINDOW))
    head_of_row = lax.broadcasted_iota(jnp.int32, (rows, 1), 0) // WINDOW
    scale = 1.0 / math.sqrt(HEAD_DIM)

    q2 = [_head_norm_rope(q_ref[:, c * LANES:(c + 1) * LANES], qg, rope_c, lo).astype(BF16)
          for c in range(Q_COLS // LANES)]
    for c in range(KV_COLS // LANES):
        cols = slice(c * LANES, (c + 1) * LANES)
        kprev = _head_norm_rope(kp_ref[:, cols], kg, rope_p, lo)
        kcur = _head_norm_rope(kc_ref[:, cols], kg, rope_c, lo)
        kfull = jnp.concatenate([kprev, kcur], axis=0).astype(BF16)
        vfull = jnp.concatenate([vp_ref[:, cols], vc_ref[:, cols]], axis=0).astype(BF16)
        for hh in range(LANES // HEAD_DIM):
            h = (LANES // HEAD_DIM) * c + hh
            kh = kfull[:, hh * HEAD_DIM:(hh + 1) * HEAD_DIM]
            vh = vfull[:, hh * HEAD_DIM:(hh + 1) * HEAD_DIM]
            heads = [GROUP * h + g for g in range(GROUP)]
            q4 = jnp.concatenate([q2[j // 2][:, (j % 2) * HEAD_DIM:(j % 2 + 1) * HEAD_DIM] for j in heads], axis=0)
            s = lax.dot_general(q4, kh, (((1,), (1,)), ((), ())), preferred_element_type=F32) * scale
            s = jnp.where(valid, s, NEG_INF)
            sink = jnp.zeros((rows, 1), F32)
            for g, j in enumerate(heads):
                sink = jnp.where(head_of_row == g, sinks_ref[j], sink)
            m = jnp.maximum(jnp.max(s, axis=1, keepdims=True), sink)
            p = jnp.exp(s - m)
            denom = jnp.sum(p, axis=1, keepdims=True) + jnp.exp(sink - m)
            o = jnp.dot(p.astype(BF16), vh, preferred_element_type=F32) / denom
            for g, j in enumerate(heads):
                o_ref[:, j * HEAD_DIM:(j + 1) * HEAD_DIM] = o[g * WINDOW:(g + 1) * WINDOW]


def _attention(qkv, q_gain, k_gain, sinks, batch, seq):
    t = batch * seq
    nb = seq // WINDOW
    rope = _rope_tables(seq)
    kcol = Q_COLS // KV_COLS
    cur = lambda b, n: (b * nb + n, 0)
    kcur = lambda b, n: (b * nb + n, kcol)
    kprev = lambda b, n: (b * nb + jnp.maximum(n - 1, 0), kcol)
    vcur = lambda b, n: (b * nb + n, kcol + 1)
    vprev = lambda b, n: (b * nb + jnp.maximum(n - 1, 0), kcol + 1)
    return pl.pallas_call(
        _attn_kernel,
        grid=(batch, nb),
        in_specs=[pl.BlockSpec(memory_space=pltpu.SMEM),
                  pl.BlockSpec((WINDOW, Q_COLS), cur),
                  pl.BlockSpec((WINDOW, KV_COLS), kcur),
                  pl.BlockSpec((WINDOW, KV_COLS), kprev),
                  pl.BlockSpec((WINDOW, KV_COLS), vcur),
                  pl.BlockSpec((WINDOW, KV_COLS), vprev),
                  pl.BlockSpec((3, WINDOW, LANES), lambda b, n: (0, n, 0)),
                  pl.BlockSpec((3, WINDOW, LANES), lambda b, n: (0, jnp.maximum(n - 1, 0), 0)),
                  pl.BlockSpec((1, LANES), lambda b, n: (0, 0)),
                  pl.BlockSpec((1, LANES), lambda b, n: (0, 0))],
        out_specs=pl.BlockSpec((WINDOW, Q_COLS), cur),
        out_shape=jax.ShapeDtypeStruct((t, Q_COLS), F32),
        compiler_params=pltpu.CompilerParams(dimension_semantics=("arbitrary", "arbitrary"),
                                             vmem_limit_bytes=VMEM_LIMIT),
        name="swa_attention",
    )(sinks, qkv, qkv, qkv, qkv, qkv, rope, rope,
      jnp.tile(q_gain, 2).reshape(1, LANES), jnp.tile(k_gain, 2).reshape(1, LANES))


def _conv_kernel(x_ref, g_ref, win_ref, cw_ref, wout_ref, o_ref, zprev_ref):
    n = pl.program_id(1)
    d = D_MODEL

    @pl.when(n == 0)
    def _():
        zprev_ref[...] = jnp.zeros_like(zprev_ref)

    x = x_ref[...]
    h = _rms(x, g_ref[...])
    bcu = jnp.dot(h.astype(BF16), win_ref[...], preferred_element_type=F32)
    gate_b = bcu[:, :d]
    z = bcu[:, d:2 * d] * bcu[:, 2 * d:]
    tm = z.shape[0]
    row = lax.broadcasted_iota(jnp.int32, z.shape, 0)
    prev = zprev_ref[...]
    p_last = prev[SUBLANES - 1:SUBLANES, :]
    p_last2 = prev[SUBLANES - 2:SUBLANES - 1, :]
    z1 = jnp.where(row == 0, p_last, pltpu.roll(z, 1, 0))
    z2 = jnp.where(row == 0, p_last2, jnp.where(row == 1, p_last, pltpu.roll(z, 2, 0)))
    cw = cw_ref[...]
    conv = cw[0:1, :] * z2 + cw[1:2, :] * z1 + cw[2:3, :] * z
    zprev_ref[...] = z[tm - SUBLANES:, :]
    o_ref[...] = x + jnp.dot((gate_b * conv).astype(BF16), wout_ref[...], preferred_element_type=F32)


def _conv_mixer(x, gain, w_in, conv_w, w_out, batch, seq, tm=256):
    t, d = x.shape
    nblk = seq // tm
    blk = lambda b, n: (b * nblk + n, 0)
    const = lambda b, n: (0, 0)
    return pl.pallas_call(
        _conv_kernel,
        grid=(batch, nblk),
        in_specs=[pl.BlockSpec((tm, d), blk),
                  pl.BlockSpec((1, d), const),
                  pl.BlockSpec((d, 3 * d), const),
                  pl.BlockSpec((CONV_WIDTH, d), const),
                  pl.BlockSpec((d, d), const)],
        out_specs=pl.BlockSpec((tm, d), blk),
        out_shape=jax.ShapeDtypeStruct((t, d), F32),
        scratch_shapes=[pltpu.VMEM((SUBLANES, d), F32)],
        compiler_params=pltpu.CompilerParams(dimension_semantics=("arbitrary", "arbitrary"),
                                             vmem_limit_bytes=VMEM_LIMIT),
        name="conv_mixer",
    )(x, gain.reshape(1, d), w_in.astype(BF16), conv_w, w_out.astype(BF16))


def _topk_axis0(s, k, ids=None, payload=None):
    n, tm = s.shape
    if ids is None:
        ids = lax.broadcasted_iota(jnp.int32, (n, tm), 0)
    krow = lax.broadcasted_iota(jnp.int32, (k, tm), 0)
    vals = jnp.zeros((k, tm), F32)
    picks = jnp.zeros((k, tm), jnp.int32)
    for r in range(k):
        m = jnp.max(s, axis=0, keepdims=True)
        pos = jnp.min(jnp.where(s == m, ids, jnp.iinfo(jnp.int32).max), axis=0, keepdims=True)
        sel = ids == pos
        if payload is None:
            picked = pos
        else:
            picked = jnp.sum(jnp.where(sel, payload, 0), axis=0, keepdims=True)
        vals = jnp.where(krow == r, m, vals)
        picks = jnp.where(krow == r, picked, picks)
        s = jnp.where(sel, -jnp.inf, s)
    return vals, picks


def _pair_candidates(s1, i1, s2, i2):
    k, tm = s1.shape
    sub = lax.broadcasted_iota(jnp.int32, (SUBLANES, tm), 0)
    scores, flat, expert = [], [], []
    for i in range(k // 2):
        width = k if i == 0 else SUBLANES
        sc = s1[i:i + 1, :] + s2[0:width, :]
        ex = i1[i:i + 1, :] * N_KEYS + i2[0:width, :]
        fl = i * k + lax.broadcasted_iota(jnp.int32, (width, tm), 0)
        reach = k // (i + 1)
        if reach < width:
            sc = jnp.where(sub < reach, sc, -jnp.inf)
        scores.append(sc)
        flat.append(fl)
        expert.append(ex)
    scores.append(s1[k // 2:, :] + s2[0:1, :])
    expert.append(i1[k // 2:, :] * N_KEYS + i2[0:1, :])
    flat.append((k // 2 + sub) * k)
    return jnp.concatenate(scores, axis=0), jnp.concatenate(flat, axis=0), jnp.concatenate(expert, axis=0)


def _route_kernel(x_ref, g_ref, wq_ref, keys_ref, after_ref, h_ref, idx_ref, gate_ref):
    del after_ref
    h = _rms(x_ref[...], g_ref[...])
    h_ref[...] = h
    q = jnp.dot(h.astype(BF16), wq_ref[...], preferred_element_type=F32).astype(BF16)
    idx_rows, gate_rows = [], []
    for head in range(PEER_HEADS):
        tops = []
        for part in range(2):
            col = (head * 2 + part) * QUERY_HALF
            s = lax.dot_general(keys_ref[head, part], q[:, col:col + QUERY_HALF],
                                (((1,), (1,)), ((), ())), preferred_element_type=F32)
            tops.append(_topk_axis0(s, PEER_TOPK))
        (s1, i1), (s2, i2) = tops
        cand, flat_ids, cand_idx = _pair_candidates(s1, i1, s2, i2)
        g_s, e_idx = _topk_axis0(cand, PEER_TOPK, ids=flat_ids, payload=cand_idx)
        e = jnp.exp(g_s - jnp.max(g_s, axis=0, keepdims=True))
        gate_rows.append(e / jnp.sum(e, axis=0, keepdims=True))
        idx_rows.append(e_idx * ROW_WORDS)
    idx_ref[...] = jnp.concatenate(idx_rows, axis=0).T
    gate_ref[...] = jnp.concatenate(gate_rows, axis=0).T


def _route(x, gain, w_query, sub_keys, first_tok, n_tok, after, tm=128):
    t, d = x.shape
    nq = w_query.shape[1]
    first_blk = first_tok // tm
    return pl.pallas_call(
        _route_kernel,
        grid=(n_tok // tm,),
        in_specs=[pl.BlockSpec((tm, d), lambda i: (i + first_blk, 0)),
                  pl.BlockSpec((1, d), lambda i: (0, 0)),
                  pl.BlockSpec((d, nq), lambda i: (0, 0)),
                  pl.BlockSpec((PEER_HEADS, 2, N_KEYS, QUERY_HALF), lambda i: (0, 0, 0, 0)),
                  pl.BlockSpec(after.shape, lambda i: (0, 0))],
        out_specs=[pl.BlockSpec((tm, d), lambda i: (i, 0)),
                   pl.BlockSpec((tm, SLOTS), lambda i: (i, 0)),
                   pl.BlockSpec((tm, SLOTS), lambda i: (i, 0))],
        out_shape=[jax.ShapeDtypeStruct((n_tok, d), F32),
                   jax.ShapeDtypeStruct((n_tok, SLOTS), jnp.int32),
                   jax.ShapeDtypeStruct((n_tok, SLOTS), F32)],
        compiler_params=pltpu.CompilerParams(dimension_semantics=("arbitrary",), vmem_limit_bytes=VMEM_LIMIT),
        name="peer_route",
    )(x, gain.reshape(1, d), w_query, sub_keys, after)


def _pack_kernel(t_ref, tiles_ref, *rows_ref):
    x = t_ref[...]
    half = x.shape[1] // 2
    hi = pltpu.bitcast(x[:, :half].astype(BF16).astype(F32), jnp.int32)
    lo = pltpu.bitcast(x[:, half:].astype(BF16).astype(F32), jnp.int32)
    words = hi | lax.shift_right_logical(lo, 16)
    for ref in rows_ref:
        ref[...] = words
    for s in range(ROW_WORDS):
        tiles_ref[pl.ds(s, x.shape[0], stride=ROW_WORDS), :] = words[:, s * LANES:(s + 1) * LANES]


def _pack_table(tabs, layer, with_rows, rows_per_step=256):
    _, ne, d = tabs.shape
    out_specs = [pl.BlockSpec((rows_per_step * ROW_WORDS, LANES), lambda i: (i, 0))]
    out_shape = [jax.ShapeDtypeStruct((ne * ROW_WORDS, LANES), jnp.int32)]
    if with_rows:
        out_specs.append(pl.BlockSpec((rows_per_step, d // 2), lambda i: (i, 0)))
        out_shape.append(jax.ShapeDtypeStruct((ne, d // 2), jnp.int32))
    return pl.pallas_call(
        _pack_kernel,
        grid=(ne // rows_per_step,),
        in_specs=[pl.BlockSpec((None, rows_per_step, d), lambda i: (layer, i, 0))],
        out_specs=out_specs,
        out_shape=out_shape,
        compiler_params=pltpu.CompilerParams(dimension_semantics=("arbitrary",), vmem_limit_bytes=VMEM_LIMIT),
        name="pack_table",
    )(tabs)


def _load_table_once(tab_hbm, tab, sem):
    @pl.when(pl.program_id(0) == 0)
    def _():
        cp = pltpu.make_async_copy(tab_hbm, tab, sem)
        cp.start()
        cp.wait()


def _gather_pair(tab, off_a, off_b):
    ra = tab[pl.ds(pl.multiple_of(off_a, ROW_WORDS), ROW_WORDS), :]
    rb = tab[pl.ds(pl.multiple_of(off_b, ROW_WORDS), ROW_WORDS), :]
    words = jnp.concatenate([ra, rb], axis=0)
    hi = pltpu.bitcast(words & jnp.int32(-65536), F32)
    lo = pltpu.bitcast(words << 16, F32)
    return hi, lo


TOKEN_UNROLL = 8


def _expert_in_kernel(*refs):
    idx_refs = refs[:TOKEN_UNROLL]
    h_ref, gate_ref, tab_hbm, w_ref, tab, sem, a_ref = refs[TOKEN_UNROLL:]
    _load_table_once(tab_hbm, tab, sem)
    sub = lax.broadcasted_iota(jnp.int32, (SUBLANES, LANES), 0)
    lane = lax.broadcasted_iota(jnp.int32, (SUBLANES, LANES), 1)
    own_half = (sub >= ROW_WORDS) == ((lane & 1) == 1)
    tb = h_ref.shape[0]

    def step(i, carry):
        xs = []
        for u in range(TOKEN_UNROLL):
            x = h_ref[i * TOKEN_UNROLL + u]
            xs.append((jnp.concatenate([x[0:ROW_WORDS], x[0:ROW_WORDS]], axis=0),
                       jnp.concatenate([x[ROW_WORDS:], x[ROW_WORDS:]], axis=0)))
        accs = [jnp.zeros((SUBLANES, LANES), F32) for _ in range(TOKEN_UNROLL)]
        for p in range(SLOTS // 2):
            for u in range(TOKEN_UNROLL):
                hi, lo = _gather_pair(tab, idx_refs[u][i, 2 * p], idx_refs[u][i, 2 * p + 1])
                part = jnp.sum(hi * xs[u][0] + lo * xs[u][1], axis=1, keepdims=True)
                accs[u] = jnp.where((lane >> 1) == p, part, accs[u])
        for u in range(TOKEN_UNROLL):
            a_ref[pl.ds(i * TOKEN_UNROLL + u, 1), :] = jnp.sum(jnp.where(own_half, accs[u], 0.0),
                                                                axis=0, keepdims=True)
        return carry

    lax.fori_loop(0, tb // TOKEN_UNROLL, step, 0)
    a = a_ref[...]
    w_ref[...] = gate_ref[...] * (0.5 * a * (1.0 + lax.erf(a * (1.0 / math.sqrt(2.0)))))


def _expert_out_kernel(*refs):
    idx_refs = refs[:TOKEN_UNROLL]
    w_ref, x_ref, tab_hbm, o_ref, tab, sem, wb_ref = refs[TOKEN_UNROLL:]
    _load_table_once(tab_hbm, tab, sem)
    sub = lax.broadcasted_iota(jnp.int32, (SUBLANES, LANES), 0)
    lower = sub < ROW_WORDS
    tb = x_ref.shape[0]

    def step(i, carry):
        for u in range(TOKEN_UNROLL):
            row = w_ref[pl.ds(i * TOKEN_UNROLL + u, 1), :]
            wb_ref[u] = jnp.broadcast_to(row, (SLOTS, LANES)).T
        acc_h = [jnp.zeros((SUBLANES, LANES), F32) for _ in range(TOKEN_UNROLL)]
        acc_l = [jnp.zeros((SUBLANES, LANES), F32) for _ in range(TOKEN_UNROLL)]
        for p in range(SLOTS // 2):
            ka, kb = 2 * p, 2 * p + 1
            for u in range(TOKEN_UNROLL):
                hi, lo = _gather_pair(tab, idx_refs[u][i, ka], idx_refs[u][i, kb])
                wa = jnp.broadcast_to(wb_ref[u, ka:ka + 1, :], (SUBLANES, LANES))
                wb = jnp.broadcast_to(wb_ref[u, kb:kb + 1, :], (SUBLANES, LANES))
                wt = jnp.where(lower, wa, wb)
                acc_h[u] = acc_h[u] + hi * wt
                acc_l[u] = acc_l[u] + lo * wt
        for u in range(TOKEN_UNROLL):
            t = i * TOKEN_UNROLL + u
            ah = acc_h[u] + pltpu.roll(acc_h[u], ROW_WORDS, 0)
            al = acc_l[u] + pltpu.roll(acc_l[u], ROW_WORDS, 0)
            o_ref[t] = x_ref[t] + jnp.where(lower, ah, al)
        return carry

    lax.fori_loop(0, tb // TOKEN_UNROLL, step, 0)


def _expert_specs(tb):
    un = TOKEN_UNROLL
    smem_blks = [pl.BlockSpec((None, tb // un, SLOTS), functools.partial(lambda u, i: (u, i, 0), u),
                              memory_space=pltpu.SMEM) for u in range(un)]
    vmem_blk = pl.BlockSpec((tb, SLOTS), lambda i: (i, 0))
    tok_blk = pl.BlockSpec((tb, D_MODEL // LANES, LANES), lambda i: (i, 0, 0))
    params = pltpu.CompilerParams(dimension_semantics=("arbitrary",), vmem_limit_bytes=VMEM_LIMIT)
    return smem_blks, vmem_blk, tok_blk, params


def _split_offsets(idx):
    n = idx.shape[0]
    return idx.reshape(n // TOKEN_UNROLL, TOKEN_UNROLL, SLOTS).transpose(1, 0, 2)


def _expert_in(h, idx_split, gate, u_packed, tb=256):
    n, d = h.shape
    assert n % tb == 0 and tb % TOKEN_UNROLL == 0 and d == D_MODEL
    smem_blks, vmem_blk, tok_blk, params = _expert_specs(tb)
    return pl.pallas_call(
        _expert_in_kernel,
        grid=(n // tb,),
        in_specs=smem_blks + [tok_blk, vmem_blk, pl.BlockSpec(memory_space=pl.ANY)],
        out_specs=vmem_blk,
        out_shape=jax.ShapeDtypeStruct((n, SLOTS), F32),
        scratch_shapes=[pltpu.VMEM(u_packed.shape, jnp.int32), pltpu.SemaphoreType.DMA, pltpu.VMEM((tb, SLOTS), F32)],
        compiler_params=params,
        name="peer_expert_in",
    )(*([idx_split] * TOKEN_UNROLL), h.reshape(n, d // LANES, LANES), gate, u_packed)


def _expert_out(x, idx_split, w, v_packed, tb=256):
    t, d = x.shape
    n = w.shape[0]
    assert n % tb == 0 and n <= t and d == D_MODEL
    smem_blks, vmem_blk, tok_blk, params = _expert_specs(tb)
    out = pl.pallas_call(
        _expert_out_kernel,
        grid=(n // tb,),
        in_specs=smem_blks + [vmem_blk, tok_blk, pl.BlockSpec(memory_space=pl.ANY)],
        out_specs=tok_blk,
        out_shape=jax.ShapeDtypeStruct((n, d // LANES, LANES), F32),
        scratch_shapes=[pltpu.VMEM(v_packed.shape, jnp.int32), pltpu.SemaphoreType.DMA,
                        pltpu.VMEM((TOKEN_UNROLL, SLOTS, LANES), F32)],
        compiler_params=params,
        name="peer_expert_out",
    )(*([idx_split] * TOKEN_UNROLL), w, x.reshape(t, d // LANES, LANES), v_packed)
    return out.reshape(n, d)


SC_LANES = 16
SC_CORES = 2
SC_WORKERS = SC_CORES * 16
SC_CHUNK = 32
SC_GROUP = 8
SC_TOKENS = 16640


def _sc_params():
    cp = pltpu.CompilerParams()
    if "needs_layout_passes" in pltpu.CompilerParams.__dataclass_fields__:
        cp = dataclasses.replace(cp, needs_layout_passes=False)
    return cp


def _sc_expert_out(table_words, idx, w, x, first_tok):
    d = x.shape[1]
    n_tok = w.shape[0] // SLOTS
    assert n_tok % (SC_WORKERS * SC_GROUP) == 0 and first_tok % SC_GROUP == 0 and d == D_MODEL
    per = n_tok // SC_WORKERS
    words = d // 2
    nq = words // SC_LANES // 2
    nchunk = SLOTS // SC_CHUNK
    group_chunks = SC_GROUP * nchunk
    mesh = plsc.VectorSubcoreMesh(core_axis_name="c", subcore_axis_name="s")

    @functools.partial(
        pl.kernel, mesh=mesh,
        out_type=jax.ShapeDtypeStruct((n_tok, d), F32),
        scratch_types=[pltpu.VMEM((SC_GROUP * SLOTS,), jnp.int32), pltpu.VMEM((SC_GROUP * SLOTS,), F32),
                       pltpu.VMEM((2, SC_CHUNK, words), jnp.int32), pltpu.VMEM((SC_GROUP, d), F32),
                       pltpu.SemaphoreType.DMA, pltpu.SemaphoreType.DMA],
        compiler_params=_sc_params(),
        name="peer_expert_out_sc",
    )
    def body(tab_hbm, idx_hbm, w_hbm, x_hbm, o_hbm, idx_v, w_v, rows_v, y_v, sem0, sem1):
        base = (lax.axis_index("s") * SC_CORES + lax.axis_index("c")) * per
        zero = jnp.zeros((SC_LANES,), jnp.int32)
        sems = (sem0, sem1)

        def gather(k, b):
            off = pl.multiple_of(k * SC_CHUNK, SC_CHUNK)
            return pltpu.make_async_copy(tab_hbm.at[idx_v.at[pl.ds(off, SC_CHUNK)]], rows_v.at[b], sems[b])

        def accumulate(k, b):
            tok = k // nchunk
            for q in range(2):
                first = q * nq
                acc0 = tuple(y_v[tok, pl.ds((first + j) * SC_LANES, SC_LANES)] for j in range(nq))

                def row_body(r, accs):
                    ws = plsc.load_gather(w_v, [zero + (k * SC_CHUNK + r)])
                    hi_acc = []
                    for j in range(nq):
                        wv = rows_v[b, r, pl.ds((first + j) * SC_LANES, SC_LANES)]
                        lo, hi = plsc.unpack(plsc.bitcast(wv, BF16), format=plsc.PackFormat.INTERLEAVED,
                                             preferred_element_type=F32)
                        hi_acc.append(accs[j] + hi * ws)
                        plsc.addupdate(y_v.at[tok, pl.ds(words + (first + j) * SC_LANES, SC_LANES)], lo * ws)
                    return tuple(hi_acc)

                accs = lax.fori_loop(0, SC_CHUNK, row_body, acc0)
                for j in range(nq):
                    y_v[tok, pl.ds((first + j) * SC_LANES, SC_LANES)] = accs[j]

        @pl.loop(0, per // SC_GROUP)
        def _(g):
            t0 = pl.multiple_of(base + g * SC_GROUP, SC_GROUP)
            pltpu.sync_copy(idx_hbm.at[pl.ds(t0 * SLOTS, SC_GROUP * SLOTS)], idx_v)
            pltpu.sync_copy(w_hbm.at[pl.ds(t0 * SLOTS, SC_GROUP * SLOTS)], w_v)
            pltpu.sync_copy(x_hbm.at[pl.ds(first_tok + t0, SC_GROUP)], y_v)
            gather(0, 0).start()

            @pl.loop(0, group_chunks // 2)
            def _(kk):
                k0 = 2 * kk
                gather(k0 + 1, 1).start()
                gather(k0, 0).wait()
                accumulate(k0, 0)

                @pl.when(k0 + 2 < group_chunks)
                def _():
                    gather(k0 + 2, 0).start()
                gather(k0 + 1, 1).wait()
                accumulate(k0 + 1, 1)

            pltpu.sync_copy(y_v, o_hbm.at[pl.ds(t0, SC_GROUP)])

    return body(table_words, idx, w, x)


def _peer(x, gain, w_query, sub_keys, experts_u, experts_v, layer):
    t = x.shape[0]
    t_tc = t - SC_TOKENS
    wq, keys = w_query.astype(BF16), sub_keys.astype(BF16)
    (u_packed,) = _pack_table(experts_u, layer, with_rows=False)
    v_packed, v_rows = _pack_table(experts_v, layer, with_rows=True)

    h, idx, gate = _route(x, gain, wq, keys, t_tc, SC_TOKENS, v_rows[:SUBLANES])
    w_sc = _expert_in(h, _split_offsets(idx), gate, u_packed)
    out_sc = _sc_expert_out(v_rows, (idx // ROW_WORDS).reshape(-1), w_sc.reshape(-1), x, t_tc)

    h, idx, gate = _route(x, gain, wq, keys, 0, t_tc, w_sc[:SUBLANES])
    idx_split = _split_offsets(idx)
    w = _expert_in(h, idx_split, gate, u_packed)
    out_tc = _expert_out(x, idx_split, w, v_packed)
    return jnp.concatenate([out_tc, out_sc], axis=0)


def kernel(x, norm_mix, norm_ffn, attn_w_qkv, attn_q_norm, attn_k_norm, attn_sinks, attn_w_o, conv_w_in, conv_w, conv_w_out, peer_w_query, peer_sub_keys, peer_u, peer_v):
    batch, seq, d = x.shape
    xt = x.reshape(batch * seq, d)
    for i in range(norm_mix.shape[0]):
        j = i // 2
        if i % 2 == 0:
            qkv = _norm_matmul(xt, norm_mix[i], attn_w_qkv[j].astype(BF16))
            o = _attention(qkv, attn_q_norm[j], attn_k_norm[j], attn_sinks[j], batch, seq)
            xt = _matmul_residual(o, attn_w_o[j].astype(BF16), xt)
        else:
            xt = _conv_mixer(xt, norm_mix[i], conv_w_in[j], conv_w[j], conv_w_out[j], batch, seq)
        xt = _peer(xt, norm_ffn[i], peer_w_query[i], peer_sub_keys[i], peer_u, peer_v, i)
    return xt.reshape(batch, seq, d)
```

```python
import dataclasses
import functools
import math

import jax
import jax.numpy as jnp
from jax import lax
from jax.experimental import pallas as pl
from jax.experimental.pallas import tpu as pltpu
from jax.experimental.pallas import tpu_sc as plsc

D_MODEL = 1024
RMS_EPS = 1e-6

HEAD_DIM = 64
N_Q_HEADS = 16
N_KV_HEADS = 4
GROUP = N_Q_HEADS // N_KV_HEADS
WINDOW = 128
ROT_DIM = HEAD_DIM // 4
ROPE_THETA = 500000.0
Q_COLS = N_Q_HEADS * HEAD_DIM
KV_COLS = N_KV_HEADS * HEAD_DIM
NEG_INF = -1e30

CONV_WIDTH = 3

PEER_HEADS = 8
N_KEYS = 128
N_EXPERTS = N_KEYS * N_KEYS
PEER_TOPK = 16
QUERY_HALF = 128
SLOTS = PEER_HEADS * PEER_TOPK

LANES = 128
SUBLANES = 8
ROW_WORDS = D_MODEL // 2 // LANES
VMEM_LIMIT = 48 * 1024 * 1024

BF16 = jnp.bfloat16
F32 = jnp.float32


def _rms(x, gain):
    return x * lax.rsqrt(jnp.mean(x * x, axis=-1, keepdims=True) + RMS_EPS) * gain


def _norm_matmul_kernel(x_ref, g_ref, w_ref, o_ref):
    h = _rms(x_ref[...], g_ref[...])
    o_ref[...] = jnp.dot(h.astype(BF16), w_ref[...], preferred_element_type=F32)


def _norm_matmul(x, gain, w, tm=512):
    t, d = x.shape
    n = w.shape[1]
    return pl.pallas_call(
        _norm_matmul_kernel,
        grid=(t // tm,),
        in_specs=[pl.BlockSpec((tm, d), lambda i: (i, 0)),
                  pl.BlockSpec((1, d), lambda i: (0, 0)),
                  pl.BlockSpec((d, n), lambda i: (0, 0))],
        out_specs=pl.BlockSpec((tm, n), lambda i: (i, 0)),
        out_shape=jax.ShapeDtypeStruct((t, n), F32),
        compiler_params=pltpu.CompilerParams(dimension_semantics=("arbitrary",), vmem_limit_bytes=VMEM_LIMIT),
        name="norm_matmul",
    )(x, gain.reshape(1, d), w)


def _matmul_residual_kernel(a_ref, w_ref, r_ref, o_ref):
    o_ref[...] = r_ref[...] + jnp.dot(a_ref[...].astype(BF16), w_ref[...], preferred_element_type=F32)


def _matmul_residual(a, w, res, tm=512):
    t, k = a.shape
    n = w.shape[1]
    return pl.pallas_call(
        _matmul_residual_kernel,
        grid=(t // tm,),
        in_specs=[pl.BlockSpec((tm, k), lambda i: (i, 0)),
                  pl.BlockSpec((k, n), lambda i: (0, 0)),
                  pl.BlockSpec((tm, n), lambda i: (i, 0))],
        out_specs=pl.BlockSpec((tm, n), lambda i: (i, 0)),
        out_shape=jax.ShapeDtypeStruct((t, n), F32),
        compiler_params=pltpu.CompilerParams(dimension_semantics=("arbitrary",), vmem_limit_bytes=VMEM_LIMIT),
        name="matmul_residual",
    )(a, w, res)


def _rope_tables(seq):
    half = ROT_DIM // 2
    freqs = ROPE_THETA ** (-jnp.arange(0, ROT_DIM, 2, dtype=F32) / ROT_DIM)
    ang = jnp.arange(seq, dtype=F32)[:, None] * freqs[None, :]
    cos, sin = jnp.cos(ang), jnp.sin(ang)
    ones = jnp.ones((seq, HEAD_DIM - ROT_DIM), F32)
    zeros = jnp.zeros((seq, HEAD_DIM - ROT_DIM), F32)
    zh = jnp.zeros((seq, half), F32)
    c = jnp.concatenate([cos, cos, ones], axis=1)
    s_next = jnp.concatenate([-sin, zh, zeros], axis=1)
    s_prev = jnp.concatenate([zh, sin, zeros], axis=1)
    return jnp.stack([jnp.tile(c, (1, 2)), jnp.tile(s_next, (1, 2)), jnp.tile(s_prev, (1, 2))])


def _head_norm_rope(x, gain2, rope, lo):
    sq = x * x
    s_lo = jnp.sum(jnp.where(lo, sq, 0.0), axis=1, keepdims=True)
    s_hi = jnp.sum(jnp.where(lo, 0.0, sq), axis=1, keepdims=True)
    ms = jnp.where(lo, s_lo, s_hi) * (1.0 / HEAD_DIM)
    xn = x * lax.rsqrt(ms + RMS_EPS) * gain2
    half = ROT_DIM // 2
    return xn * rope[0] + pltpu.roll(xn, LANES - half, 1) * rope[1] + pltpu.roll(xn, half, 1) * rope[2]


def _attn_kernel(sinks_ref, q_ref, kc_ref, kp_ref, vc_ref, vp_ref, rc_ref, rp_ref, qg_ref, kg_ref, o_ref):
    n = pl.program_id(1)
    lo = lax.broadcasted_iota(jnp.int32, (WINDOW, LANES), 1) < HEAD_DIM
    rope_c = rc_ref[...]
    rope_p = rp_ref[...]
    qg = qg_ref[...]
    kg = kg_ref[...]

    rows = GROUP * WINDOW
    qi = lax.broadcasted_iota(jnp.int32, (rows, 2 * WINDOW), 0) & (WINDOW - 1)
    ki = lax.broadcasted_iota(jnp.int32, (rows, 2 * WINDOW), 1)
    rel = WINDOW + qi - ki
    valid = (rel >= 0) & (rel < WINDOW) & ((n > 0) | (ki >= WINDOW))
    head_of_row = lax.broadcasted_iota(jnp.int32, (rows, 1), 0) // WINDOW
    scale = 1.0 / math.sqrt(HEAD_DIM)

    q2 = [_head_norm_rope(q_ref[:, c * LANES:(c + 1) * LANES], qg, rope_c, lo).astype(BF16)
          for c in range(Q_COLS // LANES)]
    for c in range(KV_COLS // LANES):
        cols = slice(c * LANES, (c + 1) * LANES)
        kprev = _head_norm_rope(kp_ref[:, cols], kg, rope_p, lo)
        kcur = _head_norm_rope(kc_ref[:, cols], kg, rope_c, lo)
        kfull = jnp.concatenate([kprev, kcur], axis=0).astype(BF16)
        vfull = jnp.concatenate([vp_ref[:, cols], vc_ref[:, cols]], axis=0).astype(BF16)
        for hh in range(LANES // HEAD_DIM):
            h = (LANES // HEAD_DIM) * c + hh
            kh = kfull[:, hh * HEAD_DIM:(hh + 1) * HEAD_DIM]
            vh = vfull[:, hh * HEAD_DIM:(hh + 1) * HEAD_DIM]
            heads = [GROUP * h + g for g in range(GROUP)]
            q4 = jnp.concatenate([q2[j // 2][:, (j % 2) * HEAD_DIM:(j % 2 + 1) * HEAD_DIM] for j in heads], axis=0)
            s = lax.dot_general(q4, kh, (((1,), (1,)), ((), ())), preferred_element_type=F32) * scale
            s = jnp.where(valid, s, NEG_INF)
            sink = jnp.zeros((rows, 1), F32)
            for g, j in enumerate(heads):
                sink = jnp.where(head_of_row == g, sinks_ref[j], sink)
            m = jnp.maximum(jnp.max(s, axis=1, keepdims=True), sink)
            p = jnp.exp(s - m)
            denom = jnp.sum(p, axis=1, keepdims=True) + jnp.exp(sink - m)
            o = jnp.dot(p.astype(BF16), vh, preferred_element_type=F32) / denom
            for g, j in enumerate(heads):
                o_ref[:, j * HEAD_DIM:(j + 1) * HEAD_DIM] = o[g * WINDOW:(g + 1) * WINDOW]


def _attention(qkv, q_gain, k_gain, sinks, batch, seq):
    t = batch * seq
    nb = seq // WINDOW
    rope = _rope_tables(seq)
    kcol = Q_COLS // KV_COLS
    cur = lambda b, n: (b * nb + n, 0)
    kcur = lambda b, n: (b * nb + n, kcol)
    kprev = lambda b, n: (b * nb + jnp.maximum(n - 1, 0), kcol)
    vcur = lambda b, n: (b * nb + n, kcol + 1)
    vprev = lambda b, n: (b * nb + jnp.maximum(n - 1, 0), kcol + 1)
    return pl.pallas_call(
        _attn_kernel,
        grid=(batch, nb),
        in_specs=[pl.BlockSpec(memory_space=pltpu.SMEM),
                  pl.BlockSpec((WINDOW, Q_COLS), cur),
                  pl.BlockSpec((WINDOW, KV_COLS), kcur),
                  pl.BlockSpec((WINDOW, KV_COLS), kprev),
                  pl.BlockSpec((WINDOW, KV_COLS), vcur),
                  pl.BlockSpec((WINDOW, KV_COLS), vprev),
                  pl.BlockSpec((3, WINDOW, LANES), lambda b, n: (0, n, 0)),
                  pl.BlockSpec((3, WINDOW, LANES), lambda b, n: (0, jnp.maximum(n - 1, 0), 0)),
                  pl.BlockSpec((1, LANES), lambda b, n: (0, 0)),
                  pl.BlockSpec((1, LANES), lambda b, n: (0, 0))],
        out_specs=pl.BlockSpec((WINDOW, Q_COLS), cur),
        out_shape=jax.ShapeDtypeStruct((t, Q_COLS), F32),
        compiler_params=pltpu.CompilerParams(dimension_semantics=("arbitrary", "arbitrary"),
                                             vmem_limit_bytes=VMEM_LIMIT),
        name="swa_attention",
    )(sinks, qkv, qkv, qkv, qkv, qkv, rope, rope,
      jnp.tile(q_gain, 2).reshape(1, LANES), jnp.tile(k_gain, 2).reshape(1, LANES))


def _conv_kernel(x_ref, g_ref, win_ref, cw_ref, wout_ref, o_ref, zprev_ref):
    n = pl.program_id(1)
    d = D_MODEL

    @pl.when(n == 0)
    def _():
        zprev_ref[...] = jnp.zeros_like(zprev_ref)

    x = x_ref[...]
    h = _rms(x, g_ref[...])
    bcu = jnp.dot(h.astype(BF16), win_ref[...], preferred_element_type=F32)
    gate_b = bcu[:, :d]
    z = bcu[:, d:2 * d] * bcu[:, 2 * d:]
    tm = z.shape[0]
    row = lax.broadcasted_iota(jnp.int32, z.shape, 0)
    prev = zprev_ref[...]
    p_last = prev[SUBLANES - 1:SUBLANES, :]
    p_last2 = prev[SUBLANES - 2:SUBLANES - 1, :]
    z1 = jnp.where(row == 0, p_last, pltpu.roll(z, 1, 0))
    z2 = jnp.where(row == 0, p_last2, jnp.where(row == 1, p_last, pltpu.roll(z, 2, 0)))
    cw = cw_ref[...]
    conv = cw[0:1, :] * z2 + cw[1:2, :] * z1 + cw[2:3, :] * z
    zprev_ref[...] = z[tm - SUBLANES:, :]
    o_ref[...] = x + jnp.dot((gate_b * conv).astype(BF16), wout_ref[...], preferred_element_type=F32)


def _conv_mixer(x, gain, w_in, conv_w, w_out, batch, seq, tm=256):
    t, d = x.shape
    nblk = seq // tm
    blk = lambda b, n: (b * nblk + n, 0)
    const = lambda b, n: (0, 0)
    return pl.pallas_call(
        _conv_kernel,
        grid=(batch, nblk),
        in_specs=[pl.BlockSpec((tm, d), blk),
                  pl.BlockSpec((1, d), const),
                  pl.BlockSpec((d, 3 * d), const),
                  pl.BlockSpec((CONV_WIDTH, d), const),
                  pl.BlockSpec((d, d), const)],
        out_specs=pl.BlockSpec((tm, d), blk),
        out_shape=jax.ShapeDtypeStruct((t, d), F32),
        scratch_shapes=[pltpu.VMEM((SUBLANES, d), F32)],
        compiler_params=pltpu.CompilerParams(dimension_semantics=("arbitrary", "arbitrary"),
                                             vmem_limit_bytes=VMEM_LIMIT),
        name="conv_mixer",
    )(x, gain.reshape(1, d), w_in.astype(BF16), conv_w, w_out.astype(BF16))


def _topk_axis0(s, k, ids=None, payload=None):
    n, tm = s.shape
    if ids is None:
        ids = lax.broadcasted_iota(jnp.int32, (n, tm), 0)
    krow = lax.broadcasted_iota(jnp.int32, (k, tm), 0)
    vals = jnp.zeros((k, tm), F32)
    picks = jnp.zeros((k, tm), jnp.int32)
    for r in range(k):
        m = jnp.max(s, axis=0, keepdims=True)
        pos = jnp.min(jnp.where(s == m, ids, jnp.iinfo(jnp.int32).max), axis=0, keepdims=True)
        sel = ids == pos
        if payload is None:
            picked = pos
        else:
            picked = jnp.sum(jnp.where(sel, payload, 0), axis=0, keepdims=True)
        vals = jnp.where(krow == r, m, vals)
        picks = jnp.where(krow == r, picked, picks)
        s = jnp.where(sel, -jnp.inf, s)
    return vals, picks


def _pair_candidates(s1, i1, s2, i2):
    k, tm = s1.shape
    sub = lax.broadcasted_iota(jnp.int32, (SUBLANES, tm), 0)
    s2a, i2a = s2[0:SUBLANES, :], i2[0:SUBLANES, :]
    scores = [s1[0:1, :] + s2, s1[1:2, :] + s2a]
    flat = [lax.broadcasted_iota(jnp.int32, (k, tm), 0), k + sub]
    expert = [i1[0:1, :] * N_KEYS + i2, i1[1:2, :] * N_KEYS + i2a]
    for n, tile in enumerate((((2, 0), (5, 5)), ((3, 0), (4, 4)), ((6, 0), (7, 2)))):
        sc = jnp.full((SUBLANES, tm), -jnp.inf, F32)
        fl = k * k + n * SUBLANES + sub
        ex = jnp.zeros((SUBLANES, tm), jnp.int32)
        for i, off in tile:
            inside = (sub >= off) & (sub < off + k // (i + 1))
            s2r = s2a if off == 0 else pltpu.roll(s2a, off, 0)
            i2r = i2a if off == 0 else pltpu.roll(i2a, off, 0)
            sc = jnp.where(inside, s1[i:i + 1, :] + s2r, sc)
            fl = jnp.where(inside, i * k + sub - off, fl)
            ex = jnp.where(inside, i1[i:i + 1, :] * N_KEYS + i2r, ex)
        scores.append(sc)
        flat.append(fl)
        expert.append(ex)
    scores.append(s1[k // 2:, :] + s2[0:1, :])
    flat.append((k // 2 + sub) * k)
    expert.append(i1[k // 2:, :] * N_KEYS + i2[0:1, :])
    return jnp.concatenate(scores, axis=0), jnp.concatenate(flat, axis=0), jnp.concatenate(expert, axis=0)


def _route_kernel(x_ref, g_ref, wq_ref, keys_ref, after_ref, h_ref, idx_ref, gate_ref):
    del after_ref
    h = _rms(x_ref[...], g_ref[...])
    h_ref[...] = h
    q = jnp.dot(h.astype(BF16), wq_ref[...], preferred_element_type=F32).astype(BF16)
    idx_rows, gate_rows = [], []
    for head in range(PEER_HEADS):
        tops = []
        for part in range(2):
            col = (head * 2 + part) * QUERY_HALF
            s = lax.dot_general(keys_ref[head, part], q[:, col:col + QUERY_HALF],
                                (((1,), (1,)), ((), ())), preferred_element_type=F32)
            tops.append(_topk_axis0(s, PEER_TOPK))
        (s1, i1), (s2, i2) = tops
        cand, flat_ids, cand_idx = _pair_candidates(s1, i1, s2, i2)
        g_s, e_idx = _topk_axis0(cand, PEER_TOPK, ids=flat_ids, payload=cand_idx)
        e = jnp.exp(g_s - jnp.max(g_s, axis=0, keepdims=True))
        gate_rows.append(e / jnp.sum(e, axis=0, keepdims=True))
        idx_rows.append(e_idx * ROW_WORDS)
    idx_ref[...] = jnp.concatenate(idx_rows, axis=0).T
    gate_ref[...] = jnp.concatenate(gate_rows, axis=0).T


def _route(x, gain, w_query, sub_keys, first_tok, n_tok, after, tm=128):
    t, d = x.shape
    nq = w_query.shape[1]
    first_blk = first_tok // tm
    return pl.pallas_call(
        _route_kernel,
        grid=(n_tok // tm,),
        in_specs=[pl.BlockSpec((tm, d), lambda i: (i + first_blk, 0)),
                  pl.BlockSpec((1, d), lambda i: (0, 0)),
                  pl.BlockSpec((d, nq), lambda i: (0, 0)),
                  pl.BlockSpec((PEER_HEADS, 2, N_KEYS, QUERY_HALF), lambda i: (0, 0, 0, 0)),
                  pl.BlockSpec(after.shape, lambda i: (0, 0))],
        out_specs=[pl.BlockSpec((tm, d), lambda i: (i, 0)),
                   pl.BlockSpec((tm, SLOTS), lambda i: (i, 0)),
                   pl.BlockSpec((tm, SLOTS), lambda i: (i, 0))],
        out_shape=[jax.ShapeDtypeStruct((n_tok, d), F32),
                   jax.ShapeDtypeStruct((n_tok, SLOTS), jnp.int32),
                   jax.ShapeDtypeStruct((n_tok, SLOTS), F32)],
        compiler_params=pltpu.CompilerParams(dimension_semantics=("arbitrary",), vmem_limit_bytes=VMEM_LIMIT),
        name="peer_route",
    )(x, gain.reshape(1, d), w_query, sub_keys, after)


def _pack_kernel(t_ref, tiles_ref, *rows_ref):
    x = t_ref[...]
    half = x.shape[1] // 2
    hi = pltpu.bitcast(x[:, :half].astype(BF16).astype(F32), jnp.int32)
    lo = pltpu.bitcast(x[:, half:].astype(BF16).astype(F32), jnp.int32)
    words = hi | lax.shift_right_logical(lo, 16)
    for ref in rows_ref:
        ref[...] = words
    for s in range(ROW_WORDS):
        tiles_ref[pl.ds(s, x.shape[0], stride=ROW_WORDS), :] = words[:, s * LANES:(s + 1) * LANES]


def _pack_table(tabs, layer, with_rows, rows_per_step=256):
    _, ne, d = tabs.shape
    out_specs = [pl.BlockSpec((rows_per_step * ROW_WORDS, LANES), lambda i: (i, 0))]
    out_shape = [jax.ShapeDtypeStruct((ne * ROW_WORDS, LANES), jnp.int32)]
    if with_rows:
        out_specs.append(pl.BlockSpec((rows_per_step, d // 2), lambda i: (i, 0)))
        out_shape.append(jax.ShapeDtypeStruct((ne, d // 2), jnp.int32))
    return pl.pallas_call(
        _pack_kernel,
        grid=(ne // rows_per_step,),
        in_specs=[pl.BlockSpec((None, rows_per_step, d), lambda i: (layer, i, 0))],
        out_specs=out_specs,
        out_shape=out_shape,
        compiler_params=pltpu.CompilerParams(dimension_semantics=("arbitrary",), vmem_limit_bytes=VMEM_LIMIT),
        name="pack_table",
    )(tabs)


def _load_table_once(tab_hbm, tab, sem):
    @pl.when(pl.program_id(0) == 0)
    def _():
        cp = pltpu.make_async_copy(tab_hbm, tab, sem)
        cp.start()
        cp.wait()


def _gather_pair(tab, off_a, off_b):
    ra = tab[pl.ds(pl.multiple_of(off_a, ROW_WORDS), ROW_WORDS), :]
    rb = tab[pl.ds(pl.multiple_of(off_b, ROW_WORDS), ROW_WORDS), :]
    words = jnp.concatenate([ra, rb], axis=0)
    hi = pltpu.bitcast(words & jnp.int32(-65536), F32)
    lo = pltpu.bitcast(words << 16, F32)
    return hi, lo


TOKEN_UNROLL = 8


def _expert_in_kernel(*refs):
    idx_refs = refs[:TOKEN_UNROLL]
    h_ref, gate_ref, tab_hbm, w_ref, tab, sem, a_ref = refs[TOKEN_UNROLL:]
    _load_table_once(tab_hbm, tab, sem)
    sub = lax.broadcasted_iota(jnp.int32, (SUBLANES, LANES), 0)
    lane = lax.broadcasted_iota(jnp.int32, (SUBLANES, LANES), 1)
    own_half = (sub >= ROW_WORDS) == ((lane & 1) == 1)
    tb = h_ref.shape[0]

    def step(i, carry):
        xs = []
        for u in range(TOKEN_UNROLL):
            x = h_ref[i * TOKEN_UNROLL + u]
            xs.append((jnp.concatenate([x[0:ROW_WORDS], x[0:ROW_WORDS]], axis=0),
                       jnp.concatenate([x[ROW_WORDS:], x[ROW_WORDS:]], axis=0)))
        accs = [jnp.zeros((SUBLANES, LANES), F32) for _ in range(TOKEN_UNROLL)]
        for p in range(SLOTS // 2):
            for u in range(TOKEN_UNROLL):
                hi, lo = _gather_pair(tab, idx_refs[u][i, 2 * p], idx_refs[u][i, 2 * p + 1])
                part = jnp.sum(hi * xs[u][0] + lo * xs[u][1], axis=1, keepdims=True)
                accs[u] = jnp.where((lane >> 1) == p, part, accs[u])
        for u in range(TOKEN_UNROLL):
            a_ref[pl.ds(i * TOKEN_UNROLL + u, 1), :] = jnp.sum(jnp.where(own_half, accs[u], 0.0),
                                                                axis=0, keepdims=True)
        return carry

    lax.fori_loop(0, tb // TOKEN_UNROLL, step, 0)
    a = a_ref[...]
    w_ref[...] = gate_ref[...] * (0.5 * a * (1.0 + lax.erf(a * (1.0 / math.sqrt(2.0)))))


def _expert_out_kernel(*refs):
    idx_refs = refs[:TOKEN_UNROLL]
    w_ref, x_ref, tab_hbm, o_ref, tab, sem, wb_ref = refs[TOKEN_UNROLL:]
    _load_table_once(tab_hbm, tab, sem)
    sub = lax.broadcasted_iota(jnp.int32, (SUBLANES, LANES), 0)
    lower = sub < ROW_WORDS
    tb = x_ref.shape[0]

    def step(i, carry):
        for u in range(TOKEN_UNROLL):
            row = w_ref[pl.ds(i * TOKEN_UNROLL + u, 1), :]
            wb_ref[u] = jnp.broadcast_to(row, (SLOTS, LANES)).T
        acc_h = [jnp.zeros((SUBLANES, LANES), F32) for _ in range(TOKEN_UNROLL)]
        acc_l = [jnp.zeros((SUBLANES, LANES), F32) for _ in range(TOKEN_UNROLL)]
        for p in range(SLOTS // 2):
            ka, kb = 2 * p, 2 * p + 1
            for u in range(TOKEN_UNROLL):
                hi, lo = _gather_pair(tab, idx_refs[u][i, ka], idx_refs[u][i, kb])
                wa = jnp.broadcast_to(wb_ref[u, ka:ka + 1, :], (SUBLANES, LANES))
                wb = jnp.broadcast_to(wb_ref[u, kb:kb + 1, :], (SUBLANES, LANES))
                wt = jnp.where(lower, wa, wb)
                acc_h[u] = acc_h[u] + hi * wt
                acc_l[u] = acc_l[u] + lo * wt
        for u in range(TOKEN_UNROLL):
            t = i * TOKEN_UNROLL + u
            ah = acc_h[u] + pltpu.roll(acc_h[u], ROW_WORDS, 0)
            al = acc_l[u] + pltpu.roll(acc_l[u], ROW_WORDS, 0)
            o_ref[t] = x_ref[t] + jnp.where(lower, ah, al)
        return carry

    lax.fori_loop(0, tb // TOKEN_UNROLL, step, 0)


def _expert_specs(tb):
    un = TOKEN_UNROLL
    smem_blks = [pl.BlockSpec((None, tb // un, SLOTS), functools.partial(lambda u, i: (u, i, 0), u),
                              memory_space=pltpu.SMEM) for u in range(un)]
    vmem_blk = pl.BlockSpec((tb, SLOTS), lambda i: (i, 0))
    tok_blk = pl.BlockSpec((tb, D_MODEL // LANES, LANES), lambda i: (i, 0, 0))
    params = pltpu.CompilerParams(dimension_semantics=("arbitrary",), vmem_limit_bytes=VMEM_LIMIT)
    return smem_blks, vmem_blk, tok_blk, params


def _split_offsets(idx):
    n = idx.shape[0]
    return idx.reshape(n // TOKEN_UNROLL, TOKEN_UNROLL, SLOTS).transpose(1, 0, 2)


def _expert_in(h, idx_split, gate, u_packed, tb=256):
    n, d = h.shape
    assert n % tb == 0 and tb % TOKEN_UNROLL == 0 and d == D_MODEL
    smem_blks, vmem_blk, tok_blk, params = _expert_specs(tb)
    return pl.pallas_call(
        _expert_in_kernel,
        grid=(n // tb,),
        in_specs=smem_blks + [tok_blk, vmem_blk, pl.BlockSpec(memory_space=pl.ANY)],
        out_specs=vmem_blk,
        out_shape=jax.ShapeDtypeStruct((n, SLOTS), F32),
        scratch_shapes=[pltpu.VMEM(u_packed.shape, jnp.int32), pltpu.SemaphoreType.DMA, pltpu.VMEM((tb, SLOTS), F32)],
        compiler_params=params,
        name="peer_expert_in",
    )(*([idx_split] * TOKEN_UNROLL), h.reshape(n, d // LANES, LANES), gate, u_packed)


def _expert_out(x, idx_split, w, v_packed, tb=256):
    t, d = x.shape
    n = w.shape[0]
    assert n % tb == 0 and n <= t and d == D_MODEL
    smem_blks, vmem_blk, tok_blk, params = _expert_specs(tb)
    out = pl.pallas_call(
        _expert_out_kernel,
        grid=(n // tb,),
        in_specs=smem_blks + [vmem_blk, tok_blk, pl.BlockSpec(memory_space=pl.ANY)],
        out_specs=tok_blk,
        out_shape=jax.ShapeDtypeStruct((n, d // LANES, LANES), F32),
        scratch_shapes=[pltpu.VMEM(v_packed.shape, jnp.int32), pltpu.SemaphoreType.DMA,
                        pltpu.VMEM((TOKEN_UNROLL, SLOTS, LANES), F32)],
        compiler_params=params,
        name="peer_expert_out",
    )(*([idx_split] * TOKEN_UNROLL), w, x.reshape(t, d // LANES, LANES), v_packed)
    return out.reshape(n, d)


SC_LANES = 16
SC_CORES = 2
SC_WORKERS = SC_CORES * 16
SC_CHUNK = 32
SC_GROUP = 8
SC_TOKENS = 16640


def _sc_params():
    cp = pltpu.CompilerParams()
    if "needs_layout_passes" in pltpu.CompilerParams.__dataclass_fields__:
        cp = dataclasses.replace(cp, needs_layout_passes=False)
    return cp


def _sc_expert_out(table_words, idx, w, x, first_tok):
    d = x.shape[1]
    n_tok = w.shape[0] // SLOTS
    assert n_tok % (SC_WORKERS * SC_GROUP) == 0 and first_tok % SC_GROUP == 0 and d == D_MODEL
    per = n_tok // SC_WORKERS
    words = d // 2
    nq = words // SC_LANES // 2
    nchunk = SLOTS // SC_CHUNK
    group_chunks = SC_GROUP * nchunk
    mesh = plsc.VectorSubcoreMesh(core_axis_name="c", subcore_axis_name="s")

    @functools.partial(
        pl.kernel, mesh=mesh,
        out_type=jax.ShapeDtypeStruct((n_tok, d), F32),
        scratch_types=[pltpu.VMEM((SC_GROUP * SLOTS,), jnp.int32), pltpu.VMEM((SC_GROUP * SLOTS,), F32),
                       pltpu.VMEM((2, SC_CHUNK, words), jnp.int32), pltpu.VMEM((SC_GROUP, d), F32),
                       pltpu.SemaphoreType.DMA, pltpu.SemaphoreType.DMA],
        compiler_params=_sc_params(),
        name="peer_expert_out_sc",
    )
    def body(tab_hbm, idx_hbm, w_hbm, x_hbm, o_hbm, idx_v, w_v, rows_v, y_v, sem0, sem1):
        base = (lax.axis_index("s") * SC_CORES + lax.axis_index("c")) * per
        zero = jnp.zeros((SC_LANES,), jnp.int32)
        sems = (sem0, sem1)

        def gather(k, b):
            off = pl.multiple_of(k * SC_CHUNK, SC_CHUNK)
            return pltpu.make_async_copy(tab_hbm.at[idx_v.at[pl.ds(off, SC_CHUNK)]], rows_v.at[b], sems[b])

        def accumulate(k, b):
            tok = k // nchunk
            for q in range(2):
                first = q * nq
                acc0 = (tuple(y_v[tok, pl.ds((first + j) * SC_LANES, SC_LANES)] for j in range(nq))
                        + tuple(y_v[tok, pl.ds(words + (first + j) * SC_LANES, SC_LANES)] for j in range(nq)))

                def row_body(r, accs):
                    ws = plsc.load_gather(w_v, [zero + (k * SC_CHUNK + r)])
                    hi_acc, lo_acc = [], []
                    for j in range(nq):
                        wv = rows_v[b, r, pl.ds((first + j) * SC_LANES, SC_LANES)]
                        lo, hi = plsc.unpack(plsc.bitcast(wv, BF16), format=plsc.PackFormat.INTERLEAVED,
                                             preferred_element_type=F32)
                        hi_acc.append(accs[j] + hi * ws)
                        lo_acc.append(accs[nq + j] + lo * ws)
                    return tuple(hi_acc) + tuple(lo_acc)

                accs = lax.fori_loop(0, SC_CHUNK, row_body, acc0)
                for j in range(nq):
                    y_v[tok, pl.ds((first + j) * SC_LANES, SC_LANES)] = accs[j]
                    y_v[tok, pl.ds(words + (first + j) * SC_LANES, SC_LANES)] = accs[nq + j]

        @pl.loop(0, per // SC_GROUP)
        def _(g):
            t0 = pl.multiple_of(base + g * SC_GROUP, SC_GROUP)
            pltpu.sync_copy(idx_hbm.at[pl.ds(t0 * SLOTS, SC_GROUP * SLOTS)], idx_v)
            pltpu.sync_copy(w_hbm.at[pl.ds(t0 * SLOTS, SC_GROUP * SLOTS)], w_v)
            pltpu.sync_copy(x_hbm.at[pl.ds(first_tok + t0, SC_GROUP)], y_v)
            gather(0, 0).start()

            @pl.loop(0, group_chunks // 2)
            def _(kk):
                k0 = 2 * kk
                gather(k0 + 1, 1).start()
                gather(k0, 0).wait()
                accumulate(k0, 0)

                @pl.when(k0 + 2 < group_chunks)
                def _():
                    gather(k0 + 2, 0).start()
                gather(k0 + 1, 1).wait()
                accumulate(k0 + 1, 1)

            pltpu.sync_copy(y_v, o_hbm.at[pl.ds(t0, SC_GROUP)])

    return body(table_words, idx, w, x)


def _peer(x, gain, w_query, sub_keys, experts_u, experts_v, layer):
    t = x.shape[0]
    t_tc = t - SC_TOKENS
    wq, keys = w_query.astype(BF16), sub_keys.astype(BF16)
    (u_packed,) = _pack_table(experts_u, layer, with_rows=False)
    v_packed, v_rows = _pack_table(experts_v, layer, with_rows=True)

    h, idx, gate = _route(x, gain, wq, keys, t_tc, SC_TOKENS, v_rows[:SUBLANES])
    w_sc = _expert_in(h, _split_offsets(idx), gate, u_packed)
    out_sc = _sc_expert_out(v_rows, (idx // ROW_WORDS).reshape(-1), w_sc.reshape(-1), x, t_tc)

    h, idx, gate = _route(x, gain, wq, keys, 0, t_tc, w_sc[:SUBLANES])
    idx_split = _split_offsets(idx)
    w = _expert_in(h, idx_split, gate, u_packed)
    out_tc = _expert_out(x, idx_split, w, v_packed)
    return jnp.concatenate([out_tc, out_sc], axis=0)


def kernel(x, norm_mix, norm_ffn, attn_w_qkv, attn_q_norm, attn_k_norm, attn_sinks, attn_w_o, conv_w_in, conv_w, conv_w_out, peer_w_query, peer_sub_keys, peer_u, peer_v):
    batch, seq, d = x.shape
    xt = x.reshape(batch * seq, d)
    for i in range(norm_mix.shape[0]):
        j = i // 2
        if i % 2 == 0:
            qkv = _norm_matmul(xt, norm_mix[i], attn_w_qkv[j].astype(BF16))
            o = _attention(qkv, attn_q_norm[j], attn_k_norm[j], attn_sinks[j], batch, seq)
            xt = _matmul_residual(o, attn_w_o[j].astype(BF16), xt)
        else:
            xt = _conv_mixer(xt, norm_mix[i], conv_w_in[j], conv_w[j], conv_w_out[j], batch, seq)
        xt = _peer(xt, norm_ffn[i], peer_w_query[i], peer_sub_keys[i], peer_u, peer_v, i)
    return xt.reshape(batch, seq, d)
```

```python
import dataclasses
import functools
import math

import jax
import jax.numpy as jnp
from jax import lax
from jax.experimental import pallas as pl
from jax.experimental.pallas import tpu as pltpu
from jax.experimental.pallas import tpu_sc as plsc

D_MODEL = 1024
RMS_EPS = 1e-6

HEAD_DIM = 64
N_Q_HEADS = 16
N_KV_HEADS = 4
GROUP = N_Q_HEADS // N_KV_HEADS
WINDOW = 128
ROT_DIM = HEAD_DIM // 4
ROPE_THETA = 500000.0
Q_COLS = N_Q_HEADS * HEAD_DIM
KV_COLS = N_KV_HEADS * HEAD_DIM
NEG_INF = -1e30

CONV_WIDTH = 3

PEER_HEADS = 8
N_KEYS = 128
N_EXPERTS = N_KEYS * N_KEYS
PEER_TOPK = 16
QUERY_HALF = 128
SLOTS = PEER_HEADS * PEER_TOPK

LANES = 128
SUBLANES = 8
ROW_WORDS = D_MODEL // 2 // LANES
VMEM_LIMIT = 48 * 1024 * 1024

BF16 = jnp.bfloat16
F32 = jnp.float32


def _rms(x, gain):
    return x * lax.rsqrt(jnp.mean(x * x, axis=-1, keepdims=True) + RMS_EPS) * gain


def _norm_matmul_kernel(x_ref, g_ref, w_ref, o_ref):
    h = _rms(x_ref[...], g_ref[...])
    o_ref[...] = jnp.dot(h.astype(BF16), w_ref[...], preferred_element_type=F32)


def _norm_matmul(x, gain, w, tm=512):
    t, d = x.shape
    n = w.shape[1]
    return pl.pallas_call(
        _norm_matmul_kernel,
        grid=(t // tm,),
        in_specs=[pl.BlockSpec((tm, d), lambda i: (i, 0)),
                  pl.BlockSpec((1, d), lambda i: (0, 0)),
                  pl.BlockSpec((d, n), lambda i: (0, 0))],
        out_specs=pl.BlockSpec((tm, n), lambda i: (i, 0)),
        out_shape=jax.ShapeDtypeStruct((t, n), F32),
        compiler_params=pltpu.CompilerParams(dimension_semantics=("arbitrary",), vmem_limit_bytes=VMEM_LIMIT),
        name="norm_matmul",
    )(x, gain.reshape(1, d), w)


def _matmul_residual_kernel(a_ref, w_ref, r_ref, o_ref):
    o_ref[...] = r_ref[...] + jnp.dot(a_ref[...].astype(BF16), w_ref[...], preferred_element_type=F32)


def _matmul_residual(a, w, res, tm=512):
    t, k = a.shape
    n = w.shape[1]
    return pl.pallas_call(
        _matmul_residual_kernel,
        grid=(t // tm,),
        in_specs=[pl.BlockSpec((tm, k), lambda i: (i, 0)),
                  pl.BlockSpec((k, n), lambda i: (0, 0)),
                  pl.BlockSpec((tm, n), lambda i: (i, 0))],
        out_specs=pl.BlockSpec((tm, n), lambda i: (i, 0)),
        out_shape=jax.ShapeDtypeStruct((t, n), F32),
        compiler_params=pltpu.CompilerParams(dimension_semantics=("arbitrary",), vmem_limit_bytes=VMEM_LIMIT),
        name="matmul_residual",
    )(a, w, res)


def _rope_tables(seq):
    half = ROT_DIM // 2
    freqs = ROPE_THETA ** (-jnp.arange(0, ROT_DIM, 2, dtype=F32) / ROT_DIM)
    ang = jnp.arange(seq, dtype=F32)[:, None] * freqs[None, :]
    cos, sin = jnp.cos(ang), jnp.sin(ang)
    ones = jnp.ones((seq, HEAD_DIM - ROT_DIM), F32)
    zeros = jnp.zeros((seq, HEAD_DIM - ROT_DIM), F32)
    zh = jnp.zeros((seq, half), F32)
    c = jnp.concatenate([cos, cos, ones], axis=1)
    s_next = jnp.concatenate([-sin, zh, zeros], axis=1)
    s_prev = jnp.concatenate([zh, sin, zeros], axis=1)
    return jnp.stack([jnp.tile(c, (1, 2)), jnp.tile(s_next, (1, 2)), jnp.tile(s_prev, (1, 2))])


def _head_norm_rope(x, gain2, rope, lo):
    sq = x * x
    s_lo = jnp.sum(jnp.where(lo, sq, 0.0), axis=1, keepdims=True)
    s_hi = jnp.sum(jnp.where(lo, 0.0, sq), axis=1, keepdims=True)
    ms = jnp.where(lo, s_lo, s_hi) * (1.0 / HEAD_DIM)
    xn = x * lax.rsqrt(ms + RMS_EPS) * gain2
    half = ROT_DIM // 2
    return xn * rope[0] + pltpu.roll(xn, LANES - half, 1) * rope[1] + pltpu.roll(xn, half, 1) * rope[2]


ATTN_WINDOWS = 2


def _attn_kernel(sinks_ref, q_ref, kc_ref, kp_ref, vc_ref, vp_ref, rc_ref, rp_ref, qg_ref, kg_ref, o_ref):
    n = pl.program_id(1)
    lo = lax.broadcasted_iota(jnp.int32, (WINDOW, LANES), 1) < HEAD_DIM
    qg = qg_ref[...]
    kg = kg_ref[...]
    win = lambda w: slice(w * WINDOW, (w + 1) * WINDOW)
    ropes = [rp_ref[...]] + [rc_ref[:, win(w), :] for w in range(ATTN_WINDOWS)]

    rows = GROUP * WINDOW
    qi = lax.broadcasted_iota(jnp.int32, (rows, 2 * WINDOW), 0) & (WINDOW - 1)
    ki = lax.broadcasted_iota(jnp.int32, (rows, 2 * WINDOW), 1)
    rel = WINDOW + qi - ki
    band = (rel >= 0) & (rel < WINDOW)
    band_first = band & ((n > 0) | (ki >= WINDOW))
    head_of_row = lax.broadcasted_iota(jnp.int32, (rows, 1), 0) // WINDOW
    scale = 1.0 / math.sqrt(HEAD_DIM)

    q2 = [[_head_norm_rope(q_ref[win(w), c * LANES:(c + 1) * LANES], qg, ropes[w + 1], lo).astype(BF16)
           for c in range(Q_COLS // LANES)] for w in range(ATTN_WINDOWS)]
    for c in range(KV_COLS // LANES):
        cols = slice(c * LANES, (c + 1) * LANES)
        kn = [_head_norm_rope(kp_ref[:, cols], kg, ropes[0], lo)]
        vn = [vp_ref[:, cols]]
        for w in range(ATTN_WINDOWS):
            kn.append(_head_norm_rope(kc_ref[win(w), cols], kg, ropes[w + 1], lo))
            vn.append(vc_ref[win(w), cols])
        for w in range(ATTN_WINDOWS):
            kfull = jnp.concatenate([kn[w], kn[w + 1]], axis=0).astype(BF16)
            vfull = jnp.concatenate([vn[w], vn[w + 1]], axis=0).astype(BF16)
            valid = band_first if w == 0 else band
            for hh in range(LANES // HEAD_DIM):
                h = (LANES // HEAD_DIM) * c + hh
                kh = kfull[:, hh * HEAD_DIM:(hh + 1) * HEAD_DIM]
                vh = vfull[:, hh * HEAD_DIM:(hh + 1) * HEAD_DIM]
                heads = [GROUP * h + g for g in range(GROUP)]
                q4 = jnp.concatenate([q2[w][j // 2][:, (j % 2) * HEAD_DIM:(j % 2 + 1) * HEAD_DIM] for j in heads],
                                     axis=0)
                s = lax.dot_general(q4, kh, (((1,), (1,)), ((), ())), preferred_element_type=F32) * scale
                s = jnp.where(valid, s, NEG_INF)
                sink = jnp.zeros((rows, 1), F32)
                for g, j in enumerate(heads):
                    sink = jnp.where(head_of_row == g, sinks_ref[j], sink)
                m = jnp.maximum(jnp.max(s, axis=1, keepdims=True), sink)
                p = jnp.exp(s - m)
                denom = jnp.sum(p, axis=1, keepdims=True) + jnp.exp(sink - m)
                o = jnp.dot(p.astype(BF16), vh, preferred_element_type=F32) / denom
                for g, j in enumerate(heads):
                    o_ref[win(w), j * HEAD_DIM:(j + 1) * HEAD_DIM] = o[g * WINDOW:(g + 1) * WINDOW]


def _attention(qkv, q_gain, k_gain, sinks, batch, seq):
    t = batch * seq
    nb = seq // WINDOW
    steps = nb // ATTN_WINDOWS
    blk = ATTN_WINDOWS * WINDOW
    rope = _rope_tables(seq)
    kcol = Q_COLS // KV_COLS
    prev_win = lambda b, n: b * nb + jnp.maximum(ATTN_WINDOWS * n - 1, 0)
    cur = lambda b, n: (b * steps + n, 0)
    kcur = lambda b, n: (b * steps + n, kcol)
    kprev = lambda b, n: (prev_win(b, n), kcol)
    vcur = lambda b, n: (b * steps + n, kcol + 1)
    vprev = lambda b, n: (prev_win(b, n), kcol + 1)
    return pl.pallas_call(
        _attn_kernel,
        grid=(batch, steps),
        in_specs=[pl.BlockSpec(memory_space=pltpu.SMEM),
                  pl.BlockSpec((blk, Q_COLS), cur),
                  pl.BlockSpec((blk, KV_COLS), kcur),
                  pl.BlockSpec((WINDOW, KV_COLS), kprev),
                  pl.BlockSpec((blk, KV_COLS), vcur),
                  pl.BlockSpec((WINDOW, KV_COLS), vprev),
                  pl.BlockSpec((3, blk, LANES), lambda b, n: (0, n, 0)),
                  pl.BlockSpec((3, WINDOW, LANES), lambda b, n: (0, jnp.maximum(ATTN_WINDOWS * n - 1, 0), 0)),
                  pl.BlockSpec((1, LANES), lambda b, n: (0, 0)),
                  pl.BlockSpec((1, LANES), lambda b, n: (0, 0))],
        out_specs=pl.BlockSpec((blk, Q_COLS), cur),
        out_shape=jax.ShapeDtypeStruct((t, Q_COLS), F32),
        compiler_params=pltpu.CompilerParams(dimension_semantics=("arbitrary", "arbitrary"),
                                             vmem_limit_bytes=VMEM_LIMIT),
        name="swa_attention",
    )(sinks, qkv, qkv, qkv, qkv, qkv, rope, rope,
      jnp.tile(q_gain, 2).reshape(1, LANES), jnp.tile(k_gain, 2).reshape(1, LANES))


def _conv_kernel(x_ref, g_ref, win_ref, cw_ref, wout_ref, o_ref, zprev_ref):
    n = pl.program_id(1)
    d = D_MODEL

    @pl.when(n == 0)
    def _():
        zprev_ref[...] = jnp.zeros_like(zprev_ref)

    x = x_ref[...]
    h = _rms(x, g_ref[...])
    bcu = jnp.dot(h.astype(BF16), win_ref[...], preferred_element_type=F32)
    gate_b = bcu[:, :d]
    z = bcu[:, d:2 * d] * bcu[:, 2 * d:]
    tm = z.shape[0]
    row = lax.broadcasted_iota(jnp.int32, z.shape, 0)
    prev = zprev_ref[...]
    p_last = prev[SUBLANES - 1:SUBLANES, :]
    p_last2 = prev[SUBLANES - 2:SUBLANES - 1, :]
    z1 = jnp.where(row == 0, p_last, pltpu.roll(z, 1, 0))
    z2 = jnp.where(row == 0, p_last2, jnp.where(row == 1, p_last, pltpu.roll(z, 2, 0)))
    cw = cw_ref[...]
    conv = cw[0:1, :] * z2 + cw[1:2, :] * z1 + cw[2:3, :] * z
    zprev_ref[...] = z[tm - SUBLANES:, :]
    o_ref[...] = x + jnp.dot((gate_b * conv).astype(BF16), wout_ref[...], preferred_element_type=F32)


def _conv_mixer(x, gain, w_in, conv_w, w_out, batch, seq, tm=256):
    t, d = x.shape
    nblk = seq // tm
    blk = lambda b, n: (b * nblk + n, 0)
    const = lambda b, n: (0, 0)
    return pl.pallas_call(
        _conv_kernel,
        grid=(batch, nblk),
        in_specs=[pl.BlockSpec((tm, d), blk),
                  pl.BlockSpec((1, d), const),
                  pl.BlockSpec((d, 3 * d), const),
                  pl.BlockSpec((CONV_WIDTH, d), const),
                  pl.BlockSpec((d, d), const)],
        out_specs=pl.BlockSpec((tm, d), blk),
        out_shape=jax.ShapeDtypeStruct((t, d), F32),
        scratch_shapes=[pltpu.VMEM((SUBLANES, d), F32)],
        compiler_params=pltpu.CompilerParams(dimension_semantics=("arbitrary", "arbitrary"),
                                             vmem_limit_bytes=VMEM_LIMIT),
        name="conv_mixer",
    )(x, gain.reshape(1, d), w_in.astype(BF16), conv_w, w_out.astype(BF16))


def _topk_axis0(s, k, ids=None, payload=None):
    n, tm = s.shape
    if ids is None:
        ids = lax.broadcasted_iota(jnp.int32, (n, tm), 0)
    krow = lax.broadcasted_iota(jnp.int32, (k, tm), 0)
    vals = jnp.zeros((k, tm), F32)
    picks = jnp.zeros((k, tm), jnp.int32)
    for r in range(k):
        m = jnp.max(s, axis=0, keepdims=True)
        pos = jnp.min(jnp.where(s == m, ids, jnp.iinfo(jnp.int32).max), axis=0, keepdims=True)
        sel = ids == pos
        if payload is None:
            picked = pos
        else:
            picked = jnp.sum(jnp.where(sel, payload, 0), axis=0, keepdims=True)
        vals = jnp.where(krow == r, m, vals)
        picks = jnp.where(krow == r, picked, picks)
        s = jnp.where(sel, -jnp.inf, s)
    return vals, picks


def _pair_candidates(s1, i1, s2, i2):
    k, tm = s1.shape
    sub = lax.broadcasted_iota(jnp.int32, (SUBLANES, tm), 0)
    s2a, i2a = s2[0:SUBLANES, :], i2[0:SUBLANES, :]
    scores = [s1[0:1, :] + s2, s1[1:2, :] + s2a]
    flat = [lax.broadcasted_iota(jnp.int32, (k, tm), 0), k + sub]
    expert = [i1[0:1, :] * N_KEYS + i2, i1[1:2, :] * N_KEYS + i2a]
    for n, tile in enumerate((((2, 0), (5, 5)), ((3, 0), (4, 4)), ((6, 0), (7, 2)))):
        sc = jnp.full((SUBLANES, tm), -jnp.inf, F32)
        fl = k * k + n * SUBLANES + sub
        ex = jnp.zeros((SUBLANES, tm), jnp.int32)
        for i, off in tile:
            inside = (sub >= off) & (sub < off + k // (i + 1))
            s2r = s2a if off == 0 else pltpu.roll(s2a, off, 0)
            i2r = i2a if off == 0 else pltpu.roll(i2a, off, 0)
            sc = jnp.where(inside, s1[i:i + 1, :] + s2r, sc)
            fl = jnp.where(inside, i * k + sub - off, fl)
            ex = jnp.where(inside, i1[i:i + 1, :] * N_KEYS + i2r, ex)
        scores.append(sc)
        flat.append(fl)
        expert.append(ex)
    scores.append(s1[k // 2:, :] + s2[0:1, :])
    flat.append((k // 2 + sub) * k)
    expert.append(i1[k // 2:, :] * N_KEYS + i2[0:1, :])
    return jnp.concatenate(scores, axis=0), jnp.concatenate(flat, axis=0), jnp.concatenate(expert, axis=0)


def _route_kernel(x_ref, g_ref, wq_ref, keys_ref, after_ref, h_ref, idx_ref, gate_ref):
    del after_ref
    h = _rms(x_ref[...], g_ref[...])
    h_ref[...] = h
    q = jnp.dot(h.astype(BF16), wq_ref[...], preferred_element_type=F32).astype(BF16)
    idx_rows, gate_rows = [], []
    for head in range(PEER_HEADS):
        tops = []
        for part in range(2):
            col = (head * 2 + part) * QUERY_HALF
            s = lax.dot_general(keys_ref[head, part], q[:, col:col + QUERY_HALF],
                                (((1,), (1,)), ((), ())), preferred_element_type=F32)
            tops.append(_topk_axis0(s, PEER_TOPK))
        (s1, i1), (s2, i2) = tops
        cand, flat_ids, cand_idx = _pair_candidates(s1, i1, s2, i2)
        g_s, e_idx = _topk_axis0(cand, PEER_TOPK, ids=flat_ids, payload=cand_idx)
        e = jnp.exp(g_s - jnp.max(g_s, axis=0, keepdims=True))
        gate_rows.append(e / jnp.sum(e, axis=0, keepdims=True))
        idx_rows.append(e_idx * ROW_WORDS)
    idx_ref[...] = jnp.concatenate(idx_rows, axis=0).T
    gate_ref[...] = jnp.concatenate(gate_rows, axis=0).T


def _route(x, gain, w_query, sub_keys, first_tok, n_tok, after, tm=128):
    t, d = x.shape
    nq = w_query.shape[1]
    first_blk = first_tok // tm
    return pl.pallas_call(
        _route_kernel,
        grid=(n_tok // tm,),
        in_specs=[pl.BlockSpec((tm, d), lambda i: (i + first_blk, 0)),
                  pl.BlockSpec((1, d), lambda i: (0, 0)),
                  pl.BlockSpec((d, nq), lambda i: (0, 0)),
                  pl.BlockSpec((PEER_HEADS, 2, N_KEYS, QUERY_HALF), lambda i: (0, 0, 0, 0)),
                  pl.BlockSpec(after.shape, lambda i: (0, 0))],
        out_specs=[pl.BlockSpec((tm, d), lambda i: (i, 0)),
                   pl.BlockSpec((tm, SLOTS), lambda i: (i, 0)),
                   pl.BlockSpec((tm, SLOTS), lambda i: (i, 0))],
        out_shape=[jax.ShapeDtypeStruct((n_tok, d), F32),
                   jax.ShapeDtypeStruct((n_tok, SLOTS), jnp.int32),
                   jax.ShapeDtypeStruct((n_tok, SLOTS), F32)],
        compiler_params=pltpu.CompilerParams(dimension_semantics=("arbitrary",), vmem_limit_bytes=VMEM_LIMIT),
        name="peer_route",
    )(x, gain.reshape(1, d), w_query, sub_keys, after)


def _pack_kernel(t_ref, tiles_ref, *rows_ref):
    x = t_ref[...]
    half = x.shape[1] // 2
    hi = pltpu.bitcast(x[:, :half].astype(BF16).astype(F32), jnp.int32)
    lo = pltpu.bitcast(x[:, half:].astype(BF16).astype(F32), jnp.int32)
    words = hi | lax.shift_right_logical(lo, 16)
    for ref in rows_ref:
        ref[...] = words
    for s in range(ROW_WORDS):
        tiles_ref[pl.ds(s, x.shape[0], stride=ROW_WORDS), :] = words[:, s * LANES:(s + 1) * LANES]


def _pack_table(tabs, layer, with_rows, rows_per_step=256):
    _, ne, d = tabs.shape
    out_specs = [pl.BlockSpec((rows_per_step * ROW_WORDS, LANES), lambda i: (i, 0))]
    out_shape = [jax.ShapeDtypeStruct((ne * ROW_WORDS, LANES), jnp.int32)]
    if with_rows:
        out_specs.append(pl.BlockSpec((rows_per_step, d // 2), lambda i: (i, 0)))
        out_shape.append(jax.ShapeDtypeStruct((ne, d // 2), jnp.int32))
    return pl.pallas_call(
        _pack_kernel,
        grid=(ne // rows_per_step,),
        in_specs=[pl.BlockSpec((None, rows_per_step, d), lambda i: (layer, i, 0))],
        out_specs=out_specs,
        out_shape=out_shape,
        compiler_params=pltpu.CompilerParams(dimension_semantics=("arbitrary",), vmem_limit_bytes=VMEM_LIMIT),
        name="pack_table",
    )(tabs)


def _load_table_once(tab_hbm, tab, sem):
    @pl.when(pl.program_id(0) == 0)
    def _():
        cp = pltpu.make_async_copy(tab_hbm, tab, sem)
        cp.start()
        cp.wait()


def _gather_pair(tab, off_a, off_b):
    ra = tab[pl.ds(pl.multiple_of(off_a, ROW_WORDS), ROW_WORDS), :]
    rb = tab[pl.ds(pl.multiple_of(off_b, ROW_WORDS), ROW_WORDS), :]
    words = jnp.concatenate([ra, rb], axis=0)
    hi = pltpu.bitcast(words & jnp.int32(-65536), F32)
    lo = pltpu.bitcast(words << 16, F32)
    return hi, lo


TOKEN_UNROLL = 8


def _expert_in_kernel(*refs):
    idx_refs = refs[:TOKEN_UNROLL]
    h_ref, gate_ref, tab_hbm, w_ref, tab, sem, a_ref = refs[TOKEN_UNROLL:]
    _load_table_once(tab_hbm, tab, sem)
    sub = lax.broadcasted_iota(jnp.int32, (SUBLANES, LANES), 0)
    lane = lax.broadcasted_iota(jnp.int32, (SUBLANES, LANES), 1)
    own_half = (sub >= ROW_WORDS) == ((lane & 1) == 1)
    tb = h_ref.shape[0]

    def step(i, carry):
        xs = []
        for u in range(TOKEN_UNROLL):
            x = h_ref[i * TOKEN_UNROLL + u]
            xs.append((jnp.concatenate([x[0:ROW_WORDS], x[0:ROW_WORDS]], axis=0),
                       jnp.concatenate([x[ROW_WORDS:], x[ROW_WORDS:]], axis=0)))
        accs = [jnp.zeros((SUBLANES, LANES), F32) for _ in range(TOKEN_UNROLL)]
        for p in range(SLOTS // 2):
            for u in range(TOKEN_UNROLL):
                hi, lo = _gather_pair(tab, idx_refs[u][i, 2 * p], idx_refs[u][i, 2 * p + 1])
                part = jnp.sum(hi * xs[u][0] + lo * xs[u][1], axis=1, keepdims=True)
                accs[u] = jnp.where((lane >> 1) == p, part, accs[u])
        for u in range(TOKEN_UNROLL):
            a_ref[pl.ds(i * TOKEN_UNROLL + u, 1), :] = jnp.sum(jnp.where(own_half, accs[u], 0.0),
                                                                axis=0, keepdims=True)
        return carry

    lax.fori_loop(0, tb // TOKEN_UNROLL, step, 0)
    a = a_ref[...]
    w_ref[...] = gate_ref[...] * (0.5 * a * (1.0 + lax.erf(a * (1.0 / math.sqrt(2.0)))))


def _expert_out_kernel(*refs):
    idx_refs = refs[:TOKEN_UNROLL]
    w_ref, x_ref, tab_hbm, o_ref, tab, sem, wb_ref = refs[TOKEN_UNROLL:]
    _load_table_once(tab_hbm, tab, sem)
    sub = lax.broadcasted_iota(jnp.int32, (SUBLANES, LANES), 0)
    lower = sub < ROW_WORDS
    tb = x_ref.shape[0]

    def step(i, carry):
        for u in range(TOKEN_UNROLL):
            row = w_ref[pl.ds(i * TOKEN_UNROLL + u, 1), :]
            wb_ref[u] = jnp.broadcast_to(row, (SLOTS, LANES)).T
        acc_h = [jnp.zeros((SUBLANES, LANES), F32) for _ in range(TOKEN_UNROLL)]
        acc_l = [jnp.zeros((SUBLANES, LANES), F32) for _ in range(TOKEN_UNROLL)]
        for p in range(SLOTS // 2):
            ka, kb = 2 * p, 2 * p + 1
            for u in range(TOKEN_UNROLL):
                hi, lo = _gather_pair(tab, idx_refs[u][i, ka], idx_refs[u][i, kb])
                wa = jnp.broadcast_to(wb_ref[u, ka:ka + 1, :], (SUBLANES, LANES))
                wb = jnp.broadcast_to(wb_ref[u, kb:kb + 1, :], (SUBLANES, LANES))
                wt = jnp.where(lower, wa, wb)
                acc_h[u] = acc_h[u] + hi * wt
                acc_l[u] = acc_l[u] + lo * wt
        for u in range(TOKEN_UNROLL):
            t = i * TOKEN_UNROLL + u
            ah = acc_h[u] + pltpu.roll(acc_h[u], ROW_WORDS, 0)
            al = acc_l[u] + pltpu.roll(acc_l[u], ROW_WORDS, 0)
            o_ref[t] = x_ref[t] + jnp.where(lower, ah, al)
        return carry

    lax.fori_loop(0, tb // TOKEN_UNROLL, step, 0)


def _expert_specs(tb):
    un = TOKEN_UNROLL
    smem_blks = [pl.BlockSpec((None, tb // un, SLOTS), functools.partial(lambda u, i: (u, i, 0), u),
                              memory_space=pltpu.SMEM) for u in range(un)]
    vmem_blk = pl.BlockSpec((tb, SLOTS), lambda i: (i, 0))
    tok_blk = pl.BlockSpec((tb, D_MODEL // LANES, LANES), lambda i: (i, 0, 0))
    params = pltpu.CompilerParams(dimension_semantics=("arbitrary",), vmem_limit_bytes=VMEM_LIMIT)
    return smem_blks, vmem_blk, tok_blk, params


def _split_offsets(idx):
    n = idx.shape[0]
    return idx.reshape(n // TOKEN_UNROLL, TOKEN_UNROLL, SLOTS).transpose(1, 0, 2)


def _expert_in(h, idx_split, gate, u_packed, tb=256):
    n, d = h.shape
    assert n % tb == 0 and tb % TOKEN_UNROLL == 0 and d == D_MODEL
    smem_blks, vmem_blk, tok_blk, params = _expert_specs(tb)
    return pl.pallas_call(
        _expert_in_kernel,
        grid=(n // tb,),
        in_specs=smem_blks + [tok_blk, vmem_blk, pl.BlockSpec(memory_space=pl.ANY)],
        out_specs=vmem_blk,
        out_shape=jax.ShapeDtypeStruct((n, SLOTS), F32),
        scratch_shapes=[pltpu.VMEM(u_packed.shape, jnp.int32), pltpu.SemaphoreType.DMA, pltpu.VMEM((tb, SLOTS), F32)],
        compiler_params=params,
        name="peer_expert_in",
    )(*([idx_split] * TOKEN_UNROLL), h.reshape(n, d // LANES, LANES), gate, u_packed)


def _expert_out(x, idx_split, w, v_packed, tb=256):
    t, d = x.shape
    n = w.shape[0]
    assert n % tb == 0 and n <= t and d == D_MODEL
    smem_blks, vmem_blk, tok_blk, params = _expert_specs(tb)
    out = pl.pallas_call(
        _expert_out_kernel,
        grid=(n // tb,),
        in_specs=smem_blks + [vmem_blk, tok_blk, pl.BlockSpec(memory_space=pl.ANY)],
        out_specs=tok_blk,
        out_shape=jax.ShapeDtypeStruct((n, d // LANES, LANES), F32),
        scratch_shapes=[pltpu.VMEM(v_packed.shape, jnp.int32), pltpu.SemaphoreType.DMA,
                        pltpu.VMEM((TOKEN_UNROLL, SLOTS, LANES), F32)],
        compiler_params=params,
        name="peer_expert_out",
    )(*([idx_split] * TOKEN_UNROLL), w, x.reshape(t, d // LANES, LANES), v_packed)
    return out.reshape(n, d)


SC_LANES = 16
SC_CORES = 2
SC_WORKERS = SC_CORES * 16
SC_CHUNK = 32
SC_GROUP = 8
SC_TOKENS = 16640


def _sc_params():
    cp = pltpu.CompilerParams()
    if "needs_layout_passes" in pltpu.CompilerParams.__dataclass_fields__:
        cp = dataclasses.replace(cp, needs_layout_passes=False)
    return cp


def _sc_expert_out(table_words, idx, w, x, first_tok):
    d = x.shape[1]
    n_tok = w.shape[0] // SLOTS
    assert n_tok % (SC_WORKERS * SC_GROUP) == 0 and first_tok % SC_GROUP == 0 and d == D_MODEL
    per = n_tok // SC_WORKERS
    words = d // 2
    nq = words // SC_LANES // 2
    nchunk = SLOTS // SC_CHUNK
    group_chunks = SC_GROUP * nchunk
    mesh = plsc.VectorSubcoreMesh(core_axis_name="c", subcore_axis_name="s")

    @functools.partial(
        pl.kernel, mesh=mesh,
        out_type=jax.ShapeDtypeStruct((n_tok, d), F32),
        scratch_types=[pltpu.VMEM((SC_GROUP * SLOTS,), jnp.int32), pltpu.VMEM((SC_GROUP * SLOTS,), F32),
                       pltpu.VMEM((2, SC_CHUNK, words), jnp.int32), pltpu.VMEM((SC_GROUP, d), F32),
                       pltpu.SemaphoreType.DMA, pltpu.SemaphoreType.DMA],
        compiler_params=_sc_params(),
        name="peer_expert_out_sc",
    )
    def body(tab_hbm, idx_hbm, w_hbm, x_hbm, o_hbm, idx_v, w_v, rows_v, y_v, sem0, sem1):
        base = (lax.axis_index("s") * SC_CORES + lax.axis_index("c")) * per
        zero = jnp.zeros((SC_LANES,), jnp.int32)
        sems = (sem0, sem1)

        def gather(k, b):
            off = pl.multiple_of(k * SC_CHUNK, SC_CHUNK)
            return pltpu.make_async_copy(tab_hbm.at[idx_v.at[pl.ds(off, SC_CHUNK)]], rows_v.at[b], sems[b])

        def accumulate(k, b):
            tok = k // nchunk
            for q in range(2):
                first = q * nq
                acc0 = (tuple(y_v[tok, pl.ds((first + j) * SC_LANES, SC_LANES)] for j in range(nq))
                        + tuple(y_v[tok, pl.ds(words + (first + j) * SC_LANES, SC_LANES)] for j in range(nq)))

                def row_body(r, accs):
                    ws = plsc.load_gather(w_v, [zero + (k * SC_CHUNK + r)])
                    hi_acc, lo_acc = [], []
                    for j in range(nq):
                        wv = rows_v[b, r, pl.ds((first + j) * SC_LANES, SC_LANES)]
                        lo, hi = plsc.unpack(plsc.bitcast(wv, BF16), format=plsc.PackFormat.INTERLEAVED,
                                             preferred_element_type=F32)
                        hi_acc.append(accs[j] + hi * ws)
                        lo_acc.append(accs[nq + j] + lo * ws)
                    return tuple(hi_acc) + tuple(lo_acc)

                accs = lax.fori_loop(0, SC_CHUNK, row_body, acc0)
                for j in range(nq):
                    y_v[tok, pl.ds((first + j) * SC_LANES, SC_LANES)] = accs[j]
                    y_v[tok, pl.ds(words + (first + j) * SC_LANES, SC_LANES)] = accs[nq + j]

        @pl.loop(0, per // SC_GROUP)
        def _(g):
            t0 = pl.multiple_of(base + g * SC_GROUP, SC_GROUP)
            pltpu.sync_copy(idx_hbm.at[pl.ds(t0 * SLOTS, SC_GROUP * SLOTS)], idx_v)
            pltpu.sync_copy(w_hbm.at[pl.ds(t0 * SLOTS, SC_GROUP * SLOTS)], w_v)
            pltpu.sync_copy(x_hbm.at[pl.ds(first_tok + t0, SC_GROUP)], y_v)
            gather(0, 0).start()

            @pl.loop(0, group_chunks // 2)
            def _(kk):
                k0 = 2 * kk
                gather(k0 + 1, 1).start()
                gather(k0, 0).wait()
                accumulate(k0, 0)

                @pl.when(k0 + 2 < group_chunks)
                def _():
                    gather(k0 + 2, 0).start()
                gather(k0 + 1, 1).wait()
                accumulate(k0 + 1, 1)

            pltpu.sync_copy(y_v, o_hbm.at[pl.ds(t0, SC_GROUP)])

    return body(table_words, idx, w, x)


def _peer(x, gain, w_query, sub_keys, experts_u, experts_v, layer):
    t = x.shape[0]
    t_tc = t - SC_TOKENS
    wq, keys = w_query.astype(BF16), sub_keys.astype(BF16)
    (u_packed,) = _pack_table(experts_u, layer, with_rows=False)
    v_packed, v_rows = _pack_table(experts_v, layer, with_rows=True)

    h, idx, gate = _route(x, gain, wq, keys, t_tc, SC_TOKENS, v_rows[:SUBLANES])
    w_sc = _expert_in(h, _split_offsets(idx), gate, u_packed)
    out_sc = _sc_expert_out(v_rows, (idx // ROW_WORDS).reshape(-1), w_sc.reshape(-1), x, t_tc)

    h, idx, gate = _route(x, gain, wq, keys, 0, t_tc, w_sc[:SUBLANES])
    idx_split = _split_offsets(idx)
    w = _expert_in(h, idx_split, gate, u_packed)
    out_tc = _expert_out(x, idx_split, w, v_packed)
    return jnp.concatenate([out_tc, out_sc], axis=0)


def kernel(x, norm_mix, norm_ffn, attn_w_qkv, attn_q_norm, attn_k_norm, attn_sinks, attn_w_o, conv_w_in, conv_w, conv_w_out, peer_w_query, peer_sub_keys, peer_u, peer_v):
    batch, seq, d = x.shape
    xt = x.reshape(batch * seq, d)
    for i in range(norm_mix.shape[0]):
        j = i // 2
        if i % 2 == 0:
            qkv = _norm_matmul(xt, norm_mix[i], attn_w_qkv[j].astype(BF16))
            o = _attention(qkv, attn_q_norm[j], attn_k_norm[j], attn_sinks[j], batch, seq)
            xt = _matmul_residual(o, attn_w_o[j].astype(BF16), xt)
        else:
            xt = _conv_mixer(xt, norm_mix[i], conv_w_in[j], conv_w[j], conv_w_out[j], batch, seq)
        xt = _peer(xt, norm_ffn[i], peer_w_query[i], peer_sub_keys[i], peer_u, peer_v, i)
    return xt.reshape(batch, seq, d)
```

```python
import dataclasses
import functools
import math

import jax
import jax.numpy as jnp
from jax import lax
from jax.experimental import pallas as pl
from jax.experimental.pallas import tpu as pltpu
from jax.experimental.pallas import tpu_sc as plsc

D_MODEL = 1024
RMS_EPS = 1e-6

HEAD_DIM = 64
N_Q_HEADS = 16
N_KV_HEADS = 4
GROUP = N_Q_HEADS // N_KV_HEADS
WINDOW = 128
ROT_DIM = HEAD_DIM // 4
ROPE_THETA = 500000.0
Q_COLS = N_Q_HEADS * HEAD_DIM
KV_COLS = N_KV_HEADS * HEAD_DIM
NEG_INF = -1e30

CONV_WIDTH = 3

PEER_HEADS = 8
N_KEYS = 128
N_EXPERTS = N_KEYS * N_KEYS
PEER_TOPK = 16
QUERY_HALF = 128
SLOTS = PEER_HEADS * PEER_TOPK

LANES = 128
SUBLANES = 8
ROW_WORDS = D_MODEL // 2 // LANES
VMEM_LIMIT = 48 * 1024 * 1024

BF16 = jnp.bfloat16
F32 = jnp.float32


def _rms(x, gain):
    return x * lax.rsqrt(jnp.mean(x * x, axis=-1, keepdims=True) + RMS_EPS) * gain


def _norm_matmul_kernel(x_ref, g_ref, w_ref, o_ref):
    h = _rms(x_ref[...], g_ref[...])
    o_ref[...] = jnp.dot(h.astype(BF16), w_ref[...], preferred_element_type=F32)


def _norm_matmul(x, gain, w, tm=512):
    t, d = x.shape
    n = w.shape[1]
    return pl.pallas_call(
        _norm_matmul_kernel,
        grid=(t // tm,),
        in_specs=[pl.BlockSpec((tm, d), lambda i: (i, 0)),
                  pl.BlockSpec((1, d), lambda i: (0, 0)),
                  pl.BlockSpec((d, n), lambda i: (0, 0))],
        out_specs=pl.BlockSpec((tm, n), lambda i: (i, 0)),
        out_shape=jax.ShapeDtypeStruct((t, n), F32),
        compiler_params=pltpu.CompilerParams(dimension_semantics=("arbitrary",), vmem_limit_bytes=VMEM_LIMIT),
        name="norm_matmul",
    )(x, gain.reshape(1, d), w)


def _matmul_residual_kernel(a_ref, w_ref, r_ref, o_ref):
    o_ref[...] = r_ref[...] + jnp.dot(a_ref[...].astype(BF16), w_ref[...], preferred_element_type=F32)


def _matmul_residual(a, w, res, tm=512):
    t, k = a.shape
    n = w.shape[1]
    return pl.pallas_call(
        _matmul_residual_kernel,
        grid=(t // tm,),
        in_specs=[pl.BlockSpec((tm, k), lambda i: (i, 0)),
                  pl.BlockSpec((k, n), lambda i: (0, 0)),
                  pl.BlockSpec((tm, n), lambda i: (i, 0))],
        out_specs=pl.BlockSpec((tm, n), lambda i: (i, 0)),
        out_shape=jax.ShapeDtypeStruct((t, n), F32),
        compiler_params=pltpu.CompilerParams(dimension_semantics=("arbitrary",), vmem_limit_bytes=VMEM_LIMIT),
        name="matmul_residual",
    )(a, w, res)


def _rope_tables(seq):
    half = ROT_DIM // 2
    freqs = ROPE_THETA ** (-jnp.arange(0, ROT_DIM, 2, dtype=F32) / ROT_DIM)
    ang = jnp.arange(seq, dtype=F32)[:, None] * freqs[None, :]
    cos, sin = jnp.cos(ang), jnp.sin(ang)
    ones = jnp.ones((seq, HEAD_DIM - ROT_DIM), F32)
    zeros = jnp.zeros((seq, HEAD_DIM - ROT_DIM), F32)
    zh = jnp.zeros((seq, half), F32)
    c = jnp.concatenate([cos, cos, ones], axis=1)
    s_next = jnp.concatenate([-sin, zh, zeros], axis=1)
    s_prev = jnp.concatenate([zh, sin, zeros], axis=1)
    return jnp.stack([jnp.tile(c, (1, 2)), jnp.tile(s_next, (1, 2)), jnp.tile(s_prev, (1, 2))])


def _head_norm_rope(x, gain2, rope, lo):
    sq = x * x
    s_lo = jnp.sum(jnp.where(lo, sq, 0.0), axis=1, keepdims=True)
    s_hi = jnp.sum(jnp.where(lo, 0.0, sq), axis=1, keepdims=True)
    ms = jnp.where(lo, s_lo, s_hi) * (1.0 / HEAD_DIM)
    xn = x * lax.rsqrt(ms + RMS_EPS) * gain2
    half = ROT_DIM // 2
    return xn * rope[0] + pltpu.roll(xn, LANES - half, 1) * rope[1] + pltpu.roll(xn, half, 1) * rope[2]


def _attn_kernel(sinks_ref, q_ref, kc_ref, kp_ref, vc_ref, vp_ref, rc_ref, rp_ref, qg_ref, kg_ref, o_ref):
    n = pl.program_id(1)
    lo = lax.broadcasted_iota(jnp.int32, (WINDOW, LANES), 1) < HEAD_DIM
    rope_c = rc_ref[...]
    rope_p = rp_ref[...]
    qg = qg_ref[...]
    kg = kg_ref[...]

    rows = GROUP * WINDOW
    qi = lax.broadcasted_iota(jnp.int32, (rows, 2 * WINDOW), 0) & (WINDOW - 1)
    ki = lax.broadcasted_iota(jnp.int32, (rows, 2 * WINDOW), 1)
    rel = WINDOW + qi - ki
    valid = (rel >= 0) & (rel < WINDOW) & ((n > 0) | (ki >= WINDOW))
    head_of_row = lax.broadcasted_iota(jnp.int32, (rows, 1), 0) // WINDOW
    scale = 1.0 / math.sqrt(HEAD_DIM)

    q2 = [_head_norm_rope(q_ref[:, c * LANES:(c + 1) * LANES], qg, rope_c, lo).astype(BF16)
          for c in range(Q_COLS // LANES)]
    for c in range(KV_COLS // LANES):
        cols = slice(c * LANES, (c + 1) * LANES)
        kprev = _head_norm_rope(kp_ref[:, cols], kg, rope_p, lo)
        kcur = _head_norm_rope(kc_ref[:, cols], kg, rope_c, lo)
        kfull = jnp.concatenate([kprev, kcur], axis=0).astype(BF16)
        vfull = jnp.concatenate([vp_ref[:, cols], vc_ref[:, cols]], axis=0).astype(BF16)
        for hh in range(LANES // HEAD_DIM):
            h = (LANES // HEAD_DIM) * c + hh
            kh = kfull[:, hh * HEAD_DIM:(hh + 1) * HEAD_DIM]
            vh = vfull[:, hh * HEAD_DIM:(hh + 1) * HEAD_DIM]
            heads = [GROUP * h + g for g in range(GROUP)]
            q4 = jnp.concatenate([q2[j // 2][:, (j % 2) * HEAD_DIM:(j % 2 + 1) * HEAD_DIM] for j in heads], axis=0)
            s = lax.dot_general(q4, kh, (((1,), (1,)), ((), ())), preferred_element_type=F32) * scale
            s = jnp.where(valid, s, NEG_INF)
            sink = jnp.zeros((rows, 1), F32)
            for g, j in enumerate(heads):
                sink = jnp.where(head_of_row == g, sinks_ref[j], sink)
            m = jnp.maximum(jnp.max(s, axis=1, keepdims=True), sink)
            p = jnp.exp(s - m)
            denom = jnp.sum(p, axis=1, keepdims=True) + jnp.exp(sink - m)
            o = jnp.dot(p.astype(BF16), vh, preferred_element_type=F32) / denom
            for g, j in enumerate(heads):
                o_ref[:, j * HEAD_DIM:(j + 1) * HEAD_DIM] = o[g * WINDOW:(g + 1) * WINDOW]


def _attention(qkv, q_gain, k_gain, sinks, batch, seq):
    t = batch * seq
    nb = seq // WINDOW
    rope = _rope_tables(seq)
    kcol = Q_COLS // KV_COLS
    cur = lambda b, n: (b * nb + n, 0)
    kcur = lambda b, n: (b * nb + n, kcol)
    kprev = lambda b, n: (b * nb + jnp.maximum(n - 1, 0), kcol)
    vcur = lambda b, n: (b * nb + n, kcol + 1)
    vprev = lambda b, n: (b * nb + jnp.maximum(n - 1, 0), kcol + 1)
    return pl.pallas_call(
        _attn_kernel,
        grid=(batch, nb),
        in_specs=[pl.BlockSpec(memory_space=pltpu.SMEM),
                  pl.BlockSpec((WINDOW, Q_COLS), cur),
                  pl.BlockSpec((WINDOW, KV_COLS), kcur),
                  pl.BlockSpec((WINDOW, KV_COLS), kprev),
                  pl.BlockSpec((WINDOW, KV_COLS), vcur),
                  pl.BlockSpec((WINDOW, KV_COLS), vprev),
                  pl.BlockSpec((3, WINDOW, LANES), lambda b, n: (0, n, 0)),
                  pl.BlockSpec((3, WINDOW, LANES), lambda b, n: (0, jnp.maximum(n - 1, 0), 0)),
                  pl.BlockSpec((1, LANES), lambda b, n: (0, 0)),
                  pl.BlockSpec((1, LANES), lambda b, n: (0, 0))],
        out_specs=pl.BlockSpec((WINDOW, Q_COLS), cur),
        out_shape=jax.ShapeDtypeStruct((t, Q_COLS), F32),
        compiler_params=pltpu.CompilerParams(dimension_semantics=("arbitrary", "arbitrary"),
                                             vmem_limit_bytes=VMEM_LIMIT),
        name="swa_attention",
    )(sinks, qkv, qkv, qkv, qkv, qkv, rope, rope,
      jnp.tile(q_gain, 2).reshape(1, LANES), jnp.tile(k_gain, 2).reshape(1, LANES))


def _conv_kernel(x_ref, g_ref, win_ref, cw_ref, wout_ref, o_ref, zprev_ref):
    n = pl.program_id(1)
    d = D_MODEL

    @pl.when(n == 0)
    def _():
        zprev_ref[...] = jnp.zeros_like(zprev_ref)

    x = x_ref[...]
    h = _rms(x, g_ref[...])
    bcu = jnp.dot(h.astype(BF16), win_ref[...], preferred_element_type=F32)
    gate_b = bcu[:, :d]
    z = bcu[:, d:2 * d] * bcu[:, 2 * d:]
    tm = z.shape[0]
    row = lax.broadcasted_iota(jnp.int32, z.shape, 0)
    prev = zprev_ref[...]
    p_last = prev[SUBLANES - 1:SUBLANES, :]
    p_last2 = prev[SUBLANES - 2:SUBLANES - 1, :]
    z1 = jnp.where(row == 0, p_last, pltpu.roll(z, 1, 0))
    z2 = jnp.where(row == 0, p_last2, jnp.where(row == 1, p_last, pltpu.roll(z, 2, 0)))
    cw = cw_ref[...]
    conv = cw[0:1, :] * z2 + cw[1:2, :] * z1 + cw[2:3, :] * z
    zprev_ref[...] = z[tm - SUBLANES:, :]
    o_ref[...] = x + jnp.dot((gate_b * conv).astype(BF16), wout_ref[...], preferred_element_type=F32)


def _conv_mixer(x, gain, w_in, conv_w, w_out, batch, seq, tm=256):
    t, d = x.shape
    nblk = seq // tm
    blk = lambda b, n: (b * nblk + n, 0)
    const = lambda b, n: (0, 0)
    return pl.pallas_call(
        _conv_kernel,
        grid=(batch, nblk),
        in_specs=[pl.BlockSpec((tm, d), blk),
                  pl.BlockSpec((1, d), const),
                  pl.BlockSpec((d, 3 * d), const),
                  pl.BlockSpec((CONV_WIDTH, d), const),
                  pl.BlockSpec((d, d), const)],
        out_specs=pl.BlockSpec((tm, d), blk),
        out_shape=jax.ShapeDtypeStruct((t, d), F32),
        scratch_shapes=[pltpu.VMEM((SUBLANES, d), F32)],
        compiler_params=pltpu.CompilerParams(dimension_semantics=("arbitrary", "arbitrary"),
                                             vmem_limit_bytes=VMEM_LIMIT),
        name="conv_mixer",
    )(x, gain.reshape(1, d), w_in.astype(BF16), conv_w, w_out.astype(BF16))


def _topk_axis0(s, k, ids=None, payload=None):
    n, tm = s.shape
    if ids is None:
        ids = lax.broadcasted_iota(jnp.int32, (n, tm), 0)
    krow = lax.broadcasted_iota(jnp.int32, (k, tm), 0)
    vals = jnp.zeros((k, tm), F32)
    picks = jnp.zeros((k, tm), jnp.int32)
    for r in range(k):
        m = jnp.max(s, axis=0, keepdims=True)
        pos = jnp.min(jnp.where(s == m, ids, jnp.iinfo(jnp.int32).max), axis=0, keepdims=True)
        sel = ids == pos
        if payload is None:
            picked = pos
        else:
            picked = jnp.sum(jnp.where(sel, payload, 0), axis=0, keepdims=True)
        vals = jnp.where(krow == r, m, vals)
        picks = jnp.where(krow == r, picked, picks)
        s = jnp.where(sel, -jnp.inf, s)
    return vals, picks


def _pair_candidates(s1, i1, s2, i2):
    k, tm = s1.shape
    sub = lax.broadcasted_iota(jnp.int32, (SUBLANES, tm), 0)
    s2a, i2a = s2[0:SUBLANES, :], i2[0:SUBLANES, :]
    scores = [s1[0:1, :] + s2, s1[1:2, :] + s2a]
    flat = [lax.broadcasted_iota(jnp.int32, (k, tm), 0), k + sub]
    expert = [i1[0:1, :] * N_KEYS + i2, i1[1:2, :] * N_KEYS + i2a]
    for n, tile in enumerate((((2, 0), (5, 5)), ((3, 0), (4, 4)), ((6, 0), (7, 2)))):
        sc = jnp.full((SUBLANES, tm), -jnp.inf, F32)
        fl = k * k + n * SUBLANES + sub
        ex = jnp.zeros((SUBLANES, tm), jnp.int32)
        for i, off in tile:
            inside = (sub >= off) & (sub < off + k // (i + 1))
            s2r = s2a if off == 0 else pltpu.roll(s2a, off, 0)
            i2r = i2a if off == 0 else pltpu.roll(i2a, off, 0)
            sc = jnp.where(inside, s1[i:i + 1, :] + s2r, sc)
            fl = jnp.where(inside, i * k + sub - off, fl)
            ex = jnp.where(inside, i1[i:i + 1, :] * N_KEYS + i2r, ex)
        scores.append(sc)
        flat.append(fl)
        expert.append(ex)
    scores.append(s1[k // 2:, :] + s2[0:1, :])
    flat.append((k // 2 + sub) * k)
    expert.append(i1[k // 2:, :] * N_KEYS + i2[0:1, :])
    return jnp.concatenate(scores, axis=0), jnp.concatenate(flat, axis=0), jnp.concatenate(expert, axis=0)


def _route_kernel(x_ref, g_ref, wq_ref, keys_ref, after_ref, h_ref, idx_ref, gate_ref):
    del after_ref
    h = _rms(x_ref[...], g_ref[...])
    h_ref[...] = h
    q = jnp.dot(h.astype(BF16), wq_ref[...], preferred_element_type=F32).astype(BF16)
    idx_rows, gate_rows = [], []
    for head in range(PEER_HEADS):
        tops = []
        for part in range(2):
            col = (head * 2 + part) * QUERY_HALF
            s = lax.dot_general(keys_ref[head, part], q[:, col:col + QUERY_HALF],
                                (((1,), (1,)), ((), ())), preferred_element_type=F32)
            tops.append(_topk_axis0(s, PEER_TOPK))
        (s1, i1), (s2, i2) = tops
        cand, flat_ids, cand_idx = _pair_candidates(s1, i1, s2, i2)
        g_s, e_idx = _topk_axis0(cand, PEER_TOPK, ids=flat_ids, payload=cand_idx)
        e = jnp.exp(g_s - jnp.max(g_s, axis=0, keepdims=True))
        gate_rows.append(e / jnp.sum(e, axis=0, keepdims=True))
        idx_rows.append(e_idx * ROW_WORDS)
    idx_ref[...] = jnp.concatenate(idx_rows, axis=0).T
    gate_ref[...] = jnp.concatenate(gate_rows, axis=0).T


def _route(x, gain, w_query, sub_keys, first_tok, n_tok, after, tm=128):
    t, d = x.shape
    nq = w_query.shape[1]
    first_blk = first_tok // tm
    return pl.pallas_call(
        _route_kernel,
        grid=(n_tok // tm,),
        in_specs=[pl.BlockSpec((tm, d), lambda i: (i + first_blk, 0)),
                  pl.BlockSpec((1, d), lambda i: (0, 0)),
                  pl.BlockSpec((d, nq), lambda i: (0, 0)),
                  pl.BlockSpec((PEER_HEADS, 2, N_KEYS, QUERY_HALF), lambda i: (0, 0, 0, 0)),
                  pl.BlockSpec(after.shape, lambda i: (0, 0))],
        out_specs=[pl.BlockSpec((tm, d), lambda i: (i, 0)),
                   pl.BlockSpec((tm, SLOTS), lambda i: (i, 0)),
                   pl.BlockSpec((tm, SLOTS), lambda i: (i, 0))],
        out_shape=[jax.ShapeDtypeStruct((n_tok, d), F32),
                   jax.ShapeDtypeStruct((n_tok, SLOTS), jnp.int32),
                   jax.ShapeDtypeStruct((n_tok, SLOTS), F32)],
        compiler_params=pltpu.CompilerParams(dimension_semantics=("arbitrary",), vmem_limit_bytes=VMEM_LIMIT),
        name="peer_route",
    )(x, gain.reshape(1, d), w_query, sub_keys, after)


def _pack_kernel(t_ref, tiles_ref, *rows_ref):
    x = t_ref[...]
    half = x.shape[1] // 2
    hi = pltpu.bitcast(x[:, :half].astype(BF16).astype(F32), jnp.int32)
    lo = pltpu.bitcast(x[:, half:].astype(BF16).astype(F32), jnp.int32)
    words = hi | lax.shift_right_logical(lo, 16)
    for ref in rows_ref:
        ref[...] = words
    for s in range(ROW_WORDS):
        tiles_ref[pl.ds(s, x.shape[0], stride=ROW_WORDS), :] = words[:, s * LANES:(s + 1) * LANES]


def _pack_table(tabs, layer, with_rows, rows_per_step=256):
    _, ne, d = tabs.shape
    out_specs = [pl.BlockSpec((rows_per_step * ROW_WORDS, LANES), lambda i: (i, 0))]
    out_shape = [jax.ShapeDtypeStruct((ne * ROW_WORDS, LANES), jnp.int32)]
    if with_rows:
        out_specs.append(pl.BlockSpec((rows_per_step, d // 2), lambda i: (i, 0)))
        out_shape.append(jax.ShapeDtypeStruct((ne, d // 2), jnp.int32))
    return pl.pallas_call(
        _pack_kernel,
        grid=(ne // rows_per_step,),
        in_specs=[pl.BlockSpec((None, rows_per_step, d), lambda i: (layer, i, 0))],
        out_specs=out_specs,
        out_shape=out_shape,
        compiler_params=pltpu.CompilerParams(dimension_semantics=("arbitrary",), vmem_limit_bytes=VMEM_LIMIT),
        name="pack_table",
    )(tabs)


def _load_table_once(tab_hbm, tab, sem):
    @pl.when(pl.program_id(0) == 0)
    def _():
        cp = pltpu.make_async_copy(tab_hbm, tab, sem)
        cp.start()
        cp.wait()


def _gather_pair(tab, off_a, off_b):
    ra = tab[pl.ds(pl.multiple_of(off_a, ROW_WORDS), ROW_WORDS), :]
    rb = tab[pl.ds(pl.multiple_of(off_b, ROW_WORDS), ROW_WORDS), :]
    words = jnp.concatenate([ra, rb], axis=0)
    hi = pltpu.bitcast(words & jnp.int32(-65536), F32)
    lo = pltpu.bitcast(words << 16, F32)
    return hi, lo


TOKEN_UNROLL = 8


def _expert_in_kernel(*refs):
    idx_refs = refs[:TOKEN_UNROLL]
    h_ref, gate_ref, tab_hbm, w_ref, tab, sem, a_ref = refs[TOKEN_UNROLL:]
    _load_table_once(tab_hbm, tab, sem)
    sub = lax.broadcasted_iota(jnp.int32, (SUBLANES, LANES), 0)
    lane = lax.broadcasted_iota(jnp.int32, (SUBLANES, LANES), 1)
    own_half = (sub >= ROW_WORDS) == ((lane & 1) == 1)
    tb = h_ref.shape[0]

    def step(i, carry):
        xs = []
        for u in range(TOKEN_UNROLL):
            x = h_ref[i * TOKEN_UNROLL + u]
            xs.append((jnp.concatenate([x[0:ROW_WORDS], x[0:ROW_WORDS]], axis=0),
                       jnp.concatenate([x[ROW_WORDS:], x[ROW_WORDS:]], axis=0)))
        accs = [jnp.zeros((SUBLANES, LANES), F32) for _ in range(TOKEN_UNROLL)]
        for p in range(SLOTS // 2):
            for u in range(TOKEN_UNROLL):
                hi, lo = _gather_pair(tab, idx_refs[u][i, 2 * p], idx_refs[u][i, 2 * p + 1])
                part = jnp.sum(hi * xs[u][0] + lo * xs[u][1], axis=1, keepdims=True)
                accs[u] = jnp.where((lane >> 1) == p, part, accs[u])
        for u in range(TOKEN_UNROLL):
            a_ref[pl.ds(i * TOKEN_UNROLL + u, 1), :] = jnp.sum(jnp.where(own_half, accs[u], 0.0),
                                                                axis=0, keepdims=True)
        return carry

    lax.fori_loop(0, tb // TOKEN_UNROLL, step, 0)
    a = a_ref[...]
    w_ref[...] = gate_ref[...] * (0.5 * a * (1.0 + lax.erf(a * (1.0 / math.sqrt(2.0)))))


def _expert_out_kernel(*refs):
    idx_refs = refs[:TOKEN_UNROLL]
    w_ref, x_ref, tab_hbm, o_ref, tab, sem, wb_ref = refs[TOKEN_UNROLL:]
    _load_table_once(tab_hbm, tab, sem)
    sub = lax.broadcasted_iota(jnp.int32, (SUBLANES, LANES), 0)
    lower = sub < ROW_WORDS
    tb = x_ref.shape[0]

    def step(i, carry):
        for u in range(TOKEN_UNROLL):
            row = w_ref[pl.ds(i * TOKEN_UNROLL + u, 1), :]
            wb_ref[u] = jnp.broadcast_to(row, (SLOTS, LANES)).T
        acc_h = [jnp.zeros((SUBLANES, LANES), F32) for _ in range(TOKEN_UNROLL)]
        acc_l = [jnp.zeros((SUBLANES, LANES), F32) for _ in range(TOKEN_UNROLL)]
        for p in range(SLOTS // 2):
            ka, kb = 2 * p, 2 * p + 1
            for u in range(TOKEN_UNROLL):
                hi, lo = _gather_pair(tab, idx_refs[u][i, ka], idx_refs[u][i, kb])
                wa = jnp.broadcast_to(wb_ref[u, ka:ka + 1, :], (SUBLANES, LANES))
                wb = jnp.broadcast_to(wb_ref[u, kb:kb + 1, :], (SUBLANES, LANES))
                wt = jnp.where(lower, wa, wb)
                acc_h[u] = acc_h[u] + hi * wt
                acc_l[u] = acc_l[u] + lo * wt
        for u in range(TOKEN_UNROLL):
            t = i * TOKEN_UNROLL + u
            ah = acc_h[u] + pltpu.roll(acc_h[u], ROW_WORDS, 0)
            al = acc_l[u] + pltpu.roll(acc_l[u], ROW_WORDS, 0)
            o_ref[t] = x_ref[t] + jnp.where(lower, ah, al)
        return carry

    lax.fori_loop(0, tb // TOKEN_UNROLL, step, 0)


def _expert_specs(tb):
    un = TOKEN_UNROLL
    smem_blks = [pl.BlockSpec((None, tb // un, SLOTS), functools.partial(lambda u, i: (u, i, 0), u),
                              memory_space=pltpu.SMEM) for u in range(un)]
    vmem_blk = pl.BlockSpec((tb, SLOTS), lambda i: (i, 0))
    tok_blk = pl.BlockSpec((tb, D_MODEL // LANES, LANES), lambda i: (i, 0, 0))
    params = pltpu.CompilerParams(dimension_semantics=("arbitrary",), vmem_limit_bytes=VMEM_LIMIT)
    return smem_blks, vmem_blk, tok_blk, params


def _split_offsets(idx):
    n = idx.shape[0]
    return idx.reshape(n // TOKEN_UNROLL, TOKEN_UNROLL, SLOTS).transpose(1, 0, 2)


def _expert_in(h, idx_split, gate, u_packed, tb=256):
    n, d = h.shape
    assert n % tb == 0 and tb % TOKEN_UNROLL == 0 and d == D_MODEL
    smem_blks, vmem_blk, tok_blk, params = _expert_specs(tb)
    return pl.pallas_call(
        _expert_in_kernel,
        grid=(n // tb,),
        in_specs=smem_blks + [tok_blk, vmem_blk, pl.BlockSpec(memory_space=pl.ANY)],
        out_specs=vmem_blk,
        out_shape=jax.ShapeDtypeStruct((n, SLOTS), F32),
        scratch_shapes=[pltpu.VMEM(u_packed.shape, jnp.int32), pltpu.SemaphoreType.DMA, pltpu.VMEM((tb, SLOTS), F32)],
        compiler_params=params,
        name="peer_expert_in",
    )(*([idx_split] * TOKEN_UNROLL), h.reshape(n, d // LANES, LANES), gate, u_packed)


def _expert_out(x, idx_split, w, v_packed, tb=256):
    t, d = x.shape
    n = w.shape[0]
    assert n % tb == 0 and n <= t and d == D_MODEL
    smem_blks, vmem_blk, tok_blk, params = _expert_specs(tb)
    out = pl.pallas_call(
        _expert_out_kernel,
        grid=(n // tb,),
        in_specs=smem_blks + [vmem_blk, tok_blk, pl.BlockSpec(memory_space=pl.ANY)],
        out_specs=tok_blk,
        out_shape=jax.ShapeDtypeStruct((n, d // LANES, LANES), F32),
        scratch_shapes=[pltpu.VMEM(v_packed.shape, jnp.int32), pltpu.SemaphoreType.DMA,
                        pltpu.VMEM((TOKEN_UNROLL, SLOTS, LANES), F32)],
        compiler_params=params,
        name="peer_expert_out",
    )(*([idx_split] * TOKEN_UNROLL), w, x.reshape(t, d // LANES, LANES), v_packed)
    return out.reshape(n, d)


SC_LANES = 16
SC_CORES = 2
SC_WORKERS = SC_CORES * 16
SC_CHUNK = 32
SC_GROUP = 8
SC_TOKENS = 16896


def _sc_params():
    cp = pltpu.CompilerParams()
    if "needs_layout_passes" in pltpu.CompilerParams.__dataclass_fields__:
        cp = dataclasses.replace(cp, needs_layout_passes=False)
    return cp


def _sc_expert_out(table_words, idx, w, x, first_tok):
    d = x.shape[1]
    n_tok = w.shape[0] // SLOTS
    assert n_tok % (SC_WORKERS * SC_GROUP) == 0 and first_tok % SC_GROUP == 0 and d == D_MODEL
    per = n_tok // SC_WORKERS
    words = d // 2
    nq = words // SC_LANES // 2
    nchunk = SLOTS // SC_CHUNK
    group_chunks = SC_GROUP * nchunk
    mesh = plsc.VectorSubcoreMesh(core_axis_name="c", subcore_axis_name="s")

    @functools.partial(
        pl.kernel, mesh=mesh,
        out_type=jax.ShapeDtypeStruct((n_tok, d), F32),
        scratch_types=[pltpu.VMEM((SC_GROUP * SLOTS,), jnp.int32), pltpu.VMEM((SC_GROUP * SLOTS,), F32),
                       pltpu.VMEM((2, SC_CHUNK, words), jnp.int32), pltpu.VMEM((SC_GROUP, d), F32),
                       pltpu.SemaphoreType.DMA, pltpu.SemaphoreType.DMA],
        compiler_params=_sc_params(),
        name="peer_expert_out_sc",
    )
    def body(tab_hbm, idx_hbm, w_hbm, x_hbm, o_hbm, idx_v, w_v, rows_v, y_v, sem0, sem1):
        base = (lax.axis_index("s") * SC_CORES + lax.axis_index("c")) * per
        zero = jnp.zeros((SC_LANES,), jnp.int32)
        sems = (sem0, sem1)

        def gather(k, b):
            off = pl.multiple_of(k * SC_CHUNK, SC_CHUNK)
            return pltpu.make_async_copy(tab_hbm.at[idx_v.at[pl.ds(off, SC_CHUNK)]], rows_v.at[b], sems[b])

        def accumulate(k, b):
            tok = k // nchunk
            for q in range(2):
                first = q * nq
                acc0 = (tuple(y_v[tok, pl.ds((first + j) * SC_LANES, SC_LANES)] for j in range(nq))
                        + tuple(y_v[tok, pl.ds(words + (first + j) * SC_LANES, SC_LANES)] for j in range(nq)))

                def row_body(r, accs):
                    ws = plsc.load_gather(w_v, [zero + (k * SC_CHUNK + r)])
                    hi_acc, lo_acc = [], []
                    for j in range(nq):
                        wv = rows_v[b, r, pl.ds((first + j) * SC_LANES, SC_LANES)]
                        lo, hi = plsc.unpack(plsc.bitcast(wv, BF16), format=plsc.PackFormat.INTERLEAVED,
                                             preferred_element_type=F32)
                        hi_acc.append(accs[j] + hi * ws)
                        lo_acc.append(accs[nq + j] + lo * ws)
                    return tuple(hi_acc) + tuple(lo_acc)

                accs = lax.fori_loop(0, SC_CHUNK, row_body, acc0)
                for j in range(nq):
                    y_v[tok, pl.ds((first + j) * SC_LANES, SC_LANES)] = accs[j]
                    y_v[tok, pl.ds(words + (first + j) * SC_LANES, SC_LANES)] = accs[nq + j]

        @pl.loop(0, per // SC_GROUP)
        def _(g):
            t0 = pl.multiple_of(base + g * SC_GROUP, SC_GROUP)
            pltpu.sync_copy(idx_hbm.at[pl.ds(t0 * SLOTS, SC_GROUP * SLOTS)], idx_v)
            pltpu.sync_copy(w_hbm.at[pl.ds(t0 * SLOTS, SC_GROUP * SLOTS)], w_v)
            pltpu.sync_copy(x_hbm.at[pl.ds(first_tok + t0, SC_GROUP)], y_v)
            gather(0, 0).start()

            @pl.loop(0, group_chunks // 2)
            def _(kk):
                k0 = 2 * kk
                gather(k0 + 1, 1).start()
                gather(k0, 0).wait()
                accumulate(k0, 0)

                @pl.when(k0 + 2 < group_chunks)
                def _():
                    gather(k0 + 2, 0).start()
                gather(k0 + 1, 1).wait()
                accumulate(k0 + 1, 1)

            pltpu.sync_copy(y_v, o_hbm.at[pl.ds(t0, SC_GROUP)])

    return body(table_words, idx, w, x)


def _peer(x, gain, w_query, sub_keys, experts_u, experts_v, layer):
    t = x.shape[0]
    t_tc = t - SC_TOKENS
    wq, keys = w_query.astype(BF16), sub_keys.astype(BF16)
    (u_packed,) = _pack_table(experts_u, layer, with_rows=False)
    v_packed, v_rows = _pack_table(experts_v, layer, with_rows=True)

    h, idx, gate = _route(x, gain, wq, keys, t_tc, SC_TOKENS, v_rows[:SUBLANES])
    w_sc = _expert_in(h, _split_offsets(idx), gate, u_packed)
    out_sc = _sc_expert_out(v_rows, (idx // ROW_WORDS).reshape(-1), w_sc.reshape(-1), x, t_tc)

    h, idx, gate = _route(x, gain, wq, keys, 0, t_tc, w_sc[:SUBLANES])
    idx_split = _split_offsets(idx)
    w = _expert_in(h, idx_split, gate, u_packed)
    out_tc = _expert_out(x, idx_split, w, v_packed)
    return jnp.concatenate([out_tc, out_sc], axis=0)


def kernel(x, norm_mix, norm_ffn, attn_w_qkv, attn_q_norm, attn_k_norm, attn_sinks, attn_w_o, conv_w_in, conv_w, conv_w_out, peer_w_query, peer_sub_keys, peer_u, peer_v):
    batch, seq, d = x.shape
    xt = x.reshape(batch * seq, d)
    for i in range(norm_mix.shape[0]):
        j = i // 2
        if i % 2 == 0:
            qkv = _norm_matmul(xt, norm_mix[i], attn_w_qkv[j].astype(BF16))
            o = _attention(qkv, attn_q_norm[j], attn_k_norm[j], attn_sinks[j], batch, seq)
            xt = _matmul_residual(o, attn_w_o[j].astype(BF16), xt)
        else:
            xt = _conv_mixer(xt, norm_mix[i], conv_w_in[j], conv_w[j], conv_w_out[j], batch, seq)
        xt = _peer(xt, norm_ffn[i], peer_w_query[i], peer_sub_keys[i], peer_u, peer_v, i)
    return xt.reshape(batch, seq, d)
```

```python
import dataclasses
import functools
import math

import jax
import jax.numpy as jnp
from jax import lax
from jax.experimental import pallas as pl
from jax.experimental.pallas import tpu as pltpu
from jax.experimental.pallas import tpu_sc as plsc

D_MODEL = 1024
RMS_EPS = 1e-6

HEAD_DIM = 64
N_Q_HEADS = 16
N_KV_HEADS = 4
GROUP = N_Q_HEADS // N_KV_HEADS
WINDOW = 128
ROT_DIM = HEAD_DIM // 4
ROPE_THETA = 500000.0
Q_COLS = N_Q_HEADS * HEAD_DIM
KV_COLS = N_KV_HEADS * HEAD_DIM
NEG_INF = -1e30

CONV_WIDTH = 3

PEER_HEADS = 8
N_KEYS = 128
N_EXPERTS = N_KEYS * N_KEYS
PEER_TOPK = 16
QUERY_HALF = 128
SLOTS = PEER_HEADS * PEER_TOPK

LANES = 128
SUBLANES = 8
ROW_WORDS = D_MODEL // 2 // LANES
VMEM_LIMIT = 48 * 1024 * 1024

BF16 = jnp.bfloat16
F32 = jnp.float32


def _rms(x, gain):
    return x * lax.rsqrt(jnp.mean(x * x, axis=-1, keepdims=True) + RMS_EPS) * gain


def _norm_matmul_kernel(x_ref, g_ref, w_ref, o_ref):
    h = _rms(x_ref[...], g_ref[...])
    o_ref[...] = jnp.dot(h.astype(BF16), w_ref[...], preferred_element_type=F32)


def _norm_matmul(x, gain, w, tm=512):
    t, d = x.shape
    n = w.shape[1]
    return pl.pallas_call(
        _norm_matmul_kernel,
        grid=(t // tm,),
        in_specs=[pl.BlockSpec((tm, d), lambda i: (i, 0)),
                  pl.BlockSpec((1, d), lambda i: (0, 0)),
                  pl.BlockSpec((d, n), lambda i: (0, 0))],
        out_specs=pl.BlockSpec((tm, n), lambda i: (i, 0)),
        out_shape=jax.ShapeDtypeStruct((t, n), F32),
        compiler_params=pltpu.CompilerParams(dimension_semantics=("arbitrary",), vmem_limit_bytes=VMEM_LIMIT),
        name="norm_matmul",
    )(x, gain.reshape(1, d), w)


def _matmul_residual_kernel(a_ref, w_ref, r_ref, o_ref):
    o_ref[...] = r_ref[...] + jnp.dot(a_ref[...].astype(BF16), w_ref[...], preferred_element_type=F32)


def _matmul_residual(a, w, res, tm=512):
    t, k = a.shape
    n = w.shape[1]
    return pl.pallas_call(
        _matmul_residual_kernel,
        grid=(t // tm,),
        in_specs=[pl.BlockSpec((tm, k), lambda i: (i, 0)),
                  pl.BlockSpec((k, n), lambda i: (0, 0)),
                  pl.BlockSpec((tm, n), lambda i: (i, 0))],
        out_specs=pl.BlockSpec((tm, n), lambda i: (i, 0)),
        out_shape=jax.ShapeDtypeStruct((t, n), F32),
        compiler_params=pltpu.CompilerParams(dimension_semantics=("arbitrary",), vmem_limit_bytes=VMEM_LIMIT),
        name="matmul_residual",
    )(a, w, res)


def _rope_tables(seq):
    half = ROT_DIM // 2
    freqs = ROPE_THETA ** (-jnp.arange(0, ROT_DIM, 2, dtype=F32) / ROT_DIM)
    ang = jnp.arange(seq, dtype=F32)[:, None] * freqs[None, :]
    cos, sin = jnp.cos(ang), jnp.sin(ang)
    ones = jnp.ones((seq, HEAD_DIM - ROT_DIM), F32)
    zeros = jnp.zeros((seq, HEAD_DIM - ROT_DIM), F32)
    zh = jnp.zeros((seq, half), F32)
    c = jnp.concatenate([cos, cos, ones], axis=1)
    s_next = jnp.concatenate([-sin, zh, zeros], axis=1)
    s_prev = jnp.concatenate([zh, sin, zeros], axis=1)
    return jnp.stack([jnp.tile(c, (1, 2)), jnp.tile(s_next, (1, 2)), jnp.tile(s_prev, (1, 2))])


def _head_norm_rope(x, gain2, rope, lo):
    sq = x * x
    s_lo = jnp.sum(jnp.where(lo, sq, 0.0), axis=1, keepdims=True)
    s_hi = jnp.sum(jnp.where(lo, 0.0, sq), axis=1, keepdims=True)
    ms = jnp.where(lo, s_lo, s_hi) * (1.0 / HEAD_DIM)
    xn = x * lax.rsqrt(ms + RMS_EPS) * gain2
    half = ROT_DIM // 2
    return xn * rope[0] + pltpu.roll(xn, LANES - half, 1) * rope[1] + pltpu.roll(xn, half, 1) * rope[2]


def _attn_kernel(sinks_ref, q_ref, kc_ref, kp_ref, vc_ref, vp_ref, rc_ref, rp_ref, qg_ref, kg_ref, o_ref):
    n = pl.program_id(1)
    lo = lax.broadcasted_iota(jnp.int32, (WINDOW, LANES), 1) < HEAD_DIM
    rope_c = rc_ref[...]
    rope_p = rp_ref[...]
    qg = qg_ref[...]
    kg = kg_ref[...]

    rows = GROUP * WINDOW
    qi = lax.broadcasted_iota(jnp.int32, (rows, 2 * WINDOW), 0) & (WINDOW - 1)
    ki = lax.broadcasted_iota(jnp.int32, (rows, 2 * WINDOW), 1)
    rel = WINDOW + qi - ki
    valid = (rel >= 0) & (rel < WINDOW) & ((n > 0) | (ki >= WINDOW))
    head_of_row = lax.broadcasted_iota(jnp.int32, (rows, 1), 0) // WINDOW
    scale = 1.0 / math.sqrt(HEAD_DIM)

    q2 = [_head_norm_rope(q_ref[:, c * LANES:(c + 1) * LANES], qg, rope_c, lo).astype(BF16)
          for c in range(Q_COLS // LANES)]
    for c in range(KV_COLS // LANES):
        cols = slice(c * LANES, (c + 1) * LANES)
        kprev = _head_norm_rope(kp_ref[:, cols], kg, rope_p, lo)
        kcur = _head_norm_rope(kc_ref[:, cols], kg, rope_c, lo)
        kfull = jnp.concatenate([kprev, kcur], axis=0).astype(BF16)
        vfull = jnp.concatenate([vp_ref[:, cols], vc_ref[:, cols]], axis=0).astype(BF16)
        for hh in range(LANES // HEAD_DIM):
            h = (LANES // HEAD_DIM) * c + hh
            kh = kfull[:, hh * HEAD_DIM:(hh + 1) * HEAD_DIM]
            vh = vfull[:, hh * HEAD_DIM:(hh + 1) * HEAD_DIM]
            heads = [GROUP * h + g for g in range(GROUP)]
            q4 = jnp.concatenate([q2[j // 2][:, (j % 2) * HEAD_DIM:(j % 2 + 1) * HEAD_DIM] for j in heads], axis=0)
            s = lax.dot_general(q4, kh, (((1,), (1,)), ((), ())), preferred_element_type=F32) * scale
            s = jnp.where(valid, s, NEG_INF)
            sink = jnp.zeros((rows, 1), F32)
            for g, j in enumerate(heads):
                sink = jnp.where(head_of_row == g, sinks_ref[j], sink)
            m = jnp.maximum(jnp.max(s, axis=1, keepdims=True), sink)
            p = jnp.exp(s - m)
            denom = jnp.sum(p, axis=1, keepdims=True) + jnp.exp(sink - m)
            o = jnp.dot(p.astype(BF16), vh, preferred_element_type=F32) / denom
            for g, j in enumerate(heads):
                o_ref[:, j * HEAD_DIM:(j + 1) * HEAD_DIM] = o[g * WINDOW:(g + 1) * WINDOW]


def _attention(qkv, q_gain, k_gain, sinks, batch, seq):
    t = batch * seq
    nb = seq // WINDOW
    rope = _rope_tables(seq)
    kcol = Q_COLS // KV_COLS
    cur = lambda b, n: (b * nb + n, 0)
    kcur = lambda b, n: (b * nb + n, kcol)
    kprev = lambda b, n: (b * nb + jnp.maximum(n - 1, 0), kcol)
    vcur = lambda b, n: (b * nb + n, kcol + 1)
    vprev = lambda b, n: (b * nb + jnp.maximum(n - 1, 0), kcol + 1)
    return pl.pallas_call(
        _attn_kernel,
        grid=(batch, nb),
        in_specs=[pl.BlockSpec(memory_space=pltpu.SMEM),
                  pl.BlockSpec((WINDOW, Q_COLS), cur),
                  pl.BlockSpec((WINDOW, KV_COLS), kcur),
                  pl.BlockSpec((WINDOW, KV_COLS), kprev),
                  pl.BlockSpec((WINDOW, KV_COLS), vcur),
                  pl.BlockSpec((WINDOW, KV_COLS), vprev),
                  pl.BlockSpec((3, WINDOW, LANES), lambda b, n: (0, n, 0)),
                  pl.BlockSpec((3, WINDOW, LANES), lambda b, n: (0, jnp.maximum(n - 1, 0), 0)),
                  pl.BlockSpec((1, LANES), lambda b, n: (0, 0)),
                  pl.BlockSpec((1, LANES), lambda b, n: (0, 0))],
        out_specs=pl.BlockSpec((WINDOW, Q_COLS), cur),
        out_shape=jax.ShapeDtypeStruct((t, Q_COLS), F32),
        compiler_params=pltpu.CompilerParams(dimension_semantics=("arbitrary", "arbitrary"),
                                             vmem_limit_bytes=VMEM_LIMIT),
        name="swa_attention",
    )(sinks, qkv, qkv, qkv, qkv, qkv, rope, rope,
      jnp.tile(q_gain, 2).reshape(1, LANES), jnp.tile(k_gain, 2).reshape(1, LANES))


def _conv_kernel(x_ref, g_ref, win_ref, cw_ref, wout_ref, o_ref, zprev_ref):
    n = pl.program_id(1)
    d = D_MODEL

    @pl.when(n == 0)
    def _():
        zprev_ref[...] = jnp.zeros_like(zprev_ref)

    x = x_ref[...]
    h = _rms(x, g_ref[...])
    bcu = jnp.dot(h.astype(BF16), win_ref[...], preferred_element_type=F32)
    gate_b = bcu[:, :d]
    z = bcu[:, d:2 * d] * bcu[:, 2 * d:]
    tm = z.shape[0]
    row = lax.broadcasted_iota(jnp.int32, z.shape, 0)
    prev = zprev_ref[...]
    p_last = prev[SUBLANES - 1:SUBLANES, :]
    p_last2 = prev[SUBLANES - 2:SUBLANES - 1, :]
    z1 = jnp.where(row == 0, p_last, pltpu.roll(z, 1, 0))
    z2 = jnp.where(row == 0, p_last2, jnp.where(row == 1, p_last, pltpu.roll(z, 2, 0)))
    cw = cw_ref[...]
    conv = cw[0:1, :] * z2 + cw[1:2, :] * z1 + cw[2:3, :] * z
    zprev_ref[...] = z[tm - SUBLANES:, :]
    o_ref[...] = x + jnp.dot((gate_b * conv).astype(BF16), wout_ref[...], preferred_element_type=F32)


def _conv_mixer(x, gain, w_in, conv_w, w_out, batch, seq, tm=256):
    t, d = x.shape
    nblk = seq // tm
    blk = lambda b, n: (b * nblk + n, 0)
    const = lambda b, n: (0, 0)
    return pl.pallas_call(
        _conv_kernel,
        grid=(batch, nblk),
        in_specs=[pl.BlockSpec((tm, d), blk),
                  pl.BlockSpec((1, d), const),
                  pl.BlockSpec((d, 3 * d), const),
                  pl.BlockSpec((CONV_WIDTH, d), const),
                  pl.BlockSpec((d, d), const)],
        out_specs=pl.BlockSpec((tm, d), blk),
        out_shape=jax.ShapeDtypeStruct((t, d), F32),
        scratch_shapes=[pltpu.VMEM((SUBLANES, d), F32)],
        compiler_params=pltpu.CompilerParams(dimension_semantics=("arbitrary", "arbitrary"),
                                             vmem_limit_bytes=VMEM_LIMIT),
        name="conv_mixer",
    )(x, gain.reshape(1, d), w_in.astype(BF16), conv_w, w_out.astype(BF16))


def _topk_axis0(s, k, ids=None, payload=None):
    n, tm = s.shape
    if ids is None:
        ids = lax.broadcasted_iota(jnp.int32, (n, tm), 0)
    krow = lax.broadcasted_iota(jnp.int32, (k, tm), 0)
    vals = jnp.zeros((k, tm), F32)
    picks = jnp.zeros((k, tm), jnp.int32)
    for r in range(k):
        m = jnp.max(s, axis=0, keepdims=True)
        pos = jnp.min(jnp.where(s == m, ids, jnp.iinfo(jnp.int32).max), axis=0, keepdims=True)
        sel = ids == pos
        if payload is None:
            picked = pos
        else:
            picked = jnp.sum(jnp.where(sel, payload, 0), axis=0, keepdims=True)
        vals = jnp.where(krow == r, m, vals)
        picks = jnp.where(krow == r, picked, picks)
        s = jnp.where(sel, -jnp.inf, s)
    return vals, picks


def _pair_candidates(s1, i1, s2, i2):
    k, tm = s1.shape
    sub = lax.broadcasted_iota(jnp.int32, (SUBLANES, tm), 0)
    s2a, i2a = s2[0:SUBLANES, :], i2[0:SUBLANES, :]
    scores = [s1[0:1, :] + s2, s1[1:2, :] + s2a]
    flat = [lax.broadcasted_iota(jnp.int32, (k, tm), 0), k + sub]
    expert = [i1[0:1, :] * N_KEYS + i2, i1[1:2, :] * N_KEYS + i2a]
    for n, tile in enumerate((((2, 0), (5, 5)), ((3, 0), (4, 4)), ((6, 0), (7, 2)))):
        sc = jnp.full((SUBLANES, tm), -jnp.inf, F32)
        fl = k * k + n * SUBLANES + sub
        ex = jnp.zeros((SUBLANES, tm), jnp.int32)
        for i, off in tile:
            inside = (sub >= off) & (sub < off + k // (i + 1))
            s2r = s2a if off == 0 else pltpu.roll(s2a, off, 0)
            i2r = i2a if off == 0 else pltpu.roll(i2a, off, 0)
            sc = jnp.where(inside, s1[i:i + 1, :] + s2r, sc)
            fl = jnp.where(inside, i * k + sub - off, fl)
            ex = jnp.where(inside, i1[i:i + 1, :] * N_KEYS + i2r, ex)
        scores.append(sc)
        flat.append(fl)
        expert.append(ex)
    scores.append(s1[k // 2:, :] + s2[0:1, :])
    flat.append((k // 2 + sub) * k)
    expert.append(i1[k // 2:, :] * N_KEYS + i2[0:1, :])
    return jnp.concatenate(scores, axis=0), jnp.concatenate(flat, axis=0), jnp.concatenate(expert, axis=0)


def _route_kernel(x_ref, g_ref, wq_ref, keys_ref, after_ref, h_ref, idx_ref, gate_ref):
    del after_ref
    h = _rms(x_ref[...], g_ref[...])
    h_ref[...] = h
    q = jnp.dot(h.astype(BF16), wq_ref[...], preferred_element_type=F32).astype(BF16)
    idx_rows, gate_rows = [], []
    for head in range(PEER_HEADS):
        tops = []
        for part in range(2):
            col = (head * 2 + part) * QUERY_HALF
            s = lax.dot_general(keys_ref[head, part], q[:, col:col + QUERY_HALF],
                                (((1,), (1,)), ((), ())), preferred_element_type=F32)
            tops.append(_topk_axis0(s, PEER_TOPK))
        (s1, i1), (s2, i2) = tops
        cand, flat_ids, cand_idx = _pair_candidates(s1, i1, s2, i2)
        g_s, e_idx = _topk_axis0(cand, PEER_TOPK, ids=flat_ids, payload=cand_idx)
        e = jnp.exp(g_s - jnp.max(g_s, axis=0, keepdims=True))
        gate_rows.append(e / jnp.sum(e, axis=0, keepdims=True))
        idx_rows.append(e_idx * ROW_WORDS)
    idx_ref[...] = jnp.concatenate(idx_rows, axis=0).T
    gate_ref[...] = jnp.concatenate(gate_rows, axis=0).T


def _route(x, gain, w_query, sub_keys, first_tok, n_tok, after, tm=128):
    t, d = x.shape
    nq = w_query.shape[1]
    first_blk = first_tok // tm
    return pl.pallas_call(
        _route_kernel,
        grid=(n_tok // tm,),
        in_specs=[pl.BlockSpec((tm, d), lambda i: (i + first_blk, 0)),
                  pl.BlockSpec((1, d), lambda i: (0, 0)),
                  pl.BlockSpec((d, nq), lambda i: (0, 0)),
                  pl.BlockSpec((PEER_HEADS, 2, N_KEYS, QUERY_HALF), lambda i: (0, 0, 0, 0)),
                  pl.BlockSpec(after.shape, lambda i: (0, 0))],
        out_specs=[pl.BlockSpec((tm, d), lambda i: (i, 0)),
                   pl.BlockSpec((tm, SLOTS), lambda i: (i, 0)),
                   pl.BlockSpec((tm, SLOTS), lambda i: (i, 0))],
        out_shape=[jax.ShapeDtypeStruct((n_tok, d), F32),
                   jax.ShapeDtypeStruct((n_tok, SLOTS), jnp.int32),
                   jax.ShapeDtypeStruct((n_tok, SLOTS), F32)],
        compiler_params=pltpu.CompilerParams(dimension_semantics=("arbitrary",), vmem_limit_bytes=VMEM_LIMIT),
        name="peer_route",
    )(x, gain.reshape(1, d), w_query, sub_keys, after)


def _pack_kernel(t_ref, tiles_ref, *rows_ref):
    x = t_ref[...]
    half = x.shape[1] // 2
    hi = pltpu.bitcast(x[:, :half].astype(BF16).astype(F32), jnp.int32)
    lo = pltpu.bitcast(x[:, half:].astype(BF16).astype(F32), jnp.int32)
    words = hi | lax.shift_right_logical(lo, 16)
    for ref in rows_ref:
        ref[...] = words
    for s in range(ROW_WORDS):
        tiles_ref[pl.ds(s, x.shape[0], stride=ROW_WORDS), :] = words[:, s * LANES:(s + 1) * LANES]


def _pack_table(tabs, layer, with_rows, rows_per_step=256):
    _, ne, d = tabs.shape
    out_specs = [pl.BlockSpec((rows_per_step * ROW_WORDS, LANES), lambda i: (i, 0))]
    out_shape = [jax.ShapeDtypeStruct((ne * ROW_WORDS, LANES), jnp.int32)]
    if with_rows:
        out_specs.append(pl.BlockSpec((rows_per_step, d // 2), lambda i: (i, 0)))
        out_shape.append(jax.ShapeDtypeStruct((ne, d // 2), jnp.int32))
    return pl.pallas_call(
        _pack_kernel,
        grid=(ne // rows_per_step,),
        in_specs=[pl.BlockSpec((None, rows_per_step, d), lambda i: (layer, i, 0))],
        out_specs=out_specs,
        out_shape=out_shape,
        compiler_params=pltpu.CompilerParams(dimension_semantics=("arbitrary",), vmem_limit_bytes=VMEM_LIMIT),
        name="pack_table",
    )(tabs)


def _load_table_once(tab_hbm, tab, sem):
    @pl.when(pl.program_id(0) == 0)
    def _():
        cp = pltpu.make_async_copy(tab_hbm, tab, sem)
        cp.start()
        cp.wait()


def _gather_pair(tab, off_a, off_b):
    ra = tab[pl.ds(pl.multiple_of(off_a, ROW_WORDS), ROW_WORDS), :]
    rb = tab[pl.ds(pl.multiple_of(off_b, ROW_WORDS), ROW_WORDS), :]
    words = jnp.concatenate([ra, rb], axis=0)
    hi = pltpu.bitcast(words & jnp.int32(-65536), F32)
    lo = pltpu.bitcast(words << 16, F32)
    return hi, lo


TOKEN_UNROLL = 8


def _expert_in_kernel(*refs):
    idx_refs = refs[:TOKEN_UNROLL]
    h_ref, gate_ref, tab_hbm, w_ref, tab, sem, a_ref = refs[TOKEN_UNROLL:]
    _load_table_once(tab_hbm, tab, sem)
    sub = lax.broadcasted_iota(jnp.int32, (SUBLANES, LANES), 0)
    lane = lax.broadcasted_iota(jnp.int32, (SUBLANES, LANES), 1)
    own_half = (sub >= ROW_WORDS) == ((lane & 1) == 1)
    tb = h_ref.shape[0]

    def step(i, carry):
        xs = []
        for u in range(TOKEN_UNROLL):
            x = h_ref[i * TOKEN_UNROLL + u]
            xs.append((jnp.concatenate([x[0:ROW_WORDS], x[0:ROW_WORDS]], axis=0),
                       jnp.concatenate([x[ROW_WORDS:], x[ROW_WORDS:]], axis=0)))
        accs = [jnp.zeros((SUBLANES, LANES), F32) for _ in range(TOKEN_UNROLL)]
        for p in range(SLOTS // 2):
            for u in range(TOKEN_UNROLL):
                hi, lo = _gather_pair(tab, idx_refs[u][i, 2 * p], idx_refs[u][i, 2 * p + 1])
                part = jnp.sum(hi * xs[u][0] + lo * xs[u][1], axis=1, keepdims=True)
                accs[u] = jnp.where((lane >> 1) == p, part, accs[u])
        for u in range(TOKEN_UNROLL):
            a_ref[pl.ds(i * TOKEN_UNROLL + u, 1), :] = jnp.sum(jnp.where(own_half, accs[u], 0.0),
                                                                axis=0, keepdims=True)
        return carry

    lax.fori_loop(0, tb // TOKEN_UNROLL, step, 0)
    a = a_ref[...]
    w_ref[...] = gate_ref[...] * (0.5 * a * (1.0 + lax.erf(a * (1.0 / math.sqrt(2.0)))))


def _expert_out_kernel(*refs):
    idx_refs = refs[:TOKEN_UNROLL]
    w_ref, x_ref, tab_hbm, o_ref, tab, sem, wb_ref = refs[TOKEN_UNROLL:]
    _load_table_once(tab_hbm, tab, sem)
    sub = lax.broadcasted_iota(jnp.int32, (SUBLANES, LANES), 0)
    lower = sub < ROW_WORDS
    tb = x_ref.shape[0]

    def step(i, carry):
        for u in range(TOKEN_UNROLL):
            row = w_ref[pl.ds(i * TOKEN_UNROLL + u, 1), :]
            wb_ref[u] = jnp.broadcast_to(row, (SLOTS, LANES)).T
        acc_h = [jnp.zeros((SUBLANES, LANES), F32) for _ in range(TOKEN_UNROLL)]
        acc_l = [jnp.zeros((SUBLANES, LANES), F32) for _ in range(TOKEN_UNROLL)]
        for p in range(SLOTS // 2):
            ka, kb = 2 * p, 2 * p + 1
            for u in range(TOKEN_UNROLL):
                hi, lo = _gather_pair(tab, idx_refs[u][i, ka], idx_refs[u][i, kb])
                wa = jnp.broadcast_to(wb_ref[u, ka:ka + 1, :], (SUBLANES, LANES))
                wb = jnp.broadcast_to(wb_ref[u, kb:kb + 1, :], (SUBLANES, LANES))
                wt = jnp.where(lower, wa, wb)
                acc_h[u] = acc_h[u] + hi * wt
                acc_l[u] = acc_l[u] + lo * wt
        for u in range(TOKEN_UNROLL):
            t = i * TOKEN_UNROLL + u
            ah = acc_h[u] + pltpu.roll(acc_h[u], ROW_WORDS, 0)
            al = acc_l[u] + pltpu.roll(acc_l[u], ROW_WORDS, 0)
            o_ref[t] = x_ref[t] + jnp.where(lower, ah, al)
        return carry

    lax.fori_loop(0, tb // TOKEN_UNROLL, step, 0)


def _expert_specs(tb):
    un = TOKEN_UNROLL
    smem_blks = [pl.BlockSpec((None, tb // un, SLOTS), functools.partial(lambda u, i: (u, i, 0), u),
                              memory_space=pltpu.SMEM) for u in range(un)]
    vmem_blk = pl.BlockSpec((tb, SLOTS), lambda i: (i, 0))
    tok_blk = pl.BlockSpec((tb, D_MODEL // LANES, LANES), lambda i: (i, 0, 0))
    params = pltpu.CompilerParams(dimension_semantics=("arbitrary",), vmem_limit_bytes=VMEM_LIMIT)
    return smem_blks, vmem_blk, tok_blk, params


def _split_offsets(idx):
    n = idx.shape[0]
    return idx.reshape(n // TOKEN_UNROLL, TOKEN_UNROLL, SLOTS).transpose(1, 0, 2)


def _expert_in(h, idx_split, gate, u_packed, tb=256):
    n, d = h.shape
    assert n % tb == 0 and tb % TOKEN_UNROLL == 0 and d == D_MODEL
    smem_blks, vmem_blk, tok_blk, params = _expert_specs(tb)
    return pl.pallas_call(
        _expert_in_kernel,
        grid=(n // tb,),
        in_specs=smem_blks + [tok_blk, vmem_blk, pl.BlockSpec(memory_space=pl.ANY)],
        out_specs=vmem_blk,
        out_shape=jax.ShapeDtypeStruct((n, SLOTS), F32),
        scratch_shapes=[pltpu.VMEM(u_packed.shape, jnp.int32), pltpu.SemaphoreType.DMA, pltpu.VMEM((tb, SLOTS), F32)],
        compiler_params=params,
        name="peer_expert_in",
    )(*([idx_split] * TOKEN_UNROLL), h.reshape(n, d // LANES, LANES), gate, u_packed)


def _expert_out(x, idx_split, w, v_packed, tb=256):
    t, d = x.shape
    n = w.shape[0]
    assert n % tb == 0 and n <= t and d == D_MODEL
    smem_blks, vmem_blk, tok_blk, params = _expert_specs(tb)
    out = pl.pallas_call(
        _expert_out_kernel,
        grid=(n // tb,),
        in_specs=smem_blks + [vmem_blk, tok_blk, pl.BlockSpec(memory_space=pl.ANY)],
        out_specs=tok_blk,
        out_shape=jax.ShapeDtypeStruct((n, d // LANES, LANES), F32),
        scratch_shapes=[pltpu.VMEM(v_packed.shape, jnp.int32), pltpu.SemaphoreType.DMA,
                        pltpu.VMEM((TOKEN_UNROLL, SLOTS, LANES), F32)],
        compiler_params=params,
        name="peer_expert_out",
    )(*([idx_split] * TOKEN_UNROLL), w, x.reshape(t, d // LANES, LANES), v_packed)
    return out.reshape(n, d)


SC_LANES = 16
SC_CORES = 2
SC_WORKERS = SC_CORES * 16
SC_CHUNK = 32
SC_GROUP = 8
SC_BUFFERS = 4
SC_TOKENS = 17408


def _sc_params():
    cp = pltpu.CompilerParams()
    if "needs_layout_passes" in pltpu.CompilerParams.__dataclass_fields__:
        cp = dataclasses.replace(cp, needs_layout_passes=False)
    return cp


def _sc_expert_out(table_words, idx, w, x, first_tok):
    d = x.shape[1]
    n_tok = w.shape[0] // SLOTS
    assert n_tok % (SC_WORKERS * SC_GROUP) == 0 and first_tok % SC_GROUP == 0 and d == D_MODEL
    per = n_tok // SC_WORKERS
    words = d // 2
    nq = words // SC_LANES // 2
    nchunk = SLOTS // SC_CHUNK
    group_chunks = SC_GROUP * nchunk
    mesh = plsc.VectorSubcoreMesh(core_axis_name="c", subcore_axis_name="s")

    @functools.partial(
        pl.kernel, mesh=mesh,
        out_type=jax.ShapeDtypeStruct((n_tok, d), F32),
        scratch_types=[pltpu.VMEM((SC_GROUP * SLOTS,), jnp.int32), pltpu.VMEM((SC_GROUP * SLOTS,), F32),
                       pltpu.VMEM((SC_BUFFERS, SC_CHUNK, words), jnp.int32), pltpu.VMEM((SC_GROUP, d), F32)]
        + [pltpu.SemaphoreType.DMA] * SC_BUFFERS,
        compiler_params=_sc_params(),
        name="peer_expert_out_sc",
    )
    def body(tab_hbm, idx_hbm, w_hbm, x_hbm, o_hbm, idx_v, w_v, rows_v, y_v, *sems):
        base = (lax.axis_index("s") * SC_CORES + lax.axis_index("c")) * per
        zero = jnp.zeros((SC_LANES,), jnp.int32)

        def gather(k, b):
            off = pl.multiple_of(k * SC_CHUNK, SC_CHUNK)
            return pltpu.make_async_copy(tab_hbm.at[idx_v.at[pl.ds(off, SC_CHUNK)]], rows_v.at[b], sems[b])

        def accumulate(k, b):
            tok = k // nchunk
            for q in range(2):
                first = q * nq
                acc0 = (tuple(y_v[tok, pl.ds((first + j) * SC_LANES, SC_LANES)] for j in range(nq))
                        + tuple(y_v[tok, pl.ds(words + (first + j) * SC_LANES, SC_LANES)] for j in range(nq)))

                def row_body(r, accs):
                    ws = plsc.load_gather(w_v, [zero + (k * SC_CHUNK + r)])
                    hi_acc, lo_acc = [], []
                    for j in range(nq):
                        wv = rows_v[b, r, pl.ds((first + j) * SC_LANES, SC_LANES)]
                        lo, hi = plsc.unpack(plsc.bitcast(wv, BF16), format=plsc.PackFormat.INTERLEAVED,
                                             preferred_element_type=F32)
                        hi_acc.append(accs[j] + hi * ws)
                        lo_acc.append(accs[nq + j] + lo * ws)
                    return tuple(hi_acc) + tuple(lo_acc)

                accs = lax.fori_loop(0, SC_CHUNK, row_body, acc0)
                for j in range(nq):
                    y_v[tok, pl.ds((first + j) * SC_LANES, SC_LANES)] = accs[j]
                    y_v[tok, pl.ds(words + (first + j) * SC_LANES, SC_LANES)] = accs[nq + j]

        @pl.loop(0, per // SC_GROUP)
        def _(g):
            t0 = pl.multiple_of(base + g * SC_GROUP, SC_GROUP)
            pltpu.sync_copy(idx_hbm.at[pl.ds(t0 * SLOTS, SC_GROUP * SLOTS)], idx_v)
            pltpu.sync_copy(w_hbm.at[pl.ds(t0 * SLOTS, SC_GROUP * SLOTS)], w_v)
            pltpu.sync_copy(x_hbm.at[pl.ds(first_tok + t0, SC_GROUP)], y_v)
            for b in range(SC_BUFFERS - 1):
                gather(b, b).start()

            @pl.loop(0, group_chunks // SC_BUFFERS)
            def _(kk):
                for b in range(SC_BUFFERS):
                    k = SC_BUFFERS * kk + b
                    ahead = k + SC_BUFFERS - 1

                    @pl.when(ahead < group_chunks)
                    def _():
                        gather(ahead, (b + SC_BUFFERS - 1) % SC_BUFFERS).start()
                    gather(k, b).wait()
                    accumulate(k, b)

            pltpu.sync_copy(y_v, o_hbm.at[pl.ds(t0, SC_GROUP)])

    return body(table_words, idx, w, x)


def _peer(x, gain, w_query, sub_keys, experts_u, experts_v, layer):
    t = x.shape[0]
    t_tc = t - SC_TOKENS
    wq, keys = w_query.astype(BF16), sub_keys.astype(BF16)
    (u_packed,) = _pack_table(experts_u, layer, with_rows=False)
    v_packed, v_rows = _pack_table(experts_v, layer, with_rows=True)

    h, idx, gate = _route(x, gain, wq, keys, t_tc, SC_TOKENS, v_rows[:SUBLANES])
    w_sc = _expert_in(h, _split_offsets(idx), gate, u_packed)
    out_sc = _sc_expert_out(v_rows, (idx // ROW_WORDS).reshape(-1), w_sc.reshape(-1), x, t_tc)

    h, idx, gate = _route(x, gain, wq, keys, 0, t_tc, w_sc[:SUBLANES])
    idx_split = _split_offsets(idx)
    w = _expert_in(h, idx_split, gate, u_packed)
    out_tc = _expert_out(x, idx_split, w, v_packed)
    return jnp.concatenate([out_tc, out_sc], axis=0)


def kernel(x, norm_mix, norm_ffn, attn_w_qkv, attn_q_norm, attn_k_norm, attn_sinks, attn_w_o, conv_w_in, conv_w, conv_w_out, peer_w_query, peer_sub_keys, peer_u, peer_v):
    batch, seq, d = x.shape
    xt = x.reshape(batch * seq, d)
    for i in range(norm_mix.shape[0]):
        j = i // 2
        if i % 2 == 0:
            qkv = _norm_matmul(xt, norm_mix[i], attn_w_qkv[j].astype(BF16))
            o = _attention(qkv, attn_q_norm[j], attn_k_norm[j], attn_sinks[j], batch, seq)
            xt = _matmul_residual(o, attn_w_o[j].astype(BF16), xt)
        else:
            xt = _conv_mixer(xt, norm_mix[i], conv_w_in[j], conv_w[j], conv_w_out[j], batch, seq)
        xt = _peer(xt, norm_ffn[i], peer_w_query[i], peer_sub_keys[i], peer_u, peer_v, i)
    return xt.reshape(batch, seq, d)
```
